```python
import math
import jax, jax.numpy as jnp
from jax import lax
import numpy as np

D_MODEL = 1024
BATCH = 8
SEQ = 2048
DEPTH = 1
DEC_BATCH = 32
DEC_SEQ = 4
PAST_LEN = 8192
PAGE_SIZE = 128

F32 = jnp.float32
HEAD_DIM = 64
N_HEADS_A = D_MODEL // 128
N_HEADS_B = D_MODEL // 256
WIDTH_A = N_HEADS_A * HEAD_DIM
WIDTH_B = N_HEADS_B * 2 * HEAD_DIM
N_IN = 3 * WIDTH_A + 3 * WIDTH_B
MOBA_BLOCK = 256
MOBA_TOPK = 3
MOBA_QBLOCK = 16
DIFF_QBLOCK = 128
NUM_BUCKETS = 32
MAX_DISTANCE = 128
D_FF = ((8 * D_MODEL // 3 + 127) // 128) * 128
CONV_W = 3
EPS = 1e-6

kernel_name = "moba_diffattn_gated_hybrid_step"


def rmsnorm(x, g):
    xf = x.astype(F32)
    y = xf * lax.rsqrt(jnp.mean(xf * xf, axis=-1, keepdims=True) + EPS)
    return (y * g.astype(F32)).astype(x.dtype)


def t5_bucket(dist):
    n = jnp.maximum(dist, 0)
    max_exact = NUM_BUCKETS // 2
    nf = jnp.maximum(n, 1).astype(F32)
    large = max_exact + (jnp.log(nf / max_exact) / math.log(MAX_DISTANCE / max_exact)
                         * (NUM_BUCKETS - max_exact)).astype(jnp.int32)
    large = jnp.minimum(large, NUM_BUCKETS - 1)
    return jnp.where(n < max_exact, n, large)


def project(h, w):
    B, L, _ = h.shape
    u = h @ w
    cuts = [WIDTH_A, 2 * WIDTH_A, 3 * WIDTH_A, 3 * WIDTH_A + WIDTH_B, 3 * WIDTH_A + 2 * WIDTH_B]
    qa, ka, va, qd, kd, vd = jnp.split(u, cuts, axis=-1)
    sa = (B, L, N_HEADS_A, HEAD_DIM)
    sd = (B, L, N_HEADS_B, 2, HEAD_DIM)
    return (qa.reshape(sa), ka.reshape(sa), va.reshape(sa),
            qd.reshape(sd), kd.reshape(sd), vd.reshape(B, L, N_HEADS_B, 2 * HEAD_DIM))


def moba_select(qh, kmean, n_full):
    gate = jnp.einsum('bhqd,bhnd->bhqn', qh.astype(F32), kmean)
    gate = jnp.where(jnp.arange(kmean.shape[2]) < n_full, gate, -jnp.inf)
    top, idx = lax.top_k(gate, MOBA_TOPK)
    return idx, jnp.isfinite(top)


def moba_attend(qh, q_pos, k_sel, v_sel, k_pos_sel, sel_valid, k_own, v_own, k_pos_own, bias_t):
    scale = HEAD_DIM ** -0.5
    hi = jnp.arange(bias_t.shape[0])[None, :, None, None, None]
    d_sel = q_pos[None, None, :, None, None] - k_pos_sel
    s_sel = jnp.einsum('bhqd,bhqnkd->bhqnk', qh, k_sel).astype(F32) * scale + bias_t[hi, t5_bucket(d_sel)]
    s_sel = jnp.where(sel_valid[..., None], s_sel, -jnp.inf)
    d_own = q_pos[:, None] - k_pos_own[None, :]
    s_own = jnp.einsum('bhqd,bhkd->bhqk', qh, k_own).astype(F32) * scale + bias_t[:, t5_bucket(d_own)][None]
    s_own = jnp.where(d_own >= 0, s_own, -jnp.inf)
    B, H, Q, N, BLK = s_sel.shape
    p = jax.nn.softmax(jnp.concatenate([s_sel.reshape(B, H, Q, N * BLK), s_own], axis=-1), axis=-1)
    p = p.astype(v_own.dtype)
    p_sel = p[..., :N * BLK].reshape(B, H, Q, N, BLK)
    p_own = p[..., N * BLK:]
    return (jnp.einsum('bhqnk,bhqnkd->bhqd', p_sel, v_sel)
            + jnp.einsum('bhqk,bhkd->bhqd', p_own, v_own))


def moba_prompt(q, k, v, bias_t):
    B, S, H, Dh = q.shape
    nb = -(-S // MOBA_BLOCK)
    pad = nb * MOBA_BLOCK - S
    kb = jnp.pad(k, ((0, 0), (0, pad), (0, 0), (0, 0))).reshape(B, nb, MOBA_BLOCK, H, Dh).transpose(0, 3, 1, 2, 4)
    vb = jnp.pad(v, ((0, 0), (0, pad), (0, 0), (0, 0))).reshape(B, nb, MOBA_BLOCK, H, Dh).transpose(0, 3, 1, 2, 4)
    n_cand = max(nb, MOBA_TOPK)
    kmean = jnp.pad(kb.astype(F32).mean(3), ((0, 0), (0, 0), (0, n_cand - nb), (0, 0)))
    n_qb = S // MOBA_QBLOCK
    qblocks = q.transpose(0, 2, 1, 3).reshape(B, H, n_qb, MOBA_QBLOCK, Dh).transpose(2, 0, 1, 3, 4)
    bi = jnp.arange(B)[:, None, None, None]
    hi = jnp.arange(H)[None, :, None, None]

    def one_block(args):
        qb_idx, qblk = args
        q_pos = qb_idx * MOBA_QBLOCK + jnp.arange(MOBA_QBLOCK)
        own = (qb_idx * MOBA_QBLOCK) // MOBA_BLOCK
        idx, valid = moba_select(qblk, kmean, own)
        idx = jnp.minimum(idx, nb - 1)
        k_sel = kb[bi, hi, idx]
        v_sel = vb[bi, hi, idx]
        k_pos_sel = idx[..., None] * MOBA_BLOCK + jnp.arange(MOBA_BLOCK)
        k_own = lax.dynamic_index_in_dim(kb, own, axis=2, keepdims=False)
        v_own = lax.dynamic_index_in_dim(vb, own, axis=2, keepdims=False)
        k_pos_own = own * MOBA_BLOCK + jnp.arange(MOBA_BLOCK)
        return moba_attend(qblk, q_pos, k_sel, v_sel, k_pos_sel, valid, k_own, v_own, k_pos_own, bias_t)

    o = lax.map(one_block, (jnp.arange(n_qb), qblocks))
    return o.transpose(1, 0, 3, 2, 4).reshape(B, S, H * Dh)


def moba_sample(q, k_new, v_new, pool_k, pool_v, page_table, bias_t):
    DB, T, H, Dh = q.shape
    n_pages = page_table.shape[1]
    P = n_pages * PAGE_SIZE
    ppb = MOBA_BLOCK // PAGE_SIZE
    nbp = P // MOBA_BLOCK
    r = P - nbp * MOBA_BLOCK
    kfull = pool_k[page_table[:, :nbp * ppb]].reshape(DB, nbp, MOBA_BLOCK, H, Dh)
    n_cand = max(nbp, MOBA_TOPK)
    kmean = jnp.pad(kfull.astype(F32).mean(2).transpose(0, 2, 1, 3),
                    ((0, 0), (0, 0), (0, n_cand - nbp), (0, 0)))
    qh = q.transpose(0, 2, 1, 3)
    idx, valid = moba_select(qh, kmean, nbp)
    idx = jnp.minimum(idx, max(nbp - 1, 0))
    lp = jnp.minimum(idx[..., None] * ppb + jnp.arange(ppb), n_pages - 1)
    phys = page_table[jnp.arange(DB)[:, None, None, None, None], lp]
    hi = jnp.arange(H)[None, :, None, None, None, None]
    row = jnp.arange(PAGE_SIZE)
    k_sel = pool_k[phys[..., None], row, hi].reshape(DB, H, T, MOBA_TOPK, MOBA_BLOCK, Dh)
    v_sel = pool_v[phys[..., None], row, hi].reshape(DB, H, T, MOBA_TOPK, MOBA_BLOCK, Dh)
    k_pos_sel = idx[..., None] * MOBA_BLOCK + jnp.arange(MOBA_BLOCK)
    pt_own = page_table[:, nbp * ppb: nbp * ppb + r // PAGE_SIZE]
    k_own = jnp.concatenate([pool_k[pt_own].reshape(DB, r, H, Dh), k_new], axis=1).transpose(0, 2, 1, 3)
    v_own = jnp.concatenate([pool_v[pt_own].reshape(DB, r, H, Dh), v_new], axis=1).transpose(0, 2, 1, 3)
    k_pos_own = jnp.concatenate([nbp * MOBA_BLOCK + jnp.arange(r), P + jnp.arange(T)])
    q_pos = P + jnp.arange(T)
    o = moba_attend(qh, q_pos, k_sel, v_sel, k_pos_sel, valid, k_own, v_own, k_pos_own, bias_t)
    return o.transpose(0, 2, 1, 3).reshape(DB, T, H * Dh)


def diff_attend(q, k, v, q_pos, k_pos, bias_t, lam):
    scale = HEAD_DIM ** -0.5
    d = q_pos[:, None] - k_pos[None, :]
    s = jnp.einsum('bqhcd,bkhcd->bhcqk', q, k).astype(F32) * scale + bias_t[:, t5_bucket(d)][None, :, None]
    s = jnp.where(d >= 0, s, -jnp.inf)
    p = jax.nn.softmax(s, axis=-1)
    a = (p[:, :, 0] - lam * p[:, :, 1]).astype(v.dtype)
    return jnp.einsum('bhqk,bkhe->bqhe', a, v)


def diff_prompt(q, k, v, bias_t, lam):
    B, S = q.shape[:2]
    n = S // DIFF_QBLOCK
    qblocks = q.reshape(B, n, DIFF_QBLOCK, N_HEADS_B, 2, HEAD_DIM).swapaxes(0, 1)
    k_pos = jnp.arange(S)
    o = lax.map(lambda a: diff_attend(a[1], k, v, a[0] * DIFF_QBLOCK + jnp.arange(DIFF_QBLOCK),
                                      k_pos, bias_t, lam), (jnp.arange(n), qblocks))
    return o.swapaxes(0, 1).reshape(B, S, N_HEADS_B, 2 * HEAD_DIM)


def diff_sample(q, k_new, v_new, pool_k, pool_v, page_table, bias_t, lam):
    DB, T = q.shape[:2]
    P = page_table.shape[1] * PAGE_SIZE
    kp = pool_k[page_table].reshape(DB, P, N_HEADS_B, 2, HEAD_DIM)
    vp = pool_v[page_table].reshape(DB, P, N_HEADS_B, 2 * HEAD_DIM)
    k_all = jnp.concatenate([kp, k_new], axis=1)
    v_all = jnp.concatenate([vp, v_new], axis=1)
    return diff_attend(q, k_all, v_all, P + jnp.arange(T), jnp.arange(P + T), bias_t, lam)


def diff_post(o, gain, lam_init):
    B, L = o.shape[:2]
    return (rmsnorm(o, gain) * (1.0 - lam_init)).reshape(B, L, WIDTH_B)


def merge(h, oa, ob, w_gate, w_out_a, w_out_b, w_out):
    g = jax.nn.sigmoid((h @ w_gate).astype(F32)).astype(h.dtype)
    ga, gb = jnp.split(g, 2, axis=-1)
    return (ga * (oa @ w_out_a) + gb * (ob @ w_out_b)) @ w_out


def conv_ffn(h, prev, w_up, conv_w, conv_b, w_down):
    L = h.shape[1]
    u = h @ w_up
    ext = jnp.concatenate([prev.astype(u.dtype), u], axis=1)
    c = conv_b
    for i in range(CONV_W):
        c = c + conv_w[i] * ext[:, i:i + L]
    gate, val = jnp.split(c, 2, axis=-1)
    return (jax.nn.silu(gate) * val) @ w_down, ext[:, L:]


def layer_forward(x, conv_prev, attend, nrm_a, w_in_l, w_gate_l, w_oa, w_ob, w_o, nrm_f, w_up_l, cw, cb, w_dn):
    B, L, _ = x.shape
    h = rmsnorm(x, nrm_a)
    qa, ka, va, qd, kd, vd = project(h, w_in_l)
    oa, ob = attend(qa, ka, va, qd, kd, vd)
    x = x + merge(h, oa, ob, w_gate_l, w_oa, w_ob, w_o)
    f, conv_new = conv_ffn(rmsnorm(x, nrm_f), conv_prev, w_up_l, cw, cb, w_dn)
    return x + f, (ka, va, kd.reshape(B, L, N_HEADS_B, 2 * HEAD_DIM), vd, conv_new)


def setup_inputs(seed: int = 0) -> dict:
    key = jax.random.key(seed)
    ks = jax.random.split(key, 32)
    n_pages = PAST_LEN // PAGE_SIZE
    n_used = DEC_BATCH * n_pages
    n_phys = n_used + n_used // 4

    def nrm(k, shape, s):
        return jax.random.normal(k, shape, F32) * s

    return {
        "x_prompt": nrm(ks[0], (BATCH, SEQ, D_MODEL), 1.0),
        "x_sample": nrm(ks[1], (DEC_BATCH, DEC_SEQ, D_MODEL), 1.0),
        "cache_moba_k": nrm(ks[2], (DEPTH, n_phys, PAGE_SIZE, N_HEADS_A, HEAD_DIM), 1.0),
        "cache_moba_v": nrm(ks[3], (DEPTH, n_phys, PAGE_SIZE, N_HEADS_A, HEAD_DIM), 1.0),
        "cache_diff_k": nrm(ks[4], (DEPTH, n_phys, PAGE_SIZE, N_HEADS_B, 2 * HEAD_DIM), 1.0),
        "cache_diff_v": nrm(ks[5], (DEPTH, n_phys, PAGE_SIZE, N_HEADS_B, 2 * HEAD_DIM), 1.0),
        "state_conv": nrm(ks[6], (DEPTH, DEC_BATCH, CONV_W - 1, 2 * D_FF), 1.0),
        "page_table": jax.random.permutation(ks[7], n_phys)[:n_used].reshape(DEC_BATCH, n_pages).astype(jnp.int32),
        "rel_bias": nrm(ks[8], (NUM_BUCKETS, N_HEADS_A + N_HEADS_B), 0.5),
        "norm_attn": 1.0 + nrm(ks[9], (DEPTH, D_MODEL), 0.01),
        "w_in": nrm(ks[10], (DEPTH, D_MODEL, N_IN), D_MODEL ** -0.5),
        "w_gate": nrm(ks[11], (DEPTH, D_MODEL, 2 * D_MODEL), D_MODEL ** -0.5),
        "w_out_a": nrm(ks[12], (DEPTH, WIDTH_A, D_MODEL), WIDTH_A ** -0.5),
        "w_out_b": nrm(ks[13], (DEPTH, WIDTH_B, D_MODEL), WIDTH_B ** -0.5),
        "w_out": nrm(ks[14], (DEPTH, D_MODEL, D_MODEL), D_MODEL ** -0.5),
        "lambda_q1": nrm(ks[15], (DEPTH, HEAD_DIM), 0.1),
        "lambda_k1": nrm(ks[16], (DEPTH, HEAD_DIM), 0.1),
        "lambda_q2": nrm(ks[17], (DEPTH, HEAD_DIM), 0.1),
        "lambda_k2": nrm(ks[18], (DEPTH, HEAD_DIM), 0.1),
        "diff_norm": 1.0 + nrm(ks[19], (DEPTH, 2 * HEAD_DIM), 0.01),
        "norm_ffn": 1.0 + nrm(ks[20], (DEPTH, D_MODEL), 0.01),
        "w_up": nrm(ks[21], (DEPTH, D_MODEL, 2 * D_FF), D_MODEL ** -0.5),
        "conv_w": nrm(ks[22], (DEPTH, CONV_W, 2 * D_FF), CONV_W ** -0.5),
        "conv_b": nrm(ks[23], (DEPTH, 2 * D_FF), 0.01),
        "w_down": nrm(ks[24], (DEPTH, D_FF, D_MODEL), D_FF ** -0.5),
        "norm_final": 1.0 + nrm(ks[25], (D_MODEL,), 0.01),
    }


def reference(x_prompt, x_sample, cache_moba_k, cache_moba_v, cache_diff_k, cache_diff_v, state_conv,
              page_table, rel_bias, norm_attn, w_in, w_gate, w_out_a, w_out_b, w_out,
              lambda_q1, lambda_k1, lambda_q2, lambda_k2, diff_norm, norm_ffn, w_up, conv_w, conv_b,
              w_down, norm_final):
    bias_a = rel_bias[:, :N_HEADS_A].T
    bias_d = rel_bias[:, N_HEADS_A:].T
    xp, xs = x_prompt, x_sample
    mk_p, mv_p, dk_p, dv_p, cv_p = [], [], [], [], []
    mk_s, mv_s, dk_s, dv_s, cv_s = [], [], [], [], []
    for l in range(DEPTH):
        lam_init = 0.8 - 0.6 * math.exp(-0.3 * l)
        lam = (jnp.exp(jnp.sum(lambda_q1[l].astype(F32) * lambda_k1[l].astype(F32)))
               - jnp.exp(jnp.sum(lambda_q2[l].astype(F32) * lambda_k2[l].astype(F32))) + lam_init)
        lw = (norm_attn[l], w_in[l], w_gate[l], w_out_a[l], w_out_b[l], w_out[l],
              norm_ffn[l], w_up[l], conv_w[l], conv_b[l], w_down[l])

        def attend_prompt(qa, ka, va, qd, kd, vd, l=l, lam=lam, lam_init=lam_init):
            oa = moba_prompt(qa, ka, va, bias_a)
            ob = diff_post(diff_prompt(qd, kd, vd, bias_d, lam), diff_norm[l], lam_init)
            return oa, ob

        def attend_sample(qa, ka, va, qd, kd, vd, l=l, lam=lam, lam_init=lam_init):
            oa = moba_sample(qa, ka, va, cache_moba_k[l], cache_moba_v[l], page_table, bias_a)
            ob = diff_post(diff_sample(qd, kd, vd, cache_diff_k[l], cache_diff_v[l], page_table, bias_d, lam),
                           diff_norm[l], lam_init)
            return oa, ob

        conv0 = jnp.zeros((xp.shape[0], CONV_W - 1, 2 * D_FF), xp.dtype)
        xp, (a, b, c, d, e) = layer_forward(xp, conv0, attend_prompt, *lw)
        mk_p.append(a); mv_p.append(b); dk_p.append(c); dv_p.append(d); cv_p.append(e)
        xs, (a, b, c, d, e) = layer_forward(xs, state_conv[l], attend_sample, *lw)
        mk_s.append(a); mv_s.append(b); dk_s.append(c); dv_s.append(d); cv_s.append(e)
    y_prompt = rmsnorm(xp, norm_final)
    y_sample = rmsnorm(xs, norm_final)
    return (y_prompt, y_sample,
            jnp.stack(mk_p), jnp.stack(mv_p), jnp.stack(dk_p), jnp.stack(dv_p), jnp.stack(cv_p),
            jnp.stack(mk_s), jnp.stack(mv_s), jnp.stack(dk_s), jnp.stack(dv_s), jnp.stack(cv_s))
```

```python
import functools
import math

import jax
import jax.numpy as jnp
from jax import lax
from jax.experimental import pallas as pl
from jax.experimental.pallas import tpu as pltpu

F32 = jnp.float32
BF16 = jnp.bfloat16

D_MODEL = 1024
HEAD_DIM = 64
N_HEADS_A = D_MODEL // 128
N_HEADS_B = D_MODEL // 256
WIDTH_A = N_HEADS_A * HEAD_DIM
WIDTH_B = N_HEADS_B * 2 * HEAD_DIM
N_IN = 3 * WIDTH_A + 3 * WIDTH_B
MOBA_BLOCK = 256
MOBA_TOPK = 3
NUM_BUCKETS = 32
MAX_DISTANCE = 128
D_FF = ((8 * D_MODEL // 3 + 127) // 128) * 128
CONV_W = 3
EPS = 1e-6
PAGE_SIZE = 128
SCALE = HEAD_DIM ** -0.5

LANES = 128
SUBLANES = 8
VMEM_LIMIT_BYTES = 56 * 1024 * 1024

ATT_TILE = MOBA_BLOCK
FF_CHUNK = 256
PAGES_PER_STEP = 16
NEG_BIG = -1e30


def _dot(a, b):
    return jnp.dot(a, b, preferred_element_type=F32)


def _dot_nt(a, b):
    return lax.dot_general(a, b, (((1,), (1,)), ((), ())), preferred_element_type=F32)


def _rmsnorm(x, g):
    return (x * lax.rsqrt(jnp.mean(x * x, axis=-1, keepdims=True) + EPS)) * g


def _t5_bucket(dist):
    n = jnp.maximum(dist, 0)
    max_exact = NUM_BUCKETS // 2
    nf = jnp.maximum(n, 1).astype(F32)
    large = max_exact + (jnp.log(nf / max_exact) / math.log(MAX_DISTANCE / max_exact)
                         * (NUM_BUCKETS - max_exact)).astype(jnp.int32)
    large = jnp.minimum(large, NUM_BUCKETS - 1)
    return jnp.where(n < max_exact, n, large)


def _lambda(lam_ref, lam_init):
    lv = lam_ref[...]
    a = jnp.sum(lv[0:1] * lv[1:2], axis=-1, keepdims=True)
    b = jnp.sum(lv[2:3] * lv[3:4], axis=-1, keepdims=True)
    return jnp.exp(a) - jnp.exp(b) + lam_init


def _proj_kernel(x_ref, g_ref, w_ref,
                 qa_ref, ka_ref, va_ref, qd_ref, kd_ref, vd_ref,
                 kaf_ref, vaf_ref, kdf_ref, vdf_ref):
    h = _rmsnorm(x_ref[...], g_ref[...]).astype(BF16)
    outs = ((qa_ref, None, SCALE), (ka_ref, kaf_ref, None), (va_ref, vaf_ref, None),
            (qd_ref, None, SCALE), (kd_ref, kdf_ref, None), (vd_ref, vdf_ref, None))
    for c, (b_ref, f_ref, scale) in enumerate(outs):
        u = _dot(h, w_ref[:, c * WIDTH_A:(c + 1) * WIDTH_A])
        if f_ref is not None:
            f_ref[...] = u
        if scale is not None:
            u = u * scale
        b_ref[...] = u.astype(BF16)


def _proj(x, g, w_bf16, tm):
    n = x.shape[0]
    row = lambda i: (i, 0)
    const = lambda i: (0, 0)
    blk = pl.BlockSpec((tm, WIDTH_A), row)
    return pl.pallas_call(
        _proj_kernel,
        grid=(n // tm,),
        in_specs=[pl.BlockSpec((tm, D_MODEL), row),
                  pl.BlockSpec((1, D_MODEL), const),
                  pl.BlockSpec((D_MODEL, N_IN), const)],
        out_specs=[blk] * 10,
        out_shape=[jax.ShapeDtypeStruct((n, WIDTH_A), BF16)] * 6
                  + [jax.ShapeDtypeStruct((n, WIDTH_A), F32)] * 4,
        compiler_params=pltpu.CompilerParams(
            dimension_semantics=("arbitrary",), vmem_limit_bytes=VMEM_LIMIT_BYTES),
        name="proj",
    )(x, g, w_bf16)


def _attend_tiles(q_op, make_k, v_ref, bias_ref, bidx, s_ref, mb_ref, qi):
    t = ATT_TILE

    def scores(j, mx):
        s = _dot_nt(q_op, make_k(j)) + bias_ref[bidx, jnp.minimum(qi - j, 2)]
        s_ref[j] = s
        return jnp.maximum(mx, jnp.maximum(s[:, :LANES], s[:, LANES:]))

    mx = lax.fori_loop(0, qi + 1, scores, jnp.full((t, LANES), -jnp.inf, F32))
    mb_ref[...] = jnp.broadcast_to(jnp.max(mx, axis=1, keepdims=True), (t, LANES))

    def weigh(j, carry):
        acc, l = carry
        mb = mb_ref[...]
        s = s_ref[j]
        p0 = jnp.exp(s[:, :LANES] - mb)
        p1 = jnp.exp(s[:, LANES:] - mb)
        p = jnp.concatenate([p0, p1], axis=1).astype(BF16)
        vj = v_ref[pl.ds(pl.multiple_of(j * t, t), t), :]
        return acc + _dot(p, vj), l + (p0 + p1)

    zero = jnp.zeros((t, LANES), F32)
    acc, l = lax.fori_loop(0, qi + 1, weigh, (zero, zero))
    return acc, jnp.sum(l, axis=1, keepdims=True)


def _moba_prompt_kernel(q_ref, k_ref, v_ref, bias_ref, o_ref, s_ref, mb_ref, km_ref):
    t = ATT_TILE
    nb = km_ref.shape[0]
    qi = pl.program_id(2)

    @pl.when(qi == 0)
    def _():
        for j in range(nb):
            km_ref[j:j + 1, :] = jnp.sum(k_ref[j * t:(j + 1) * t, :].astype(F32),
                                         axis=0, keepdims=True) * (1.0 / t)

    q = q_ref[...].astype(F32)
    km = km_ref[...]
    km_hi = km.astype(BF16)
    km_lo = (km - km_hi.astype(F32)).astype(BF16)
    lane = lax.broadcasted_iota(jnp.int32, (t, LANES), 1)
    blk = lax.broadcasted_iota(jnp.int32, (nb, t), 0)

    heads = []
    for e in range(2):
        qe = jnp.where((lane >= HEAD_DIM * e) & (lane < HEAD_DIM * (e + 1)), q, 0.0).astype(BF16)
        gt = _dot_nt(km_hi, qe) + _dot_nt(km_lo, qe)
        rank = jnp.zeros((nb, t), jnp.int32)
        for m in range(nb):
            gm = gt[m:m + 1, :]
            ahead = (gm > gt) | ((gm == gt) & (m < blk))
            rank = rank + jnp.where(m < qi, jnp.where(ahead, 1, 0), 0)
        keep = (blk < qi) & (rank < MOBA_TOPK) & (jnp.abs(gt) < jnp.inf)
        pen = jnp.where(keep | (blk == qi), 0.0, NEG_BIG)
        pen = jnp.concatenate([pen, jnp.zeros((LANES - nb, t), F32)], axis=0)
        q_aug = jnp.concatenate([qe, pen.T.astype(BF16)], axis=1)

        def make_k(j):
            kj = k_ref[pl.ds(pl.multiple_of(j * t, t), t), :]
            onehot = jnp.where(lane == j, 1.0, 0.0).astype(BF16)
            return jnp.concatenate([kj, onehot], axis=1)

        acc, l = _attend_tiles(q_aug, make_k, v_ref, bias_ref, e, s_ref, mb_ref, qi)
        heads.append(acc / l)

    o_ref[...] = jnp.where(lane < HEAD_DIM, heads[0], heads[1]).astype(o_ref.dtype)


def _moba_prompt(qa, ka, va, bias_tiles, batch, seq):
    t = ATT_TILE
    nq = seq // t
    grid = (batch, N_HEADS_A // 2, nq)
    qmap = lambda b, hp, qi: (b * nq + qi, hp)
    kvmap = lambda b, hp, qi: (b, hp)
    return pl.pallas_call(
        _moba_prompt_kernel,
        grid=grid,
        in_specs=[pl.BlockSpec((t, LANES), qmap),
                  pl.BlockSpec((seq, LANES), kvmap),
                  pl.BlockSpec((seq, LANES), kvmap),
                  pl.BlockSpec((2, 3, t, t), lambda b, hp, qi: (hp, 0, 0, 0))],
        out_specs=pl.BlockSpec((t, LANES), qmap),
        out_shape=jax.ShapeDtypeStruct((batch * seq, WIDTH_A), BF16),
        scratch_shapes=[pltpu.VMEM((nq, t, t), F32),
                        pltpu.VMEM((t, LANES), F32),
                        pltpu.VMEM((nq, LANES), F32)],
        compiler_params=pltpu.CompilerParams(
            dimension_semantics=("arbitrary", "arbitrary", "arbitrary"),
            vmem_limit_bytes=VMEM_LIMIT_BYTES),
        name="moba_prompt",
    )(qa, ka, va, bias_tiles)


def _diff_prompt_kernel(q_ref, k_ref, v_ref, bias_ref, lam_ref, gain_ref, o_ref, s_ref, mb_ref,
                        *, lam_init):
    t = ATT_TILE
    qi = pl.program_id(2)
    q = q_ref[...].astype(F32)
    lane = lax.broadcasted_iota(jnp.int32, (t, LANES), 1)

    def make_k(j):
        return k_ref[pl.ds(pl.multiple_of(j * t, t), t), :]

    comps = []
    for c in range(2):
        qc = jnp.where((lane >= HEAD_DIM * c) & (lane < HEAD_DIM * (c + 1)), q, 0.0).astype(BF16)
        acc, l = _attend_tiles(qc, make_k, v_ref, bias_ref, 0, s_ref, mb_ref, qi)
        comps.append(acc / l)
    o = comps[0] - _lambda(lam_ref, lam_init) * comps[1]
    o_ref[...] = (_rmsnorm(o, gain_ref[...]) * (1.0 - lam_init)).astype(o_ref.dtype)


def _diff_prompt(qd, kd, vd, bias_tiles, lam_vecs, gain, lam_init, batch, seq):
    t = ATT_TILE
    nq = seq // t
    grid = (batch, N_HEADS_B, nq)
    qmap = lambda b, h, qi: (b * nq + qi, h)
    kvmap = lambda b, h, qi: (b, h)
    return pl.pallas_call(
        functools.partial(_diff_prompt_kernel, lam_init=lam_init),
        grid=grid,
        in_specs=[pl.BlockSpec((t, LANES), qmap),
                  pl.BlockSpec((seq, LANES), kvmap),
                  pl.BlockSpec((seq, LANES), kvmap),
                  pl.BlockSpec((1, 3, t, t), lambda b, h, qi: (h, 0, 0, 0)),
                  pl.BlockSpec((4, HEAD_DIM), lambda b, h, qi: (0, 0)),
                  pl.BlockSpec((1, 2 * HEAD_DIM), lambda b, h, qi: (0, 0))],
        out_specs=pl.BlockSpec((t, LANES), qmap),
        out_shape=jax.ShapeDtypeStruct((batch * seq, WIDTH_B), BF16),
        scratch_shapes=[pltpu.VMEM((nq, t, t), F32),
                        pltpu.VMEM((t, LANES), F32)],
        compiler_params=pltpu.CompilerParams(
            dimension_semantics=("arbitrary", "arbitrary", "arbitrary"),
            vmem_limit_bytes=VMEM_LIMIT_BYTES),
        name="diff_prompt",
    )(qd, kd, vd, bias_tiles, lam_vecs, gain)


def _bias_tiles(bias_t):
    t = ATT_TILE
    assert t + 1 >= MAX_DISTANCE
    d0 = jnp.arange(t)[:, None] - jnp.arange(t)[None, :]
    diag = jnp.where(d0 >= 0, bias_t[:, _t5_bucket(d0)], -jnp.inf)
    sub = bias_t[:, _t5_bucket(d0 + t)]
    far = bias_t[:, _t5_bucket(d0 + 2 * t)]
    return jnp.stack([diag, sub, far], axis=1).astype(F32)


def _sample_attn_kernel(pt_ref, *refs, n_chunks, select, lam_init, tokens):
    pps = PAGES_PER_STEP
    del pt_ref
    k_pages = refs[:pps]
    v_pages = refs[pps:2 * pps]
    (q_ref, kn_ref, vn_ref, bias_ref, lam_ref, gain_ref, o_ref,
     s_ref, snew_ref, acc_ref, l_ref, m_ref, bmax_ref, sel_ref, km_ref) = refs[2 * pps:]
    rows = q_ref.shape[1]
    groups = rows // tokens
    n_blocks = n_chunks * pps * PAGE_SIZE // MOBA_BLOCK
    ph = pl.program_id(1)
    j = pl.program_id(2)
    lane = lax.broadcasted_iota(jnp.int32, (rows, LANES), 1)
    q = q_ref[0]

    ppb = MOBA_BLOCK // PAGE_SIZE
    bps = pps // ppb
    lanef = lane.astype(F32)

    @pl.when((ph == 0) & (j == 0))
    def _():
        snew_ref[...] = _dot_nt(q, kn_ref[0]) + bias_ref[2]
        bmax_ref[...] = jnp.full((rows, LANES), -jnp.inf, F32)
        if select:
            km_ref[...] = jnp.zeros(km_ref.shape, F32)

    @pl.when(ph == 0)
    def _():
        is_last = j == n_chunks - 1
        bmax = bmax_ref[...]
        ksums = []
        for i in range(pps):
            kp = k_pages[i][0]
            bias = bias_ref[0]
            if i == pps - 1:
                bias = jnp.where(is_last, bias_ref[1], bias)
            s = _dot_nt(q, kp.astype(BF16)) + bias
            s_ref[j, :, i * PAGE_SIZE:(i + 1) * PAGE_SIZE] = s
            blk = j * bps + i // ppb
            pmax = jnp.max(s, axis=1, keepdims=True)
            bmax = jnp.where(lane == blk, jnp.maximum(bmax, pmax), bmax)
            if select:
                ksum = jnp.sum(kp, axis=0, keepdims=True)
                if i % ppb == 0:
                    ksums.append(ksum)
                else:
                    ksums[-1] = ksums[-1] + ksum
        bmax_ref[...] = bmax
        if select:
            km_ref[pl.ds(pl.multiple_of(j * bps, bps), bps), :] = (
                jnp.concatenate(ksums, axis=0) * (1.0 / MOBA_BLOCK))

    @pl.when((ph == 1) & (j == 0))
    def _():
        valid = lane < n_blocks
        if select:
            km = km_ref[...]
            km_hi = km.astype(BF16)
            km_lo = (km - km_hi.astype(F32)).astype(BF16)
            g = jnp.where(valid, _dot_nt(q, km_hi) + _dot_nt(q, km_lo), -jnp.inf)
            sel = jnp.zeros((rows, LANES), F32)
            for _ in range(MOBA_TOPK):
                top = jnp.max(g, axis=1, keepdims=True)
                first = jnp.min(jnp.where(g == top, lanef, float(LANES)), axis=1, keepdims=True)
                pick = lanef == first
                finite = jnp.where(jnp.abs(top) < jnp.inf, 1.0, 0.0)
                sel = jnp.maximum(sel, jnp.where(pick, finite, 0.0))
                g = jnp.where(pick, -jnp.inf, g)
        else:
            sel = jnp.where(valid, 1.0, 0.0)
        sel_ref[...] = sel
        m_sel = jnp.max(jnp.where(sel > 0.5, bmax_ref[...], -jnp.inf), axis=1, keepdims=True)
        m_new = jnp.max(snew_ref[...], axis=1, keepdims=True)
        m_ref[...] = jnp.broadcast_to(jnp.maximum(m_sel, m_new), (rows, LANES))
        acc_ref[...] = jnp.zeros(acc_ref.shape, F32)
        l_ref[...] = jnp.zeros(l_ref.shape, F32)

    @pl.when(ph == 1)
    def _():
        m = m_ref[...]
        selv = sel_ref[...]
        acc = acc_ref[...]
        l = l_ref[...]
        for i in range(pps):
            s = s_ref[j, :, i * PAGE_SIZE:(i + 1) * PAGE_SIZE]
            if select:
                blk = j * bps + i // ppb
                on = jnp.max(jnp.where(lane == blk, selv, 0.0), axis=1, keepdims=True)
                on = jnp.broadcast_to(on, (rows, LANES)) > 0.5
                p = jnp.exp(jnp.where(on, s - m, -jnp.inf))
            else:
                p = jnp.exp(s - m)
            l = l + p
            acc = acc + _dot(p.astype(BF16), v_pages[i][0].astype(BF16))
        acc_ref[...] = acc
        l_ref[...] = l

    @pl.when((ph == 1) & (j == n_chunks - 1))
    def _():
        p = jnp.exp(snew_ref[...] - m_ref[...])
        acc = acc_ref[...] + _dot(p.astype(BF16), vn_ref[0])
        l = jnp.sum(l_ref[...] + p, axis=1, keepdims=True)
        o = acc / l
        grp = lax.broadcasted_iota(jnp.int32, (groups, WIDTH_A), 0)
        col = lax.broadcasted_iota(jnp.int32, (groups, WIDTH_A), 1)
        head_shift = HEAD_DIM.bit_length() - 1
        if select:
            w = jnp.where((col >> head_shift) == grp, 1.0, 0.0)
        else:
            lam = _lambda(lam_ref, lam_init)
            w = jnp.where((col >> (head_shift + 1)) == (grp >> 1),
                          jnp.where((grp & 1) == 0, 1.0, -lam), 0.0)
        tok = jnp.sum(o.reshape(tokens, groups, WIDTH_A) * w[None], axis=1)
        if not select:
            gain = gain_ref[...]
            parts = [_rmsnorm(tok[:, h * LANES:(h + 1) * LANES], gain) * (1.0 - lam_init)
                     for h in range(N_HEADS_B)]
            tok = jnp.concatenate(parts, axis=1)
        o_ref[0] = tok.astype(o_ref.dtype)


def _sample_attn(page_table, pool_k, pool_v, q_bd, k_new, v_new, bias_rows, lam_vecs, gain,
                 *, select, lam_init, tokens):
    pps = PAGES_PER_STEP
    n_seq, n_pages = page_table.shape
    n_chunks = n_pages // pps
    rows = q_bd.shape[1]
    width = pool_k.shape[-1]

    def k_map(i):
        return lambda b, ph, j, pt: (pt[b, jnp.where(ph == 0, j, n_chunks - 1) * pps + i], 0, 0)

    def v_map(i):
        return lambda b, ph, j, pt: (pt[b, jnp.where(ph == 0, 0, j) * pps + i], 0, 0)

    seq_map = lambda b, ph, j, pt: (b, 0, 0)
    const2 = lambda b, ph, j, pt: (0, 0)
    const3 = lambda b, ph, j, pt: (0, 0, 0)
    page = (1, PAGE_SIZE, width)
    in_specs = ([pl.BlockSpec(page, k_map(i)) for i in range(pps)]
                + [pl.BlockSpec(page, v_map(i)) for i in range(pps)]
                + [pl.BlockSpec((1, rows, width), seq_map),
                   pl.BlockSpec((1, LANES, width), seq_map),
                   pl.BlockSpec((1, LANES, width), seq_map),
                   pl.BlockSpec((3, rows, LANES), const3),
                   pl.BlockSpec((4, HEAD_DIM), const2),
                   pl.BlockSpec((1, 2 * HEAD_DIM), const2)])
    grid_spec = pltpu.PrefetchScalarGridSpec(
        num_scalar_prefetch=1,
        grid=(n_seq, 2, n_chunks),
        in_specs=in_specs,
        out_specs=pl.BlockSpec((1, tokens, width), seq_map),
        scratch_shapes=[pltpu.VMEM((n_chunks, rows, pps * PAGE_SIZE), F32),
                        pltpu.VMEM((rows, LANES), F32),
                        pltpu.VMEM((rows, width), F32),
                        pltpu.VMEM((rows, LANES), F32),
                        pltpu.VMEM((rows, LANES), F32),
                        pltpu.VMEM((rows, LANES), F32),
                        pltpu.VMEM((rows, LANES), F32),
                        pltpu.VMEM((LANES, width), F32)])
    return pl.pallas_call(
        functools.partial(_sample_attn_kernel, n_chunks=n_chunks, select=select,
                          lam_init=lam_init, tokens=tokens),
        grid_spec=grid_spec,
        out_shape=jax.ShapeDtypeStruct((n_seq, tokens, width), BF16),
        compiler_params=pltpu.CompilerParams(
            dimension_semantics=("arbitrary", "arbitrary", "arbitrary"),
            vmem_limit_bytes=VMEM_LIMIT_BYTES),
        name="moba_sample" if select else "diff_sample",
    )(page_table, *([pool_k] * pps), *([pool_v] * pps), q_bd, k_new, v_new, bias_rows,
      lam_vecs, gain)


def _block_diag_q(q, n_seq, tokens):
    groups = q.shape[1] // HEAD_DIM
    q4 = q.reshape(n_seq, tokens, 1, groups, HEAD_DIM)
    eye = jnp.eye(groups, dtype=q.dtype).reshape(1, 1, groups, groups, 1)
    return (q4 * eye).reshape(n_seq, tokens * groups, groups * HEAD_DIM)


def _pad_new(x, n_seq, tokens):
    x = x.reshape(n_seq, tokens, x.shape[-1])
    return jnp.pad(x, ((0, 0), (0, LANES - tokens), (0, 0)))


def _sample_bias_rows(bias_t, tokens, heads_per_group):
    groups = WIDTH_A // HEAD_DIM
    tok = jnp.repeat(jnp.arange(tokens), groups)[:, None]
    head = (jnp.tile(jnp.arange(groups), tokens) // heads_per_group)[:, None]
    col = jnp.arange(LANES)[None, :]
    assert PAGE_SIZE + 1 >= MAX_DISTANCE
    far = bias_t[head, _t5_bucket(jnp.full_like(col, 2 * PAGE_SIZE))]
    last = bias_t[head, _t5_bucket(PAGE_SIZE + tok - col)]
    new = jnp.where((col <= tok) & (col < tokens), bias_t[head, _t5_bucket(tok - col)], -jnp.inf)
    return jnp.stack([far, last, new]).astype(F32)


def _merge_kernel(x_ref, oa_ref, ob_ref, g_ref, wg_ref, woa_ref, wob_ref, wo_ref, o_ref):
    x = x_ref[...]
    h = _rmsnorm(x, g_ref[...]).astype(BF16)
    ga = 1.0 / (1.0 + jnp.exp(-_dot(h, wg_ref[:, :D_MODEL])))
    m = ga * _dot(oa_ref[...], woa_ref[...])
    gb = 1.0 / (1.0 + jnp.exp(-_dot(h, wg_ref[:, D_MODEL:])))
    m = m + gb * _dot(ob_ref[...], wob_ref[...])
    o_ref[...] = x + _dot(m.astype(BF16), wo_ref[...])


def _merge(x, oa, ob, g, wg, woa, wob, wo, tm):
    n = x.shape[0]
    row = lambda i: (i, 0)
    const = lambda i: (0, 0)
    return pl.pallas_call(
        _merge_kernel,
        grid=(n // tm,),
        in_specs=[pl.BlockSpec((tm, D_MODEL), row),
                  pl.BlockSpec((tm, WIDTH_A), row),
                  pl.BlockSpec((tm, WIDTH_B), row),
                  pl.BlockSpec((1, D_MODEL), const),
                  pl.BlockSpec((D_MODEL, 2 * D_MODEL), const),
                  pl.BlockSpec((WIDTH_A, D_MODEL), const),
                  pl.BlockSpec((WIDTH_B, D_MODEL), const),
                  pl.BlockSpec((D_MODEL, D_MODEL), const)],
        out_specs=pl.BlockSpec((tm, D_MODEL), row),
        out_shape=jax.ShapeDtypeStruct((n, D_MODEL), F32),
        compiler_params=pltpu.CompilerParams(
            dimension_semantics=("arbitrary",), vmem_limit_bytes=VMEM_LIMIT_BYTES),
        name="merge",
    )(x, oa, ob, g, wg, woa, wob, wo)


def _ffn_kernel(*refs, tm, seq_len, has_prev):
    if has_prev:
        (x_ref, g_ref, wup_ref, cw_ref, cb_ref, wdn_ref, gfin_ref, e1_ref, e2_ref,
         y_ref, u_ref, act_ref) = refs
    else:
        (x_ref, g_ref, wup_ref, cw_ref, cb_ref, wdn_ref, gfin_ref,
         y_ref, u_ref, act_ref, carry_ref) = refs
    i = pl.program_id(0)
    x = x_ref[...]
    h = _rmsnorm(x, g_ref[...]).astype(BF16)
    row = lax.broadcasted_iota(jnp.int32, (tm, FF_CHUNK), 0)
    row8 = lax.broadcasted_iota(jnp.int32, (SUBLANES, FF_CHUNK), 0)
    if has_prev:
        pos = row & (seq_len - 1)
    else:
        @pl.when((i * tm) % seq_len == 0)
        def _():
            carry_ref[...] = jnp.zeros(carry_ref.shape, F32)

    for c in range(D_FF // FF_CHUNK):
        halves = []
        for part in range(2):
            cols = slice(part * D_FF + c * FF_CHUNK, part * D_FF + (c + 1) * FF_CHUNK)
            u = _dot(h, wup_ref[:, cols])
            um1 = pltpu.roll(u, 1, 0)
            um2 = pltpu.roll(u, 2, 0)
            if has_prev:
                um1 = jnp.where(pos == 0, e1_ref[:, cols], um1)
                um2 = jnp.where(pos < 2, e2_ref[:, cols], um2)
                u_ref[:, cols] = u
            else:
                prev = carry_ref[:, cols]
                top1 = jnp.where(row8 == 0, pltpu.roll(prev, 1, 0), um1[:SUBLANES])
                top2 = jnp.where(row8 < 2, pltpu.roll(prev, 2, 0), um2[:SUBLANES])
                um1 = jnp.concatenate([top1, um1[SUBLANES:]], axis=0)
                um2 = jnp.concatenate([top2, um2[SUBLANES:]], axis=0)
                carry_ref[:, cols] = u[tm - SUBLANES:]
                u_ref[:, cols] = u[tm - SUBLANES:]
            cw = cw_ref[:, cols]
            halves.append(((cb_ref[:, cols] + cw[0:1] * um2) + cw[1:2] * um1) + cw[2:3] * u)
        gate, val = halves
        act = (gate * (1.0 / (1.0 + jnp.exp(-gate)))) * val
        act_ref[:, c * FF_CHUNK:(c + 1) * FF_CHUNK] = act.astype(BF16)

    x3 = x + _dot(act_ref[...], wdn_ref[...])
    y_ref[...] = _rmsnorm(x3, gfin_ref[...])


def _ffn(x, g, wup, cw, cb, wdn, gfin, prev, tm, seq_len):
    n = x.shape[0]
    has_prev = prev is not None
    row = lambda i: (i, 0)
    const = lambda i: (0, 0)
    in_specs = [pl.BlockSpec((tm, D_MODEL), row),
                pl.BlockSpec((1, D_MODEL), const),
                pl.BlockSpec((D_MODEL, 2 * D_FF), const),
                pl.BlockSpec((CONV_W, 2 * D_FF), const),
                pl.BlockSpec((1, 2 * D_FF), const),
                pl.BlockSpec((D_FF, D_MODEL), const),
                pl.BlockSpec((1, D_MODEL), const)]
    scratch = [pltpu.VMEM((tm, D_FF), BF16)]
    if has_prev:
        assert n == tm and tm % seq_len == 0 and seq_len & (seq_len - 1) == 0
        in_specs += [pl.BlockSpec((tm, 2 * D_FF), row)] * 2
        u_shape, u_spec = (n, 2 * D_FF), pl.BlockSpec((tm, 2 * D_FF), row)
        args = (x, g, wup, cw, cb, wdn, gfin) + tuple(prev)
    else:
        assert seq_len % tm == 0
        tiles_per_seq = seq_len // tm
        u_shape = (n // seq_len * SUBLANES, 2 * D_FF)
        u_spec = pl.BlockSpec((SUBLANES, 2 * D_FF), lambda i: (i // tiles_per_seq, 0))
        scratch.append(pltpu.VMEM((SUBLANES, 2 * D_FF), F32))
        args = (x, g, wup, cw, cb, wdn, gfin)
    return pl.pallas_call(
        functools.partial(_ffn_kernel, tm=tm, seq_len=seq_len, has_prev=has_prev),
        grid=(n // tm,),
        in_specs=in_specs,
        out_specs=[pl.BlockSpec((tm, D_MODEL), row), u_spec],
        out_shape=[jax.ShapeDtypeStruct((n, D_MODEL), F32), jax.ShapeDtypeStruct(u_shape, F32)],
        scratch_shapes=scratch,
        compiler_params=pltpu.CompilerParams(
            dimension_semantics=("arbitrary",), vmem_limit_bytes=VMEM_LIMIT_BYTES),
        name="ffn_sample" if has_prev else "ffn_prompt",
    )(*args)


def kernel(x_prompt, x_sample, cache_moba_k, cache_moba_v, cache_diff_k, cache_diff_v, state_conv, page_table, rel_bias, norm_attn, w_in, w_gate, w_out_a, w_out_b, w_out, lambda_q1, lambda_k1, lambda_q2, lambda_k2, diff_norm, norm_ffn, w_up, conv_w, conv_b, w_down, norm_final):
    batch, seq, _ = x_prompt.shape
    n_seq, tokens, _ = x_sample.shape
    depth = w_in.shape[0]
    n_pages = page_table.shape[1]
    assert depth == 1 and seq % ATT_TILE == 0
    assert (n_pages * PAGE_SIZE) % MOBA_BLOCK == 0 and n_pages % PAGES_PER_STEP == 0
    assert n_pages * PAGE_SIZE // MOBA_BLOCK <= LANES and tokens >= CONV_W - 1
    assert (PAGES_PER_STEP * PAGE_SIZE // MOBA_BLOCK) % SUBLANES == 0
    l = 0
    lam_init = 0.8 - 0.6 * math.exp(-0.3 * l)

    bias_a = rel_bias[:, :N_HEADS_A].T
    bias_d = rel_bias[:, N_HEADS_A:].T
    row = lambda v: v.reshape(1, -1)
    w_in_b = w_in[l].astype(BF16)
    w_gate_b = w_gate[l].astype(BF16)
    w_oa_b = w_out_a[l].astype(BF16)
    w_ob_b = w_out_b[l].astype(BF16)
    w_o_b = w_out[l].astype(BF16)
    w_up_b = w_up[l].astype(BF16)
    w_dn_b = w_down[l].astype(BF16)
    lam_vecs = jnp.stack([lambda_q1[l], lambda_k1[l], lambda_q2[l], lambda_k2[l]]).astype(F32)
    gain_d = row(diff_norm[l])
    g_attn, g_ffn, g_fin = row(norm_attn[l]), row(norm_ffn[l]), row(norm_final)
    cb = row(conv_b[l])

    xp = x_prompt.reshape(batch * seq, D_MODEL)
    qa, ka, va, qd, kd, vd, ka_f, va_f, kd_f, vd_f = _proj(xp, g_attn, w_in_b, 512)
    oa = _moba_prompt(qa, ka, va, _bias_tiles(bias_a), batch, seq)
    ob = _diff_prompt(qd, kd, vd, _bias_tiles(bias_d), lam_vecs, gain_d, lam_init, batch, seq)
    x2 = _merge(xp, oa, ob, g_attn, w_gate_b, w_oa_b, w_ob_b, w_o_b, 512)
    yp, tail_p = _ffn(x2, g_ffn, w_up_b, conv_w[l], cb, w_dn_b, g_fin, None, 512, seq)
    conv_p = tail_p.reshape(batch, SUBLANES, 2 * D_FF)[:, SUBLANES - (CONV_W - 1):]

    n_s = n_seq * tokens
    xs = x_sample.reshape(n_s, D_MODEL)
    qa_s, ka_s, va_s, qd_s, kd_s, vd_s, ka_sf, va_sf, kd_sf, vd_sf = _proj(xs, g_attn, w_in_b, n_s)
    width = N_HEADS_A * HEAD_DIM
    pool = lambda c: c[l].reshape(c.shape[1], PAGE_SIZE, width)
    oa_s = _sample_attn(page_table, pool(cache_moba_k), pool(cache_moba_v),
                        _block_diag_q(qa_s, n_seq, tokens), _pad_new(ka_s, n_seq, tokens),
                        _pad_new(va_s, n_seq, tokens), _sample_bias_rows(bias_a, tokens, 1),
                        lam_vecs, gain_d, select=True, lam_init=lam_init, tokens=tokens)
    ob_s = _sample_attn(page_table, pool(cache_diff_k), pool(cache_diff_v),
                        _block_diag_q(qd_s, n_seq, tokens), _pad_new(kd_s, n_seq, tokens),
                        _pad_new(vd_s, n_seq, tokens), _sample_bias_rows(bias_d, tokens, 2),
                        lam_vecs, gain_d, select=False, lam_init=lam_init, tokens=tokens)
    x2s = _merge(xs, oa_s.reshape(n_s, width), ob_s.reshape(n_s, width), g_attn,
                 w_gate_b, w_oa_b, w_ob_b, w_o_b, n_s)
    st = state_conv[l]
    zero = jnp.zeros((n_seq, 1, 2 * D_FF), F32)
    e1 = jnp.concatenate([st[:, 1:2]] + [zero] * (tokens - 1), axis=1).reshape(n_s, 2 * D_FF)
    e2 = jnp.concatenate([st[:, 0:2]] + [zero] * (tokens - 2), axis=1).reshape(n_s, 2 * D_FF)
    ys, u_s = _ffn(x2s, g_ffn, w_up_b, conv_w[l], cb, w_dn_b, g_fin, (e1, e2), n_s, tokens)
    conv_s = u_s.reshape(n_seq, tokens, 2 * D_FF)[:, tokens - (CONV_W - 1):]

    shp_a = lambda a, b_, t_: a.reshape(1, b_, t_, N_HEADS_A, HEAD_DIM)
    shp_d = lambda a, b_, t_: a.reshape(1, b_, t_, N_HEADS_B, 2 * HEAD_DIM)
    return (yp.reshape(batch, seq, D_MODEL), ys.reshape(n_seq, tokens, D_MODEL),
            shp_a(ka_f, batch, seq), shp_a(va_f, batch, seq),
            shp_d(kd_f, batch, seq), shp_d(vd_f, batch, seq), conv_p[None],
            shp_a(ka_sf, n_seq, tokens), shp_a(va_sf, n_seq, tokens),
            shp_d(kd_sf, n_seq, tokens), shp_d(vd_sf, n_seq, tokens), conv_s[None])
```

```python
import functools
import math

import jax
import jax.numpy as jnp
from jax import lax
from jax.experimental import pallas as pl
from jax.experimental.pallas import tpu as pltpu

F32 = jnp.float32
BF16 = jnp.bfloat16

D_MODEL = 1024
HEAD_DIM = 64
N_HEADS_A = D_MODEL // 128
N_HEADS_B = D_MODEL // 256
WIDTH_A = N_HEADS_A * HEAD_DIM
WIDTH_B = N_HEADS_B * 2 * HEAD_DIM
N_IN = 3 * WIDTH_A + 3 * WIDTH_B
MOBA_BLOCK = 256
MOBA_TOPK = 3
NUM_BUCKETS = 32
MAX_DISTANCE = 128
D_FF = ((8 * D_MODEL // 3 + 127) // 128) * 128
CONV_W = 3
EPS = 1e-6
PAGE_SIZE = 128
SCALE = HEAD_DIM ** -0.5

LANES = 128
SUBLANES = 8
VMEM_LIMIT_BYTES = 56 * 1024 * 1024

ATT_TILE = MOBA_BLOCK
FF_CHUNK = 256
PAGES_PER_STEP = 16
NEG_BIG = -1e30


def _dot(a, b):
    return jnp.dot(a, b, preferred_element_type=F32)


def _dot_nt(a, b):
    return lax.dot_general(a, b, (((1,), (1,)), ((), ())), preferred_element_type=F32)


def _rmsnorm(x, g):
    return (x * lax.rsqrt(jnp.mean(x * x, axis=-1, keepdims=True) + EPS)) * g


def _lane_fold(op, x):
    out = x[:, :LANES]
    for c in range(1, x.shape[1] // LANES):
        out = op(out, x[:, c * LANES:(c + 1) * LANES])
    return out


def _t5_bucket(dist):
    n = jnp.maximum(dist, 0)
    max_exact = NUM_BUCKETS // 2
    nf = jnp.maximum(n, 1).astype(F32)
    large = max_exact + (jnp.log(nf / max_exact) / math.log(MAX_DISTANCE / max_exact)
                         * (NUM_BUCKETS - max_exact)).astype(jnp.int32)
    large = jnp.minimum(large, NUM_BUCKETS - 1)
    return jnp.where(n < max_exact, n, large)


def _bias_of(bias_t, dist):
    onehot = jax.nn.one_hot(_t5_bucket(dist), NUM_BUCKETS, dtype=F32)
    return jnp.einsum('hb,...b->h...', bias_t.astype(F32), onehot, precision=lax.Precision.HIGHEST)


def _lambda(lam_ref, lam_init):
    lv = lam_ref[...]
    a = jnp.sum(lv[0:1] * lv[1:2], axis=-1, keepdims=True)
    b = jnp.sum(lv[2:3] * lv[3:4], axis=-1, keepdims=True)
    return jnp.exp(a) - jnp.exp(b) + lam_init


def _proj_kernel(x_ref, g_ref, w_ref,
                 qa_ref, ka_ref, va_ref, qd_ref, kd_ref, vd_ref,
                 kaf_ref, vaf_ref, kdf_ref, vdf_ref):
    h = _rmsnorm(x_ref[...], g_ref[...]).astype(BF16)
    outs = ((qa_ref, None, SCALE), (ka_ref, kaf_ref, None), (va_ref, vaf_ref, None),
            (qd_ref, None, SCALE), (kd_ref, kdf_ref, None), (vd_ref, vdf_ref, None))
    for c, (b_ref, f_ref, scale) in enumerate(outs):
        u = _dot(h, w_ref[:, c * WIDTH_A:(c + 1) * WIDTH_A])
        if f_ref is not None:
            f_ref[...] = u
        if scale is not None:
            u = u * scale
        b_ref[...] = u.astype(BF16)


def _proj(x, g, w_bf16, tm):
    n = x.shape[0]
    row = lambda i: (i, 0)
    const = lambda i: (0, 0)
    blk = pl.BlockSpec((tm, WIDTH_A), row)
    return pl.pallas_call(
        _proj_kernel,
        grid=(n // tm,),
        in_specs=[pl.BlockSpec((tm, D_MODEL), row),
                  pl.BlockSpec((1, D_MODEL), const),
                  pl.BlockSpec((D_MODEL, N_IN), const)],
        out_specs=[blk] * 10,
        out_shape=[jax.ShapeDtypeStruct((n, WIDTH_A), BF16)] * 6
                  + [jax.ShapeDtypeStruct((n, WIDTH_A), F32)] * 4,
        compiler_params=pltpu.CompilerParams(
            dimension_semantics=("arbitrary",), vmem_limit_bytes=VMEM_LIMIT_BYTES),
        name="proj",
    )(x, g, w_bf16)


def _attend_tiles(q_ops, make_k, v_ref, bias_ref, bidx, s_refs, mb_refs, qi):
    t = ATT_TILE
    n = len(q_ops)
    trips = (qi >> 1) + 1

    def bias_pair(b, jj):
        j0 = 2 * jj
        i0 = jnp.minimum(qi - j0, 2)
        i1 = jnp.where(j0 + 1 > qi, 3, jnp.minimum(qi - j0 - 1, 2))
        return jnp.concatenate([bias_ref[b, i0], bias_ref[b, i1]], axis=1)

    def scores(jj, mxs):
        kk = make_k(jj)
        biases = {b: bias_pair(b, jj) for b in set(bidx)}
        out = []
        for i in range(n):
            s = _dot_nt(q_ops[i], kk) + biases[bidx[i]]
            s_refs[i][jj] = s
            out.append(jnp.maximum(mxs[i], _lane_fold(jnp.maximum, s)))
        return tuple(out)

    neg = jnp.full((t, LANES), -jnp.inf, F32)
    mxs = lax.fori_loop(0, trips, scores, (neg,) * n)
    for i in range(n):
        mb_refs[i][...] = jnp.broadcast_to(jnp.max(mxs[i], axis=1, keepdims=True), (t, LANES))

    def weigh(jj, carry):
        vv = v_ref[pl.ds(pl.multiple_of(jj * 2 * t, 2 * t), 2 * t), :]
        out = []
        for i in range(n):
            acc, l = carry[i]
            mb = mb_refs[i][...]
            p = jnp.exp(s_refs[i][jj] - jnp.concatenate([mb] * 4, axis=1))
            out.append((acc + _dot(p.astype(BF16), vv), l + _lane_fold(jnp.add, p)))
        return tuple(out)

    zero = jnp.zeros((t, LANES), F32)
    res = lax.fori_loop(0, trips, weigh, ((zero, zero),) * n)
    return [(acc, jnp.sum(l, axis=1, keepdims=True)) for acc, l in res]


def _moba_prompt_kernel(q_ref, k_ref, v_ref, bias_ref, o_ref, s0_ref, s1_ref, mb0_ref, mb1_ref,
                        km_ref):
    t = ATT_TILE
    nb = km_ref.shape[0]
    qi = pl.program_id(2)

    @pl.when(qi == 0)
    def _():
        for j in range(nb):
            km_ref[j:j + 1, :] = jnp.sum(k_ref[j * t:(j + 1) * t, :].astype(F32),
                                         axis=0, keepdims=True) * (1.0 / t)

    q = q_ref[...].astype(F32)
    km = km_ref[...]
    km_hi = km.astype(BF16)
    km_lo = (km - km_hi.astype(F32)).astype(BF16)
    lane = lax.broadcasted_iota(jnp.int32, (t, LANES), 1)
    blk = lax.broadcasted_iota(jnp.int32, (nb, t), 0)

    q_ops = []
    for e in range(2):
        qe = jnp.where((lane >= HEAD_DIM * e) & (lane < HEAD_DIM * (e + 1)), q, 0.0).astype(BF16)
        gt = _dot_nt(km_hi, qe) + _dot_nt(km_lo, qe)
        rank = jnp.zeros((nb, t), jnp.int32)
        for m in range(nb):
            gm = gt[m:m + 1, :]
            ahead = (gm > gt) | ((gm == gt) & (m < blk))
            rank = rank + jnp.where(m < qi, jnp.where(ahead, 1, 0), 0)
        keep = (blk < qi) & (rank < MOBA_TOPK) & (jnp.abs(gt) < jnp.inf)
        pen = jnp.where(keep | (blk == qi), 0.0, NEG_BIG)
        pen = jnp.concatenate([pen, jnp.zeros((LANES - nb, t), F32)], axis=0)
        q_ops.append(jnp.concatenate([qe, pen.T.astype(BF16)], axis=1))

    lane2 = lax.broadcasted_iota(jnp.int32, (2 * t, LANES), 1)
    row2 = lax.broadcasted_iota(jnp.int32, (2 * t, LANES), 0)

    def make_k(jj):
        kk = k_ref[pl.ds(pl.multiple_of(jj * 2 * t, 2 * t), 2 * t), :]
        block = 2 * jj + jnp.where(row2 >= t, 1, 0)
        onehot = jnp.where(lane2 == block, 1.0, 0.0).astype(BF16)
        return jnp.concatenate([kk, onehot], axis=1)

    (a0, l0), (a1, l1) = _attend_tiles(q_ops, make_k, v_ref, bias_ref, (0, 1),
                                       (s0_ref, s1_ref), (mb0_ref, mb1_ref), qi)
    o_ref[...] = jnp.where(lane < HEAD_DIM, a0 / l0, a1 / l1).astype(o_ref.dtype)


def _moba_prompt(qa, ka, va, bias_tiles, batch, seq):
    t = ATT_TILE
    nq = seq // t
    assert nq % 2 == 0
    grid = (batch, N_HEADS_A // 2, nq)
    qmap = lambda b, hp, qi: (b * nq + qi, hp)
    kvmap = lambda b, hp, qi: (b, hp)
    return pl.pallas_call(
        _moba_prompt_kernel,
        grid=grid,
        in_specs=[pl.BlockSpec((t, LANES), qmap),
                  pl.BlockSpec((seq, LANES), kvmap),
                  pl.BlockSpec((seq, LANES), kvmap),
                  pl.BlockSpec((2, 4, t, t), lambda b, hp, qi: (hp, 0, 0, 0))],
        out_specs=pl.BlockSpec((t, LANES), qmap),
        out_shape=jax.ShapeDtypeStruct((batch * seq, WIDTH_A), BF16),
        scratch_shapes=[pltpu.VMEM((nq // 2, t, 2 * t), F32),
                        pltpu.VMEM((nq // 2, t, 2 * t), F32),
                        pltpu.VMEM((t, LANES), F32),
                        pltpu.VMEM((t, LANES), F32),
                        pltpu.VMEM((nq, LANES), F32)],
        compiler_params=pltpu.CompilerParams(
            dimension_semantics=("arbitrary", "arbitrary", "arbitrary"),
            vmem_limit_bytes=VMEM_LIMIT_BYTES),
        name="moba_prompt",
    )(qa, ka, va, bias_tiles)


def _diff_prompt_kernel(q_ref, k_ref, v_ref, bias_ref, lam_ref, gain_ref, o_ref,
                        s0_ref, s1_ref, mb0_ref, mb1_ref, *, lam_init):
    t = ATT_TILE
    qi = pl.program_id(2)
    q = q_ref[...].astype(F32)
    lane = lax.broadcasted_iota(jnp.int32, (t, LANES), 1)

    def make_k(jj):
        return k_ref[pl.ds(pl.multiple_of(jj * 2 * t, 2 * t), 2 * t), :]

    q_ops = [jnp.where((lane >= HEAD_DIM * c) & (lane < HEAD_DIM * (c + 1)), q, 0.0).astype(BF16)
             for c in range(2)]
    (a0, l0), (a1, l1) = _attend_tiles(q_ops, make_k, v_ref, bias_ref, (0, 0),
                                       (s0_ref, s1_ref), (mb0_ref, mb1_ref), qi)
    o = a0 / l0 - _lambda(lam_ref, lam_init) * (a1 / l1)
    o_ref[...] = (_rmsnorm(o, gain_ref[...]) * (1.0 - lam_init)).astype(o_ref.dtype)


def _diff_prompt(qd, kd, vd, bias_tiles, lam_vecs, gain, lam_init, batch, seq):
    t = ATT_TILE
    nq = seq // t
    assert nq % 2 == 0
    grid = (batch, N_HEADS_B, nq)
    qmap = lambda b, h, qi: (b * nq + qi, h)
    kvmap = lambda b, h, qi: (b, h)
    return pl.pallas_call(
        functools.partial(_diff_prompt_kernel, lam_init=lam_init),
        grid=grid,
        in_specs=[pl.BlockSpec((t, LANES), qmap),
                  pl.BlockSpec((seq, LANES), kvmap),
                  pl.BlockSpec((seq, LANES), kvmap),
                  pl.BlockSpec((1, 4, t, t), lambda b, h, qi: (h, 0, 0, 0)),
                  pl.BlockSpec((4, HEAD_DIM), lambda b, h, qi: (0, 0)),
                  pl.BlockSpec((1, 2 * HEAD_DIM), lambda b, h, qi: (0, 0))],
        out_specs=pl.BlockSpec((t, LANES), qmap),
        out_shape=jax.ShapeDtypeStruct((batch * seq, WIDTH_B), BF16),
        scratch_shapes=[pltpu.VMEM((nq // 2, t, 2 * t), F32),
                        pltpu.VMEM((nq // 2, t, 2 * t), F32),
                        pltpu.VMEM((t, LANES), F32),
                        pltpu.VMEM((t, LANES), F32)],
        compiler_params=pltpu.CompilerParams(
            dimension_semantics=("arbitrary", "arbitrary", "arbitrary"),
            vmem_limit_bytes=VMEM_LIMIT_BYTES),
        name="diff_prompt",
    )(qd, kd, vd, bias_tiles, lam_vecs, gain)


def _toeplitz(w, t):
    h = w.shape[0]
    m = jnp.broadcast_to(w[:, None, :], (h, t, 2 * t)).reshape(h, 2 * t * t)
    return m[:, :t * (2 * t - 1)].reshape(h, t, 2 * t - 1)[:, :, :t]


def _bias_tiles(bias_t):
    t = ATT_TILE
    assert t + 1 >= MAX_DISTANCE
    k = jnp.arange(2 * t)
    d = jnp.where(k <= t, -k, 2 * t - k)
    diag = _toeplitz(jnp.where(d >= 0, _bias_of(bias_t, d), -jnp.inf), t)
    sub = _toeplitz(_bias_of(bias_t, d + t), t)
    far = _toeplitz(_bias_of(bias_t, d + 2 * t), t)
    return jnp.stack([diag, sub, far, jnp.full_like(far, -jnp.inf)], axis=1)


def _page_specs(n_chunks, page_shape):
    pps = PAGES_PER_STEP

    def k_map(i):
        return lambda b, ph, j, pt: (pt[b, jnp.where(ph == 0, j, n_chunks - 1) * pps + i], 0, 0)

    def v_map(i):
        return lambda b, ph, j, pt: (pt[b, jnp.where(ph == 0, 0, j) * pps + i], 0, 0)

    return ([pl.BlockSpec(page_shape, k_map(i)) for i in range(pps)]
            + [pl.BlockSpec(page_shape, v_map(i)) for i in range(pps)])


def _moba_sample_kernel(pt_ref, *refs, n_chunks, tokens):
    pps = PAGES_PER_STEP
    del pt_ref
    kt_pages = refs[:pps]
    vt_pages = refs[pps:2 * pps]
    (q_ref, kn_ref, vn_ref, bias_ref, o_ref,
     s_ref, snew_ref, acc_ref, l_ref, m_ref, bmax_ref, gate_ref, sel_ref) = refs[2 * pps:]
    rows = q_ref.shape[1]
    groups = rows // tokens
    ppb = MOBA_BLOCK // PAGE_SIZE
    bps = pps // ppb
    n_blocks = n_chunks * bps
    ph = pl.program_id(1)
    j = pl.program_id(2)
    lane = lax.broadcasted_iota(jnp.int32, (rows, LANES), 1)
    lanef = lane.astype(F32)
    q = q_ref[0]

    @pl.when((ph == 0) & (j == 0))
    def _():
        snew_ref[...] = _dot_nt(q, kn_ref[0]) + bias_ref[2]
        bmax_ref[...] = jnp.full((rows, LANES), -jnp.inf, F32)
        gate_ref[...] = jnp.zeros((rows, LANES), F32)

    @pl.when(ph == 0)
    def _():
        is_last = j == n_chunks - 1
        bmax = bmax_ref[...]
        gate = gate_ref[...]
        for b in range(bps):
            smax = ssum = None
            for pg in range(ppb):
                i = b * ppb + pg
                raw = _dot(q, kt_pages[i][0].astype(BF16))
                bias = bias_ref[0]
                if i == pps - 1:
                    bias = jnp.where(is_last, bias_ref[1], bias)
                s = raw + bias
                s_ref[j, :, i * PAGE_SIZE:(i + 1) * PAGE_SIZE] = s
                smax = s if smax is None else jnp.maximum(smax, s)
                ssum = raw if ssum is None else ssum + raw
            blk = j * bps + b
            bmax = jnp.where(lane == blk, jnp.max(smax, axis=1, keepdims=True), bmax)
            gate = jnp.where(lane == blk, jnp.sum(ssum, axis=1, keepdims=True), gate)
        bmax_ref[...] = bmax
        gate_ref[...] = gate

    @pl.when((ph == 1) & (j == 0))
    def _():
        g = jnp.where(lane < n_blocks, gate_ref[...], -jnp.inf)
        sel = jnp.zeros((rows, LANES), F32)
        for _ in range(MOBA_TOPK):
            top = jnp.max(g, axis=1, keepdims=True)
            first = jnp.min(jnp.where(g == top, lanef, float(LANES)), axis=1, keepdims=True)
            pick = lanef == first
            finite = jnp.where(jnp.abs(top) < jnp.inf, 1.0, 0.0)
            sel = jnp.maximum(sel, jnp.where(pick, finite, 0.0))
            g = jnp.where(pick, -jnp.inf, g)
        sel_ref[...] = sel
        m_sel = jnp.max(jnp.where(sel > 0.5, bmax_ref[...], -jnp.inf), axis=1, keepdims=True)
        m_new = jnp.max(snew_ref[...], axis=1, keepdims=True)
        m_ref[...] = jnp.broadcast_to(jnp.maximum(m_sel, m_new), (rows, LANES))
        acc_ref[...] = jnp.zeros(acc_ref.shape, F32)
        l_ref[...] = jnp.zeros(l_ref.shape, F32)

    @pl.when(ph == 1)
    def _():
        m = m_ref[...]
        selv = sel_ref[...]
        acc = acc_ref[...]
        l = l_ref[...]
        for i in range(pps):
            blk = j * bps + i // ppb
            on = jnp.max(jnp.where(lane == blk, selv, 0.0), axis=1, keepdims=True)
            on = jnp.broadcast_to(on, (rows, LANES)) > 0.5
            s = s_ref[j, :, i * PAGE_SIZE:(i + 1) * PAGE_SIZE]
            p = jnp.exp(jnp.where(on, s - m, -jnp.inf))
            l = l + p
            acc = acc + _dot_nt(p.astype(BF16), vt_pages[i][0].astype(BF16))
        acc_ref[...] = acc
        l_ref[...] = l

    @pl.when((ph == 1) & (j == n_chunks - 1))
    def _():
        p = jnp.exp(snew_ref[...] - m_ref[...])
        acc = acc_ref[...] + _dot(p.astype(BF16), vn_ref[0])
        l = jnp.sum(l_ref[...] + p, axis=1, keepdims=True)
        o = acc / l
        grp = lax.broadcasted_iota(jnp.int32, (groups, WIDTH_A), 0)
        col = lax.broadcasted_iota(jnp.int32, (groups, WIDTH_A), 1)
        head_shift = HEAD_DIM.bit_length() - 1
        w = jnp.where((col >> head_shift) == grp, 1.0, 0.0)
        tok = jnp.sum(o.reshape(tokens, groups, WIDTH_A) * w[None], axis=1)
        o_ref[0] = tok.astype(o_ref.dtype)


def _moba_sample(page_table, pool_kt, pool_vt, q_bd, k_new, v_new, bias_rows, *, tokens):
    pps = PAGES_PER_STEP
    n_seq, n_pages = page_table.shape
    n_chunks = n_pages // pps
    rows = q_bd.shape[1]
    seq_map = lambda b, ph, j, pt: (b, 0, 0)
    in_specs = (_page_specs(n_chunks, (1, WIDTH_A, PAGE_SIZE))
                + [pl.BlockSpec((1, rows, WIDTH_A), seq_map),
                   pl.BlockSpec((1, LANES, WIDTH_A), seq_map),
                   pl.BlockSpec((1, LANES, WIDTH_A), seq_map),
                   pl.BlockSpec((3, rows, LANES), lambda b, ph, j, pt: (0, 0, 0))])
    vec = pltpu.VMEM((rows, LANES), F32)
    grid_spec = pltpu.PrefetchScalarGridSpec(
        num_scalar_prefetch=1,
        grid=(n_seq, 2, n_chunks),
        in_specs=in_specs,
        out_specs=pl.BlockSpec((1, tokens, WIDTH_A), seq_map),
        scratch_shapes=[pltpu.VMEM((n_chunks, rows, pps * PAGE_SIZE), F32),
                        vec,
                        pltpu.VMEM((rows, WIDTH_A), F32),
                        vec, vec, vec, vec, vec])
    return pl.pallas_call(
        functools.partial(_moba_sample_kernel, n_chunks=n_chunks, tokens=tokens),
        grid_spec=grid_spec,
        out_shape=jax.ShapeDtypeStruct((n_seq, tokens, WIDTH_A), BF16),
        compiler_params=pltpu.CompilerParams(
            dimension_semantics=("arbitrary", "arbitrary", "arbitrary"),
            vmem_limit_bytes=VMEM_LIMIT_BYTES),
        name="moba_sample",
    )(page_table, *([pool_kt] * pps), *([pool_vt] * pps), q_bd, k_new, v_new, bias_rows)


def _diff_sample_kernel(pt_ref, *refs, n_chunks, tokens, lam_init):
    pps = PAGES_PER_STEP
    del pt_ref
    k_pages = refs[:pps]
    v_pages = refs[pps:2 * pps]
    (q_ref, kn_ref, vn_ref, bias_ref, bnew_ref, lam_ref, gain_ref, o_ref,
     s_ref, snew_ref, acc_ref, l_ref, m_ref) = refs[2 * pps:]
    rows = q_ref.shape[1]
    pw = k_pages[0].shape[1]
    ph = pl.program_id(1)
    j = pl.program_id(2)
    q = q_ref[0]

    @pl.when((ph == 0) & (j == 0))
    def _():
        snew_ref[...] = _dot_nt(q, kn_ref[0]) + bnew_ref[...]
        m_ref[...] = jnp.full((rows, LANES), -jnp.inf, F32)

    @pl.when(ph == 0)
    def _():
        is_last = j == n_chunks - 1
        mx = m_ref[...]
        for i in range(pps):
            bias = bias_ref[0]
            if i == pps - 1:
                bias = jnp.where(is_last, bias_ref[1], bias)
            s = _dot_nt(q, k_pages[i][0].astype(BF16)) + bias
            s_ref[j, :, i * pw:(i + 1) * pw] = s
            mx = jnp.maximum(mx, _lane_fold(jnp.maximum, s))
        m_ref[...] = mx

    @pl.when((ph == 1) & (j == 0))
    def _():
        m = jnp.maximum(jnp.max(m_ref[...], axis=1, keepdims=True),
                        jnp.max(snew_ref[...], axis=1, keepdims=True))
        m_ref[...] = jnp.broadcast_to(m, (rows, LANES))
        acc_ref[...] = jnp.zeros(acc_ref.shape, F32)
        l_ref[...] = jnp.zeros(l_ref.shape, F32)

    @pl.when(ph == 1)
    def _():
        m = m_ref[...]
        mw = jnp.concatenate([m] * (pw // LANES), axis=1)
        acc = acc_ref[...]
        l = l_ref[...]
        for i in range(pps):
            p = jnp.exp(s_ref[j, :, i * pw:(i + 1) * pw] - mw)
            l = l + _lane_fold(jnp.add, p)
            acc = acc + _dot(p.astype(BF16), v_pages[i][0].astype(BF16))
        acc_ref[...] = acc
        l_ref[...] = l

    @pl.when((ph == 1) & (j == n_chunks - 1))
    def _():
        p = jnp.exp(snew_ref[...] - m_ref[...])
        acc = acc_ref[...] + _dot(p.astype(BF16), vn_ref[0])
        l = jnp.sum(l_ref[...] + p, axis=1, keepdims=True)
        o = (acc / l).reshape(tokens, rows // tokens, LANES)
        lam = _lambda(lam_ref, lam_init)
        sub = lax.broadcasted_iota(jnp.int32, (rows // tokens, LANES), 0)
        gain = gain_ref[...]
        parts = []
        for h in range(N_HEADS_B):
            w = jnp.where(sub == 2 * h, 1.0, jnp.where(sub == 2 * h + 1, -lam, 0.0))
            parts.append(_rmsnorm(jnp.sum(o * w[None], axis=1), gain) * (1.0 - lam_init))
        o_ref[0] = jnp.concatenate(parts, axis=1).astype(o_ref.dtype)


def _diff_sample(page_table, pool_k, pool_v, q2, k_new, v_new, bias_rows, bias_new, lam_vecs, gain,
                 *, tokens, lam_init):
    pps = PAGES_PER_STEP
    n_seq, n_pages = page_table.shape
    n_chunks = n_pages // pps
    rows = q2.shape[1]
    pw = pool_k.shape[1]
    seq_map = lambda b, ph, j, pt: (b, 0, 0)
    const2 = lambda b, ph, j, pt: (0, 0)
    in_specs = (_page_specs(n_chunks, (1, pw, LANES))
                + [pl.BlockSpec((1, rows, LANES), seq_map),
                   pl.BlockSpec((1, LANES, LANES), seq_map),
                   pl.BlockSpec((1, LANES, LANES), seq_map),
                   pl.BlockSpec((2, rows, pw), lambda b, ph, j, pt: (0, 0, 0)),
                   pl.BlockSpec((rows, LANES), const2),
                   pl.BlockSpec((4, HEAD_DIM), const2),
                   pl.BlockSpec((1, 2 * HEAD_DIM), const2)])
    vec = pltpu.VMEM((rows, LANES), F32)
    grid_spec = pltpu.PrefetchScalarGridSpec(
        num_scalar_prefetch=1,
        grid=(n_seq, 2, n_chunks),
        in_specs=in_specs,
        out_specs=pl.BlockSpec((1, tokens, WIDTH_B), seq_map),
        scratch_shapes=[pltpu.VMEM((n_chunks, rows, pps * pw), F32),
                        vec, vec, vec, vec])
    return pl.pallas_call(
        functools.partial(_diff_sample_kernel, n_chunks=n_chunks, tokens=tokens, lam_init=lam_init),
        grid_spec=grid_spec,
        out_shape=jax.ShapeDtypeStruct((n_seq, tokens, WIDTH_B), BF16),
        compiler_params=pltpu.CompilerParams(
            dimension_semantics=("arbitrary", "arbitrary", "arbitrary"),
            vmem_limit_bytes=VMEM_LIMIT_BYTES),
        name="diff_sample",
    )(page_table, *([pool_k] * pps), *([pool_v] * pps), q2, k_new, v_new, bias_rows, bias_new,
      lam_vecs, gain)


def _block_diag(q, n_seq, tokens, groups, keep):
    w = q.shape[1] // groups
    q4 = q.reshape(n_seq, tokens, groups, 1, w)
    eye = (jnp.arange(groups)[:, None] % keep == jnp.arange(keep)[None, :]).astype(q.dtype)
    return (q4 * eye.reshape(1, 1, groups, keep, 1)).reshape(n_seq, tokens * groups, keep * w)


def _pad_rows(x, n_seq):
    x = x.reshape(n_seq, -1, x.shape[-1])
    return jnp.pad(x, ((0, 0), (0, LANES - x.shape[1]), (0, 0)))


def _sample_bias(bias_t, tokens, heads_per_group):
    assert PAGE_SIZE + 1 >= MAX_DISTANCE
    tok = jnp.arange(tokens)[:, None]
    col = jnp.arange(LANES)[None, :]
    far = _bias_of(bias_t, jnp.broadcast_to(2 * PAGE_SIZE, (tokens, LANES)))
    last = _bias_of(bias_t, PAGE_SIZE + tok - col)
    new = jnp.where((col <= tok) & (col < tokens), _bias_of(bias_t, tok - col), -jnp.inf)

    def rows(x):
        x = jnp.repeat(x.transpose(1, 0, 2), heads_per_group, axis=1)
        return x.reshape(-1, LANES)

    return rows(far), rows(last), rows(new)


def _spread_heads(x, n_heads, groups):
    r, k = x.shape
    row_head = (jnp.arange(r) % groups) // (groups // n_heads)
    own = row_head[:, None, None] == jnp.arange(n_heads)[None, None, :]
    return jnp.where(own, x[:, :, None], -jnp.inf).reshape(r, k * n_heads)


def _merge_kernel(x_ref, oa_ref, ob_ref, g_ref, wg_ref, woa_ref, wob_ref, wo_ref, o_ref):
    x = x_ref[...]
    h = _rmsnorm(x, g_ref[...]).astype(BF16)
    ga = 1.0 / (1.0 + jnp.exp(-_dot(h, wg_ref[:, :D_MODEL])))
    m = ga * _dot(oa_ref[...], woa_ref[...])
    gb = 1.0 / (1.0 + jnp.exp(-_dot(h, wg_ref[:, D_MODEL:])))
    m = m + gb * _dot(ob_ref[...], wob_ref[...])
    o_ref[...] = x + _dot(m.astype(BF16), wo_ref[...])


def _merge(x, oa, ob, g, wg, woa, wob, wo, tm):
    n = x.shape[0]
    row = lambda i: (i, 0)
    const = lambda i: (0, 0)
    return pl.pallas_call(
        _merge_kernel,
        grid=(n // tm,),
        in_specs=[pl.BlockSpec((tm, D_MODEL), row),
                  pl.BlockSpec((tm, WIDTH_A), row),
                  pl.BlockSpec((tm, WIDTH_B), row),
                  pl.BlockSpec((1, D_MODEL), const),
                  pl.BlockSpec((D_MODEL, 2 * D_MODEL), const),
                  pl.BlockSpec((WIDTH_A, D_MODEL), const),
                  pl.BlockSpec((WIDTH_B, D_MODEL), const),
                  pl.BlockSpec((D_MODEL, D_MODEL), const)],
        out_specs=pl.BlockSpec((tm, D_MODEL), row),
        out_shape=jax.ShapeDtypeStruct((n, D_MODEL), F32),
        compiler_params=pltpu.CompilerParams(
            dimension_semantics=("arbitrary",), vmem_limit_bytes=VMEM_LIMIT_BYTES),
        name="merge",
    )(x, oa, ob, g, wg, woa, wob, wo)


def _ffn_kernel(*refs, tm, seq_len, has_prev):
    if has_prev:
        (x_ref, g_ref, wup_ref, cw_ref, cb_ref, wdn_ref, gfin_ref, e1_ref, e2_ref,
         y_ref, u_ref, act_ref) = refs
    else:
        (x_ref, g_ref, wup_ref, cw_ref, cb_ref, wdn_ref, gfin_ref,
         y_ref, u_ref, act_ref, carry_ref) = refs
    i = pl.program_id(0)
    x = x_ref[...]
    h = _rmsnorm(x, g_ref[...]).astype(BF16)
    row = lax.broadcasted_iota(jnp.int32, (tm, FF_CHUNK), 0)
    row8 = lax.broadcasted_iota(jnp.int32, (SUBLANES, FF_CHUNK), 0)
    if has_prev:
        pos = row & (seq_len - 1)
    else:
        @pl.when((i * tm) % seq_len == 0)
        def _():
            carry_ref[...] = jnp.zeros(carry_ref.shape, F32)

    for c in range(D_FF // FF_CHUNK):
        halves = []
        for part in range(2):
            cols = slice(part * D_FF + c * FF_CHUNK, part * D_FF + (c + 1) * FF_CHUNK)
            u = _dot(h, wup_ref[:, cols])
            um1 = pltpu.roll(u, 1, 0)
            um2 = pltpu.roll(u, 2, 0)
            if has_prev:
                um1 = jnp.where(pos == 0, e1_ref[:, cols], um1)
                um2 = jnp.where(pos < 2, e2_ref[:, cols], um2)
                u_ref[:, cols] = u
            else:
                prev = carry_ref[:, cols]
                top1 = jnp.where(row8 == 0, pltpu.roll(prev, 1, 0), um1[:SUBLANES])
                top2 = jnp.where(row8 < 2, pltpu.roll(prev, 2, 0), um2[:SUBLANES])
                um1 = jnp.concatenate([top1, um1[SUBLANES:]], axis=0)
                um2 = jnp.concatenate([top2, um2[SUBLANES:]], axis=0)
                carry_ref[:, cols] = u[tm - SUBLANES:]
                u_ref[:, cols] = u[tm - SUBLANES:]
            cw = cw_ref[:, cols]
            halves.append(((cb_ref[:, cols] + cw[0:1] * um2) + cw[1:2] * um1) + cw[2:3] * u)
        gate, val = halves
        act = (gate * (1.0 / (1.0 + jnp.exp(-gate)))) * val
        act_ref[:, c * FF_CHUNK:(c + 1) * FF_CHUNK] = act.astype(BF16)

    x3 = x + _dot(act_ref[...], wdn_ref[...])
    y_ref[...] = _rmsnorm(x3, gfin_ref[...])


def _ffn(x, g, wup, cw, cb, wdn, gfin, prev, tm, seq_len):
    n = x.shape[0]
    has_prev = prev is not None
    row = lambda i: (i, 0)
    const = lambda i: (0, 0)
    in_specs = [pl.BlockSpec((tm, D_MODEL), row),
                pl.BlockSpec((1, D_MODEL), const),
                pl.BlockSpec((D_MODEL, 2 * D_FF), const),
                pl.BlockSpec((CONV_W, 2 * D_FF), const),
                pl.BlockSpec((1, 2 * D_FF), const),
                pl.BlockSpec((D_FF, D_MODEL), const),
                pl.BlockSpec((1, D_MODEL), const)]
    scratch = [pltpu.VMEM((tm, D_FF), BF16)]
    if has_prev:
        assert n == tm and tm % seq_len == 0 and seq_len & (seq_len - 1) == 0
        in_specs += [pl.BlockSpec((tm, 2 * D_FF), row)] * 2
        u_shape, u_spec = (n, 2 * D_FF), pl.BlockSpec((tm, 2 * D_FF), row)
        args = (x, g, wup, cw, cb, wdn, gfin) + tuple(prev)
    else:
        assert seq_len % tm == 0
        tiles_per_seq = seq_len // tm
        u_shape = (n // seq_len * SUBLANES, 2 * D_FF)
        u_spec = pl.BlockSpec((SUBLANES, 2 * D_FF), lambda i: (i // tiles_per_seq, 0))
        scratch.append(pltpu.VMEM((SUBLANES, 2 * D_FF), F32))
        args = (x, g, wup, cw, cb, wdn, gfin)
    return pl.pallas_call(
        functools.partial(_ffn_kernel, tm=tm, seq_len=seq_len, has_prev=has_prev),
        grid=(n // tm,),
        in_specs=in_specs,
        out_specs=[pl.BlockSpec((tm, D_MODEL), row), u_spec],
        out_shape=[jax.ShapeDtypeStruct((n, D_MODEL), F32), jax.ShapeDtypeStruct(u_shape, F32)],
        scratch_shapes=scratch,
        compiler_params=pltpu.CompilerParams(
            dimension_semantics=("arbitrary",), vmem_limit_bytes=VMEM_LIMIT_BYTES),
        name="ffn_sample" if has_prev else "ffn_prompt",
    )(*args)


def kernel(x_prompt, x_sample, cache_moba_k, cache_moba_v, cache_diff_k, cache_diff_v, state_conv, page_table, rel_bias, norm_attn, w_in, w_gate, w_out_a, w_out_b, w_out, lambda_q1, lambda_k1, lambda_q2, lambda_k2, diff_norm, norm_ffn, w_up, conv_w, conv_b, w_down, norm_final):
    batch, seq, _ = x_prompt.shape
    n_seq, tokens, _ = x_sample.shape
    depth = w_in.shape[0]
    n_phys = cache_moba_k.shape[1]
    n_pages = page_table.shape[1]
    assert depth == 1 and seq % (2 * ATT_TILE) == 0
    assert (n_pages * PAGE_SIZE) % MOBA_BLOCK == 0 and n_pages % PAGES_PER_STEP == 0
    assert n_pages * PAGE_SIZE // MOBA_BLOCK <= LANES and CONV_W - 1 <= tokens <= LANES // N_HEADS_B
    l = 0
    lam_init = 0.8 - 0.6 * math.exp(-0.3 * l)

    bias_a = rel_bias[:, :N_HEADS_A].T
    bias_d = rel_bias[:, N_HEADS_A:].T
    row = lambda v: v.reshape(1, -1)
    w_in_b = w_in[l].astype(BF16)
    w_gate_b = w_gate[l].astype(BF16)
    w_oa_b = w_out_a[l].astype(BF16)
    w_ob_b = w_out_b[l].astype(BF16)
    w_o_b = w_out[l].astype(BF16)
    w_up_b = w_up[l].astype(BF16)
    w_dn_b = w_down[l].astype(BF16)
    lam_vecs = jnp.stack([lambda_q1[l], lambda_k1[l], lambda_q2[l], lambda_k2[l]]).astype(F32)
    gain_d = row(diff_norm[l])
    g_attn, g_ffn, g_fin = row(norm_attn[l]), row(norm_ffn[l]), row(norm_final)
    cb = row(conv_b[l])

    xp = x_prompt.reshape(batch * seq, D_MODEL)
    qa, ka, va, qd, kd, vd, ka_f, va_f, kd_f, vd_f = _proj(xp, g_attn, w_in_b, 512)
    oa = _moba_prompt(qa, ka, va, _bias_tiles(bias_a), batch, seq)
    ob = _diff_prompt(qd, kd, vd, _bias_tiles(bias_d), lam_vecs, gain_d, lam_init, batch, seq)
    x2 = _merge(xp, oa, ob, g_attn, w_gate_b, w_oa_b, w_ob_b, w_o_b, 512)
    yp, tail_p = _ffn(x2, g_ffn, w_up_b, conv_w[l], cb, w_dn_b, g_fin, None, 512, seq)
    conv_p = tail_p.reshape(batch, SUBLANES, 2 * D_FF)[:, SUBLANES - (CONV_W - 1):]

    n_s = n_seq * tokens
    xs = x_sample.reshape(n_s, D_MODEL)
    qa_s, ka_s, va_s, qd_s, kd_s, vd_s, ka_sf, va_sf, kd_sf, vd_sf = _proj(xs, g_attn, w_in_b, n_s)

    pool_t = lambda c: jnp.transpose(c[l], (0, 2, 3, 1)).reshape(n_phys, WIDTH_A, PAGE_SIZE)
    pool_r = lambda c: c[l].reshape(n_phys, PAGE_SIZE * N_HEADS_B, 2 * HEAD_DIM)
    far_a, last_a, new_a = _sample_bias(bias_a, tokens, 1)
    oa_s = _moba_sample(page_table, pool_t(cache_moba_k), pool_t(cache_moba_v),
                        _block_diag(qa_s, n_seq, tokens, N_HEADS_A, N_HEADS_A),
                        _pad_rows(ka_s, n_seq), _pad_rows(va_s, n_seq),
                        jnp.stack([far_a, last_a, new_a]), tokens=tokens)
    far_d, last_d, new_d = _sample_bias(bias_d, tokens, 2)
    groups_d = 2 * N_HEADS_B
    spread = lambda x: _spread_heads(x, N_HEADS_B, groups_d)
    per_head = lambda x: x.reshape(n_s * N_HEADS_B, 2 * HEAD_DIM)
    oa_d = _diff_sample(page_table, pool_r(cache_diff_k), pool_r(cache_diff_v),
                        _block_diag(qd_s, n_seq, tokens, groups_d, 2),
                        _pad_rows(per_head(kd_s), n_seq), _pad_rows(per_head(vd_s), n_seq),
                        jnp.stack([spread(far_d), spread(last_d)]),
                        spread(new_d[:, :LANES // N_HEADS_B]),
                        lam_vecs, gain_d, tokens=tokens, lam_init=lam_init)
    x2s = _merge(xs, oa_s.reshape(n_s, WIDTH_A), oa_d.reshape(n_s, WIDTH_B), g_attn,
                 w_gate_b, w_oa_b, w_ob_b, w_o_b, n_s)
    st = state_conv[l]
    zero = jnp.zeros((n_seq, 1, 2 * D_FF), F32)
    e1 = jnp.concatenate([st[:, 1:2]] + [zero] * (tokens - 1), axis=1).reshape(n_s, 2 * D_FF)
    e2 = jnp.concatenate([st[:, 0:2]] + [zero] * (tokens - 2), axis=1).reshape(n_s, 2 * D_FF)
    ys, u_s = _ffn(x2s, g_ffn, w_up_b, conv_w[l], cb, w_dn_b, g_fin, (e1, e2), n_s, tokens)
    conv_s = u_s.reshape(n_seq, tokens, 2 * D_FF)[:, tokens - (CONV_W - 1):]

    shp_a = lambda a, b_, t_: a.reshape(1, b_, t_, N_HEADS_A, HEAD_DIM)
    shp_d = lambda a, b_, t_: a.reshape(1, b_, t_, N_HEADS_B, 2 * HEAD_DIM)
    return (yp.reshape(batch, seq, D_MODEL), ys.reshape(n_seq, tokens, D_MODEL),
            shp_a(ka_f, batch, seq), shp_a(va_f, batch, seq),
            shp_d(kd_f, batch, seq), shp_d(vd_f, batch, seq), conv_p[None],
            shp_a(ka_sf, n_seq, tokens), shp_a(va_sf, n_seq, tokens),
            shp_d(kd_sf, n_seq, tokens), shp_d(vd_sf, n_seq, tokens), conv_s[None])
```

```python
import functools
import math

import jax
import jax.numpy as jnp
from jax import lax
from jax.experimental import pallas as pl
from jax.experimental.pallas import tpu as pltpu

F32 = jnp.float32
BF16 = jnp.bfloat16

D_MODEL = 1024
HEAD_DIM = 64
N_HEADS_A = D_MODEL // 128
N_HEADS_B = D_MODEL // 256
WIDTH_A = N_HEADS_A * HEAD_DIM
WIDTH_B = N_HEADS_B * 2 * HEAD_DIM
N_IN = 3 * WIDTH_A + 3 * WIDTH_B
MOBA_BLOCK = 256
MOBA_TOPK = 3
NUM_BUCKETS = 32
MAX_DISTANCE = 128
D_FF = ((8 * D_MODEL // 3 + 127) // 128) * 128
CONV_W = 3
EPS = 1e-6
PAGE_SIZE = 128
SCALE = HEAD_DIM ** -0.5

LANES = 128
SUBLANES = 8
VMEM_LIMIT_BYTES = 56 * 1024 * 1024

ATT_TILE = MOBA_BLOCK
Q_TILE = 2 * ATT_TILE
FF_CHUNK = 256
PAGES_PER_STEP = 16
NEG_BIG = -1e30


def _dot(a, b):
    return jnp.dot(a, b, preferred_element_type=F32)


def _dot_nt(a, b):
    return lax.dot_general(a, b, (((1,), (1,)), ((), ())), preferred_element_type=F32)


def _rmsnorm(x, g):
    return (x * lax.rsqrt(jnp.mean(x * x, axis=-1, keepdims=True) + EPS)) * g


def _lane_fold(op, x):
    out = x[:, :LANES]
    for c in range(1, x.shape[1] // LANES):
        out = op(out, x[:, c * LANES:(c + 1) * LANES])
    return out


def _t5_bucket(dist):
    n = jnp.maximum(dist, 0)
    max_exact = NUM_BUCKETS // 2
    nf = jnp.maximum(n, 1).astype(F32)
    large = max_exact + (jnp.log(nf / max_exact) / math.log(MAX_DISTANCE / max_exact)
                         * (NUM_BUCKETS - max_exact)).astype(jnp.int32)
    large = jnp.minimum(large, NUM_BUCKETS - 1)
    return jnp.where(n < max_exact, n, large)


def _bias_of(bias_t, dist):
    onehot = jax.nn.one_hot(_t5_bucket(dist), NUM_BUCKETS, dtype=F32)
    return jnp.einsum('hb,...b->h...', bias_t.astype(F32), onehot, precision=lax.Precision.HIGHEST)


def _lambda(lam_ref, lam_init):
    lv = lam_ref[...]
    a = jnp.sum(lv[0:1] * lv[1:2], axis=-1, keepdims=True)
    b = jnp.sum(lv[2:3] * lv[3:4], axis=-1, keepdims=True)
    return jnp.exp(a) - jnp.exp(b) + lam_init


def _proj_kernel(x_ref, g_ref, w_ref,
                 qa_ref, ka_ref, va_ref, qd_ref, kd_ref, vd_ref,
                 kaf_ref, vaf_ref, kdf_ref, vdf_ref):
    h = _rmsnorm(x_ref[...], g_ref[...]).astype(BF16)
    outs = ((qa_ref, None, SCALE), (ka_ref, kaf_ref, None), (va_ref, vaf_ref, None),
            (qd_ref, None, SCALE), (kd_ref, kdf_ref, None), (vd_ref, vdf_ref, None))
    for c, (b_ref, f_ref, scale) in enumerate(outs):
        u = _dot(h, w_ref[:, c * WIDTH_A:(c + 1) * WIDTH_A])
        if f_ref is not None:
            f_ref[...] = u
        if scale is not None:
            u = u * scale
        b_ref[...] = u.astype(BF16)


def _proj(x, g, w_bf16, tm):
    n = x.shape[0]
    row = lambda i: (i, 0)
    const = lambda i: (0, 0)
    blk = pl.BlockSpec((tm, WIDTH_A), row)
    return pl.pallas_call(
        _proj_kernel,
        grid=(n // tm,),
        in_specs=[pl.BlockSpec((tm, D_MODEL), row),
                  pl.BlockSpec((1, D_MODEL), const),
                  pl.BlockSpec((D_MODEL, N_IN), const)],
        out_specs=[blk] * 10,
        out_shape=[jax.ShapeDtypeStruct((n, WIDTH_A), BF16)] * 6
                  + [jax.ShapeDtypeStruct((n, WIDTH_A), F32)] * 4,
        compiler_params=pltpu.CompilerParams(
            dimension_semantics=("arbitrary",), vmem_limit_bytes=VMEM_LIMIT_BYTES),
        name="proj",
    )(x, g, w_bf16)


def _attend_static(q_ops, k_ref, v_ref, bias_ref, bidx, s_refs, p_refs, mb_ref, qi):
    t = ATT_TILE
    tq = Q_TILE
    halves = tq // t
    trips = qi + 1
    outs = []
    for i, q_op in enumerate(q_ops):
        s_ref, p_ref = s_refs[i], p_refs[i]
        mx = [None] * halves
        for jj in range(trips):
            s = _dot_nt(q_op, k_ref[jj * tq:(jj + 1) * tq, :])
            for hq in range(halves):
                for hk in range(halves):
                    dist = (halves * qi + hq) - (halves * jj + hk)
                    if dist < 0:
                        continue
                    blk = (s[hq * t:(hq + 1) * t, hk * t:(hk + 1) * t]
                           + bias_ref[bidx[i], min(dist, 2)])
                    s_ref[jj, hq * t:(hq + 1) * t, hk * t:(hk + 1) * t] = blk
                    f = _lane_fold(jnp.maximum, blk)
                    mx[hq] = f if mx[hq] is None else jnp.maximum(mx[hq], f)
        m = jnp.max(jnp.concatenate(mx, axis=0), axis=1, keepdims=True)
        mb_ref[...] = jnp.broadcast_to(m, (tq, LANES))
        l = [jnp.zeros((t, LANES), F32) for _ in range(halves)]
        for jj in range(trips):
            for hq in range(halves):
                mb = mb_ref[hq * t:(hq + 1) * t, :]
                for hk in range(halves):
                    dist = (halves * qi + hq) - (halves * jj + hk)
                    rows, cols = slice(hq * t, (hq + 1) * t), slice(jj * tq + hk * t, jj * tq + (hk + 1) * t)
                    if dist < 0:
                        p_ref[rows, cols] = jnp.zeros((t, t), BF16)
                        continue
                    p = jnp.exp(s_ref[jj, rows, hk * t:(hk + 1) * t] - jnp.concatenate([mb] * (t // LANES), axis=1))
                    l[hq] = l[hq] + _lane_fold(jnp.add, p)
                    p_ref[rows, cols] = p.astype(BF16)
        acc = _dot(p_ref[:, :trips * tq], v_ref[:trips * tq, :])
        outs.append((acc, jnp.sum(jnp.concatenate(l, axis=0), axis=1, keepdims=True)))
    return outs


def _per_query_tile(nq, body):
    qi = pl.program_id(2)
    for n in range(nq):
        pl.when(qi == n)(functools.partial(body, n))


def _moba_prompt_kernel(q_ref, k_ref, v_ref, bias_ref, o_ref,
                        s0_ref, s1_ref, p0_ref, p1_ref, mb_ref, km_ref, kaug_ref, *, nq):
    t = ATT_TILE
    tq = Q_TILE
    tile_shift = t.bit_length() - 1
    nb = km_ref.shape[0]
    qi = pl.program_id(2)

    @pl.when(qi == 0)
    def _():
        for j in range(nb):
            km_ref[j:j + 1, :] = jnp.sum(k_ref[j * t:(j + 1) * t, :].astype(F32),
                                         axis=0, keepdims=True) * (1.0 / t)
        seq = k_ref.shape[0]
        row_blk = lax.broadcasted_iota(jnp.int32, (seq, LANES), 0) >> tile_shift
        col = lax.broadcasted_iota(jnp.int32, (seq, LANES), 1)
        kaug_ref[:, :LANES] = k_ref[...]
        kaug_ref[:, LANES:] = jnp.where(col == row_blk, 1.0, 0.0).astype(BF16)

    q = q_ref[...].astype(F32)
    km = km_ref[...]
    km_hi = km.astype(BF16)
    km_lo = (km - km_hi.astype(F32)).astype(BF16)
    lane = lax.broadcasted_iota(jnp.int32, (tq, LANES), 1)
    blk = lax.broadcasted_iota(jnp.int32, (nb, tq), 0)
    own = (tq // t) * qi + (lax.broadcasted_iota(jnp.int32, (nb, tq), 1) >> tile_shift)

    q_ops = []
    for e in range(2):
        qe = jnp.where((lane >= HEAD_DIM * e) & (lane < HEAD_DIM * (e + 1)), q, 0.0).astype(BF16)
        gt = _dot_nt(km_hi, qe) + _dot_nt(km_lo, qe)
        rank = jnp.zeros((nb, tq), jnp.int32)
        for m in range(nb):
            gm = gt[m:m + 1, :]
            ahead = (gm > gt) | ((gm == gt) & (m < blk))
            rank = rank + jnp.where(ahead & (m < own), 1, 0)
        keep = (blk < own) & (rank < MOBA_TOPK) & (jnp.abs(gt) < jnp.inf)
        pen = jnp.where(keep | (blk == own), 0.0, NEG_BIG)
        pen = jnp.concatenate([pen, jnp.zeros((LANES - nb, tq), F32)], axis=0)
        q_ops.append(jnp.concatenate([qe, pen.T.astype(BF16)], axis=1))

    def body(n):
        (a0, l0), (a1, l1) = _attend_static(q_ops, kaug_ref, v_ref, bias_ref, (0, 1),
                                            (s0_ref, s1_ref), (p0_ref, p1_ref), mb_ref, n)
        o_ref[...] = jnp.where(lane < HEAD_DIM, a0 / l0, a1 / l1).astype(o_ref.dtype)

    _per_query_tile(nq, body)


def _attn_scratch(nq, seq):
    return [pltpu.VMEM((nq, Q_TILE, Q_TILE), F32),
            pltpu.VMEM((nq, Q_TILE, Q_TILE), F32),
            pltpu.VMEM((Q_TILE, seq), BF16),
            pltpu.VMEM((Q_TILE, seq), BF16),
            pltpu.VMEM((Q_TILE, LANES), F32)]


def _moba_prompt(qa, ka, va, bias_tiles, batch, seq):
    tq = Q_TILE
    nq = seq // tq
    grid = (batch, N_HEADS_A // 2, nq)
    qmap = lambda b, hp, qi: (b * nq + qi, hp)
    kvmap = lambda b, hp, qi: (b, hp)
    return pl.pallas_call(
        functools.partial(_moba_prompt_kernel, nq=nq),
        grid=grid,
        in_specs=[pl.BlockSpec((tq, LANES), qmap),
                  pl.BlockSpec((seq, LANES), kvmap),
                  pl.BlockSpec((seq, LANES), kvmap),
                  pl.BlockSpec((2, 3, ATT_TILE, ATT_TILE), lambda b, hp, qi: (hp, 0, 0, 0))],
        out_specs=pl.BlockSpec((tq, LANES), qmap),
        out_shape=jax.ShapeDtypeStruct((batch * seq, WIDTH_A), BF16),
        scratch_shapes=_attn_scratch(nq, seq) + [pltpu.VMEM((seq // MOBA_BLOCK, LANES), F32),
                                                 pltpu.VMEM((seq, 2 * LANES), BF16)],
        compiler_params=pltpu.CompilerParams(
            dimension_semantics=("arbitrary", "arbitrary", "arbitrary"),
            vmem_limit_bytes=VMEM_LIMIT_BYTES),
        name="moba_prompt",
    )(qa, ka, va, bias_tiles)


def _diff_prompt_kernel(q_ref, k_ref, v_ref, bias_ref, lam_ref, gain_ref, o_ref,
                        s0_ref, s1_ref, p0_ref, p1_ref, mb_ref, *, lam_init, nq):
    q = q_ref[...].astype(F32)
    lane = lax.broadcasted_iota(jnp.int32, (Q_TILE, LANES), 1)
    q_ops = [jnp.where((lane >= HEAD_DIM * c) & (lane < HEAD_DIM * (c + 1)), q, 0.0).astype(BF16)
             for c in range(2)]

    def body(n):
        (a0, l0), (a1, l1) = _attend_static(q_ops, k_ref, v_ref, bias_ref, (0, 0),
                                            (s0_ref, s1_ref), (p0_ref, p1_ref), mb_ref, n)
        o = a0 / l0 - _lambda(lam_ref, lam_init) * (a1 / l1)
        o_ref[...] = (_rmsnorm(o, gain_ref[...]) * (1.0 - lam_init)).astype(o_ref.dtype)

    _per_query_tile(nq, body)


def _diff_prompt(qd, kd, vd, bias_tiles, lam_vecs, gain, lam_init, batch, seq):
    tq = Q_TILE
    nq = seq // tq
    grid = (batch, N_HEADS_B, nq)
    qmap = lambda b, h, qi: (b * nq + qi, h)
    kvmap = lambda b, h, qi: (b, h)
    return pl.pallas_call(
        functools.partial(_diff_prompt_kernel, lam_init=lam_init, nq=nq),
        grid=grid,
        in_specs=[pl.BlockSpec((tq, LANES), qmap),
                  pl.BlockSpec((seq, LANES), kvmap),
                  pl.BlockSpec((seq, LANES), kvmap),
                  pl.BlockSpec((1, 3, ATT_TILE, ATT_TILE), lambda b, h, qi: (h, 0, 0, 0)),
                  pl.BlockSpec((4, HEAD_DIM), lambda b, h, qi: (0, 0)),
                  pl.BlockSpec((1, 2 * HEAD_DIM), lambda b, h, qi: (0, 0))],
        out_specs=pl.BlockSpec((tq, LANES), qmap),
        out_shape=jax.ShapeDtypeStruct((batch * seq, WIDTH_B), BF16),
        scratch_shapes=_attn_scratch(nq, seq),
        compiler_params=pltpu.CompilerParams(
            dimension_semantics=("arbitrary", "arbitrary", "arbitrary"),
            vmem_limit_bytes=VMEM_LIMIT_BYTES),
        name="diff_prompt",
    )(qd, kd, vd, bias_tiles, lam_vecs, gain)


def _toeplitz(w, t):
    h = w.shape[0]
    m = jnp.broadcast_to(w[:, None, :], (h, t, 2 * t)).reshape(h, 2 * t * t)
    return m[:, :t * (2 * t - 1)].reshape(h, t, 2 * t - 1)[:, :, :t]


def _bias_tiles(bias_t):
    t = ATT_TILE
    assert t + 1 >= MAX_DISTANCE
    k = jnp.arange(2 * t)
    d = jnp.where(k <= t, -k, 2 * t - k)
    diag = _toeplitz(jnp.where(d >= 0, _bias_of(bias_t, d), -jnp.inf), t)
    sub = _toeplitz(_bias_of(bias_t, d + t), t)
    far = _toeplitz(_bias_of(bias_t, d + 2 * t), t)
    return jnp.stack([diag, sub, far], axis=1)


def _page_specs(n_chunks, page_shape):
    pps = PAGES_PER_STEP

    def k_map(i):
        return lambda b, ph, j, pt: (pt[b, jnp.where(ph == 0, j, n_chunks - 1) * pps + i], 0, 0)

    def v_map(i):
        return lambda b, ph, j, pt: (pt[b, jnp.where(ph == 0, 0, j) * pps + i], 0, 0)

    return ([pl.BlockSpec(page_shape, k_map(i)) for i in range(pps)]
            + [pl.BlockSpec(page_shape, v_map(i)) for i in range(pps)])


def _moba_sample_kernel(pt_ref, *refs, n_chunks, tokens):
    pps = PAGES_PER_STEP
    del pt_ref
    kt_pages = refs[:pps]
    vt_pages = refs[pps:2 * pps]
    (q_ref, kn_ref, vn_ref, bias_ref, o_ref,
     s_ref, snew_ref, acc_ref, l_ref, m_ref, bmax_ref, gate_ref, sel_ref) = refs[2 * pps:]
    rows = q_ref.shape[1]
    groups = rows // tokens
    ppb = MOBA_BLOCK // PAGE_SIZE
    bps = pps // ppb
    n_blocks = n_chunks * bps
    ph = pl.program_id(1)
    j = pl.program_id(2)
    lane = lax.broadcasted_iota(jnp.int32, (rows, LANES), 1)
    lanef = lane.astype(F32)
    q = q_ref[0]

    @pl.when((ph == 0) & (j == 0))
    def _():
        snew_ref[...] = _dot_nt(q, kn_ref[0]) + bias_ref[2]
        bmax_ref[...] = jnp.full((rows, LANES), -jnp.inf, F32)
        gate_ref[...] = jnp.zeros((rows, LANES), F32)

    @pl.when(ph == 0)
    def _():
        is_last = j == n_chunks - 1
        bmax = bmax_ref[...]
        gate = gate_ref[...]
        for b in range(bps):
            smax = ssum = None
            for pg in range(ppb):
                i = b * ppb + pg
                raw = _dot(q, kt_pages[i][0].astype(BF16))
                bias = bias_ref[0]
                if i == pps - 1:
                    bias = jnp.where(is_last, bias_ref[1], bias)
                s = raw + bias
                s_ref[j, :, i * PAGE_SIZE:(i + 1) * PAGE_SIZE] = s
                smax = s if smax is None else jnp.maximum(smax, s)
                ssum = raw if ssum is None else ssum + raw
            blk = j * bps + b
            bmax = jnp.where(lane == blk, jnp.max(smax, axis=1, keepdims=True), bmax)
            gate = jnp.where(lane == blk, jnp.sum(ssum, axis=1, keepdims=True), gate)
        bmax_ref[...] = bmax
        gate_ref[...] = gate

    @pl.when((ph == 1) & (j == 0))
    def _():
        g = jnp.where(lane < n_blocks, gate_ref[...], -jnp.inf)
        sel = jnp.zeros((rows, LANES), F32)
        for _ in range(MOBA_TOPK):
            top = jnp.max(g, axis=1, keepdims=True)
            first = jnp.min(jnp.where(g == top, lanef, float(LANES)), axis=1, keepdims=True)
            pick = lanef == first
            finite = jnp.where(jnp.abs(top) < jnp.inf, 1.0, 0.0)
            sel = jnp.maximum(sel, jnp.where(pick, finite, 0.0))
            g = jnp.where(pick, -jnp.inf, g)
        sel_ref[...] = sel
        m_sel = jnp.max(jnp.where(sel > 0.5, bmax_ref[...], -jnp.inf), axis=1, keepdims=True)
        m_new = jnp.max(snew_ref[...], axis=1, keepdims=True)
        m_ref[...] = jnp.broadcast_to(jnp.maximum(m_sel, m_new), (rows, LANES))
        acc_ref[...] = jnp.zeros(acc_ref.shape, F32)
        l_ref[...] = jnp.zeros(l_ref.shape, F32)

    @pl.when(ph == 1)
    def _():
        m = m_ref[...]
        selv = sel_ref[...]
        acc = acc_ref[...]
        l = l_ref[...]
        for i in range(pps):
            blk = j * bps + i // ppb
            on = jnp.max(jnp.where(lane == blk, selv, 0.0), axis=1, keepdims=True)
            on = jnp.broadcast_to(on, (rows, LANES)) > 0.5
            s = s_ref[j, :, i * PAGE_SIZE:(i + 1) * PAGE_SIZE]
            p = jnp.exp(jnp.where(on, s - m, -jnp.inf))
            l = l + p
            acc = acc + _dot_nt(p.astype(BF16), vt_pages[i][0].astype(BF16))
        acc_ref[...] = acc
        l_ref[...] = l

    @pl.when((ph == 1) & (j == n_chunks - 1))
    def _():
        p = jnp.exp(snew_ref[...] - m_ref[...])
        acc = acc_ref[...] + _dot(p.astype(BF16), vn_ref[0])
        l = jnp.sum(l_ref[...] + p, axis=1, keepdims=True)
        o = acc / l
        grp = lax.broadcasted_iota(jnp.int32, (groups, WIDTH_A), 0)
        col = lax.broadcasted_iota(jnp.int32, (groups, WIDTH_A), 1)
        head_shift = HEAD_DIM.bit_length() - 1
        w = jnp.where((col >> head_shift) == grp, 1.0, 0.0)
        tok = jnp.sum(o.reshape(tokens, groups, WIDTH_A) * w[None], axis=1)
        o_ref[0] = tok.astype(o_ref.dtype)


def _moba_sample(page_table, pool_kt, pool_vt, q_bd, k_new, v_new, bias_rows, *, tokens):
    pps = PAGES_PER_STEP
    n_seq, n_pages = page_table.shape
    n_chunks = n_pages // pps
    rows = q_bd.shape[1]
    seq_map = lambda b, ph, j, pt: (b, 0, 0)
    in_specs = (_page_specs(n_chunks, (1, WIDTH_A, PAGE_SIZE))
                + [pl.BlockSpec((1, rows, WIDTH_A), seq_map),
                   pl.BlockSpec((1, LANES, WIDTH_A), seq_map),
                   pl.BlockSpec((1, LANES, WIDTH_A), seq_map),
                   pl.BlockSpec((3, rows, LANES), lambda b, ph, j, pt: (0, 0, 0))])
    vec = pltpu.VMEM((rows, LANES), F32)
    grid_spec = pltpu.PrefetchScalarGridSpec(
        num_scalar_prefetch=1,
        grid=(n_seq, 2, n_chunks),
        in_specs=in_specs,
        out_specs=pl.BlockSpec((1, tokens, WIDTH_A), seq_map),
        scratch_shapes=[pltpu.VMEM((n_chunks, rows, pps * PAGE_SIZE), F32),
                        vec,
                        pltpu.VMEM((rows, WIDTH_A), F32),
                        vec, vec, vec, vec, vec])
    return pl.pallas_call(
        functools.partial(_moba_sample_kernel, n_chunks=n_chunks, tokens=tokens),
        grid_spec=grid_spec,
        out_shape=jax.ShapeDtypeStruct((n_seq, tokens, WIDTH_A), BF16),
        compiler_params=pltpu.CompilerParams(
            dimension_semantics=("arbitrary", "arbitrary", "arbitrary"),
            vmem_limit_bytes=VMEM_LIMIT_BYTES),
        name="moba_sample",
    )(page_table, *([pool_kt] * pps), *([pool_vt] * pps), q_bd, k_new, v_new, bias_rows)


def _diff_sample_kernel(pt_ref, *refs, n_chunks, tokens, lam_init):
    pps = PAGES_PER_STEP
    del pt_ref
    k_pages = refs[:pps]
    v_pages = refs[pps:2 * pps]
    (q_ref, kn_ref, vn_ref, bias_ref, bnew_ref, lam_ref, gain_ref, o_ref,
     s_ref, snew_ref, acc_ref, l_ref, m_ref) = refs[2 * pps:]
    rows = q_ref.shape[1]
    pw = k_pages[0].shape[1]
    ph = pl.program_id(1)
    j = pl.program_id(2)
    q = q_ref[0]

    @pl.when((ph == 0) & (j == 0))
    def _():
        snew_ref[...] = _dot_nt(q, kn_ref[0]) + bnew_ref[...]
        m_ref[...] = jnp.full((rows, LANES), -jnp.inf, F32)

    @pl.when(ph == 0)
    def _():
        is_last = j == n_chunks - 1
        mx = m_ref[...]
        for i in range(pps):
            bias = bias_ref[0]
            if i == pps - 1:
                bias = jnp.where(is_last, bias_ref[1], bias)
            s = _dot_nt(q, k_pages[i][0].astype(BF16)) + bias
            s_ref[j, :, i * pw:(i + 1) * pw] = s
            mx = jnp.maximum(mx, _lane_fold(jnp.maximum, s))
        m_ref[...] = mx

    @pl.when((ph == 1) & (j == 0))
    def _():
        m = jnp.maximum(jnp.max(m_ref[...], axis=1, keepdims=True),
                        jnp.max(snew_ref[...], axis=1, keepdims=True))
        m_ref[...] = jnp.broadcast_to(m, (rows, LANES))
        acc_ref[...] = jnp.zeros(acc_ref.shape, F32)
        l_ref[...] = jnp.zeros(l_ref.shape, F32)

    @pl.when(ph == 1)
    def _():
        m = m_ref[...]
        mw = jnp.concatenate([m] * (pw // LANES), axis=1)
        acc = acc_ref[...]
        l = l_ref[...]
        for i in range(pps):
            p = jnp.exp(s_ref[j, :, i * pw:(i + 1) * pw] - mw)
            l = l + _lane_fold(jnp.add, p)
            acc = acc + _dot(p.astype(BF16), v_pages[i][0].astype(BF16))
        acc_ref[...] = acc
        l_ref[...] = l

    @pl.when((ph == 1) & (j == n_chunks - 1))
    def _():
        p = jnp.exp(snew_ref[...] - m_ref[...])
        acc = acc_ref[...] + _dot(p.astype(BF16), vn_ref[0])
        l = jnp.sum(l_ref[...] + p, axis=1, keepdims=True)
        o = (acc / l).reshape(tokens, rows // tokens, LANES)
        lam = _lambda(lam_ref, lam_init)
        sub = lax.broadcasted_iota(jnp.int32, (rows // tokens, LANES), 0)
        gain = gain_ref[...]
        parts = []
        for h in range(N_HEADS_B):
            w = jnp.where(sub == 2 * h, 1.0, jnp.where(sub == 2 * h + 1, -lam, 0.0))
            parts.append(_rmsnorm(jnp.sum(o * w[None], axis=1), gain) * (1.0 - lam_init))
        o_ref[0] = jnp.concatenate(parts, axis=1).astype(o_ref.dtype)


def _diff_sample(page_table, pool_k, pool_v, q2, k_new, v_new, bias_rows, bias_new, lam_vecs, gain,
                 *, tokens, lam_init):
    pps = PAGES_PER_STEP
    n_seq, n_pages = page_table.shape
    n_chunks = n_pages // pps
    rows = q2.shape[1]
    pw = pool_k.shape[1]
    seq_map = lambda b, ph, j, pt: (b, 0, 0)
    const2 = lambda b, ph, j, pt: (0, 0)
    in_specs = (_page_specs(n_chunks, (1, pw, LANES))
                + [pl.BlockSpec((1, rows, LANES), seq_map),
                   pl.BlockSpec((1, LANES, LANES), seq_map),
                   pl.BlockSpec((1, LANES, LANES), seq_map),
                   pl.BlockSpec((2, rows, pw), lambda b, ph, j, pt: (0, 0, 0)),
                   pl.BlockSpec((rows, LANES), const2),
                   pl.BlockSpec((4, HEAD_DIM), const2),
                   pl.BlockSpec((1, 2 * HEAD_DIM), const2)])
    vec = pltpu.VMEM((rows, LANES), F32)
    grid_spec = pltpu.PrefetchScalarGridSpec(
        num_scalar_prefetch=1,
        grid=(n_seq, 2, n_chunks),
        in_specs=in_specs,
        out_specs=pl.BlockSpec((1, tokens, WIDTH_B), seq_map),
        scratch_shapes=[pltpu.VMEM((n_chunks, rows, pps * pw), F32),
                        vec, vec, vec, vec])
    return pl.pallas_call(
        functools.partial(_diff_sample_kernel, n_chunks=n_chunks, tokens=tokens, lam_init=lam_init),
        grid_spec=grid_spec,
        out_shape=jax.ShapeDtypeStruct((n_seq, tokens, WIDTH_B), BF16),
        compiler_params=pltpu.CompilerParams(
            dimension_semantics=("arbitrary", "arbitrary", "arbitrary"),
            vmem_limit_bytes=VMEM_LIMIT_BYTES),
        name="diff_sample",
    )(page_table, *([pool_k] * pps), *([pool_v] * pps), q2, k_new, v_new, bias_rows, bias_new,
      lam_vecs, gain)


def _block_diag(q, n_seq, tokens, groups, keep):
    w = q.shape[1] // groups
    q4 = q.reshape(n_seq, tokens, groups, 1, w)
    eye = (jnp.arange(groups)[:, None] % keep == jnp.arange(keep)[None, :]).astype(q.dtype)
    return (q4 * eye.reshape(1, 1, groups, keep, 1)).reshape(n_seq, tokens * groups, keep * w)


def _pad_rows(x, n_seq):
    x = x.reshape(n_seq, -1, x.shape[-1])
    return jnp.pad(x, ((0, 0), (0, LANES - x.shape[1]), (0, 0)))


def _sample_bias(bias_t, tokens, heads_per_group):
    assert PAGE_SIZE + 1 >= MAX_DISTANCE
    tok = jnp.arange(tokens)[:, None]
    col = jnp.arange(LANES)[None, :]
    far = _bias_of(bias_t, jnp.broadcast_to(2 * PAGE_SIZE, (tokens, LANES)))
    last = _bias_of(bias_t, PAGE_SIZE + tok - col)
    new = jnp.where((col <= tok) & (col < tokens), _bias_of(bias_t, tok - col), -jnp.inf)

    def rows(x):
        x = jnp.repeat(x.transpose(1, 0, 2), heads_per_group, axis=1)
        return x.reshape(-1, LANES)

    return rows(far), rows(last), rows(new)


def _spread_heads(x, n_heads, groups):
    r, k = x.shape
    row_head = (jnp.arange(r) % groups) // (groups // n_heads)
    own = row_head[:, None, None] == jnp.arange(n_heads)[None, None, :]
    return jnp.where(own, x[:, :, None], -jnp.inf).reshape(r, k * n_heads)


def _merge_kernel(x_ref, oa_ref, ob_ref, g_ref, wg_ref, woa_ref, wob_ref, wo_ref, o_ref):
    x = x_ref[...]
    h = _rmsnorm(x, g_ref[...]).astype(BF16)
    ga = 1.0 / (1.0 + jnp.exp(-_dot(h, wg_ref[:, :D_MODEL])))
    m = ga * _dot(oa_ref[...], woa_ref[...])
    gb = 1.0 / (1.0 + jnp.exp(-_dot(h, wg_ref[:, D_MODEL:])))
    m = m + gb * _dot(ob_ref[...], wob_ref[...])
    o_ref[...] = x + _dot(m.astype(BF16), wo_ref[...])


def _merge(x, oa, ob, g, wg, woa, wob, wo, tm):
    n = x.shape[0]
    row = lambda i: (i, 0)
    const = lambda i: (0, 0)
    return pl.pallas_call(
        _merge_kernel,
        grid=(n // tm,),
        in_specs=[pl.BlockSpec((tm, D_MODEL), row),
                  pl.BlockSpec((tm, WIDTH_A), row),
                  pl.BlockSpec((tm, WIDTH_B), row),
                  pl.BlockSpec((1, D_MODEL), const),
                  pl.BlockSpec((D_MODEL, 2 * D_MODEL), const),
                  pl.BlockSpec((WIDTH_A, D_MODEL), const),
                  pl.BlockSpec((WIDTH_B, D_MODEL), const),
                  pl.BlockSpec((D_MODEL, D_MODEL), const)],
        out_specs=pl.BlockSpec((tm, D_MODEL), row),
        out_shape=jax.ShapeDtypeStruct((n, D_MODEL), F32),
        compiler_params=pltpu.CompilerParams(
            dimension_semantics=("arbitrary",), vmem_limit_bytes=VMEM_LIMIT_BYTES),
        name="merge",
    )(x, oa, ob, g, wg, woa, wob, wo)


def _ffn_kernel(*refs, tm, seq_len, has_prev):
    if has_prev:
        (x_ref, g_ref, wup_ref, cw_ref, cb_ref, wdn_ref, gfin_ref, e1_ref, e2_ref,
         y_ref, u_ref, act_ref) = refs
    else:
        (x_ref, g_ref, wup_ref, cw_ref, cb_ref, wdn_ref, gfin_ref,
         y_ref, u_ref, act_ref, carry_ref) = refs
    i = pl.program_id(0)
    x = x_ref[...]
    h = _rmsnorm(x, g_ref[...]).astype(BF16)
    row = lax.broadcasted_iota(jnp.int32, (tm, FF_CHUNK), 0)
    row8 = lax.broadcasted_iota(jnp.int32, (SUBLANES, FF_CHUNK), 0)
    if has_prev:
        pos = row & (seq_len - 1)
    else:
        @pl.when((i * tm) % seq_len == 0)
        def _():
            carry_ref[...] = jnp.zeros(carry_ref.shape, F32)

    for c in range(D_FF // FF_CHUNK):
        halves = []
        for part in range(2):
            cols = slice(part * D_FF + c * FF_CHUNK, part * D_FF + (c + 1) * FF_CHUNK)
            u = _dot(h, wup_ref[:, cols])
            um1 = pltpu.roll(u, 1, 0)
            um2 = pltpu.roll(u, 2, 0)
            if has_prev:
                um1 = jnp.where(pos == 0, e1_ref[:, cols], um1)
                um2 = jnp.where(pos < 2, e2_ref[:, cols], um2)
                u_ref[:, cols] = u
            else:
                prev = carry_ref[:, cols]
                top1 = jnp.where(row8 == 0, pltpu.roll(prev, 1, 0), um1[:SUBLANES])
                top2 = jnp.where(row8 < 2, pltpu.roll(prev, 2, 0), um2[:SUBLANES])
                um1 = jnp.concatenate([top1, um1[SUBLANES:]], axis=0)
                um2 = jnp.concatenate([top2, um2[SUBLANES:]], axis=0)
                carry_ref[:, cols] = u[tm - SUBLANES:]
                u_ref[:, cols] = u[tm - SUBLANES:]
            cw = cw_ref[:, cols]
            halves.append(((cb_ref[:, cols] + cw[0:1] * um2) + cw[1:2] * um1) + cw[2:3] * u)
        gate, val = halves
        act = (gate * (1.0 / (1.0 + jnp.exp(-gate)))) * val
        act_ref[:, c * FF_CHUNK:(c + 1) * FF_CHUNK] = act.astype(BF16)

    x3 = x + _dot(act_ref[...], wdn_ref[...])
    y_ref[...] = _rmsnorm(x3, gfin_ref[...])


def _ffn(x, g, wup, cw, cb, wdn, gfin, prev, tm, seq_len):
    n = x.shape[0]
    has_prev = prev is not None
    row = lambda i: (i, 0)
    const = lambda i: (0, 0)
    in_specs = [pl.BlockSpec((tm, D_MODEL), row),
                pl.BlockSpec((1, D_MODEL), const),
                pl.BlockSpec((D_MODEL, 2 * D_FF), const),
                pl.BlockSpec((CONV_W, 2 * D_FF), const),
                pl.BlockSpec((1, 2 * D_FF), const),
                pl.BlockSpec((D_FF, D_MODEL), const),
                pl.BlockSpec((1, D_MODEL), const)]
    scratch = [pltpu.VMEM((tm, D_FF), BF16)]
    if has_prev:
        assert n == tm and tm % seq_len == 0 and seq_len & (seq_len - 1) == 0
        in_specs += [pl.BlockSpec((tm, 2 * D_FF), row)] * 2
        u_shape, u_spec = (n, 2 * D_FF), pl.BlockSpec((tm, 2 * D_FF), row)
        args = (x, g, wup, cw, cb, wdn, gfin) + tuple(prev)
    else:
        assert seq_len % tm == 0
        tiles_per_seq = seq_len // tm
        u_shape = (n // seq_len * SUBLANES, 2 * D_FF)
        u_spec = pl.BlockSpec((SUBLANES, 2 * D_FF), lambda i: (i // tiles_per_seq, 0))
        scratch.append(pltpu.VMEM((SUBLANES, 2 * D_FF), F32))
        args = (x, g, wup, cw, cb, wdn, gfin)
    return pl.pallas_call(
        functools.partial(_ffn_kernel, tm=tm, seq_len=seq_len, has_prev=has_prev),
        grid=(n // tm,),
        in_specs=in_specs,
        out_specs=[pl.BlockSpec((tm, D_MODEL), row), u_spec],
        out_shape=[jax.ShapeDtypeStruct((n, D_MODEL), F32), jax.ShapeDtypeStruct(u_shape, F32)],
        scratch_shapes=scratch,
        compiler_params=pltpu.CompilerParams(
            dimension_semantics=("arbitrary",), vmem_limit_bytes=VMEM_LIMIT_BYTES),
        name="ffn_sample" if has_prev else "ffn_prompt",
    )(*args)


def kernel(x_prompt, x_sample, cache_moba_k, cache_moba_v, cache_diff_k, cache_diff_v, state_conv, page_table, rel_bias, norm_attn, w_in, w_gate, w_out_a, w_out_b, w_out, lambda_q1, lambda_k1, lambda_q2, lambda_k2, diff_norm, norm_ffn, w_up, conv_w, conv_b, w_down, norm_final):
    batch, seq, _ = x_prompt.shape
    n_seq, tokens, _ = x_sample.shape
    depth = w_in.shape[0]
    n_phys = cache_moba_k.shape[1]
    n_pages = page_table.shape[1]
    assert depth == 1 and seq % Q_TILE == 0 and ATT_TILE & (ATT_TILE - 1) == 0
    assert (n_pages * PAGE_SIZE) % MOBA_BLOCK == 0 and n_pages % PAGES_PER_STEP == 0
    assert n_pages * PAGE_SIZE // MOBA_BLOCK <= LANES and CONV_W - 1 <= tokens <= LANES // N_HEADS_B
    l = 0
    lam_init = 0.8 - 0.6 * math.exp(-0.3 * l)

    bias_a = rel_bias[:, :N_HEADS_A].T
    bias_d = rel_bias[:, N_HEADS_A:].T
    row = lambda v: v.reshape(1, -1)
    w_in_b = w_in[l].astype(BF16)
    w_gate_b = w_gate[l].astype(BF16)
    w_oa_b = w_out_a[l].astype(BF16)
    w_ob_b = w_out_b[l].astype(BF16)
    w_o_b = w_out[l].astype(BF16)
    w_up_b = w_up[l].astype(BF16)
    w_dn_b = w_down[l].astype(BF16)
    lam_vecs = jnp.stack([lambda_q1[l], lambda_k1[l], lambda_q2[l], lambda_k2[l]]).astype(F32)
    gain_d = row(diff_norm[l])
    g_attn, g_ffn, g_fin = row(norm_attn[l]), row(norm_ffn[l]), row(norm_final)
    cb = row(conv_b[l])

    xp = x_prompt.reshape(batch * seq, D_MODEL)
    qa, ka, va, qd, kd, vd, ka_f, va_f, kd_f, vd_f = _proj(xp, g_attn, w_in_b, 512)
    oa = _moba_prompt(qa, ka, va, _bias_tiles(bias_a), batch, seq)
    ob = _diff_prompt(qd, kd, vd, _bias_tiles(bias_d), lam_vecs, gain_d, lam_init, batch, seq)
    x2 = _merge(xp, oa, ob, g_attn, w_gate_b, w_oa_b, w_ob_b, w_o_b, 512)
    yp, tail_p = _ffn(x2, g_ffn, w_up_b, conv_w[l], cb, w_dn_b, g_fin, None, 512, seq)
    conv_p = tail_p.reshape(batch, SUBLANES, 2 * D_FF)[:, SUBLANES - (CONV_W - 1):]

    n_s = n_seq * tokens
    xs = x_sample.reshape(n_s, D_MODEL)
    qa_s, ka_s, va_s, qd_s, kd_s, vd_s, ka_sf, va_sf, kd_sf, vd_sf = _proj(xs, g_attn, w_in_b, n_s)

    pool_t = lambda c: jnp.transpose(c[l], (0, 2, 3, 1)).reshape(n_phys, WIDTH_A, PAGE_SIZE)
    pool_r = lambda c: c[l].reshape(n_phys, PAGE_SIZE * N_HEADS_B, 2 * HEAD_DIM)
    far_a, last_a, new_a = _sample_bias(bias_a, tokens, 1)
    oa_s = _moba_sample(page_table, pool_t(cache_moba_k), pool_t(cache_moba_v),
                        _block_diag(qa_s, n_seq, tokens, N_HEADS_A, N_HEADS_A),
                        _pad_rows(ka_s, n_seq), _pad_rows(va_s, n_seq),
                        jnp.stack([far_a, last_a, new_a]), tokens=tokens)
    far_d, last_d, new_d = _sample_bias(bias_d, tokens, 2)
    groups_d = 2 * N_HEADS_B
    spread = lambda x: _spread_heads(x, N_HEADS_B, groups_d)
    per_head = lambda x: x.reshape(n_s * N_HEADS_B, 2 * HEAD_DIM)
    oa_d = _diff_sample(page_table, pool_r(cache_diff_k), pool_r(cache_diff_v),
                        _block_diag(qd_s, n_seq, tokens, groups_d, 2),
                        _pad_rows(per_head(kd_s), n_seq), _pad_rows(per_head(vd_s), n_seq),
                        jnp.stack([spread(far_d), spread(last_d)]),
                        spread(new_d[:, :LANES // N_HEADS_B]),
                        lam_vecs, gain_d, tokens=tokens, lam_init=lam_init)
    x2s = _merge(xs, oa_s.reshape(n_s, WIDTH_A), oa_d.reshape(n_s, WIDTH_B), g_attn,
                 w_gate_b, w_oa_b, w_ob_b, w_o_b, n_s)
    st = state_conv[l]
    zero = jnp.zeros((n_seq, 1, 2 * D_FF), F32)
    e1 = jnp.concatenate([st[:, 1:2]] + [zero] * (tokens - 1), axis=1).reshape(n_s, 2 * D_FF)
    e2 = jnp.concatenate([st[:, 0:2]] + [zero] * (tokens - 2), axis=1).reshape(n_s, 2 * D_FF)
    ys, u_s = _ffn(x2s, g_ffn, w_up_b, conv_w[l], cb, w_dn_b, g_fin, (e1, e2), n_s, tokens)
    conv_s = u_s.reshape(n_seq, tokens, 2 * D_FF)[:, tokens - (CONV_W - 1):]

    shp_a = lambda a, b_, t_: a.reshape(1, b_, t_, N_HEADS_A, HEAD_DIM)
    shp_d = lambda a, b_, t_: a.reshape(1, b_, t_, N_HEADS_B, 2 * HEAD_DIM)
    return (yp.reshape(batch, seq, D_MODEL), ys.reshape(n_seq, tokens, D_MODEL),
            shp_a(ka_f, batch, seq), shp_a(va_f, batch, seq),
            shp_d(kd_f, batch, seq), shp_d(vd_f, batch, seq), conv_p[None],
            shp_a(ka_sf, n_seq, tokens), shp_a(va_sf, n_seq, tokens),
            shp_d(kd_sf, n_seq, tokens), shp_d(vd_sf, n_seq, tokens), conv_s[None])
```

```python
import functools
import math

import jax
import jax.numpy as jnp
from jax import lax
from jax.experimental import pallas as pl
from jax.experimental.pallas import tpu as pltpu

F32 = jnp.float32
BF16 = jnp.bfloat16

D_MODEL = 1024
HEAD_DIM = 64
N_HEADS_A = D_MODEL // 128
N_HEADS_B = D_MODEL // 256
WIDTH_A = N_HEADS_A * HEAD_DIM
WIDTH_B = N_HEADS_B * 2 * HEAD_DIM
N_IN = 3 * WIDTH_A + 3 * WIDTH_B
MOBA_BLOCK = 256
MOBA_TOPK = 3
NUM_BUCKETS = 32
MAX_DISTANCE = 128
D_FF = ((8 * D_MODEL // 3 + 127) // 128) * 128
CONV_W = 3
EPS = 1e-6
PAGE_SIZE = 128
SCALE = HEAD_DIM ** -0.5

LANES = 128
SUBLANES = 8
VMEM_LIMIT_BYTES = 56 * 1024 * 1024

ATT_TILE = MOBA_BLOCK
Q_TILE = 2 * ATT_TILE
FF_CHUNK = 256
PAGES_PER_STEP = 16
RING_SLOTS = 3
NEG_BIG = -1e30


def _dot(a, b):
    return jnp.dot(a, b, preferred_element_type=F32)


def _dot_nt(a, b):
    return lax.dot_general(a, b, (((1,), (1,)), ((), ())), preferred_element_type=F32)


def _rmsnorm(x, g):
    return (x * lax.rsqrt(jnp.mean(x * x, axis=-1, keepdims=True) + EPS)) * g


def _lane_fold(op, x):
    out = x[:, :LANES]
    for c in range(1, x.shape[1] // LANES):
        out = op(out, x[:, c * LANES:(c + 1) * LANES])
    return out


def _t5_bucket(dist):
    n = jnp.maximum(dist, 0)
    max_exact = NUM_BUCKETS // 2
    nf = jnp.maximum(n, 1).astype(F32)
    large = max_exact + (jnp.log(nf / max_exact) / math.log(MAX_DISTANCE / max_exact)
                         * (NUM_BUCKETS - max_exact)).astype(jnp.int32)
    large = jnp.minimum(large, NUM_BUCKETS - 1)
    return jnp.where(n < max_exact, n, large)


def _bias_of(bias_t, dist):
    onehot = jax.nn.one_hot(_t5_bucket(dist), NUM_BUCKETS, dtype=F32)
    return jnp.einsum('hb,...b->h...', bias_t.astype(F32), onehot, precision=lax.Precision.HIGHEST)


def _lambda(lam_ref, lam_init):
    lv = lam_ref[...]
    a = jnp.sum(lv[0:1] * lv[1:2], axis=-1, keepdims=True)
    b = jnp.sum(lv[2:3] * lv[3:4], axis=-1, keepdims=True)
    return jnp.exp(a) - jnp.exp(b) + lam_init


def _proj_kernel(x_ref, g_ref, w_ref,
                 qa_ref, ka_ref, va_ref, qd_ref, kd_ref, vd_ref,
                 kaf_ref, vaf_ref, kdf_ref, vdf_ref):
    h = _rmsnorm(x_ref[...], g_ref[...]).astype(BF16)
    outs = ((qa_ref, None, SCALE), (ka_ref, kaf_ref, None), (va_ref, vaf_ref, None),
            (qd_ref, None, SCALE), (kd_ref, kdf_ref, None), (vd_ref, vdf_ref, None))
    for c, (b_ref, f_ref, scale) in enumerate(outs):
        u = _dot(h, w_ref[:, c * WIDTH_A:(c + 1) * WIDTH_A])
        if f_ref is not None:
            f_ref[...] = u
        if scale is not None:
            u = u * scale
        b_ref[...] = u.astype(BF16)


def _proj(x, g, w_bf16, tm):
    n = x.shape[0]
    row = lambda i: (i, 0)
    const = lambda i: (0, 0)
    blk = pl.BlockSpec((tm, WIDTH_A), row)
    return pl.pallas_call(
        _proj_kernel,
        grid=(n // tm,),
        in_specs=[pl.BlockSpec((tm, D_MODEL), row),
                  pl.BlockSpec((1, D_MODEL), const),
                  pl.BlockSpec((D_MODEL, N_IN), const)],
        out_specs=[blk] * 10,
        out_shape=[jax.ShapeDtypeStruct((n, WIDTH_A), BF16)] * 6
                  + [jax.ShapeDtypeStruct((n, WIDTH_A), F32)] * 4,
        compiler_params=pltpu.CompilerParams(
            dimension_semantics=("arbitrary",), vmem_limit_bytes=VMEM_LIMIT_BYTES),
        name="proj",
    )(x, g, w_bf16)


def _attend_static(q_ops, k_ref, v_ref, bias_ref, bidx, s_refs, p_refs, mb_ref, qi):
    t = ATT_TILE
    tq = Q_TILE
    halves = tq // t
    trips = qi + 1
    outs = []
    for i, q_op in enumerate(q_ops):
        s_ref, p_ref = s_refs[i], p_refs[i]
        mx = [None] * halves
        for jj in range(trips):
            s = _dot_nt(q_op, k_ref[jj * tq:(jj + 1) * tq, :])
            for hq in range(halves):
                for hk in range(halves):
                    dist = (halves * qi + hq) - (halves * jj + hk)
                    if dist < 0:
                        continue
                    blk = (s[hq * t:(hq + 1) * t, hk * t:(hk + 1) * t]
                           + bias_ref[bidx[i], min(dist, 2)])
                    s_ref[jj, hq * t:(hq + 1) * t, hk * t:(hk + 1) * t] = blk
                    f = _lane_fold(jnp.maximum, blk)
                    mx[hq] = f if mx[hq] is None else jnp.maximum(mx[hq], f)
        m = jnp.max(jnp.concatenate(mx, axis=0), axis=1, keepdims=True)
        mb_ref[...] = jnp.broadcast_to(m, (tq, LANES))
        l = [jnp.zeros((t, LANES), F32) for _ in range(halves)]
        for jj in range(trips):
            for hq in range(halves):
                mb = mb_ref[hq * t:(hq + 1) * t, :]
                for hk in range(halves):
                    dist = (halves * qi + hq) - (halves * jj + hk)
                    rows, cols = slice(hq * t, (hq + 1) * t), slice(jj * tq + hk * t, jj * tq + (hk + 1) * t)
                    if dist < 0:
                        p_ref[rows, cols] = jnp.zeros((t, t), BF16)
                        continue
                    p = jnp.exp(s_ref[jj, rows, hk * t:(hk + 1) * t] - jnp.concatenate([mb] * (t // LANES), axis=1))
                    l[hq] = l[hq] + _lane_fold(jnp.add, p)
                    p_ref[rows, cols] = p.astype(BF16)
        acc = _dot(p_ref[:, :trips * tq], v_ref[:trips * tq, :])
        outs.append((acc, jnp.sum(jnp.concatenate(l, axis=0), axis=1, keepdims=True)))
    return outs


def _per_query_tile(nq, body):
    qi = pl.program_id(2)
    for n in range(nq):
        pl.when(qi == n)(functools.partial(body, n))


def _moba_prompt_kernel(q_ref, k_ref, v_ref, bias_ref, o_ref,
                        s0_ref, s1_ref, p0_ref, p1_ref, mb_ref, km_ref, kaug_ref, *, nq):
    t = ATT_TILE
    tq = Q_TILE
    tile_shift = t.bit_length() - 1
    nb = km_ref.shape[0]
    qi = pl.program_id(2)

    @pl.when(qi == 0)
    def _():
        for j in range(nb):
            km_ref[j:j + 1, :] = jnp.sum(k_ref[j * t:(j + 1) * t, :].astype(F32),
                                         axis=0, keepdims=True) * (1.0 / t)
        seq = k_ref.shape[0]
        row_blk = lax.broadcasted_iota(jnp.int32, (seq, LANES), 0) >> tile_shift
        col = lax.broadcasted_iota(jnp.int32, (seq, LANES), 1)
        kaug_ref[:, :LANES] = k_ref[...]
        kaug_ref[:, LANES:] = jnp.where(col == row_blk, 1.0, 0.0).astype(BF16)

    q = q_ref[...].astype(F32)
    km = km_ref[...]
    km_hi = km.astype(BF16)
    km_lo = (km - km_hi.astype(F32)).astype(BF16)
    lane = lax.broadcasted_iota(jnp.int32, (tq, LANES), 1)
    blk = lax.broadcasted_iota(jnp.int32, (nb, tq), 0)
    own = (tq // t) * qi + (lax.broadcasted_iota(jnp.int32, (nb, tq), 1) >> tile_shift)

    q_ops = []
    for e in range(2):
        qe = jnp.where((lane >= HEAD_DIM * e) & (lane < HEAD_DIM * (e + 1)), q, 0.0).astype(BF16)
        gt = _dot_nt(km_hi, qe) + _dot_nt(km_lo, qe)
        rank = jnp.zeros((nb, tq), jnp.int32)
        for m in range(nb):
            gm = gt[m:m + 1, :]
            ahead = (gm > gt) | ((gm == gt) & (m < blk))
            rank = rank + jnp.where(ahead & (m < own), 1, 0)
        keep = (blk < own) & (rank < MOBA_TOPK) & (jnp.abs(gt) < jnp.inf)
        pen = jnp.where(keep | (blk == own), 0.0, NEG_BIG)
        pen = jnp.concatenate([pen, jnp.zeros((LANES - nb, tq), F32)], axis=0)
        q_ops.append(jnp.concatenate([qe, pen.T.astype(BF16)], axis=1))

    def body(n):
        (a0, l0), (a1, l1) = _attend_static(q_ops, kaug_ref, v_ref, bias_ref, (0, 1),
                                            (s0_ref, s1_ref), (p0_ref, p1_ref), mb_ref, n)
        o_ref[...] = jnp.where(lane < HEAD_DIM, a0 / l0, a1 / l1).astype(o_ref.dtype)

    _per_query_tile(nq, body)


def _attn_scratch(nq, seq):
    return [pltpu.VMEM((nq, Q_TILE, Q_TILE), F32),
            pltpu.VMEM((nq, Q_TILE, Q_TILE), F32),
            pltpu.VMEM((Q_TILE, seq), BF16),
            pltpu.VMEM((Q_TILE, seq), BF16),
            pltpu.VMEM((Q_TILE, LANES), F32)]


def _moba_prompt(qa, ka, va, bias_tiles, batch, seq):
    tq = Q_TILE
    nq = seq // tq
    grid = (batch, N_HEADS_A // 2, nq)
    qmap = lambda b, hp, qi: (b * nq + qi, hp)
    kvmap = lambda b, hp, qi: (b, hp)
    return pl.pallas_call(
        functools.partial(_moba_prompt_kernel, nq=nq),
        grid=grid,
        in_specs=[pl.BlockSpec((tq, LANES), qmap),
                  pl.BlockSpec((seq, LANES), kvmap),
                  pl.BlockSpec((seq, LANES), kvmap),
                  pl.BlockSpec((2, 3, ATT_TILE, ATT_TILE), lambda b, hp, qi: (hp, 0, 0, 0))],
        out_specs=pl.BlockSpec((tq, LANES), qmap),
        out_shape=jax.ShapeDtypeStruct((batch * seq, WIDTH_A), BF16),
        scratch_shapes=_attn_scratch(nq, seq) + [pltpu.VMEM((seq // MOBA_BLOCK, LANES), F32),
                                                 pltpu.VMEM((seq, 2 * LANES), BF16)],
        compiler_params=pltpu.CompilerParams(
            dimension_semantics=("arbitrary", "arbitrary", "arbitrary"),
            vmem_limit_bytes=VMEM_LIMIT_BYTES),
        name="moba_prompt",
    )(qa, ka, va, bias_tiles)


def _diff_prompt_kernel(q_ref, k_ref, v_ref, bias_ref, lam_ref, gain_ref, o_ref,
                        s0_ref, s1_ref, p0_ref, p1_ref, mb_ref, *, lam_init, nq):
    q = q_ref[...].astype(F32)
    lane = lax.broadcasted_iota(jnp.int32, (Q_TILE, LANES), 1)
    q_ops = [jnp.where((lane >= HEAD_DIM * c) & (lane < HEAD_DIM * (c + 1)), q, 0.0).astype(BF16)
             for c in range(2)]

    def body(n):
        (a0, l0), (a1, l1) = _attend_static(q_ops, k_ref, v_ref, bias_ref, (0, 0),
                                            (s0_ref, s1_ref), (p0_ref, p1_ref), mb_ref, n)
        o = a0 / l0 - _lambda(lam_ref, lam_init) * (a1 / l1)
        o_ref[...] = (_rmsnorm(o, gain_ref[...]) * (1.0 - lam_init)).astype(o_ref.dtype)

    _per_query_tile(nq, body)


def _diff_prompt(qd, kd, vd, bias_tiles, lam_vecs, gain, lam_init, batch, seq):
    tq = Q_TILE
    nq = seq // tq
    grid = (batch, N_HEADS_B, nq)
    qmap = lambda b, h, qi: (b * nq + qi, h)
    kvmap = lambda b, h, qi: (b, h)
    return pl.pallas_call(
        functools.partial(_diff_prompt_kernel, lam_init=lam_init, nq=nq),
        grid=grid,
        in_specs=[pl.BlockSpec((tq, LANES), qmap),
                  pl.BlockSpec((seq, LANES), kvmap),
                  pl.BlockSpec((seq, LANES), kvmap),
                  pl.BlockSpec((1, 3, ATT_TILE, ATT_TILE), lambda b, h, qi: (h, 0, 0, 0)),
                  pl.BlockSpec((4, HEAD_DIM), lambda b, h, qi: (0, 0)),
                  pl.BlockSpec((1, 2 * HEAD_DIM), lambda b, h, qi: (0, 0))],
        out_specs=pl.BlockSpec((tq, LANES), qmap),
        out_shape=jax.ShapeDtypeStruct((batch * seq, WIDTH_B), BF16),
        scratch_shapes=_attn_scratch(nq, seq),
        compiler_params=pltpu.CompilerParams(
            dimension_semantics=("arbitrary", "arbitrary", "arbitrary"),
            vmem_limit_bytes=VMEM_LIMIT_BYTES),
        name="diff_prompt",
    )(qd, kd, vd, bias_tiles, lam_vecs, gain)


def _toeplitz(w, t):
    h = w.shape[0]
    m = jnp.broadcast_to(w[:, None, :], (h, t, 2 * t)).reshape(h, 2 * t * t)
    return m[:, :t * (2 * t - 1)].reshape(h, t, 2 * t - 1)[:, :, :t]


def _bias_tiles(bias_t):
    t = ATT_TILE
    assert t + 1 >= MAX_DISTANCE
    k = jnp.arange(2 * t)
    d = jnp.where(k <= t, -k, 2 * t - k)
    diag = _toeplitz(jnp.where(d >= 0, _bias_of(bias_t, d), -jnp.inf), t)
    sub = _toeplitz(_bias_of(bias_t, d + t), t)
    far = _toeplitz(_bias_of(bias_t, d + 2 * t), t)
    return jnp.stack([diag, sub, far], axis=1)


def _page_stream(pt_ref, pk_ref, pv_ref, buf_ref, sem_ref, n_seq, n_chunks):
    pps = PAGES_PER_STEP
    ahead = RING_SLOTS - 1
    b, ph, j = pl.program_id(0), pl.program_id(1), pl.program_id(2)
    step = (b * 2 + ph) * n_chunks + j
    total = n_seq * 2 * n_chunks

    def copies(pool_ref, bb, jj, slot):
        return [pltpu.make_async_copy(pool_ref.at[pt_ref[bb, jj * pps + i]],
                                      buf_ref.at[slot, i], sem_ref.at[slot])
                for i in range(pps)]

    def start(chunk):
        jj, seq_phase = chunk % n_chunks, chunk // n_chunks
        for phase, pool_ref in ((0, pk_ref), (1, pv_ref)):
            @pl.when(seq_phase % 2 == phase)
            def _(pool_ref=pool_ref):
                for i, cp in enumerate(copies(pool_ref, seq_phase // 2, jj, chunk % RING_SLOTS)):
                    cp.start(priority=i % 2)

    @pl.when(step == 0)
    def _():
        for chunk in range(min(ahead, total)):
            start(jnp.int32(chunk))

    @pl.when(step + ahead < total)
    def _():
        start(step + ahead)

    slot = step % RING_SLOTS

    def wait(pool_ref):
        for cp in copies(pool_ref, b, j, slot):
            cp.wait()

    return slot, wait


def _moba_sample_kernel(pt_ref, pk_ref, pv_ref, q_ref, kn_ref, vn_ref, bias_ref, o_ref,
                        s_ref, snew_ref, acc_ref, l_ref, m_ref, bmax_ref, gate_ref, sel_ref,
                        buf_ref, sem_ref, *, n_seq, n_chunks, tokens):
    pps = PAGES_PER_STEP
    slot, wait = _page_stream(pt_ref, pk_ref, pv_ref, buf_ref, sem_ref, n_seq, n_chunks)
    rows = q_ref.shape[1]
    groups = rows // tokens
    ppb = MOBA_BLOCK // PAGE_SIZE
    bps = pps // ppb
    n_blocks = n_chunks * bps
    ph = pl.program_id(1)
    j = pl.program_id(2)
    lane = lax.broadcasted_iota(jnp.int32, (rows, LANES), 1)
    lanef = lane.astype(F32)
    q = q_ref[0]

    @pl.when((ph == 0) & (j == 0))
    def _():
        snew_ref[...] = _dot_nt(q, kn_ref[0]) + bias_ref[2]
        bmax_ref[...] = jnp.full((rows, LANES), -jnp.inf, F32)
        gate_ref[...] = jnp.zeros((rows, LANES), F32)

    @pl.when(ph == 0)
    def _():
        wait(pk_ref)
        is_last = j == n_chunks - 1
        bmax = bmax_ref[...]
        gate = gate_ref[...]
        for b in range(bps):
            smax = ssum = None
            for pg in range(ppb):
                i = b * ppb + pg
                raw = _dot(q, buf_ref[slot, i].astype(BF16))
                bias = bias_ref[0]
                if i == pps - 1:
                    bias = jnp.where(is_last, bias_ref[1], bias)
                s = raw + bias
                s_ref[j, :, i * PAGE_SIZE:(i + 1) * PAGE_SIZE] = s
                smax = s if smax is None else jnp.maximum(smax, s)
                ssum = raw if ssum is None else ssum + raw
            blk = j * bps + b
            bmax = jnp.where(lane == blk, jnp.max(smax, axis=1, keepdims=True), bmax)
            gate = jnp.where(lane == blk, jnp.sum(ssum, axis=1, keepdims=True), gate)
        bmax_ref[...] = bmax
        gate_ref[...] = gate

    @pl.when((ph == 1) & (j == 0))
    def _():
        g = jnp.where(lane < n_blocks, gate_ref[...], -jnp.inf)
        sel = jnp.zeros((rows, LANES), F32)
        for _ in range(MOBA_TOPK):
            top = jnp.max(g, axis=1, keepdims=True)
            first = jnp.min(jnp.where(g == top, lanef, float(LANES)), axis=1, keepdims=True)
            pick = lanef == first
            finite = jnp.where(jnp.abs(top) < jnp.inf, 1.0, 0.0)
            sel = jnp.maximum(sel, jnp.where(pick, finite, 0.0))
            g = jnp.where(pick, -jnp.inf, g)
        sel_ref[...] = sel
        m_sel = jnp.max(jnp.where(sel > 0.5, bmax_ref[...], -jnp.inf), axis=1, keepdims=True)
        m_new = jnp.max(snew_ref[...], axis=1, keepdims=True)
        m_ref[...] = jnp.broadcast_to(jnp.maximum(m_sel, m_new), (rows, LANES))
        acc_ref[...] = jnp.zeros(acc_ref.shape, F32)
        l_ref[...] = jnp.zeros(l_ref.shape, F32)

    @pl.when(ph == 1)
    def _():
        wait(pv_ref)
        m = m_ref[...]
        selv = sel_ref[...]
        acc = acc_ref[...]
        l = l_ref[...]
        for i in range(pps):
            blk = j * bps + i // ppb
            on = jnp.max(jnp.where(lane == blk, selv, 0.0), axis=1, keepdims=True)
            on = jnp.broadcast_to(on, (rows, LANES)) > 0.5
            s = s_ref[j, :, i * PAGE_SIZE:(i + 1) * PAGE_SIZE]
            p = jnp.exp(jnp.where(on, s - m, -jnp.inf))
            l = l + p
            acc = acc + _dot_nt(p.astype(BF16), buf_ref[slot, i].astype(BF16))
        acc_ref[...] = acc
        l_ref[...] = l

    @pl.when((ph == 1) & (j == n_chunks - 1))
    def _():
        p = jnp.exp(snew_ref[...] - m_ref[...])
        acc = acc_ref[...] + _dot(p.astype(BF16), vn_ref[0])
        l = jnp.sum(l_ref[...] + p, axis=1, keepdims=True)
        o = acc / l
        grp = lax.broadcasted_iota(jnp.int32, (groups, WIDTH_A), 0)
        col = lax.broadcasted_iota(jnp.int32, (groups, WIDTH_A), 1)
        head_shift = HEAD_DIM.bit_length() - 1
        w = jnp.where((col >> head_shift) == grp, 1.0, 0.0)
        tok = jnp.sum(o.reshape(tokens, groups, WIDTH_A) * w[None], axis=1)
        o_ref[0] = tok.astype(o_ref.dtype)


def _moba_sample(page_table, pool_kt, pool_vt, q_bd, k_new, v_new, bias_rows, *, tokens):
    pps = PAGES_PER_STEP
    n_seq, n_pages = page_table.shape
    n_chunks = n_pages // pps
    rows = q_bd.shape[1]
    seq_map = lambda b, ph, j, pt: (b, 0, 0)
    hbm = pl.BlockSpec(memory_space=pl.ANY)
    in_specs = [hbm, hbm,
                pl.BlockSpec((1, rows, WIDTH_A), seq_map),
                pl.BlockSpec((1, LANES, WIDTH_A), seq_map),
                pl.BlockSpec((1, LANES, WIDTH_A), seq_map),
                pl.BlockSpec((3, rows, LANES), lambda b, ph, j, pt: (0, 0, 0))]
    vec = pltpu.VMEM((rows, LANES), F32)
    grid_spec = pltpu.PrefetchScalarGridSpec(
        num_scalar_prefetch=1,
        grid=(n_seq, 2, n_chunks),
        in_specs=in_specs,
        out_specs=pl.BlockSpec((1, tokens, WIDTH_A), seq_map),
        scratch_shapes=[pltpu.VMEM((n_chunks, rows, pps * PAGE_SIZE), F32),
                        vec,
                        pltpu.VMEM((rows, WIDTH_A), F32),
                        vec, vec, vec, vec, vec,
                        pltpu.VMEM((RING_SLOTS, pps) + pool_kt.shape[1:], F32),
                        pltpu.SemaphoreType.DMA((RING_SLOTS,))])
    return pl.pallas_call(
        functools.partial(_moba_sample_kernel, n_seq=n_seq, n_chunks=n_chunks, tokens=tokens),
        grid_spec=grid_spec,
        out_shape=jax.ShapeDtypeStruct((n_seq, tokens, WIDTH_A), BF16),
        compiler_params=pltpu.CompilerParams(
            dimension_semantics=("arbitrary", "arbitrary", "arbitrary"),
            vmem_limit_bytes=VMEM_LIMIT_BYTES),
        name="moba_sample",
    )(page_table, pool_kt, pool_vt, q_bd, k_new, v_new, bias_rows)


def _diff_sample_kernel(pt_ref, pk_ref, pv_ref, q_ref, kn_ref, vn_ref, bias_ref, bnew_ref,
                        lam_ref, gain_ref, o_ref, s_ref, snew_ref, acc_ref, l_ref, m_ref,
                        buf_ref, sem_ref, *, n_seq, n_chunks, tokens, lam_init):
    pps = PAGES_PER_STEP
    slot, wait = _page_stream(pt_ref, pk_ref, pv_ref, buf_ref, sem_ref, n_seq, n_chunks)
    rows = q_ref.shape[1]
    pw = buf_ref.shape[2]
    ph = pl.program_id(1)
    j = pl.program_id(2)
    q = q_ref[0]

    @pl.when((ph == 0) & (j == 0))
    def _():
        snew_ref[...] = _dot_nt(q, kn_ref[0]) + bnew_ref[...]
        m_ref[...] = jnp.full((rows, LANES), -jnp.inf, F32)

    @pl.when(ph == 0)
    def _():
        wait(pk_ref)
        is_last = j == n_chunks - 1
        mx = m_ref[...]
        for i in range(pps):
            bias = bias_ref[0]
            if i == pps - 1:
                bias = jnp.where(is_last, bias_ref[1], bias)
            s = _dot_nt(q, buf_ref[slot, i].astype(BF16)) + bias
            s_ref[j, :, i * pw:(i + 1) * pw] = s
            mx = jnp.maximum(mx, _lane_fold(jnp.maximum, s))
        m_ref[...] = mx

    @pl.when((ph == 1) & (j == 0))
    def _():
        m = jnp.maximum(jnp.max(m_ref[...], axis=1, keepdims=True),
                        jnp.max(snew_ref[...], axis=1, keepdims=True))
        m_ref[...] = jnp.broadcast_to(m, (rows, LANES))
        acc_ref[...] = jnp.zeros(acc_ref.shape, F32)
        l_ref[...] = jnp.zeros(l_ref.shape, F32)

    @pl.when(ph == 1)
    def _():
        wait(pv_ref)
        m = m_ref[...]
        mw = jnp.concatenate([m] * (pw // LANES), axis=1)
        acc = acc_ref[...]
        l = l_ref[...]
        for i in range(pps):
            p = jnp.exp(s_ref[j, :, i * pw:(i + 1) * pw] - mw)
            l = l + _lane_fold(jnp.add, p)
            acc = acc + _dot(p.astype(BF16), buf_ref[slot, i].astype(BF16))
        acc_ref[...] = acc
        l_ref[...] = l

    @pl.when((ph == 1) & (j == n_chunks - 1))
    def _():
        p = jnp.exp(snew_ref[...] - m_ref[...])
        acc = acc_ref[...] + _dot(p.astype(BF16), vn_ref[0])
        l = jnp.sum(l_ref[...] + p, axis=1, keepdims=True)
        o = (acc / l).reshape(tokens, rows // tokens, LANES)
        lam = _lambda(lam_ref, lam_init)
        sub = lax.broadcasted_iota(jnp.int32, (rows // tokens, LANES), 0)
        gain = gain_ref[...]
        parts = []
        for h in range(N_HEADS_B):
            w = jnp.where(sub == 2 * h, 1.0, jnp.where(sub == 2 * h + 1, -lam, 0.0))
            parts.append(_rmsnorm(jnp.sum(o * w[None], axis=1), gain) * (1.0 - lam_init))
        o_ref[0] = jnp.concatenate(parts, axis=1).astype(o_ref.dtype)


def _diff_sample(page_table, pool_k, pool_v, q2, k_new, v_new, bias_rows, bias_new, lam_vecs, gain,
                 *, tokens, lam_init):
    pps = PAGES_PER_STEP
    n_seq, n_pages = page_table.shape
    n_chunks = n_pages // pps
    rows = q2.shape[1]
    pw = pool_k.shape[1]
    seq_map = lambda b, ph, j, pt: (b, 0, 0)
    const2 = lambda b, ph, j, pt: (0, 0)
    hbm = pl.BlockSpec(memory_space=pl.ANY)
    in_specs = [hbm, hbm,
                pl.BlockSpec((1, rows, LANES), seq_map),
                pl.BlockSpec((1, LANES, LANES), seq_map),
                pl.BlockSpec((1, LANES, LANES), seq_map),
                pl.BlockSpec((2, rows, pw), lambda b, ph, j, pt: (0, 0, 0)),
                pl.BlockSpec((rows, LANES), const2),
                pl.BlockSpec((4, HEAD_DIM), const2),
                pl.BlockSpec((1, 2 * HEAD_DIM), const2)]
    vec = pltpu.VMEM((rows, LANES), F32)
    grid_spec = pltpu.PrefetchScalarGridSpec(
        num_scalar_prefetch=1,
        grid=(n_seq, 2, n_chunks),
        in_specs=in_specs,
        out_specs=pl.BlockSpec((1, tokens, WIDTH_B), seq_map),
        scratch_shapes=[pltpu.VMEM((n_chunks, rows, pps * pw), F32),
                        vec, vec, vec, vec,
                        pltpu.VMEM((RING_SLOTS, pps) + pool_k.shape[1:], F32),
                        pltpu.SemaphoreType.DMA((RING_SLOTS,))])
    return pl.pallas_call(
        functools.partial(_diff_sample_kernel, n_seq=n_seq, n_chunks=n_chunks, tokens=tokens,
                          lam_init=lam_init),
        grid_spec=grid_spec,
        out_shape=jax.ShapeDtypeStruct((n_seq, tokens, WIDTH_B), BF16),
        compiler_params=pltpu.CompilerParams(
            dimension_semantics=("arbitrary", "arbitrary", "arbitrary"),
            vmem_limit_bytes=VMEM_LIMIT_BYTES),
        name="diff_sample",
    )(page_table, pool_k, pool_v, q2, k_new, v_new, bias_rows, bias_new, lam_vecs, gain)


def _block_diag(q, n_seq, tokens, groups, keep):
    w = q.shape[1] // groups
    q4 = q.reshape(n_seq, tokens, groups, 1, w)
    eye = (jnp.arange(groups)[:, None] % keep == jnp.arange(keep)[None, :]).astype(q.dtype)
    return (q4 * eye.reshape(1, 1, groups, keep, 1)).reshape(n_seq, tokens * groups, keep * w)


def _pad_rows(x, n_seq):
    x = x.reshape(n_seq, -1, x.shape[-1])
    return jnp.pad(x, ((0, 0), (0, LANES - x.shape[1]), (0, 0)))


def _sample_bias(bias_t, tokens, heads_per_group):
    assert PAGE_SIZE + 1 >= MAX_DISTANCE
    tok = jnp.arange(tokens)[:, None]
    col = jnp.arange(LANES)[None, :]
    far = _bias_of(bias_t, jnp.broadcast_to(2 * PAGE_SIZE, (tokens, LANES)))
    last = _bias_of(bias_t, PAGE_SIZE + tok - col)
    new = jnp.where((col <= tok) & (col < tokens), _bias_of(bias_t, tok - col), -jnp.inf)

    def rows(x):
        x = jnp.repeat(x.transpose(1, 0, 2), heads_per_group, axis=1)
        return x.reshape(-1, LANES)

    return rows(far), rows(last), rows(new)


def _spread_heads(x, n_heads, groups):
    r, k = x.shape
    row_head = (jnp.arange(r) % groups) // (groups // n_heads)
    own = row_head[:, None, None] == jnp.arange(n_heads)[None, None, :]
    return jnp.where(own, x[:, :, None], -jnp.inf).reshape(r, k * n_heads)


def _merge_kernel(x_ref, oa_ref, ob_ref, g_ref, wg_ref, woa_ref, wob_ref, wo_ref, o_ref):
    x = x_ref[...]
    h = _rmsnorm(x, g_ref[...]).astype(BF16)
    ga = 1.0 / (1.0 + jnp.exp(-_dot(h, wg_ref[:, :D_MODEL])))
    m = ga * _dot(oa_ref[...], woa_ref[...])
    gb = 1.0 / (1.0 + jnp.exp(-_dot(h, wg_ref[:, D_MODEL:])))
    m = m + gb * _dot(ob_ref[...], wob_ref[...])
    o_ref[...] = x + _dot(m.astype(BF16), wo_ref[...])


def _merge(x, oa, ob, g, wg, woa, wob, wo, tm):
    n = x.shape[0]
    row = lambda i: (i, 0)
    const = lambda i: (0, 0)
    return pl.pallas_call(
        _merge_kernel,
        grid=(n // tm,),
        in_specs=[pl.BlockSpec((tm, D_MODEL), row),
                  pl.BlockSpec((tm, WIDTH_A), row),
                  pl.BlockSpec((tm, WIDTH_B), row),
                  pl.BlockSpec((1, D_MODEL), const),
                  pl.BlockSpec((D_MODEL, 2 * D_MODEL), const),
                  pl.BlockSpec((WIDTH_A, D_MODEL), const),
                  pl.BlockSpec((WIDTH_B, D_MODEL), const),
                  pl.BlockSpec((D_MODEL, D_MODEL), const)],
        out_specs=pl.BlockSpec((tm, D_MODEL), row),
        out_shape=jax.ShapeDtypeStruct((n, D_MODEL), F32),
        compiler_params=pltpu.CompilerParams(
            dimension_semantics=("arbitrary",), vmem_limit_bytes=VMEM_LIMIT_BYTES),
        name="merge",
    )(x, oa, ob, g, wg, woa, wob, wo)


def _ffn_kernel(*refs, tm, seq_len, has_prev):
    if has_prev:
        (x_ref, g_ref, wup_ref, cw_ref, cb_ref, wdn_ref, gfin_ref, e1_ref, e2_ref,
         y_ref, u_ref, act_ref) = refs
    else:
        (x_ref, g_ref, wup_ref, cw_ref, cb_ref, wdn_ref, gfin_ref,
         y_ref, u_ref, act_ref, carry_ref) = refs
    i = pl.program_id(0)
    x = x_ref[...]
    h = _rmsnorm(x, g_ref[...]).astype(BF16)
    row = lax.broadcasted_iota(jnp.int32, (tm, FF_CHUNK), 0)
    row8 = lax.broadcasted_iota(jnp.int32, (SUBLANES, FF_CHUNK), 0)
    if has_prev:
        pos = row & (seq_len - 1)
    else:
        @pl.when((i * tm) % seq_len == 0)
        def _():
            carry_ref[...] = jnp.zeros(carry_ref.shape, F32)

    for c in range(D_FF // FF_CHUNK):
        halves = []
        for part in range(2):
            cols = slice(part * D_FF + c * FF_CHUNK, part * D_FF + (c + 1) * FF_CHUNK)
            u = _dot(h, wup_ref[:, cols])
            um1 = pltpu.roll(u, 1, 0)
            um2 = pltpu.roll(u, 2, 0)
            if has_prev:
                um1 = jnp.where(pos == 0, e1_ref[:, cols], um1)
                um2 = jnp.where(pos < 2, e2_ref[:, cols], um2)
                u_ref[:, cols] = u
            else:
                prev = carry_ref[:, cols]
                top1 = jnp.where(row8 == 0, pltpu.roll(prev, 1, 0), um1[:SUBLANES])
                top2 = jnp.where(row8 < 2, pltpu.roll(prev, 2, 0), um2[:SUBLANES])
                um1 = jnp.concatenate([top1, um1[SUBLANES:]], axis=0)
                um2 = jnp.concatenate([top2, um2[SUBLANES:]], axis=0)
                carry_ref[:, cols] = u[tm - SUBLANES:]
                u_ref[:, cols] = u[tm - SUBLANES:]
            cw = cw_ref[:, cols]
            halves.append(((cb_ref[:, cols] + cw[0:1] * um2) + cw[1:2] * um1) + cw[2:3] * u)
        gate, val = halves
        act = (gate * (1.0 / (1.0 + jnp.exp(-gate)))) * val
        act_ref[:, c * FF_CHUNK:(c + 1) * FF_CHUNK] = act.astype(BF16)

    x3 = x + _dot(act_ref[...], wdn_ref[...])
    y_ref[...] = _rmsnorm(x3, gfin_ref[...])


def _ffn(x, g, wup, cw, cb, wdn, gfin, prev, tm, seq_len):
    n = x.shape[0]
    has_prev = prev is not None
    row = lambda i: (i, 0)
    const = lambda i: (0, 0)
    in_specs = [pl.BlockSpec((tm, D_MODEL), row),
                pl.BlockSpec((1, D_MODEL), const),
                pl.BlockSpec((D_MODEL, 2 * D_FF), const),
                pl.BlockSpec((CONV_W, 2 * D_FF), const),
                pl.BlockSpec((1, 2 * D_FF), const),
                pl.BlockSpec((D_FF, D_MODEL), const),
                pl.BlockSpec((1, D_MODEL), const)]
    scratch = [pltpu.VMEM((tm, D_FF), BF16)]
    if has_prev:
        assert n == tm and tm % seq_len == 0 and seq_len & (seq_len - 1) == 0
        in_specs += [pl.BlockSpec((tm, 2 * D_FF), row)] * 2
        u_shape, u_spec = (n, 2 * D_FF), pl.BlockSpec((tm, 2 * D_FF), row)
        args = (x, g, wup, cw, cb, wdn, gfin) + tuple(prev)
    else:
        assert seq_len % tm == 0
        tiles_per_seq = seq_len // tm
        u_shape = (n // seq_len * SUBLANES, 2 * D_FF)
        u_spec = pl.BlockSpec((SUBLANES, 2 * D_FF), lambda i: (i // tiles_per_seq, 0))
        scratch.append(pltpu.VMEM((SUBLANES, 2 * D_FF), F32))
        args = (x, g, wup, cw, cb, wdn, gfin)
    return pl.pallas_call(
        functools.partial(_ffn_kernel, tm=tm, seq_len=seq_len, has_prev=has_prev),
        grid=(n // tm,),
        in_specs=in_specs,
        out_specs=[pl.BlockSpec((tm, D_MODEL), row), u_spec],
        out_shape=[jax.ShapeDtypeStruct((n, D_MODEL), F32), jax.ShapeDtypeStruct(u_shape, F32)],
        scratch_shapes=scratch,
        compiler_params=pltpu.CompilerParams(
            dimension_semantics=("arbitrary",), vmem_limit_bytes=VMEM_LIMIT_BYTES),
        name="ffn_sample" if has_prev else "ffn_prompt",
    )(*args)


def kernel(x_prompt, x_sample, cache_moba_k, cache_moba_v, cache_diff_k, cache_diff_v, state_conv, page_table, rel_bias, norm_attn, w_in, w_gate, w_out_a, w_out_b, w_out, lambda_q1, lambda_k1, lambda_q2, lambda_k2, diff_norm, norm_ffn, w_up, conv_w, conv_b, w_down, norm_final):
    batch, seq, _ = x_prompt.shape
    n_seq, tokens, _ = x_sample.shape
    depth = w_in.shape[0]
    n_phys = cache_moba_k.shape[1]
    n_pages = page_table.shape[1]
    assert depth == 1 and seq % Q_TILE == 0 and ATT_TILE & (ATT_TILE - 1) == 0
    assert (n_pages * PAGE_SIZE) % MOBA_BLOCK == 0 and n_pages % PAGES_PER_STEP == 0
    assert n_pages * PAGE_SIZE // MOBA_BLOCK <= LANES and CONV_W - 1 <= tokens <= LANES // N_HEADS_B
    l = 0
    lam_init = 0.8 - 0.6 * math.exp(-0.3 * l)

    bias_a = rel_bias[:, :N_HEADS_A].T
    bias_d = rel_bias[:, N_HEADS_A:].T
    row = lambda v: v.reshape(1, -1)
    w_in_b = w_in[l].astype(BF16)
    w_gate_b = w_gate[l].astype(BF16)
    w_oa_b = w_out_a[l].astype(BF16)
    w_ob_b = w_out_b[l].astype(BF16)
    w_o_b = w_out[l].astype(BF16)
    w_up_b = w_up[l].astype(BF16)
    w_dn_b = w_down[l].astype(BF16)
    lam_vecs = jnp.stack([lambda_q1[l], lambda_k1[l], lambda_q2[l], lambda_k2[l]]).astype(F32)
    gain_d = row(diff_norm[l])
    g_attn, g_ffn, g_fin = row(norm_attn[l]), row(norm_ffn[l]), row(norm_final)
    cb = row(conv_b[l])

    xp = x_prompt.reshape(batch * seq, D_MODEL)
    qa, ka, va, qd, kd, vd, ka_f, va_f, kd_f, vd_f = _proj(xp, g_attn, w_in_b, 512)
    oa = _moba_prompt(qa, ka, va, _bias_tiles(bias_a), batch, seq)
    ob = _diff_prompt(qd, kd, vd, _bias_tiles(bias_d), lam_vecs, gain_d, lam_init, batch, seq)
    x2 = _merge(xp, oa, ob, g_attn, w_gate_b, w_oa_b, w_ob_b, w_o_b, 512)
    yp, tail_p = _ffn(x2, g_ffn, w_up_b, conv_w[l], cb, w_dn_b, g_fin, None, 512, seq)
    conv_p = tail_p.reshape(batch, SUBLANES, 2 * D_FF)[:, SUBLANES - (CONV_W - 1):]

    n_s = n_seq * tokens
    xs = x_sample.reshape(n_s, D_MODEL)
    qa_s, ka_s, va_s, qd_s, kd_s, vd_s, ka_sf, va_sf, kd_sf, vd_sf = _proj(xs, g_attn, w_in_b, n_s)

    pool_t = lambda c: jnp.transpose(c[l], (0, 2, 3, 1)).reshape(n_phys, WIDTH_A, PAGE_SIZE)
    pool_r = lambda c: c[l].reshape(n_phys, PAGE_SIZE * N_HEADS_B, 2 * HEAD_DIM)
    far_a, last_a, new_a = _sample_bias(bias_a, tokens, 1)
    oa_s = _moba_sample(page_table, pool_t(cache_moba_k), pool_t(cache_moba_v),
                        _block_diag(qa_s, n_seq, tokens, N_HEADS_A, N_HEADS_A),
                        _pad_rows(ka_s, n_seq), _pad_rows(va_s, n_seq),
                        jnp.stack([far_a, last_a, new_a]), tokens=tokens)
    far_d, last_d, new_d = _sample_bias(bias_d, tokens, 2)
    groups_d = 2 * N_HEADS_B
    spread = lambda x: _spread_heads(x, N_HEADS_B, groups_d)
    per_head = lambda x: x.reshape(n_s * N_HEADS_B, 2 * HEAD_DIM)
    oa_d = _diff_sample(page_table, pool_r(cache_diff_k), pool_r(cache_diff_v),
                        _block_diag(qd_s, n_seq, tokens, groups_d, 2),
                        _pad_rows(per_head(kd_s), n_seq), _pad_rows(per_head(vd_s), n_seq),
                        jnp.stack([spread(far_d), spread(last_d)]),
                        spread(new_d[:, :LANES // N_HEADS_B]),
                        lam_vecs, gain_d, tokens=tokens, lam_init=lam_init)
    x2s = _merge(xs, oa_s.reshape(n_s, WIDTH_A), oa_d.reshape(n_s, WIDTH_B), g_attn,
                 w_gate_b, w_oa_b, w_ob_b, w_o_b, n_s)
    st = state_conv[l]
    zero = jnp.zeros((n_seq, 1, 2 * D_FF), F32)
    e1 = jnp.concatenate([st[:, 1:2]] + [zero] * (tokens - 1), axis=1).reshape(n_s, 2 * D_FF)
    e2 = jnp.concatenate([st[:, 0:2]] + [zero] * (tokens - 2), axis=1).reshape(n_s, 2 * D_FF)
    ys, u_s = _ffn(x2s, g_ffn, w_up_b, conv_w[l], cb, w_dn_b, g_fin, (e1, e2), n_s, tokens)
    conv_s = u_s.reshape(n_seq, tokens, 2 * D_FF)[:, tokens - (CONV_W - 1):]

    shp_a = lambda a, b_, t_: a.reshape(1, b_, t_, N_HEADS_A, HEAD_DIM)
    shp_d = lambda a, b_, t_: a.reshape(1, b_, t_, N_HEADS_B, 2 * HEAD_DIM)
    return (yp.reshape(batch, seq, D_MODEL), ys.reshape(n_seq, tokens, D_MODEL),
            shp_a(ka_f, batch, seq), shp_a(va_f, batch, seq),
            shp_d(kd_f, batch, seq), shp_d(vd_f, batch, seq), conv_p[None],
            shp_a(ka_sf, n_seq, tokens), shp_a(va_sf, n_seq, tokens),
            shp_d(kd_sf, n_seq, tokens), shp_d(vd_sf, n_seq, tokens), conv_s[None])
```

```python
import functools
import math

import jax
import jax.numpy as jnp
from jax import lax
from jax.experimental import pallas as pl
from jax.experimental.pallas import tpu as pltpu

F32 = jnp.float32
BF16 = jnp.bfloat16

D_MODEL = 1024
HEAD_DIM = 64
N_HEADS_A = D_MODEL // 128
N_HEADS_B = D_MODEL // 256
WIDTH_A = N_HEADS_A * HEAD_DIM
WIDTH_B = N_HEADS_B * 2 * HEAD_DIM
N_IN = 3 * WIDTH_A + 3 * WIDTH_B
MOBA_BLOCK = 256
MOBA_TOPK = 3
NUM_BUCKETS = 32
MAX_DISTANCE = 128
D_FF = ((8 * D_MODEL // 3 + 127) // 128) * 128
CONV_W = 3
EPS = 1e-6
PAGE_SIZE = 128
SCALE = HEAD_DIM ** -0.5

LANES = 128
SUBLANES = 8
VMEM_LIMIT_BYTES = 56 * 1024 * 1024

ATT_TILE = MOBA_BLOCK
Q_TILE = 2 * ATT_TILE
FF_CHUNK = 256
PAGES_PER_STEP = 16
RING_SLOTS = 3
NEG_BIG = -1e30


def _dot(a, b):
    return jnp.dot(a, b, preferred_element_type=F32)


def _dot_nt(a, b):
    return lax.dot_general(a, b, (((1,), (1,)), ((), ())), preferred_element_type=F32)


def _rmsnorm(x, g):
    return (x * lax.rsqrt(jnp.mean(x * x, axis=-1, keepdims=True) + EPS)) * g


def _lane_fold(op, x):
    out = x[:, :LANES]
    for c in range(1, x.shape[1] // LANES):
        out = op(out, x[:, c * LANES:(c + 1) * LANES])
    return out


def _t5_bucket(dist):
    n = jnp.maximum(dist, 0)
    max_exact = NUM_BUCKETS // 2
    nf = jnp.maximum(n, 1).astype(F32)
    large = max_exact + (jnp.log(nf / max_exact) / math.log(MAX_DISTANCE / max_exact)
                         * (NUM_BUCKETS - max_exact)).astype(jnp.int32)
    large = jnp.minimum(large, NUM_BUCKETS - 1)
    return jnp.where(n < max_exact, n, large)


def _bias_of(bias_t, dist):
    onehot = jax.nn.one_hot(_t5_bucket(dist), NUM_BUCKETS, dtype=F32)
    return jnp.einsum('hb,...b->h...', bias_t.astype(F32), onehot, precision=lax.Precision.HIGHEST)


def _lambda(lam_ref, lam_init):
    lv = lam_ref[...]
    a = jnp.sum(lv[0:1] * lv[1:2], axis=-1, keepdims=True)
    b = jnp.sum(lv[2:3] * lv[3:4], axis=-1, keepdims=True)
    return jnp.exp(a) - jnp.exp(b) + lam_init


def _proj_kernel(x_ref, g_ref, w_ref,
                 qa_ref, ka_ref, va_ref, qd_ref, kd_ref, vd_ref,
                 kaf_ref, vaf_ref, kdf_ref, vdf_ref, *, cache_layout):
    tm = x_ref.shape[0]
    h = _rmsnorm(x_ref[...], g_ref[...]).astype(BF16)
    outs = ((qa_ref, None, SCALE), (ka_ref, kaf_ref, None), (va_ref, vaf_ref, None),
            (qd_ref, None, SCALE), (kd_ref, kdf_ref, None), (vd_ref, vdf_ref, None))
    for c, (b_ref, f_ref, scale) in enumerate(outs):
        u = _dot(h, w_ref[:, c * WIDTH_A:(c + 1) * WIDTH_A])
        if f_ref is not None:
            if not cache_layout:
                f_ref[...] = u
            elif c < 3:
                f_ref[0] = u.T
            else:
                for hd in range(N_HEADS_B):
                    f_ref[pl.ds(hd, tm, stride=N_HEADS_B), :] = u[:, hd * LANES:(hd + 1) * LANES]
        if scale is not None:
            u = u * scale
        b_ref[...] = u.astype(BF16)


def _proj(x, g, w_bf16, tm, seq=None):
    n = x.shape[0]
    row = lambda i: (i, 0)
    const = lambda i: (0, 0)
    blk = pl.BlockSpec((tm, WIDTH_A), row)
    f_specs = [blk] * 4
    f_shapes = [jax.ShapeDtypeStruct((n, WIDTH_A), F32)] * 4
    if seq is not None:
        assert seq % tm == 0 and 2 * HEAD_DIM == LANES
        tiles = seq // tm
        t_spec = pl.BlockSpec((1, WIDTH_A, tm), lambda i: (i // tiles, 0, i % tiles))
        t_shape = jax.ShapeDtypeStruct((n // seq, WIDTH_A, seq), F32)
        r_spec = pl.BlockSpec((tm * N_HEADS_B, LANES), row)
        r_shape = jax.ShapeDtypeStruct((n * N_HEADS_B, LANES), F32)
        f_specs = [t_spec, t_spec, r_spec, r_spec]
        f_shapes = [t_shape, t_shape, r_shape, r_shape]
    return pl.pallas_call(
        functools.partial(_proj_kernel, cache_layout=seq is not None),
        grid=(n // tm,),
        in_specs=[pl.BlockSpec((tm, D_MODEL), row),
                  pl.BlockSpec((1, D_MODEL), const),
                  pl.BlockSpec((D_MODEL, N_IN), const)],
        out_specs=[blk] * 6 + f_specs,
        out_shape=[jax.ShapeDtypeStruct((n, WIDTH_A), BF16)] * 6 + f_shapes,
        compiler_params=pltpu.CompilerParams(
            dimension_semantics=("arbitrary",), vmem_limit_bytes=VMEM_LIMIT_BYTES),
        name="proj",
    )(x, g, w_bf16)


def _attend_static(q_ops, k_ref, v_ref, bias_ref, bidx, s_refs, p_refs, mb_ref, qi):
    t = ATT_TILE
    tq = Q_TILE
    halves = tq // t
    trips = qi + 1
    outs = []
    for i, q_op in enumerate(q_ops):
        s_ref, p_ref = s_refs[i], p_refs[i]
        mx = [None] * halves
        for jj in range(trips):
            s = _dot_nt(q_op, k_ref[jj * tq:(jj + 1) * tq, :])
            for hq in range(halves):
                for hk in range(halves):
                    dist = (halves * qi + hq) - (halves * jj + hk)
                    if dist < 0:
                        continue
                    blk = (s[hq * t:(hq + 1) * t, hk * t:(hk + 1) * t]
                           + bias_ref[bidx[i], min(dist, 2)])
                    s_ref[jj, hq * t:(hq + 1) * t, hk * t:(hk + 1) * t] = blk
                    f = _lane_fold(jnp.maximum, blk)
                    mx[hq] = f if mx[hq] is None else jnp.maximum(mx[hq], f)
        m = jnp.max(jnp.concatenate(mx, axis=0), axis=1, keepdims=True)
        mb_ref[...] = jnp.broadcast_to(m, (tq, LANES))
        l = [jnp.zeros((t, LANES), F32) for _ in range(halves)]
        for jj in range(trips):
            for hq in range(halves):
                mb = mb_ref[hq * t:(hq + 1) * t, :]
                for hk in range(halves):
                    dist = (halves * qi + hq) - (halves * jj + hk)
                    rows, cols = slice(hq * t, (hq + 1) * t), slice(jj * tq + hk * t, jj * tq + (hk + 1) * t)
                    if dist < 0:
                        p_ref[rows, cols] = jnp.zeros((t, t), BF16)
                        continue
                    p = jnp.exp(s_ref[jj, rows, hk * t:(hk + 1) * t] - jnp.concatenate([mb] * (t // LANES), axis=1))
                    l[hq] = l[hq] + _lane_fold(jnp.add, p)
                    p_ref[rows, cols] = p.astype(BF16)
        acc = _dot(p_ref[:, :trips * tq], v_ref[:trips * tq, :])
        outs.append((acc, jnp.sum(jnp.concatenate(l, axis=0), axis=1, keepdims=True)))
    return outs


def _per_query_tile(nq, body):
    qi = pl.program_id(2)
    for n in range(nq):
        pl.when(qi == n)(functools.partial(body, n))


def _moba_prompt_kernel(q_ref, k_ref, v_ref, bias_ref, o_ref,
                        s0_ref, s1_ref, p0_ref, p1_ref, mb_ref, km_ref, kaug_ref, *, nq):
    t = ATT_TILE
    tq = Q_TILE
    tile_shift = t.bit_length() - 1
    nb = km_ref.shape[0]
    qi = pl.program_id(2)

    @pl.when(qi == 0)
    def _():
        for j in range(nb):
            km_ref[j:j + 1, :] = jnp.sum(k_ref[j * t:(j + 1) * t, :].astype(F32),
                                         axis=0, keepdims=True) * (1.0 / t)
        seq = k_ref.shape[0]
        row_blk = lax.broadcasted_iota(jnp.int32, (seq, LANES), 0) >> tile_shift
        col = lax.broadcasted_iota(jnp.int32, (seq, LANES), 1)
        kaug_ref[:, :LANES] = k_ref[...]
        kaug_ref[:, LANES:] = jnp.where(col == row_blk, 1.0, 0.0).astype(BF16)

    q = q_ref[...].astype(F32)
    km = km_ref[...]
    km_hi = km.astype(BF16)
    km_lo = (km - km_hi.astype(F32)).astype(BF16)
    lane = lax.broadcasted_iota(jnp.int32, (tq, LANES), 1)
    blk = lax.broadcasted_iota(jnp.int32, (nb, tq), 0)
    own = (tq // t) * qi + (lax.broadcasted_iota(jnp.int32, (nb, tq), 1) >> tile_shift)

    q_ops = []
    for e in range(2):
        qe = jnp.where((lane >= HEAD_DIM * e) & (lane < HEAD_DIM * (e + 1)), q, 0.0).astype(BF16)
        gt = _dot_nt(km_hi, qe) + _dot_nt(km_lo, qe)
        rank = jnp.zeros((nb, tq), jnp.int32)
        for m in range(nb):
            gm = gt[m:m + 1, :]
            ahead = (gm > gt) | ((gm == gt) & (m < blk))
            rank = rank + jnp.where(ahead & (m < own), 1, 0)
        keep = (blk < own) & (rank < MOBA_TOPK) & (jnp.abs(gt) < jnp.inf)
        pen = jnp.where(keep | (blk == own), 0.0, NEG_BIG)
        pen = jnp.concatenate([pen, jnp.zeros((LANES - nb, tq), F32)], axis=0)
        q_ops.append(jnp.concatenate([qe, pen.T.astype(BF16)], axis=1))

    def body(n):
        (a0, l0), (a1, l1) = _attend_static(q_ops, kaug_ref, v_ref, bias_ref, (0, 1),
                                            (s0_ref, s1_ref), (p0_ref, p1_ref), mb_ref, n)
        o_ref[...] = jnp.where(lane < HEAD_DIM, a0 / l0, a1 / l1).astype(o_ref.dtype)

    _per_query_tile(nq, body)


def _attn_scratch(nq, seq):
    return [pltpu.VMEM((nq, Q_TILE, Q_TILE), F32),
            pltpu.VMEM((nq, Q_TILE, Q_TILE), F32),
            pltpu.VMEM((Q_TILE, seq), BF16),
            pltpu.VMEM((Q_TILE, seq), BF16),
            pltpu.VMEM((Q_TILE, LANES), F32)]


def _moba_prompt(qa, ka, va, bias_tiles, batch, seq):
    tq = Q_TILE
    nq = seq // tq
    grid = (batch, N_HEADS_A // 2, nq)
    qmap = lambda b, hp, qi: (b * nq + qi, hp)
    kvmap = lambda b, hp, qi: (b, hp)
    return pl.pallas_call(
        functools.partial(_moba_prompt_kernel, nq=nq),
        grid=grid,
        in_specs=[pl.BlockSpec((tq, LANES), qmap),
                  pl.BlockSpec((seq, LANES), kvmap),
                  pl.BlockSpec((seq, LANES), kvmap),
                  pl.BlockSpec((2, 3, ATT_TILE, ATT_TILE), lambda b, hp, qi: (hp, 0, 0, 0))],
        out_specs=pl.BlockSpec((tq, LANES), qmap),
        out_shape=jax.ShapeDtypeStruct((batch * seq, WIDTH_A), BF16),
        scratch_shapes=_attn_scratch(nq, seq) + [pltpu.VMEM((seq // MOBA_BLOCK, LANES), F32),
                                                 pltpu.VMEM((seq, 2 * LANES), BF16)],
        compiler_params=pltpu.CompilerParams(
            dimension_semantics=("arbitrary", "arbitrary", "arbitrary"),
            vmem_limit_bytes=VMEM_LIMIT_BYTES),
        name="moba_prompt",
    )(qa, ka, va, bias_tiles)


def _diff_prompt_kernel(q_ref, k_ref, v_ref, bias_ref, lam_ref, gain_ref, o_ref,
                        s0_ref, s1_ref, p0_ref, p1_ref, mb_ref, *, lam_init, nq):
    q = q_ref[...].astype(F32)
    lane = lax.broadcasted_iota(jnp.int32, (Q_TILE, LANES), 1)
    q_ops = [jnp.where((lane >= HEAD_DIM * c) & (lane < HEAD_DIM * (c + 1)), q, 0.0).astype(BF16)
             for c in range(2)]

    def body(n):
        (a0, l0), (a1, l1) = _attend_static(q_ops, k_ref, v_ref, bias_ref, (0, 0),
                                            (s0_ref, s1_ref), (p0_ref, p1_ref), mb_ref, n)
        o = a0 / l0 - _lambda(lam_ref, lam_init) * (a1 / l1)
        o_ref[...] = (_rmsnorm(o, gain_ref[...]) * (1.0 - lam_init)).astype(o_ref.dtype)

    _per_query_tile(nq, body)


def _diff_prompt(qd, kd, vd, bias_tiles, lam_vecs, gain, lam_init, batch, seq):
    tq = Q_TILE
    nq = seq // tq
    grid = (batch, N_HEADS_B, nq)
    qmap = lambda b, h, qi: (b * nq + qi, h)
    kvmap = lambda b, h, qi: (b, h)
    return pl.pallas_call(
        functools.partial(_diff_prompt_kernel, lam_init=lam_init, nq=nq),
        grid=grid,
        in_specs=[pl.BlockSpec((tq, LANES), qmap),
                  pl.BlockSpec((seq, LANES), kvmap),
                  pl.BlockSpec((seq, LANES), kvmap),
                  pl.BlockSpec((1, 3, ATT_TILE, ATT_TILE), lambda b, h, qi: (h, 0, 0, 0)),
                  pl.BlockSpec((4, HEAD_DIM), lambda b, h, qi: (0, 0)),
                  pl.BlockSpec((1, 2 * HEAD_DIM), lambda b, h, qi: (0, 0))],
        out_specs=pl.BlockSpec((tq, LANES), qmap),
        out_shape=jax.ShapeDtypeStruct((batch * seq, WIDTH_B), BF16),
        scratch_shapes=_attn_scratch(nq, seq),
        compiler_params=pltpu.CompilerParams(
            dimension_semantics=("arbitrary", "arbitrary", "arbitrary"),
            vmem_limit_bytes=VMEM_LIMIT_BYTES),
        name="diff_prompt",
    )(qd, kd, vd, bias_tiles, lam_vecs, gain)


def _toeplitz(w, t):
    h = w.shape[0]
    m = jnp.broadcast_to(w[:, None, :], (h, t, 2 * t)).reshape(h, 2 * t * t)
    return m[:, :t * (2 * t - 1)].reshape(h, t, 2 * t - 1)[:, :, :t]


def _bias_tiles(bias_t):
    t = ATT_TILE
    assert t + 1 >= MAX_DISTANCE
    k = jnp.arange(2 * t)
    d = jnp.where(k <= t, -k, 2 * t - k)
    diag = _toeplitz(jnp.where(d >= 0, _bias_of(bias_t, d), -jnp.inf), t)
    sub = _toeplitz(_bias_of(bias_t, d + t), t)
    far = _toeplitz(_bias_of(bias_t, d + 2 * t), t)
    return jnp.stack([diag, sub, far], axis=1)


def _page_stream(pt_ref, pk_ref, pv_ref, buf_ref, sem_ref, n_seq, n_chunks):
    pps = PAGES_PER_STEP
    ahead = RING_SLOTS - 1
    b, ph, j = pl.program_id(0), pl.program_id(1), pl.program_id(2)
    step = (b * 2 + ph) * n_chunks + j
    total = n_seq * 2 * n_chunks

    def copies(pool_ref, bb, jj, slot):
        return [pltpu.make_async_copy(pool_ref.at[pt_ref[bb, jj * pps + i]],
                                      buf_ref.at[slot, i], sem_ref.at[slot])
                for i in range(pps)]

    def start(chunk):
        jj, seq_phase = chunk % n_chunks, chunk // n_chunks
        for phase, pool_ref in ((0, pk_ref), (1, pv_ref)):
            @pl.when(seq_phase % 2 == phase)
            def _(pool_ref=pool_ref):
                for i, cp in enumerate(copies(pool_ref, seq_phase // 2, jj, chunk % RING_SLOTS)):
                    cp.start(priority=i % 2)

    @pl.when(step == 0)
    def _():
        for chunk in range(min(ahead, total)):
            start(jnp.int32(chunk))

    @pl.when(step + ahead < total)
    def _():
        start(step + ahead)

    slot = step % RING_SLOTS

    def wait(pool_ref):
        for cp in copies(pool_ref, b, j, slot):
            cp.wait()

    return slot, wait


def _moba_sample_kernel(pt_ref, pk_ref, pv_ref, q_ref, kn_ref, vn_ref, bias_ref, o_ref,
                        s_ref, snew_ref, acc_ref, l_ref, m_ref, bmax_ref, gate_ref, sel_ref,
                        buf_ref, sem_ref, *, n_seq, n_chunks, tokens):
    pps = PAGES_PER_STEP
    slot, wait = _page_stream(pt_ref, pk_ref, pv_ref, buf_ref, sem_ref, n_seq, n_chunks)
    rows = q_ref.shape[1]
    groups = rows // tokens
    ppb = MOBA_BLOCK // PAGE_SIZE
    bps = pps // ppb
    n_blocks = n_chunks * bps
    ph = pl.program_id(1)
    j = pl.program_id(2)
    lane = lax.broadcasted_iota(jnp.int32, (rows, LANES), 1)
    lanef = lane.astype(F32)
    q = q_ref[0]

    @pl.when((ph == 0) & (j == 0))
    def _():
        snew_ref[...] = _dot_nt(q, kn_ref[0]) + bias_ref[2]
        bmax_ref[...] = jnp.full((rows, LANES), -jnp.inf, F32)
        gate_ref[...] = jnp.zeros((rows, LANES), F32)

    @pl.when(ph == 0)
    def _():
        wait(pk_ref)
        is_last = j == n_chunks - 1
        bmax = bmax_ref[...]
        gate = gate_ref[...]
        for b in range(bps):
            smax = ssum = None
            for pg in range(ppb):
                i = b * ppb + pg
                raw = _dot(q, buf_ref[slot, i].astype(BF16))
                bias = bias_ref[0]
                if i == pps - 1:
                    bias = jnp.where(is_last, bias_ref[1], bias)
                s = raw + bias
                s_ref[j, :, i * PAGE_SIZE:(i + 1) * PAGE_SIZE] = s
                smax = s if smax is None else jnp.maximum(smax, s)
                ssum = raw if ssum is None else ssum + raw
            blk = j * bps + b
            bmax = jnp.where(lane == blk, jnp.max(smax, axis=1, keepdims=True), bmax)
            gate = jnp.where(lane == blk, jnp.sum(ssum, axis=1, keepdims=True), gate)
        bmax_ref[...] = bmax
        gate_ref[...] = gate

    @pl.when((ph == 1) & (j == 0))
    def _():
        g = jnp.where(lane < n_blocks, gate_ref[...], -jnp.inf)
        sel = jnp.zeros((rows, LANES), F32)
        for _ in range(MOBA_TOPK):
            top = jnp.max(g, axis=1, keepdims=True)
            first = jnp.min(jnp.where(g == top, lanef, float(LANES)), axis=1, keepdims=True)
            pick = lanef == first
            finite = jnp.where(jnp.abs(top) < jnp.inf, 1.0, 0.0)
            sel = jnp.maximum(sel, jnp.where(pick, finite, 0.0))
            g = jnp.where(pick, -jnp.inf, g)
        sel_ref[...] = sel
        m_sel = jnp.max(jnp.where(sel > 0.5, bmax_ref[...], -jnp.inf), axis=1, keepdims=True)
        m_new = jnp.max(snew_ref[...], axis=1, keepdims=True)
        m_ref[...] = jnp.broadcast_to(jnp.maximum(m_sel, m_new), (rows, LANES))
        acc_ref[...] = jnp.zeros(acc_ref.shape, F32)
        l_ref[...] = jnp.zeros(l_ref.shape, F32)

    @pl.when(ph == 1)
    def _():
        wait(pv_ref)
        m = m_ref[...]
        selv = sel_ref[...]
        acc = acc_ref[...]
        l = l_ref[...]
        for i in range(pps):
            blk = j * bps + i // ppb
            on = jnp.max(jnp.where(lane == blk, selv, 0.0), axis=1, keepdims=True)
            on = jnp.broadcast_to(on, (rows, LANES)) > 0.5
            s = s_ref[j, :, i * PAGE_SIZE:(i + 1) * PAGE_SIZE]
            p = jnp.exp(jnp.where(on, s - m, -jnp.inf))
            l = l + p
            acc = acc + _dot_nt(p.astype(BF16), buf_ref[slot, i].astype(BF16))
        acc_ref[...] = acc
        l_ref[...] = l

    @pl.when((ph == 1) & (j == n_chunks - 1))
    def _():
        p = jnp.exp(snew_ref[...] - m_ref[...])
        acc = acc_ref[...] + _dot(p.astype(BF16), vn_ref[0])
        l = jnp.sum(l_ref[...] + p, axis=1, keepdims=True)
        o = acc / l
        grp = lax.broadcasted_iota(jnp.int32, (groups, WIDTH_A), 0)
        col = lax.broadcasted_iota(jnp.int32, (groups, WIDTH_A), 1)
        head_shift = HEAD_DIM.bit_length() - 1
        w = jnp.where((col >> head_shift) == grp, 1.0, 0.0)
        tok = jnp.sum(o.reshape(tokens, groups, WIDTH_A) * w[None], axis=1)
        o_ref[0] = tok.astype(o_ref.dtype)


def _moba_sample(page_table, pool_kt, pool_vt, q_bd, k_new, v_new, bias_rows, *, tokens):
    pps = PAGES_PER_STEP
    n_seq, n_pages = page_table.shape
    n_chunks = n_pages // pps
    rows = q_bd.shape[1]
    seq_map = lambda b, ph, j, pt: (b, 0, 0)
    hbm = pl.BlockSpec(memory_space=pl.ANY)
    in_specs = [hbm, hbm,
                pl.BlockSpec((1, rows, WIDTH_A), seq_map),
                pl.BlockSpec((1, LANES, WIDTH_A), seq_map),
                pl.BlockSpec((1, LANES, WIDTH_A), seq_map),
                pl.BlockSpec((3, rows, LANES), lambda b, ph, j, pt: (0, 0, 0))]
    vec = pltpu.VMEM((rows, LANES), F32)
    grid_spec = pltpu.PrefetchScalarGridSpec(
        num_scalar_prefetch=1,
        grid=(n_seq, 2, n_chunks),
        in_specs=in_specs,
        out_specs=pl.BlockSpec((1, tokens, WIDTH_A), seq_map),
        scratch_shapes=[pltpu.VMEM((n_chunks, rows, pps * PAGE_SIZE), F32),
                        vec,
                        pltpu.VMEM((rows, WIDTH_A), F32),
                        vec, vec, vec, vec, vec,
                        pltpu.VMEM((RING_SLOTS, pps) + pool_kt.shape[1:], F32),
                        pltpu.SemaphoreType.DMA((RING_SLOTS,))])
    return pl.pallas_call(
        functools.partial(_moba_sample_kernel, n_seq=n_seq, n_chunks=n_chunks, tokens=tokens),
        grid_spec=grid_spec,
        out_shape=jax.ShapeDtypeStruct((n_seq, tokens, WIDTH_A), BF16),
        compiler_params=pltpu.CompilerParams(
            dimension_semantics=("arbitrary", "arbitrary", "arbitrary"),
            vmem_limit_bytes=VMEM_LIMIT_BYTES),
        name="moba_sample",
    )(page_table, pool_kt, pool_vt, q_bd, k_new, v_new, bias_rows)


def _diff_sample_kernel(pt_ref, pk_ref, pv_ref, q_ref, kn_ref, vn_ref, bias_ref, bnew_ref,
                        lam_ref, gain_ref, o_ref, s_ref, snew_ref, acc_ref, l_ref, m_ref,
                        buf_ref, sem_ref, *, n_seq, n_chunks, tokens, lam_init):
    pps = PAGES_PER_STEP
    slot, wait = _page_stream(pt_ref, pk_ref, pv_ref, buf_ref, sem_ref, n_seq, n_chunks)
    rows = q_ref.shape[1]
    pw = buf_ref.shape[2]
    ph = pl.program_id(1)
    j = pl.program_id(2)
    q = q_ref[0]

    @pl.when((ph == 0) & (j == 0))
    def _():
        snew_ref[...] = _dot_nt(q, kn_ref[0]) + bnew_ref[...]
        m_ref[...] = jnp.full((rows, LANES), -jnp.inf, F32)

    @pl.when(ph == 0)
    def _():
        wait(pk_ref)
        is_last = j == n_chunks - 1
        mx = m_ref[...]
        for i in range(pps):
            bias = bias_ref[0]
            if i == pps - 1:
                bias = jnp.where(is_last, bias_ref[1], bias)
            s = _dot_nt(q, buf_ref[slot, i].astype(BF16)) + bias
            s_ref[j, :, i * pw:(i + 1) * pw] = s
            mx = jnp.maximum(mx, _lane_fold(jnp.maximum, s))
        m_ref[...] = mx

    @pl.when((ph == 1) & (j == 0))
    def _():
        m = jnp.maximum(jnp.max(m_ref[...], axis=1, keepdims=True),
                        jnp.max(snew_ref[...], axis=1, keepdims=True))
        m_ref[...] = jnp.broadcast_to(m, (rows, LANES))
        acc_ref[...] = jnp.zeros(acc_ref.shape, F32)
        l_ref[...] = jnp.zeros(l_ref.shape, F32)

    @pl.when(ph == 1)
    def _():
        wait(pv_ref)
        m = m_ref[...]
        mw = jnp.concatenate([m] * (pw // LANES), axis=1)
        acc = acc_ref[...]
        l = l_ref[...]
        for i in range(pps):
            p = jnp.exp(s_ref[j, :, i * pw:(i + 1) * pw] - mw)
            l = l + _lane_fold(jnp.add, p)
            acc = acc + _dot(p.astype(BF16), buf_ref[slot, i].astype(BF16))
        acc_ref[...] = acc
        l_ref[...] = l

    @pl.when((ph == 1) & (j == n_chunks - 1))
    def _():
        p = jnp.exp(snew_ref[...] - m_ref[...])
        acc = acc_ref[...] + _dot(p.astype(BF16), vn_ref[0])
        l = jnp.sum(l_ref[...] + p, axis=1, keepdims=True)
        o = (acc / l).reshape(tokens, rows // tokens, LANES)
        lam = _lambda(lam_ref, lam_init)
        sub = lax.broadcasted_iota(jnp.int32, (rows // tokens, LANES), 0)
        gain = gain_ref[...]
        parts = []
        for h in range(N_HEADS_B):
            w = jnp.where(sub == 2 * h, 1.0, jnp.where(sub == 2 * h + 1, -lam, 0.0))
            parts.append(_rmsnorm(jnp.sum(o * w[None], axis=1), gain) * (1.0 - lam_init))
        o_ref[0] = jnp.concatenate(parts, axis=1).astype(o_ref.dtype)


def _diff_sample(page_table, pool_k, pool_v, q2, k_new, v_new, bias_rows, bias_new, lam_vecs, gain,
                 *, tokens, lam_init):
    pps = PAGES_PER_STEP
    n_seq, n_pages = page_table.shape
    n_chunks = n_pages // pps
    rows = q2.shape[1]
    pw = pool_k.shape[1]
    seq_map = lambda b, ph, j, pt: (b, 0, 0)
    const2 = lambda b, ph, j, pt: (0, 0)
    hbm = pl.BlockSpec(memory_space=pl.ANY)
    in_specs = [hbm, hbm,
                pl.BlockSpec((1, rows, LANES), seq_map),
                pl.BlockSpec((1, LANES, LANES), seq_map),
                pl.BlockSpec((1, LANES, LANES), seq_map),
                pl.BlockSpec((2, rows, pw), lambda b, ph, j, pt: (0, 0, 0)),
                pl.BlockSpec((rows, LANES), const2),
                pl.BlockSpec((4, HEAD_DIM), const2),
                pl.BlockSpec((1, 2 * HEAD_DIM), const2)]
    vec = pltpu.VMEM((rows, LANES), F32)
    grid_spec = pltpu.PrefetchScalarGridSpec(
        num_scalar_prefetch=1,
        grid=(n_seq, 2, n_chunks),
        in_specs=in_specs,
        out_specs=pl.BlockSpec((1, tokens, WIDTH_B), seq_map),
        scratch_shapes=[pltpu.VMEM((n_chunks, rows, pps * pw), F32),
                        vec, vec, vec, vec,
                        pltpu.VMEM((RING_SLOTS, pps) + pool_k.shape[1:], F32),
                        pltpu.SemaphoreType.DMA((RING_SLOTS,))])
    return pl.pallas_call(
        functools.partial(_diff_sample_kernel, n_seq=n_seq, n_chunks=n_chunks, tokens=tokens,
                          lam_init=lam_init),
        grid_spec=grid_spec,
        out_shape=jax.ShapeDtypeStruct((n_seq, tokens, WIDTH_B), BF16),
        compiler_params=pltpu.CompilerParams(
            dimension_semantics=("arbitrary", "arbitrary", "arbitrary"),
            vmem_limit_bytes=VMEM_LIMIT_BYTES),
        name="diff_sample",
    )(page_table, pool_k, pool_v, q2, k_new, v_new, bias_rows, bias_new, lam_vecs, gain)


def _block_diag(q, n_seq, tokens, groups, keep):
    w = q.shape[1] // groups
    q4 = q.reshape(n_seq, tokens, groups, 1, w)
    eye = (jnp.arange(groups)[:, None] % keep == jnp.arange(keep)[None, :]).astype(q.dtype)
    return (q4 * eye.reshape(1, 1, groups, keep, 1)).reshape(n_seq, tokens * groups, keep * w)


def _pad_rows(x, n_seq):
    x = x.reshape(n_seq, -1, x.shape[-1])
    return jnp.pad(x, ((0, 0), (0, LANES - x.shape[1]), (0, 0)))


def _sample_bias(bias_t, tokens, heads_per_group):
    assert PAGE_SIZE + 1 >= MAX_DISTANCE
    tok = jnp.arange(tokens)[:, None]
    col = jnp.arange(LANES)[None, :]
    far = _bias_of(bias_t, jnp.broadcast_to(2 * PAGE_SIZE, (tokens, LANES)))
    last = _bias_of(bias_t, PAGE_SIZE + tok - col)
    new = jnp.where((col <= tok) & (col < tokens), _bias_of(bias_t, tok - col), -jnp.inf)

    def rows(x):
        x = jnp.repeat(x.transpose(1, 0, 2), heads_per_group, axis=1)
        return x.reshape(-1, LANES)

    return rows(far), rows(last), rows(new)


def _spread_heads(x, n_heads, groups):
    r, k = x.shape
    row_head = (jnp.arange(r) % groups) // (groups // n_heads)
    own = row_head[:, None, None] == jnp.arange(n_heads)[None, None, :]
    return jnp.where(own, x[:, :, None], -jnp.inf).reshape(r, k * n_heads)


def _merge_kernel(x_ref, oa_ref, ob_ref, g_ref, wg_ref, woa_ref, wob_ref, wo_ref, o_ref):
    x = x_ref[...]
    h = _rmsnorm(x, g_ref[...]).astype(BF16)
    ga = 1.0 / (1.0 + jnp.exp(-_dot(h, wg_ref[:, :D_MODEL])))
    m = ga * _dot(oa_ref[...], woa_ref[...])
    gb = 1.0 / (1.0 + jnp.exp(-_dot(h, wg_ref[:, D_MODEL:])))
    m = m + gb * _dot(ob_ref[...], wob_ref[...])
    o_ref[...] = x + _dot(m.astype(BF16), wo_ref[...])


def _merge(x, oa, ob, g, wg, woa, wob, wo, tm):
    n = x.shape[0]
    row = lambda i: (i, 0)
    const = lambda i: (0, 0)
    return pl.pallas_call(
        _merge_kernel,
        grid=(n // tm,),
        in_specs=[pl.BlockSpec((tm, D_MODEL), row),
                  pl.BlockSpec((tm, WIDTH_A), row),
                  pl.BlockSpec((tm, WIDTH_B), row),
                  pl.BlockSpec((1, D_MODEL), const),
                  pl.BlockSpec((D_MODEL, 2 * D_MODEL), const),
                  pl.BlockSpec((WIDTH_A, D_MODEL), const),
                  pl.BlockSpec((WIDTH_B, D_MODEL), const),
                  pl.BlockSpec((D_MODEL, D_MODEL), const)],
        out_specs=pl.BlockSpec((tm, D_MODEL), row),
        out_shape=jax.ShapeDtypeStruct((n, D_MODEL), F32),
        compiler_params=pltpu.CompilerParams(
            dimension_semantics=("arbitrary",), vmem_limit_bytes=VMEM_LIMIT_BYTES),
        name="merge",
    )(x, oa, ob, g, wg, woa, wob, wo)


def _ffn_kernel(*refs, tm, seq_len, has_prev):
    if has_prev:
        (x_ref, g_ref, wup_ref, cw_ref, cb_ref, wdn_ref, gfin_ref, e1_ref, e2_ref,
         y_ref, u_ref, act_ref) = refs
    else:
        (x_ref, g_ref, wup_ref, cw_ref, cb_ref, wdn_ref, gfin_ref,
         y_ref, u_ref, act_ref, carry_ref) = refs
    i = pl.program_id(0)
    x = x_ref[...]
    h = _rmsnorm(x, g_ref[...]).astype(BF16)
    row = lax.broadcasted_iota(jnp.int32, (tm, FF_CHUNK), 0)
    row8 = lax.broadcasted_iota(jnp.int32, (SUBLANES, FF_CHUNK), 0)
    if has_prev:
        pos = row & (seq_len - 1)
    else:
        @pl.when((i * tm) % seq_len == 0)
        def _():
            carry_ref[...] = jnp.zeros(carry_ref.shape, F32)

    for c in range(D_FF // FF_CHUNK):
        halves = []
        for part in range(2):
            cols = slice(part * D_FF + c * FF_CHUNK, part * D_FF + (c + 1) * FF_CHUNK)
            u = _dot(h, wup_ref[:, cols])
            um1 = pltpu.roll(u, 1, 0)
            um2 = pltpu.roll(u, 2, 0)
            if has_prev:
                um1 = jnp.where(pos == 0, e1_ref[:, cols], um1)
                um2 = jnp.where(pos < 2, e2_ref[:, cols], um2)
                u_ref[:, cols] = u
            else:
                prev = carry_ref[:, cols]
                top1 = jnp.where(row8 == 0, pltpu.roll(prev, 1, 0), um1[:SUBLANES])
                top2 = jnp.where(row8 < 2, pltpu.roll(prev, 2, 0), um2[:SUBLANES])
                um1 = jnp.concatenate([top1, um1[SUBLANES:]], axis=0)
                um2 = jnp.concatenate([top2, um2[SUBLANES:]], axis=0)
                carry_ref[:, cols] = u[tm - SUBLANES:]
                u_ref[:, cols] = u[tm - SUBLANES:]
            cw = cw_ref[:, cols]
            halves.append(((cb_ref[:, cols] + cw[0:1] * um2) + cw[1:2] * um1) + cw[2:3] * u)
        gate, val = halves
        act = (gate * (1.0 / (1.0 + jnp.exp(-gate)))) * val
        act_ref[:, c * FF_CHUNK:(c + 1) * FF_CHUNK] = act.astype(BF16)

    x3 = x + _dot(act_ref[...], wdn_ref[...])
    y_ref[...] = _rmsnorm(x3, gfin_ref[...])


def _ffn(x, g, wup, cw, cb, wdn, gfin, prev, tm, seq_len):
    n = x.shape[0]
    has_prev = prev is not None
    row = lambda i: (i, 0)
    const = lambda i: (0, 0)
    in_specs = [pl.BlockSpec((tm, D_MODEL), row),
                pl.BlockSpec((1, D_MODEL), const),
                pl.BlockSpec((D_MODEL, 2 * D_FF), const),
                pl.BlockSpec((CONV_W, 2 * D_FF), const),
                pl.BlockSpec((1, 2 * D_FF), const),
                pl.BlockSpec((D_FF, D_MODEL), const),
                pl.BlockSpec((1, D_MODEL), const)]
    scratch = [pltpu.VMEM((tm, D_FF), BF16)]
    if has_prev:
        assert n == tm and tm % seq_len == 0 and seq_len & (seq_len - 1) == 0
        in_specs += [pl.BlockSpec((tm, 2 * D_FF), row)] * 2
        u_shape, u_spec = (n, 2 * D_FF), pl.BlockSpec((tm, 2 * D_FF), row)
        args = (x, g, wup, cw, cb, wdn, gfin) + tuple(prev)
    else:
        assert seq_len % tm == 0
        tiles_per_seq = seq_len // tm
        u_shape = (n // seq_len * SUBLANES, 2 * D_FF)
        u_spec = pl.BlockSpec((SUBLANES, 2 * D_FF), lambda i: (i // tiles_per_seq, 0))
        scratch.append(pltpu.VMEM((SUBLANES, 2 * D_FF), F32))
        args = (x, g, wup, cw, cb, wdn, gfin)
    return pl.pallas_call(
        functools.partial(_ffn_kernel, tm=tm, seq_len=seq_len, has_prev=has_prev),
        grid=(n // tm,),
        in_specs=in_specs,
        out_specs=[pl.BlockSpec((tm, D_MODEL), row), u_spec],
        out_shape=[jax.ShapeDtypeStruct((n, D_MODEL), F32), jax.ShapeDtypeStruct(u_shape, F32)],
        scratch_shapes=scratch,
        compiler_params=pltpu.CompilerParams(
            dimension_semantics=("arbitrary",), vmem_limit_bytes=VMEM_LIMIT_BYTES),
        name="ffn_sample" if has_prev else "ffn_prompt",
    )(*args)


def kernel(x_prompt, x_sample, cache_moba_k, cache_moba_v, cache_diff_k, cache_diff_v, state_conv, page_table, rel_bias, norm_attn, w_in, w_gate, w_out_a, w_out_b, w_out, lambda_q1, lambda_k1, lambda_q2, lambda_k2, diff_norm, norm_ffn, w_up, conv_w, conv_b, w_down, norm_final):
    batch, seq, _ = x_prompt.shape
    n_seq, tokens, _ = x_sample.shape
    depth = w_in.shape[0]
    n_phys = cache_moba_k.shape[1]
    n_pages = page_table.shape[1]
    assert depth == 1 and seq % Q_TILE == 0 and ATT_TILE & (ATT_TILE - 1) == 0
    assert (n_pages * PAGE_SIZE) % MOBA_BLOCK == 0 and n_pages % PAGES_PER_STEP == 0
    assert n_pages * PAGE_SIZE // MOBA_BLOCK <= LANES and CONV_W - 1 <= tokens <= LANES // N_HEADS_B
    l = 0
    lam_init = 0.8 - 0.6 * math.exp(-0.3 * l)

    bias_a = rel_bias[:, :N_HEADS_A].T
    bias_d = rel_bias[:, N_HEADS_A:].T
    row = lambda v: v.reshape(1, -1)
    w_in_b = w_in[l].astype(BF16)
    w_gate_b = w_gate[l].astype(BF16)
    w_oa_b = w_out_a[l].astype(BF16)
    w_ob_b = w_out_b[l].astype(BF16)
    w_o_b = w_out[l].astype(BF16)
    w_up_b = w_up[l].astype(BF16)
    w_dn_b = w_down[l].astype(BF16)
    lam_vecs = jnp.stack([lambda_q1[l], lambda_k1[l], lambda_q2[l], lambda_k2[l]]).astype(F32)
    gain_d = row(diff_norm[l])
    g_attn, g_ffn, g_fin = row(norm_attn[l]), row(norm_ffn[l]), row(norm_final)
    cb = row(conv_b[l])

    xp = x_prompt.reshape(batch * seq, D_MODEL)
    qa, ka, va, qd, kd, vd, ka_t, va_t, kd_r, vd_r = _proj(xp, g_attn, w_in_b, 512, seq)
    oa = _moba_prompt(qa, ka, va, _bias_tiles(bias_a), batch, seq)
    ob = _diff_prompt(qd, kd, vd, _bias_tiles(bias_d), lam_vecs, gain_d, lam_init, batch, seq)
    x2 = _merge(xp, oa, ob, g_attn, w_gate_b, w_oa_b, w_ob_b, w_o_b, 512)
    yp, tail_p = _ffn(x2, g_ffn, w_up_b, conv_w[l], cb, w_dn_b, g_fin, None, 512, seq)
    conv_p = tail_p.reshape(batch, SUBLANES, 2 * D_FF)[:, SUBLANES - (CONV_W - 1):]

    n_s = n_seq * tokens
    xs = x_sample.reshape(n_s, D_MODEL)
    qa_s, ka_s, va_s, qd_s, kd_s, vd_s, ka_sf, va_sf, kd_sf, vd_sf = _proj(xs, g_attn, w_in_b, n_s)

    pool_t = lambda c: jnp.transpose(c[l], (0, 2, 3, 1)).reshape(n_phys, WIDTH_A, PAGE_SIZE)
    pool_r = lambda c: c[l].reshape(n_phys, PAGE_SIZE * N_HEADS_B, 2 * HEAD_DIM)
    far_a, last_a, new_a = _sample_bias(bias_a, tokens, 1)
    oa_s = _moba_sample(page_table, pool_t(cache_moba_k), pool_t(cache_moba_v),
                        _block_diag(qa_s, n_seq, tokens, N_HEADS_A, N_HEADS_A),
                        _pad_rows(ka_s, n_seq), _pad_rows(va_s, n_seq),
                        jnp.stack([far_a, last_a, new_a]), tokens=tokens)
    far_d, last_d, new_d = _sample_bias(bias_d, tokens, 2)
    groups_d = 2 * N_HEADS_B
    spread = lambda x: _spread_heads(x, N_HEADS_B, groups_d)
    per_head = lambda x: x.reshape(n_s * N_HEADS_B, 2 * HEAD_DIM)
    oa_d = _diff_sample(page_table, pool_r(cache_diff_k), pool_r(cache_diff_v),
                        _block_diag(qd_s, n_seq, tokens, groups_d, 2),
                        _pad_rows(per_head(kd_s), n_seq), _pad_rows(per_head(vd_s), n_seq),
                        jnp.stack([spread(far_d), spread(last_d)]),
                        spread(new_d[:, :LANES // N_HEADS_B]),
                        lam_vecs, gain_d, tokens=tokens, lam_init=lam_init)
    x2s = _merge(xs, oa_s.reshape(n_s, WIDTH_A), oa_d.reshape(n_s, WIDTH_B), g_attn,
                 w_gate_b, w_oa_b, w_ob_b, w_o_b, n_s)
    st = state_conv[l]
    zero = jnp.zeros((n_seq, 1, 2 * D_FF), F32)
    e1 = jnp.concatenate([st[:, 1:2]] + [zero] * (tokens - 1), axis=1).reshape(n_s, 2 * D_FF)
    e2 = jnp.concatenate([st[:, 0:2]] + [zero] * (tokens - 2), axis=1).reshape(n_s, 2 * D_FF)
    ys, u_s = _ffn(x2s, g_ffn, w_up_b, conv_w[l], cb, w_dn_b, g_fin, (e1, e2), n_s, tokens)
    conv_s = u_s.reshape(n_seq, tokens, 2 * D_FF)[:, tokens - (CONV_W - 1):]

    shp_a = lambda a, b_, t_: a.reshape(1, b_, t_, N_HEADS_A, HEAD_DIM)
    shp_d = lambda a, b_, t_: a.reshape(1, b_, t_, N_HEADS_B, 2 * HEAD_DIM)
    untr = lambda a: a.reshape(batch, N_HEADS_A, HEAD_DIM, seq).transpose(0, 3, 1, 2)[None]
    return (yp.reshape(batch, seq, D_MODEL), ys.reshape(n_seq, tokens, D_MODEL),
            untr(ka_t), untr(va_t),
            shp_d(kd_r, batch, seq), shp_d(vd_r, batch, seq), conv_p[None],
            shp_a(ka_sf, n_seq, tokens), shp_a(va_sf, n_seq, tokens),
            shp_d(kd_sf, n_seq, tokens), shp_d(vd_sf, n_seq, tokens), conv_s[None])
```

```python
import functools
import math

import jax
import jax.numpy as jnp
from jax import lax
from jax.experimental import pallas as pl
from jax.experimental.pallas import tpu as pltpu

F32 = jnp.float32
BF16 = jnp.bfloat16

D_MODEL = 1024
HEAD_DIM = 64
N_HEADS_A = D_MODEL // 128
N_HEADS_B = D_MODEL // 256
WIDTH_A = N_HEADS_A * HEAD_DIM
WIDTH_B = N_HEADS_B * 2 * HEAD_DIM
N_IN = 3 * WIDTH_A + 3 * WIDTH_B
MOBA_BLOCK = 256
MOBA_TOPK = 3
NUM_BUCKETS = 32
MAX_DISTANCE = 128
D_FF = ((8 * D_MODEL // 3 + 127) // 128) * 128
CONV_W = 3
EPS = 1e-6
PAGE_SIZE = 128
SCALE = HEAD_DIM ** -0.5

LANES = 128
SUBLANES = 8
VMEM_LIMIT_BYTES = 56 * 1024 * 1024

ATT_TILE = MOBA_BLOCK
Q_TILE = 2 * ATT_TILE
ATT_PAIRS = 2
ATT_RING = 2
FF_CHUNK = 256
PAGES_PER_STEP = 16
RING_SLOTS = 3
NEG_BIG = -1e30


def _dot(a, b):
    return jnp.dot(a, b, preferred_element_type=F32)


def _dot_nt(a, b):
    return lax.dot_general(a, b, (((1,), (1,)), ((), ())), preferred_element_type=F32)


def _rmsnorm(x, g):
    return (x * lax.rsqrt(jnp.mean(x * x, axis=-1, keepdims=True) + EPS)) * g


def _lane_fold(op, x):
    out = x[:, :LANES]
    for c in range(1, x.shape[1] // LANES):
        out = op(out, x[:, c * LANES:(c + 1) * LANES])
    return out


def _t5_bucket(dist):
    n = jnp.maximum(dist, 0)
    max_exact = NUM_BUCKETS // 2
    nf = jnp.maximum(n, 1).astype(F32)
    large = max_exact + (jnp.log(nf / max_exact) / math.log(MAX_DISTANCE / max_exact)
                         * (NUM_BUCKETS - max_exact)).astype(jnp.int32)
    large = jnp.minimum(large, NUM_BUCKETS - 1)
    return jnp.where(n < max_exact, n, large)


def _bias_of(bias_t, dist):
    onehot = jax.nn.one_hot(_t5_bucket(dist), NUM_BUCKETS, dtype=F32)
    return jnp.einsum('hb,...b->h...', bias_t.astype(F32), onehot, precision=lax.Precision.HIGHEST)


def _lambda(lam_ref, lam_init):
    lv = lam_ref[...]
    a = jnp.sum(lv[0:1] * lv[1:2], axis=-1, keepdims=True)
    b = jnp.sum(lv[2:3] * lv[3:4], axis=-1, keepdims=True)
    return jnp.exp(a) - jnp.exp(b) + lam_init


def _proj_kernel(x_ref, g_ref, w_ref,
                 qa_ref, ka_ref, va_ref, qd_ref, kd_ref, vd_ref,
                 kaf_ref, vaf_ref, kdf_ref, vdf_ref, *km_ref, prompt):
    tm = x_ref.shape[0]
    h = _rmsnorm(x_ref[...], g_ref[...]).astype(BF16)
    outs = ((qa_ref, None, SCALE), (ka_ref, kaf_ref, None), (va_ref, vaf_ref, None),
            (qd_ref, None, SCALE), (kd_ref, kdf_ref, None), (vd_ref, vdf_ref, None))
    for c, (b_ref, f_ref, scale) in enumerate(outs):
        u = _dot(h, w_ref[:, c * WIDTH_A:(c + 1) * WIDTH_A])
        if scale is not None:
            u = u * scale
        if not prompt:
            if f_ref is not None:
                f_ref[...] = u
            b_ref[...] = u.astype(BF16)
            continue
        is_key = c in (1, 4)
        ut = u.T if (is_key or c == 2) else None
        if c in (1, 2):
            f_ref[0] = ut
        elif f_ref is not None:
            for hd in range(N_HEADS_B):
                f_ref[pl.ds(hd, tm, stride=N_HEADS_B), :] = u[:, hd * LANES:(hd + 1) * LANES]
        if is_key:
            b_ref[0] = ut.astype(BF16)
        else:
            b_ref[...] = u.astype(BF16)
        if c == 1:
            for j in range(tm // MOBA_BLOCK):
                km_ref[0][j] = jnp.sum(u[j * MOBA_BLOCK:(j + 1) * MOBA_BLOCK], axis=0,
                                       keepdims=True) * (1.0 / MOBA_BLOCK)


def _proj(x, g, w_bf16, tm, seq=None):
    n = x.shape[0]
    row = lambda i: (i, 0)
    const = lambda i: (0, 0)
    blk = pl.BlockSpec((tm, WIDTH_A), row)
    b_specs = [blk] * 6
    b_shapes = [jax.ShapeDtypeStruct((n, WIDTH_A), BF16)] * 6
    f_specs = [blk] * 4
    f_shapes = [jax.ShapeDtypeStruct((n, WIDTH_A), F32)] * 4
    if seq is not None:
        assert seq % tm == 0 and tm % MOBA_BLOCK == 0 and 2 * HEAD_DIM == LANES
        tiles = seq // tm
        t_spec = pl.BlockSpec((1, WIDTH_A, tm), lambda i: (i // tiles, 0, i % tiles))
        r_spec = pl.BlockSpec((tm * N_HEADS_B, LANES), row)
        r_shape = jax.ShapeDtypeStruct((n * N_HEADS_B, LANES), F32)
        t_shape = lambda dt: jax.ShapeDtypeStruct((n // seq, WIDTH_A, seq), dt)
        b_specs = [blk, t_spec, blk, blk, t_spec, blk]
        b_shapes = [b_shapes[0], t_shape(BF16), b_shapes[0], b_shapes[0], t_shape(BF16), b_shapes[0]]
        f_specs = [t_spec, t_spec, r_spec, r_spec,
                   pl.BlockSpec((tm // MOBA_BLOCK, 1, WIDTH_A), lambda i: (i, 0, 0))]
        f_shapes = [t_shape(F32), t_shape(F32), r_shape, r_shape,
                    jax.ShapeDtypeStruct((n // MOBA_BLOCK, 1, WIDTH_A), F32)]
    return pl.pallas_call(
        functools.partial(_proj_kernel, prompt=seq is not None),
        grid=(n // tm,),
        in_specs=[pl.BlockSpec((tm, D_MODEL), row),
                  pl.BlockSpec((1, D_MODEL), const),
                  pl.BlockSpec((D_MODEL, N_IN), const)],
        out_specs=b_specs + f_specs,
        out_shape=b_shapes + f_shapes,
        compiler_params=pltpu.CompilerParams(
            dimension_semantics=("arbitrary",), vmem_limit_bytes=VMEM_LIMIT_BYTES),
        name="proj",
    )(x, g, w_bf16)


def _attend_static(q_ops, kt_tile, v_ref, v_cols, bias_ref, bidx, scratch, qi):
    t = ATT_TILE
    tq = Q_TILE
    halves = tq // t
    trips = qi + 1
    outs = []
    for i, q_op in enumerate(q_ops):
        s_ref, p_ref, mb_ref = scratch[i % len(scratch)]
        mx = [None] * halves
        for jj in range(trips):
            s = _dot(q_op, kt_tile(i, jj))
            for hq in range(halves):
                for hk in range(halves):
                    dist = (halves * qi + hq) - (halves * jj + hk)
                    if dist < 0:
                        continue
                    blk = (s[hq * t:(hq + 1) * t, hk * t:(hk + 1) * t]
                           + bias_ref[bidx[i], min(dist, 2)])
                    s_ref[jj, hq * t:(hq + 1) * t, hk * t:(hk + 1) * t] = blk
                    f = _lane_fold(jnp.maximum, blk)
                    mx[hq] = f if mx[hq] is None else jnp.maximum(mx[hq], f)
        m = jnp.max(jnp.concatenate(mx, axis=0), axis=1, keepdims=True)
        mb_ref[...] = jnp.broadcast_to(m, (tq, LANES))
        l = [jnp.zeros((t, LANES), F32) for _ in range(halves)]
        for jj in range(trips):
            for hq in range(halves):
                mb = mb_ref[hq * t:(hq + 1) * t, :]
                for hk in range(halves):
                    dist = (halves * qi + hq) - (halves * jj + hk)
                    rows, cols = slice(hq * t, (hq + 1) * t), slice(jj * tq + hk * t, jj * tq + (hk + 1) * t)
                    if dist < 0:
                        p_ref[rows, cols] = jnp.zeros((t, t), BF16)
                        continue
                    p = jnp.exp(s_ref[jj, rows, hk * t:(hk + 1) * t] - jnp.concatenate([mb] * (t // LANES), axis=1))
                    l[hq] = l[hq] + _lane_fold(jnp.add, p)
                    p_ref[rows, cols] = p.astype(BF16)
        acc = _dot(p_ref[:, :trips * tq], v_ref[:trips * tq, v_cols[i]])
        outs.append((acc, jnp.sum(jnp.concatenate(l, axis=0), axis=1, keepdims=True)))
    return outs


def _per_query_tile(nq, body):
    qi = pl.program_id(2)
    for n in range(nq):
        pl.when(qi == n)(functools.partial(body, n))


def _half_masked(q, lane, half):
    return jnp.where((lane >= HEAD_DIM * half) & (lane < HEAD_DIM * (half + 1)), q, 0.0).astype(BF16)


def _scratch_ring(refs):
    return [tuple(refs[3 * r:3 * r + 3]) for r in range(ATT_RING)]


def _moba_prompt_kernel(q_ref, kt_ref, v_ref, km_ref, bias_ref, o_ref, *scratch, nq):
    t = ATT_TILE
    tq = Q_TILE
    tile_shift = t.bit_length() - 1
    nb = km_ref.shape[0]
    qi = pl.program_id(2)
    lane = lax.broadcasted_iota(jnp.int32, (tq, LANES), 1)
    blk = lax.broadcasted_iota(jnp.int32, (nb, tq), 0)
    own = (tq // t) * qi + (lax.broadcasted_iota(jnp.int32, (nb, tq), 1) >> tile_shift)

    q_ops = []
    for pair in range(ATT_PAIRS):
        cols = slice(pair * LANES, (pair + 1) * LANES)
        q = q_ref[:, cols].astype(F32)
        km = km_ref[:, 0, cols]
        km_hi = km.astype(BF16)
        km_lo = (km - km_hi.astype(F32)).astype(BF16)
        for e in range(2):
            qe = _half_masked(q, lane, e)
            gt = _dot_nt(km_hi, qe) + _dot_nt(km_lo, qe)
            rank = jnp.zeros((nb, tq), jnp.int32)
            for m in range(nb):
                gm = gt[m:m + 1, :]
                ahead = (gm > gt) | ((gm == gt) & (m < blk))
                rank = rank + jnp.where(ahead & (m < own), 1, 0)
            keep = (blk < own) & (rank < MOBA_TOPK) & (jnp.abs(gt) < jnp.inf)
            pen = jnp.where(keep | (blk == own), 0.0, NEG_BIG)
            pen = jnp.concatenate([pen, jnp.zeros((LANES - nb, tq), F32)], axis=0)
            q_ops.append(jnp.concatenate([qe, pen.T.astype(BF16)], axis=1))

    kt_row = lax.broadcasted_iota(jnp.int32, (LANES, tq), 0)
    kt_blk = lax.broadcasted_iota(jnp.int32, (LANES, tq), 1) >> tile_shift

    def kt_tile(i, jj):
        pair = i // 2
        onehot = jnp.where(kt_row == (tq // t) * jj + kt_blk, 1.0, 0.0).astype(BF16)
        return jnp.concatenate(
            [kt_ref[0, pair * LANES:(pair + 1) * LANES, jj * tq:(jj + 1) * tq], onehot], axis=0)

    n_ops = 2 * ATT_PAIRS
    v_cols = [slice((i // 2) * LANES, (i // 2 + 1) * LANES) for i in range(n_ops)]

    def body(n):
        res = _attend_static(q_ops, kt_tile, v_ref, v_cols, bias_ref, tuple(range(n_ops)),
                             _scratch_ring(scratch), n)
        for pair in range(ATT_PAIRS):
            (a0, l0), (a1, l1) = res[2 * pair], res[2 * pair + 1]
            o_ref[:, pair * LANES:(pair + 1) * LANES] = jnp.where(
                lane < HEAD_DIM, a0 / l0, a1 / l1).astype(o_ref.dtype)

    _per_query_tile(nq, body)


def _attn_scratch(nq, seq):
    one = [pltpu.VMEM((nq, Q_TILE, Q_TILE), F32),
           pltpu.VMEM((Q_TILE, seq), BF16),
           pltpu.VMEM((Q_TILE, LANES), F32)]
    return one * ATT_RING


def _moba_prompt(qa, ka_t, va, km, bias_tiles, batch, seq):
    tq = Q_TILE
    nq = seq // tq
    nb = seq // MOBA_BLOCK
    w = ATT_PAIRS * LANES
    assert WIDTH_A % w == 0
    grid = (batch, WIDTH_A // w, nq)
    qmap = lambda b, g, qi: (b * nq + qi, g)
    return pl.pallas_call(
        functools.partial(_moba_prompt_kernel, nq=nq),
        grid=grid,
        in_specs=[pl.BlockSpec((tq, w), qmap),
                  pl.BlockSpec((1, w, seq), lambda b, g, qi: (b, g, 0)),
                  pl.BlockSpec((seq, w), lambda b, g, qi: (b, g)),
                  pl.BlockSpec((nb, 1, w), lambda b, g, qi: (b, 0, g)),
                  pl.BlockSpec((2 * ATT_PAIRS, 3, ATT_TILE, ATT_TILE),
                               lambda b, g, qi: (g, 0, 0, 0))],
        out_specs=pl.BlockSpec((tq, w), qmap),
        out_shape=jax.ShapeDtypeStruct((batch * seq, WIDTH_A), BF16),
        scratch_shapes=_attn_scratch(nq, seq),
        compiler_params=pltpu.CompilerParams(
            dimension_semantics=("arbitrary", "arbitrary", "arbitrary"),
            vmem_limit_bytes=VMEM_LIMIT_BYTES),
        name="moba_prompt",
    )(qa, ka_t, va, km, bias_tiles)


def _diff_prompt_kernel(q_ref, kt_ref, v_ref, bias_ref, lam_ref, gain_ref, o_ref, *scratch,
                        lam_init, nq):
    tq = Q_TILE
    lane = lax.broadcasted_iota(jnp.int32, (tq, LANES), 1)
    q_ops = []
    for head in range(ATT_PAIRS):
        q = q_ref[:, head * LANES:(head + 1) * LANES].astype(F32)
        q_ops += [_half_masked(q, lane, c) for c in range(2)]

    def kt_tile(i, jj):
        head = i // 2
        return kt_ref[0, head * LANES:(head + 1) * LANES, jj * tq:(jj + 1) * tq]

    n_ops = 2 * ATT_PAIRS
    v_cols = [slice((i // 2) * LANES, (i // 2 + 1) * LANES) for i in range(n_ops)]

    def body(n):
        res = _attend_static(q_ops, kt_tile, v_ref, v_cols, bias_ref,
                             tuple(i // 2 for i in range(n_ops)), _scratch_ring(scratch), n)
        lam = _lambda(lam_ref, lam_init)
        for head in range(ATT_PAIRS):
            (a0, l0), (a1, l1) = res[2 * head], res[2 * head + 1]
            o = a0 / l0 - lam * (a1 / l1)
            o_ref[:, head * LANES:(head + 1) * LANES] = (
                _rmsnorm(o, gain_ref[...]) * (1.0 - lam_init)).astype(o_ref.dtype)

    _per_query_tile(nq, body)


def _diff_prompt(qd, kd_t, vd, bias_tiles, lam_vecs, gain, lam_init, batch, seq):
    tq = Q_TILE
    nq = seq // tq
    w = ATT_PAIRS * LANES
    assert WIDTH_B % w == 0
    grid = (batch, WIDTH_B // w, nq)
    qmap = lambda b, g, qi: (b * nq + qi, g)
    return pl.pallas_call(
        functools.partial(_diff_prompt_kernel, lam_init=lam_init, nq=nq),
        grid=grid,
        in_specs=[pl.BlockSpec((tq, w), qmap),
                  pl.BlockSpec((1, w, seq), lambda b, g, qi: (b, g, 0)),
                  pl.BlockSpec((seq, w), lambda b, g, qi: (b, g)),
                  pl.BlockSpec((ATT_PAIRS, 3, ATT_TILE, ATT_TILE), lambda b, g, qi: (g, 0, 0, 0)),
                  pl.BlockSpec((4, HEAD_DIM), lambda b, g, qi: (0, 0)),
                  pl.BlockSpec((1, 2 * HEAD_DIM), lambda b, g, qi: (0, 0))],
        out_specs=pl.BlockSpec((tq, w), qmap),
        out_shape=jax.ShapeDtypeStruct((batch * seq, WIDTH_B), BF16),
        scratch_shapes=_attn_scratch(nq, seq),
        compiler_params=pltpu.CompilerParams(
            dimension_semantics=("arbitrary", "arbitrary", "arbitrary"),
            vmem_limit_bytes=VMEM_LIMIT_BYTES),
        name="diff_prompt",
    )(qd, kd_t, vd, bias_tiles, lam_vecs, gain)


def _toeplitz(w, t):
    h = w.shape[0]
    m = jnp.broadcast_to(w[:, None, :], (h, t, 2 * t)).reshape(h, 2 * t * t)
    return m[:, :t * (2 * t - 1)].reshape(h, t, 2 * t - 1)[:, :, :t]


def _bias_tiles(bias_t):
    t = ATT_TILE
    assert t + 1 >= MAX_DISTANCE
    k = jnp.arange(2 * t)
    d = jnp.where(k <= t, -k, 2 * t - k)
    diag = _toeplitz(jnp.where(d >= 0, _bias_of(bias_t, d), -jnp.inf), t)
    sub = _toeplitz(_bias_of(bias_t, d + t), t)
    far = _toeplitz(_bias_of(bias_t, d + 2 * t), t)
    return jnp.stack([diag, sub, far], axis=1)


def _page_stream(pt_ref, pk_ref, pv_ref, buf_ref, sem_ref, n_seq, n_chunks):
    pps = PAGES_PER_STEP
    ahead = RING_SLOTS - 1
    b, ph, j = pl.program_id(0), pl.program_id(1), pl.program_id(2)
    step = (b * 2 + ph) * n_chunks + j
    total = n_seq * 2 * n_chunks

    def copies(pool_ref, bb, jj, slot):
        return [pltpu.make_async_copy(pool_ref.at[pt_ref[bb, jj * pps + i]],
                                      buf_ref.at[slot, i], sem_ref.at[slot])
                for i in range(pps)]

    def start(chunk):
        jj, seq_phase = chunk % n_chunks, chunk // n_chunks
        for phase, pool_ref in ((0, pk_ref), (1, pv_ref)):
            @pl.when(seq_phase % 2 == phase)
            def _(pool_ref=pool_ref):
                for i, cp in enumerate(copies(pool_ref, seq_phase // 2, jj, chunk % RING_SLOTS)):
                    cp.start(priority=i % 2)

    @pl.when(step == 0)
    def _():
        for chunk in range(min(ahead, total)):
            start(jnp.int32(chunk))

    @pl.when(step + ahead < total)
    def _():
        start(step + ahead)

    slot = step % RING_SLOTS

    def wait(pool_ref):
        for cp in copies(pool_ref, b, j, slot):
            cp.wait()

    return slot, wait


def _moba_sample_kernel(pt_ref, pk_ref, pv_ref, q_ref, kn_ref, vn_ref, bias_ref, o_ref,
                        s_ref, snew_ref, acc_ref, l_ref, m_ref, bmax_ref, gate_ref, sel_ref,
                        buf_ref, sem_ref, *, n_seq, n_chunks, tokens):
    pps = PAGES_PER_STEP
    slot, wait = _page_stream(pt_ref, pk_ref, pv_ref, buf_ref, sem_ref, n_seq, n_chunks)
    rows = q_ref.shape[1]
    groups = rows // tokens
    ppb = MOBA_BLOCK // PAGE_SIZE
    bps = pps // ppb
    n_blocks = n_chunks * bps
    ph = pl.program_id(1)
    j = pl.program_id(2)
    lane = lax.broadcasted_iota(jnp.int32, (rows, LANES), 1)
    lanef = lane.astype(F32)
    q = q_ref[0]

    @pl.when((ph == 0) & (j == 0))
    def _():
        snew_ref[...] = _dot_nt(q, kn_ref[0]) + bias_ref[2]
        bmax_ref[...] = jnp.full((rows, LANES), -jnp.inf, F32)
        gate_ref[...] = jnp.zeros((rows, LANES), F32)

    @pl.when(ph == 0)
    def _():
        wait(pk_ref)
        is_last = j == n_chunks - 1
        bmax = bmax_ref[...]
        gate = gate_ref[...]
        for b in range(bps):
            smax = ssum = None
            for pg in range(ppb):
                i = b * ppb + pg
                raw = _dot(q, buf_ref[slot, i].astype(BF16))
                bias = bias_ref[0]
                if i == pps - 1:
                    bias = jnp.where(is_last, bias_ref[1], bias)
                s = raw + bias
                s_ref[j, :, i * PAGE_SIZE:(i + 1) * PAGE_SIZE] = s
                smax = s if smax is None else jnp.maximum(smax, s)
                ssum = raw if ssum is None else ssum + raw
            blk = j * bps + b
            bmax = jnp.where(lane == blk, jnp.max(smax, axis=1, keepdims=True), bmax)
            gate = jnp.where(lane == blk, jnp.sum(ssum, axis=1, keepdims=True), gate)
        bmax_ref[...] = bmax
        gate_ref[...] = gate

    @pl.when((ph == 1) & (j == 0))
    def _():
        g = jnp.where(lane < n_blocks, gate_ref[...], -jnp.inf)
        sel = jnp.zeros((rows, LANES), F32)
        for _ in range(MOBA_TOPK):
            top = jnp.max(g, axis=1, keepdims=True)
            first = jnp.min(jnp.where(g == top, lanef, float(LANES)), axis=1, keepdims=True)
            pick = lanef == first
            finite = jnp.where(jnp.abs(top) < jnp.inf, 1.0, 0.0)
            sel = jnp.maximum(sel, jnp.where(pick, finite, 0.0))
            g = jnp.where(pick, -jnp.inf, g)
        sel_ref[...] = sel
        m_sel = jnp.max(jnp.where(sel > 0.5, bmax_ref[...], -jnp.inf), axis=1, keepdims=True)
        m_new = jnp.max(snew_ref[...], axis=1, keepdims=True)
        m_ref[...] = jnp.broadcast_to(jnp.maximum(m_sel, m_new), (rows, LANES))
        acc_ref[...] = jnp.zeros(acc_ref.shape, F32)
        l_ref[...] = jnp.zeros(l_ref.shape, F32)

    @pl.when(ph == 1)
    def _():
        wait(pv_ref)
        m = m_ref[...]
        selv = sel_ref[...]
        acc = acc_ref[...]
        l = l_ref[...]
        for i in range(pps):
            blk = j * bps + i // ppb
            on = jnp.max(jnp.where(lane == blk, selv, 0.0), axis=1, keepdims=True)
            on = jnp.broadcast_to(on, (rows, LANES)) > 0.5
            s = s_ref[j, :, i * PAGE_SIZE:(i + 1) * PAGE_SIZE]
            p = jnp.exp(jnp.where(on, s - m, -jnp.inf))
            l = l + p
            acc = acc + _dot_nt(p.astype(BF16), buf_ref[slot, i].astype(BF16))
        acc_ref[...] = acc
        l_ref[...] = l

    @pl.when((ph == 1) & (j == n_chunks - 1))
    def _():
        p = jnp.exp(snew_ref[...] - m_ref[...])
        acc = acc_ref[...] + _dot(p.astype(BF16), vn_ref[0])
        l = jnp.sum(l_ref[...] + p, axis=1, keepdims=True)
        o = acc / l
        grp = lax.broadcasted_iota(jnp.int32, (groups, WIDTH_A), 0)
        col = lax.broadcasted_iota(jnp.int32, (groups, WIDTH_A), 1)
        head_shift = HEAD_DIM.bit_length() - 1
        w = jnp.where((col >> head_shift) == grp, 1.0, 0.0)
        tok = jnp.sum(o.reshape(tokens, groups, WIDTH_A) * w[None], axis=1)
        o_ref[0] = tok.astype(o_ref.dtype)


def _moba_sample(page_table, pool_kt, pool_vt, q_bd, k_new, v_new, bias_rows, *, tokens):
    pps = PAGES_PER_STEP
    n_seq, n_pages = page_table.shape
    n_chunks = n_pages // pps
    rows = q_bd.shape[1]
    seq_map = lambda b, ph, j, pt: (b, 0, 0)
    hbm = pl.BlockSpec(memory_space=pl.ANY)
    in_specs = [hbm, hbm,
                pl.BlockSpec((1, rows, WIDTH_A), seq_map),
                pl.BlockSpec((1, LANES, WIDTH_A), seq_map),
                pl.BlockSpec((1, LANES, WIDTH_A), seq_map),
                pl.BlockSpec((3, rows, LANES), lambda b, ph, j, pt: (0, 0, 0))]
    vec = pltpu.VMEM((rows, LANES), F32)
    grid_spec = pltpu.PrefetchScalarGridSpec(
        num_scalar_prefetch=1,
        grid=(n_seq, 2, n_chunks),
        in_specs=in_specs,
        out_specs=pl.BlockSpec((1, tokens, WIDTH_A), seq_map),
        scratch_shapes=[pltpu.VMEM((n_chunks, rows, pps * PAGE_SIZE), F32),
                        vec,
                        pltpu.VMEM((rows, WIDTH_A), F32),
                        vec, vec, vec, vec, vec,
                        pltpu.VMEM((RING_SLOTS, pps) + pool_kt.shape[1:], F32),
                        pltpu.SemaphoreType.DMA((RING_SLOTS,))])
    return pl.pallas_call(
        functools.partial(_moba_sample_kernel, n_seq=n_seq, n_chunks=n_chunks, tokens=tokens),
        grid_spec=grid_spec,
        out_shape=jax.ShapeDtypeStruct((n_seq, tokens, WIDTH_A), BF16),
        compiler_params=pltpu.CompilerParams(
            dimension_semantics=("arbitrary", "arbitrary", "arbitrary"),
            vmem_limit_bytes=VMEM_LIMIT_BYTES),
        name="moba_sample",
    )(page_table, pool_kt, pool_vt, q_bd, k_new, v_new, bias_rows)


def _diff_sample_kernel(pt_ref, pk_ref, pv_ref, q_ref, kn_ref, vn_ref, bias_ref, bnew_ref,
                        lam_ref, gain_ref, o_ref, s_ref, snew_ref, acc_ref, l_ref, m_ref,
                        buf_ref, sem_ref, *, n_seq, n_chunks, tokens, lam_init):
    pps = PAGES_PER_STEP
    slot, wait = _page_stream(pt_ref, pk_ref, pv_ref, buf_ref, sem_ref, n_seq, n_chunks)
    rows = q_ref.shape[1]
    pw = buf_ref.shape[2]
    ph = pl.program_id(1)
    j = pl.program_id(2)
    q = q_ref[0]

    @pl.when((ph == 0) & (j == 0))
    def _():
        snew_ref[...] = _dot_nt(q, kn_ref[0]) + bnew_ref[...]
        m_ref[...] = jnp.full((rows, LANES), -jnp.inf, F32)

    @pl.when(ph == 0)
    def _():
        wait(pk_ref)
        is_last = j == n_chunks - 1
        mx = m_ref[...]
        for i in range(pps):
            bias = bias_ref[0]
            if i == pps - 1:
                bias = jnp.where(is_last, bias_ref[1], bias)
            s = _dot_nt(q, buf_ref[slot, i].astype(BF16)) + bias
            s_ref[j, :, i * pw:(i + 1) * pw] = s
            mx = jnp.maximum(mx, _lane_fold(jnp.maximum, s))
        m_ref[...] = mx

    @pl.when((ph == 1) & (j == 0))
    def _():
        m = jnp.maximum(jnp.max(m_ref[...], axis=1, keepdims=True),
                        jnp.max(snew_ref[...], axis=1, keepdims=True))
        m_ref[...] = jnp.broadcast_to(m, (rows, LANES))
        acc_ref[...] = jnp.zeros(acc_ref.shape, F32)
        l_ref[...] = jnp.zeros(l_ref.shape, F32)

    @pl.when(ph == 1)
    def _():
        wait(pv_ref)
        m = m_ref[...]
        mw = jnp.concatenate([m] * (pw // LANES), axis=1)
        acc = acc_ref[...]
        l = l_ref[...]
        for i in range(pps):
            p = jnp.exp(s_ref[j, :, i * pw:(i + 1) * pw] - mw)
            l = l + _lane_fold(jnp.add, p)
            acc = acc + _dot(p.astype(BF16), buf_ref[slot, i].astype(BF16))
        acc_ref[...] = acc
        l_ref[...] = l

    @pl.when((ph == 1) & (j == n_chunks - 1))
    def _():
        p = jnp.exp(snew_ref[...] - m_ref[...])
        acc = acc_ref[...] + _dot(p.astype(BF16), vn_ref[0])
        l = jnp.sum(l_ref[...] + p, axis=1, keepdims=True)
        o = (acc / l).reshape(tokens, rows // tokens, LANES)
        lam = _lambda(lam_ref, lam_init)
        sub = lax.broadcasted_iota(jnp.int32, (rows // tokens, LANES), 0)
        gain = gain_ref[...]
        parts = []
        for h in range(N_HEADS_B):
            w = jnp.where(sub == 2 * h, 1.0, jnp.where(sub == 2 * h + 1, -lam, 0.0))
            parts.append(_rmsnorm(jnp.sum(o * w[None], axis=1), gain) * (1.0 - lam_init))
        o_ref[0] = jnp.concatenate(parts, axis=1).astype(o_ref.dtype)


def _diff_sample(page_table, pool_k, pool_v, q2, k_new, v_new, bias_rows, bias_new, lam_vecs, gain,
                 *, tokens, lam_init):
    pps = PAGES_PER_STEP
    n_seq, n_pages = page_table.shape
    n_chunks = n_pages // pps
    rows = q2.shape[1]
    pw = pool_k.shape[1]
    seq_map = lambda b, ph, j, pt: (b, 0, 0)
    const2 = lambda b, ph, j, pt: (0, 0)
    hbm = pl.BlockSpec(memory_space=pl.ANY)
    in_specs = [hbm, hbm,
                pl.BlockSpec((1, rows, LANES), seq_map),
                pl.BlockSpec((1, LANES, LANES), seq_map),
                pl.BlockSpec((1, LANES, LANES), seq_map),
                pl.BlockSpec((2, rows, pw), lambda b, ph, j, pt: (0, 0, 0)),
                pl.BlockSpec((rows, LANES), const2),
                pl.BlockSpec((4, HEAD_DIM), const2),
                pl.BlockSpec((1, 2 * HEAD_DIM), const2)]
    vec = pltpu.VMEM((rows, LANES), F32)
    grid_spec = pltpu.PrefetchScalarGridSpec(
        num_scalar_prefetch=1,
        grid=(n_seq, 2, n_chunks),
        in_specs=in_specs,
        out_specs=pl.BlockSpec((1, tokens, WIDTH_B), seq_map),
        scratch_shapes=[pltpu.VMEM((n_chunks, rows, pps * pw), F32),
                        vec, vec, vec, vec,
                        pltpu.VMEM((RING_SLOTS, pps) + pool_k.shape[1:], F32),
                        pltpu.SemaphoreType.DMA((RING_SLOTS,))])
    return pl.pallas_call(
        functools.partial(_diff_sample_kernel, n_seq=n_seq, n_chunks=n_chunks, tokens=tokens,
                          lam_init=lam_init),
        grid_spec=grid_spec,
        out_shape=jax.ShapeDtypeStruct((n_seq, tokens, WIDTH_B), BF16),
        compiler_params=pltpu.CompilerParams(
            dimension_semantics=("arbitrary", "arbitrary", "arbitrary"),
            vmem_limit_bytes=VMEM_LIMIT_BYTES),
        name="diff_sample",
    )(page_table, pool_k, pool_v, q2, k_new, v_new, bias_rows, bias_new, lam_vecs, gain)


def _block_diag(q, n_seq, tokens, groups, keep):
    w = q.shape[1] // groups
    q4 = q.reshape(n_seq, tokens, groups, 1, w)
    eye = (jnp.arange(groups)[:, None] % keep == jnp.arange(keep)[None, :]).astype(q.dtype)
    return (q4 * eye.reshape(1, 1, groups, keep, 1)).reshape(n_seq, tokens * groups, keep * w)


def _pad_rows(x, n_seq):
    x = x.reshape(n_seq, -1, x.shape[-1])
    return jnp.pad(x, ((0, 0), (0, LANES - x.shape[1]), (0, 0)))


def _sample_bias(bias_t, tokens, heads_per_group):
    assert PAGE_SIZE + 1 >= MAX_DISTANCE
    tok = jnp.arange(tokens)[:, None]
    col = jnp.arange(LANES)[None, :]
    far = _bias_of(bias_t, jnp.broadcast_to(2 * PAGE_SIZE, (tokens, LANES)))
    last = _bias_of(bias_t, PAGE_SIZE + tok - col)
    new = jnp.where((col <= tok) & (col < tokens), _bias_of(bias_t, tok - col), -jnp.inf)

    def rows(x):
        x = jnp.repeat(x.transpose(1, 0, 2), heads_per_group, axis=1)
        return x.reshape(-1, LANES)

    return rows(far), rows(last), rows(new)


def _spread_heads(x, n_heads, groups):
    r, k = x.shape
    row_head = (jnp.arange(r) % groups) // (groups // n_heads)
    own = row_head[:, None, None] == jnp.arange(n_heads)[None, None, :]
    return jnp.where(own, x[:, :, None], -jnp.inf).reshape(r, k * n_heads)


def _merge_kernel(x_ref, oa_ref, ob_ref, g_ref, wg_ref, woa_ref, wob_ref, wo_ref, o_ref):
    x = x_ref[...]
    h = _rmsnorm(x, g_ref[...]).astype(BF16)
    ga = 1.0 / (1.0 + jnp.exp(-_dot(h, wg_ref[:, :D_MODEL])))
    m = ga * _dot(oa_ref[...], woa_ref[...])
    gb = 1.0 / (1.0 + jnp.exp(-_dot(h, wg_ref[:, D_MODEL:])))
    m = m + gb * _dot(ob_ref[...], wob_ref[...])
    o_ref[...] = x + _dot(m.astype(BF16), wo_ref[...])


def _merge(x, oa, ob, g, wg, woa, wob, wo, tm):
    n = x.shape[0]
    row = lambda i: (i, 0)
    const = lambda i: (0, 0)
    return pl.pallas_call(
        _merge_kernel,
        grid=(n // tm,),
        in_specs=[pl.BlockSpec((tm, D_MODEL), row),
                  pl.BlockSpec((tm, WIDTH_A), row),
                  pl.BlockSpec((tm, WIDTH_B), row),
                  pl.BlockSpec((1, D_MODEL), const),
                  pl.BlockSpec((D_MODEL, 2 * D_MODEL), const),
                  pl.BlockSpec((WIDTH_A, D_MODEL), const),
                  pl.BlockSpec((WIDTH_B, D_MODEL), const),
                  pl.BlockSpec((D_MODEL, D_MODEL), const)],
        out_specs=pl.BlockSpec((tm, D_MODEL), row),
        out_shape=jax.ShapeDtypeStruct((n, D_MODEL), F32),
        compiler_params=pltpu.CompilerParams(
            dimension_semantics=("arbitrary",), vmem_limit_bytes=VMEM_LIMIT_BYTES),
        name="merge",
    )(x, oa, ob, g, wg, woa, wob, wo)


def _ffn_kernel(*refs, tm, seq_len, has_prev):
    if has_prev:
        (x_ref, g_ref, wup_ref, cw_ref, cb_ref, wdn_ref, gfin_ref, e1_ref, e2_ref,
         y_ref, u_ref, act_ref) = refs
    else:
        (x_ref, g_ref, wup_ref, cw_ref, cb_ref, wdn_ref, gfin_ref,
         y_ref, u_ref, act_ref, carry_ref) = refs
    i = pl.program_id(0)
    x = x_ref[...]
    h = _rmsnorm(x, g_ref[...]).astype(BF16)
    row = lax.broadcasted_iota(jnp.int32, (tm, FF_CHUNK), 0)
    row8 = lax.broadcasted_iota(jnp.int32, (SUBLANES, FF_CHUNK), 0)
    if has_prev:
        pos = row & (seq_len - 1)
    else:
        @pl.when((i * tm) % seq_len == 0)
        def _():
            carry_ref[...] = jnp.zeros(carry_ref.shape, F32)

    for c in range(D_FF // FF_CHUNK):
        halves = []
        for part in range(2):
            cols = slice(part * D_FF + c * FF_CHUNK, part * D_FF + (c + 1) * FF_CHUNK)
            u = _dot(h, wup_ref[:, cols])
            um1 = pltpu.roll(u, 1, 0)
            um2 = pltpu.roll(u, 2, 0)
            if has_prev:
                um1 = jnp.where(pos == 0, e1_ref[:, cols], um1)
                um2 = jnp.where(pos < 2, e2_ref[:, cols], um2)
                u_ref[:, cols] = u
            else:
                prev = carry_ref[:, cols]
                top1 = jnp.where(row8 == 0, pltpu.roll(prev, 1, 0), um1[:SUBLANES])
                top2 = jnp.where(row8 < 2, pltpu.roll(prev, 2, 0), um2[:SUBLANES])
                um1 = jnp.concatenate([top1, um1[SUBLANES:]], axis=0)
                um2 = jnp.concatenate([top2, um2[SUBLANES:]], axis=0)
                carry_ref[:, cols] = u[tm - SUBLANES:]
                u_ref[:, cols] = u[tm - SUBLANES:]
            cw = cw_ref[:, cols]
            halves.append(((cb_ref[:, cols] + cw[0:1] * um2) + cw[1:2] * um1) + cw[2:3] * u)
        gate, val = halves
        act = (gate * (1.0 / (1.0 + jnp.exp(-gate)))) * val
        act_ref[:, c * FF_CHUNK:(c + 1) * FF_CHUNK] = act.astype(BF16)

    x3 = x + _dot(act_ref[...], wdn_ref[...])
    y_ref[...] = _rmsnorm(x3, gfin_ref[...])


def _ffn(x, g, wup, cw, cb, wdn, gfin, prev, tm, seq_len):
    n = x.shape[0]
    has_prev = prev is not None
    row = lambda i: (i, 0)
    const = lambda i: (0, 0)
    in_specs = [pl.BlockSpec((tm, D_MODEL), row),
                pl.BlockSpec((1, D_MODEL), const),
                pl.BlockSpec((D_MODEL, 2 * D_FF), const),
                pl.BlockSpec((CONV_W, 2 * D_FF), const),
                pl.BlockSpec((1, 2 * D_FF), const),
                pl.BlockSpec((D_FF, D_MODEL), const),
                pl.BlockSpec((1, D_MODEL), const)]
    scratch = [pltpu.VMEM((tm, D_FF), BF16)]
    if has_prev:
        assert n == tm and tm % seq_len == 0 and seq_len & (seq_len - 1) == 0
        in_specs += [pl.BlockSpec((tm, 2 * D_FF), row)] * 2
        u_shape, u_spec = (n, 2 * D_FF), pl.BlockSpec((tm, 2 * D_FF), row)
        args = (x, g, wup, cw, cb, wdn, gfin) + tuple(prev)
    else:
        assert seq_len % tm == 0
        tiles_per_seq = seq_len // tm
        u_shape = (n // seq_len * SUBLANES, 2 * D_FF)
        u_spec = pl.BlockSpec((SUBLANES, 2 * D_FF), lambda i: (i // tiles_per_seq, 0))
        scratch.append(pltpu.VMEM((SUBLANES, 2 * D_FF), F32))
        args = (x, g, wup, cw, cb, wdn, gfin)
    return pl.pallas_call(
        functools.partial(_ffn_kernel, tm=tm, seq_len=seq_len, has_prev=has_prev),
        grid=(n // tm,),
        in_specs=in_specs,
        out_specs=[pl.BlockSpec((tm, D_MODEL), row), u_spec],
        out_shape=[jax.ShapeDtypeStruct((n, D_MODEL), F32), jax.ShapeDtypeStruct(u_shape, F32)],
        scratch_shapes=scratch,
        compiler_params=pltpu.CompilerParams(
            dimension_semantics=("arbitrary",), vmem_limit_bytes=VMEM_LIMIT_BYTES),
        name="ffn_sample" if has_prev else "ffn_prompt",
    )(*args)


def kernel(x_prompt, x_sample, cache_moba_k, cache_moba_v, cache_diff_k, cache_diff_v, state_conv, page_table, rel_bias, norm_attn, w_in, w_gate, w_out_a, w_out_b, w_out, lambda_q1, lambda_k1, lambda_q2, lambda_k2, diff_norm, norm_ffn, w_up, conv_w, conv_b, w_down, norm_final):
    batch, seq, _ = x_prompt.shape
    n_seq, tokens, _ = x_sample.shape
    depth = w_in.shape[0]
    n_phys = cache_moba_k.shape[1]
    n_pages = page_table.shape[1]
    assert depth == 1 and seq % Q_TILE == 0 and ATT_TILE & (ATT_TILE - 1) == 0
    assert (n_pages * PAGE_SIZE) % MOBA_BLOCK == 0 and n_pages % PAGES_PER_STEP == 0
    assert n_pages * PAGE_SIZE // MOBA_BLOCK <= LANES and CONV_W - 1 <= tokens <= LANES // N_HEADS_B
    l = 0
    lam_init = 0.8 - 0.6 * math.exp(-0.3 * l)

    bias_a = rel_bias[:, :N_HEADS_A].T
    bias_d = rel_bias[:, N_HEADS_A:].T
    row = lambda v: v.reshape(1, -1)
    w_in_b = w_in[l].astype(BF16)
    w_gate_b = w_gate[l].astype(BF16)
    w_oa_b = w_out_a[l].astype(BF16)
    w_ob_b = w_out_b[l].astype(BF16)
    w_o_b = w_out[l].astype(BF16)
    w_up_b = w_up[l].astype(BF16)
    w_dn_b = w_down[l].astype(BF16)
    lam_vecs = jnp.stack([lambda_q1[l], lambda_k1[l], lambda_q2[l], lambda_k2[l]]).astype(F32)
    gain_d = row(diff_norm[l])
    g_attn, g_ffn, g_fin = row(norm_attn[l]), row(norm_ffn[l]), row(norm_final)
    cb = row(conv_b[l])

    xp = x_prompt.reshape(batch * seq, D_MODEL)
    qa, ka_tb, va, qd, kd_tb, vd, ka_t, va_t, kd_r, vd_r, km = _proj(xp, g_attn, w_in_b, 512, seq)
    oa = _moba_prompt(qa, ka_tb, va, km, _bias_tiles(bias_a), batch, seq)
    ob = _diff_prompt(qd, kd_tb, vd, _bias_tiles(bias_d), lam_vecs, gain_d, lam_init, batch, seq)
    x2 = _merge(xp, oa, ob, g_attn, w_gate_b, w_oa_b, w_ob_b, w_o_b, 512)
    yp, tail_p = _ffn(x2, g_ffn, w_up_b, conv_w[l], cb, w_dn_b, g_fin, None, 512, seq)
    conv_p = tail_p.reshape(batch, SUBLANES, 2 * D_FF)[:, SUBLANES - (CONV_W - 1):]

    n_s = n_seq * tokens
    xs = x_sample.reshape(n_s, D_MODEL)
    qa_s, ka_s, va_s, qd_s, kd_s, vd_s, ka_sf, va_sf, kd_sf, vd_sf = _proj(xs, g_attn, w_in_b, n_s)

    pool_t = lambda c: jnp.transpose(c[l], (0, 2, 3, 1)).reshape(n_phys, WIDTH_A, PAGE_SIZE)
    pool_r = lambda c: c[l].reshape(n_phys, PAGE_SIZE * N_HEADS_B, 2 * HEAD_DIM)
    far_a, last_a, new_a = _sample_bias(bias_a, tokens, 1)
    oa_s = _moba_sample(page_table, pool_t(cache_moba_k), pool_t(cache_moba_v),
                        _block_diag(qa_s, n_seq, tokens, N_HEADS_A, N_HEADS_A),
                        _pad_rows(ka_s, n_seq), _pad_rows(va_s, n_seq),
                        jnp.stack([far_a, last_a, new_a]), tokens=tokens)
    far_d, last_d, new_d = _sample_bias(bias_d, tokens, 2)
    groups_d = 2 * N_HEADS_B
    spread = lambda x: _spread_heads(x, N_HEADS_B, groups_d)
    per_head = lambda x: x.reshape(n_s * N_HEADS_B, 2 * HEAD_DIM)
    oa_d = _diff_sample(page_table, pool_r(cache_diff_k), pool_r(cache_diff_v),
                        _block_diag(qd_s, n_seq, tokens, groups_d, 2),
                        _pad_rows(per_head(kd_s), n_seq), _pad_rows(per_head(vd_s), n_seq),
                        jnp.stack([spread(far_d), spread(last_d)]),
                        spread(new_d[:, :LANES // N_HEADS_B]),
                        lam_vecs, gain_d, tokens=tokens, lam_init=lam_init)
    x2s = _merge(xs, oa_s.reshape(n_s, WIDTH_A), oa_d.reshape(n_s, WIDTH_B), g_attn,
                 w_gate_b, w_oa_b, w_ob_b, w_o_b, n_s)
    st = state_conv[l]
    zero = jnp.zeros((n_seq, 1, 2 * D_FF), F32)
    e1 = jnp.concatenate([st[:, 1:2]] + [zero] * (tokens - 1), axis=1).reshape(n_s, 2 * D_FF)
    e2 = jnp.concatenate([st[:, 0:2]] + [zero] * (tokens - 2), axis=1).reshape(n_s, 2 * D_FF)
    ys, u_s = _ffn(x2s, g_ffn, w_up_b, conv_w[l], cb, w_dn_b, g_fin, (e1, e2), n_s, tokens)
    conv_s = u_s.reshape(n_seq, tokens, 2 * D_FF)[:, tokens - (CONV_W - 1):]

    shp_a = lambda a, b_, t_: a.reshape(1, b_, t_, N_HEADS_A, HEAD_DIM)
    shp_d = lambda a, b_, t_: a.reshape(1, b_, t_, N_HEADS_B, 2 * HEAD_DIM)
    untr = lambda a: a.reshape(batch, N_HEADS_A, HEAD_DIM, seq).transpose(0, 3, 1, 2)[None]
    return (yp.reshape(batch, seq, D_MODEL), ys.reshape(n_seq, tokens, D_MODEL),
            untr(ka_t), untr(va_t),
            shp_d(kd_r, batch, seq), shp_d(vd_r, batch, seq), conv_p[None],
            shp_a(ka_sf, n_seq, tokens), shp_a(va_sf, n_seq, tokens),
            shp_d(kd_sf, n_seq, tokens), shp_d(vd_sf, n_seq, tokens), conv_s[None])
```

```python
import functools
import math

import jax
import jax.numpy as jnp
from jax import lax
from jax.experimental import pallas as pl
from jax.experimental.pallas import tpu as pltpu

F32 = jnp.float32
BF16 = jnp.bfloat16

D_MODEL = 1024
HEAD_DIM = 64
N_HEADS_A = D_MODEL // 128
N_HEADS_B = D_MODEL // 256
WIDTH_A = N_HEADS_A * HEAD_DIM
WIDTH_B = N_HEADS_B * 2 * HEAD_DIM
N_IN = 3 * WIDTH_A + 3 * WIDTH_B
MOBA_BLOCK = 256
MOBA_TOPK = 3
NUM_BUCKETS = 32
MAX_DISTANCE = 128
D_FF = ((8 * D_MODEL // 3 + 127) // 128) * 128
CONV_W = 3
EPS = 1e-6
PAGE_SIZE = 128
SCALE = HEAD_DIM ** -0.5
LOG2E = math.log2(math.e)

LANES = 128
SUBLANES = 8
VMEM_LIMIT_BYTES = 56 * 1024 * 1024

ATT_TILE = MOBA_BLOCK
Q_TILE = 2 * ATT_TILE
ATT_PAIRS = 2
ATT_RING = 2
FF_CHUNK = 256
PAGES_PER_STEP = 16
RING_SLOTS = 3
NEG_BIG = -1e30


def _dot(a, b):
    return jnp.dot(a, b, preferred_element_type=F32)


def _dot_nt(a, b):
    return lax.dot_general(a, b, (((1,), (1,)), ((), ())), preferred_element_type=F32)


def _rmsnorm(x, g):
    return (x * lax.rsqrt(jnp.mean(x * x, axis=-1, keepdims=True) + EPS)) * g


def _lane_fold(op, x):
    out = x[:, :LANES]
    for c in range(1, x.shape[1] // LANES):
        out = op(out, x[:, c * LANES:(c + 1) * LANES])
    return out


def _t5_bucket(dist):
    n = jnp.maximum(dist, 0)
    max_exact = NUM_BUCKETS // 2
    nf = jnp.maximum(n, 1).astype(F32)
    large = max_exact + (jnp.log(nf / max_exact) / math.log(MAX_DISTANCE / max_exact)
                         * (NUM_BUCKETS - max_exact)).astype(jnp.int32)
    large = jnp.minimum(large, NUM_BUCKETS - 1)
    return jnp.where(n < max_exact, n, large)


def _bias_of(bias_t, dist):
    onehot = jax.nn.one_hot(_t5_bucket(dist), NUM_BUCKETS, dtype=F32)
    return jnp.einsum('hb,...b->h...', bias_t.astype(F32), onehot, precision=lax.Precision.HIGHEST)


def _lambda(lam_ref, lam_init):
    lv = lam_ref[...]
    a = jnp.sum(lv[0:1] * lv[1:2], axis=-1, keepdims=True)
    b = jnp.sum(lv[2:3] * lv[3:4], axis=-1, keepdims=True)
    return jnp.exp(a) - jnp.exp(b) + lam_init


def _proj_kernel(x_ref, g_ref, w_ref,
                 qa_ref, ka_ref, va_ref, qd_ref, kd_ref, vd_ref,
                 kaf_ref, vaf_ref, kdf_ref, vdf_ref, *km_ref, prompt):
    tm = x_ref.shape[0]
    h = _rmsnorm(x_ref[...], g_ref[...]).astype(BF16)
    outs = ((qa_ref, None, SCALE), (ka_ref, kaf_ref, None), (va_ref, vaf_ref, None),
            (qd_ref, None, SCALE), (kd_ref, kdf_ref, None), (vd_ref, vdf_ref, None))
    for c, (b_ref, f_ref, scale) in enumerate(outs):
        u = _dot(h, w_ref[:, c * WIDTH_A:(c + 1) * WIDTH_A])
        if scale is not None:
            u = u * (scale * LOG2E if prompt else scale)
        if not prompt:
            if f_ref is not None:
                f_ref[...] = u
            b_ref[...] = u.astype(BF16)
            continue
        is_key = c in (1, 4)
        ut = u.T if c != 4 else None
        if c in (1, 2):
            f_ref[0] = ut
        elif f_ref is not None:
            for hd in range(N_HEADS_B):
                f_ref[pl.ds(hd, tm, stride=N_HEADS_B), :] = u[:, hd * LANES:(hd + 1) * LANES]
        if is_key:
            b_ref[...] = u.astype(BF16)
        else:
            b_ref[0] = ut.astype(BF16)
        if c == 1:
            for j in range(tm // MOBA_BLOCK):
                km_ref[0][j] = jnp.sum(u[j * MOBA_BLOCK:(j + 1) * MOBA_BLOCK], axis=0,
                                       keepdims=True) * (1.0 / MOBA_BLOCK)


def _proj(x, g, w_bf16, tm, seq=None):
    n = x.shape[0]
    row = lambda i: (i, 0)
    const = lambda i: (0, 0)
    blk = pl.BlockSpec((tm, WIDTH_A), row)
    b_specs = [blk] * 6
    b_shapes = [jax.ShapeDtypeStruct((n, WIDTH_A), BF16)] * 6
    f_specs = [blk] * 4
    f_shapes = [jax.ShapeDtypeStruct((n, WIDTH_A), F32)] * 4
    if seq is not None:
        assert seq % tm == 0 and tm % MOBA_BLOCK == 0 and 2 * HEAD_DIM == LANES
        tiles = seq // tm
        t_spec = pl.BlockSpec((1, WIDTH_A, tm), lambda i: (i // tiles, 0, i % tiles))
        r_spec = pl.BlockSpec((tm * N_HEADS_B, LANES), row)
        r_shape = jax.ShapeDtypeStruct((n * N_HEADS_B, LANES), F32)
        t_shape = lambda dt: jax.ShapeDtypeStruct((n // seq, WIDTH_A, seq), dt)
        b_specs = [t_spec, blk, t_spec, t_spec, blk, t_spec]
        b_shapes = [t_shape(BF16), b_shapes[0], t_shape(BF16), t_shape(BF16), b_shapes[0], t_shape(BF16)]
        f_specs = [t_spec, t_spec, r_spec, r_spec,
                   pl.BlockSpec((tm // MOBA_BLOCK, 1, WIDTH_A), lambda i: (i, 0, 0))]
        f_shapes = [t_shape(F32), t_shape(F32), r_shape, r_shape,
                    jax.ShapeDtypeStruct((n // MOBA_BLOCK, 1, WIDTH_A), F32)]
    return pl.pallas_call(
        functools.partial(_proj_kernel, prompt=seq is not None),
        grid=(n // tm,),
        in_specs=[pl.BlockSpec((tm, D_MODEL), row),
                  pl.BlockSpec((1, D_MODEL), const),
                  pl.BlockSpec((D_MODEL, N_IN), const)],
        out_specs=b_specs + f_specs,
        out_shape=b_shapes + f_shapes,
        compiler_params=pltpu.CompilerParams(
            dimension_semantics=("arbitrary",), vmem_limit_bytes=VMEM_LIMIT_BYTES),
        name="proj",
    )(x, g, w_bf16)


def _row_fold(op, x):
    y = x.reshape(x.shape[0] // SUBLANES, SUBLANES, x.shape[1])
    out = y[0]
    for g in range(1, y.shape[0]):
        out = op(out, y[g])
    return out


def _attend_static(qt_ops, k_tile, vt_ref, vt_rows, bias_ref, bidx, scratch, qi):
    t = ATT_TILE
    tq = Q_TILE
    halves = tq // t
    trips = qi + 1
    outs = []
    for i, qt_op in enumerate(qt_ops):
        s_ref, p_ref, mb_ref = scratch[i % len(scratch)]
        mx = [None] * halves
        for jj in range(trips):
            s = _dot(k_tile(i, jj), qt_op)
            for hk in range(halves):
                for hq in range(halves):
                    dist = (halves * qi + hq) - (halves * jj + hk)
                    if dist < 0:
                        continue
                    blk = s[hk * t:(hk + 1) * t, hq * t:(hq + 1) * t]
                    if dist < 2:
                        blk = blk + bias_ref[bidx[i], dist]
                    s_ref[jj, hk * t:(hk + 1) * t, hq * t:(hq + 1) * t] = blk
                    f = _row_fold(jnp.maximum, blk)
                    mx[hq] = f if mx[hq] is None else jnp.maximum(mx[hq], f)
        m = jnp.max(jnp.concatenate(mx, axis=1), axis=0, keepdims=True)
        mb_ref[...] = jnp.broadcast_to(m, (SUBLANES, tq))
        l = [jnp.zeros((SUBLANES, t), F32) for _ in range(halves)]
        for jj in range(trips):
            for hq in range(halves):
                cols = slice(hq * t, (hq + 1) * t)
                mb = mb_ref[:, cols]
                for hk in range(halves):
                    dist = (halves * qi + hq) - (halves * jj + hk)
                    rows = slice(jj * tq + hk * t, jj * tq + (hk + 1) * t)
                    if dist < 0:
                        p_ref[rows, cols] = jnp.zeros((t, t), BF16)
                        continue
                    sv = s_ref[jj, hk * t:(hk + 1) * t, cols].reshape(t // SUBLANES, SUBLANES, t)
                    p = jnp.exp2(sv - mb[None]).reshape(t, t)
                    l[hq] = l[hq] + _row_fold(jnp.add, p)
                    p_ref[rows, cols] = p.astype(BF16)
        acc = _dot(vt_ref[0, vt_rows[i], :trips * tq], p_ref[:trips * tq, :])
        outs.append((acc, jnp.sum(jnp.concatenate(l, axis=1), axis=0, keepdims=True)))
    return outs


def _per_query_tile(nq, body):
    qi = pl.program_id(2)
    for n in range(nq):
        pl.when(qi == n)(functools.partial(body, n))


def _half_rows(qt, half):
    zero = jnp.zeros((HEAD_DIM, qt.shape[1]), qt.dtype)
    if half == 0:
        return jnp.concatenate([qt[:HEAD_DIM], zero], axis=0)
    return jnp.concatenate([zero, qt[HEAD_DIM:]], axis=0)


def _scratch_ring(refs):
    return [tuple(refs[3 * r:3 * r + 3]) for r in range(ATT_RING)]


def _moba_prompt_kernel(qt_ref, k_ref, vt_ref, km_ref, bias_ref, o_ref, *scratch, nq):
    t = ATT_TILE
    tq = Q_TILE
    tile_shift = t.bit_length() - 1
    nb = km_ref.shape[0]
    qi = pl.program_id(2)
    blk = lax.broadcasted_iota(jnp.int32, (nb, tq), 0)
    own = (tq // t) * qi + (lax.broadcasted_iota(jnp.int32, (nb, tq), 1) >> tile_shift)

    qt_ops = []
    for pair in range(ATT_PAIRS):
        rows = slice(pair * LANES, (pair + 1) * LANES)
        qt = qt_ref[0, rows, :]
        km = km_ref[:, 0, rows]
        km_hi = km.astype(BF16)
        km_lo = (km - km_hi.astype(F32)).astype(BF16)
        for e in range(2):
            qet = _half_rows(qt, e)
            gt = _dot(km_hi, qet) + _dot(km_lo, qet)
            rank = jnp.zeros((nb, tq), jnp.int32)
            for m in range(nb):
                gm = gt[m:m + 1, :]
                ahead = (gm > gt) | ((gm == gt) & (m < blk))
                rank = rank + jnp.where(ahead & (m < own), 1, 0)
            keep = (blk < own) & (rank < MOBA_TOPK) & (jnp.abs(gt) < jnp.inf)
            pen = jnp.where(keep | (blk == own), 0.0, NEG_BIG)
            pen = jnp.concatenate([pen, jnp.zeros((LANES - nb, tq), F32)], axis=0)
            qt_ops.append(jnp.concatenate([qet, pen.astype(BF16)], axis=0))

    k_lane = lax.broadcasted_iota(jnp.int32, (tq, LANES), 1)
    k_blk = lax.broadcasted_iota(jnp.int32, (tq, LANES), 0) >> tile_shift

    def k_tile(i, jj):
        pair = i // 2
        onehot = jnp.where(k_lane == (tq // t) * jj + k_blk, 1.0, 0.0).astype(BF16)
        return jnp.concatenate(
            [k_ref[jj * tq:(jj + 1) * tq, pair * LANES:(pair + 1) * LANES], onehot], axis=1)

    n_ops = 2 * ATT_PAIRS
    vt_rows = [slice((i // 2) * LANES, (i // 2 + 1) * LANES) for i in range(n_ops)]

    def body(n):
        res = _attend_static(qt_ops, k_tile, vt_ref, vt_rows, bias_ref, tuple(range(n_ops)),
                             _scratch_ring(scratch), n)
        for pair in range(ATT_PAIRS):
            (a0, l0), (a1, l1) = res[2 * pair], res[2 * pair + 1]
            ot = jnp.concatenate([a0[:HEAD_DIM] / l0, a1[HEAD_DIM:] / l1], axis=0)
            o_ref[:, pair * LANES:(pair + 1) * LANES] = ot.T.astype(o_ref.dtype)

    _per_query_tile(nq, body)


def _attn_scratch(nq, seq):
    one = [pltpu.VMEM((nq, Q_TILE, Q_TILE), F32),
           pltpu.VMEM((seq, Q_TILE), BF16),
           pltpu.VMEM((SUBLANES, Q_TILE), F32)]
    return one * ATT_RING


def _attn_specs(w, seq, nq):
    return ([pl.BlockSpec((1, w, Q_TILE), lambda b, g, qi: (b, g, qi)),
             pl.BlockSpec((seq, w), lambda b, g, qi: (b, g)),
             pl.BlockSpec((1, w, seq), lambda b, g, qi: (b, g, 0))],
            pl.BlockSpec((Q_TILE, w), lambda b, g, qi: (b * nq + qi, g)))


def _moba_prompt(qa_t, ka, va_t, km, bias_tiles, batch, seq):
    nq = seq // Q_TILE
    nb = seq // MOBA_BLOCK
    w = ATT_PAIRS * LANES
    assert WIDTH_A % w == 0
    in_specs, out_spec = _attn_specs(w, seq, nq)
    return pl.pallas_call(
        functools.partial(_moba_prompt_kernel, nq=nq),
        grid=(batch, WIDTH_A // w, nq),
        in_specs=in_specs + [pl.BlockSpec((nb, 1, w), lambda b, g, qi: (b, 0, g)),
                             pl.BlockSpec((2 * ATT_PAIRS, 2, ATT_TILE, ATT_TILE),
                                          lambda b, g, qi: (g, 0, 0, 0))],
        out_specs=out_spec,
        out_shape=jax.ShapeDtypeStruct((batch * seq, WIDTH_A), BF16),
        scratch_shapes=_attn_scratch(nq, seq),
        compiler_params=pltpu.CompilerParams(
            dimension_semantics=("arbitrary", "arbitrary", "arbitrary"),
            vmem_limit_bytes=VMEM_LIMIT_BYTES),
        name="moba_prompt",
    )(qa_t, ka, va_t, km, bias_tiles)


def _diff_prompt_kernel(qt_ref, k_ref, vt_ref, bias_ref, lam_ref, gain_ref, o_ref, *scratch,
                        lam_init, nq):
    tq = Q_TILE
    qt_ops = []
    for head in range(ATT_PAIRS):
        qt = qt_ref[0, head * LANES:(head + 1) * LANES, :]
        qt_ops += [_half_rows(qt, c) for c in range(2)]

    def k_tile(i, jj):
        head = i // 2
        return k_ref[jj * tq:(jj + 1) * tq, head * LANES:(head + 1) * LANES]

    n_ops = 2 * ATT_PAIRS
    vt_rows = [slice((i // 2) * LANES, (i // 2 + 1) * LANES) for i in range(n_ops)]

    def body(n):
        res = _attend_static(qt_ops, k_tile, vt_ref, vt_rows, bias_ref,
                             tuple(i // 2 for i in range(n_ops)), _scratch_ring(scratch), n)
        lam = _lambda(lam_ref, lam_init)
        for head in range(ATT_PAIRS):
            (a0, l0), (a1, l1) = res[2 * head], res[2 * head + 1]
            o = (a0 / l0 - lam * (a1 / l1)).T
            o_ref[:, head * LANES:(head + 1) * LANES] = (
                _rmsnorm(o, gain_ref[...]) * (1.0 - lam_init)).astype(o_ref.dtype)

    _per_query_tile(nq, body)


def _diff_prompt(qd_t, kd, vd_t, bias_tiles, lam_vecs, gain, lam_init, batch, seq):
    nq = seq // Q_TILE
    w = ATT_PAIRS * LANES
    assert WIDTH_B % w == 0
    in_specs, out_spec = _attn_specs(w, seq, nq)
    return pl.pallas_call(
        functools.partial(_diff_prompt_kernel, lam_init=lam_init, nq=nq),
        grid=(batch, WIDTH_B // w, nq),
        in_specs=in_specs + [
                  pl.BlockSpec((ATT_PAIRS, 2, ATT_TILE, ATT_TILE), lambda b, g, qi: (g, 0, 0, 0)),
                  pl.BlockSpec((4, HEAD_DIM), lambda b, g, qi: (0, 0)),
                  pl.BlockSpec((1, 2 * HEAD_DIM), lambda b, g, qi: (0, 0))],
        out_specs=out_spec,
        out_shape=jax.ShapeDtypeStruct((batch * seq, WIDTH_B), BF16),
        scratch_shapes=_attn_scratch(nq, seq),
        compiler_params=pltpu.CompilerParams(
            dimension_semantics=("arbitrary", "arbitrary", "arbitrary"),
            vmem_limit_bytes=VMEM_LIMIT_BYTES),
        name="diff_prompt",
    )(qd_t, kd, vd_t, bias_tiles, lam_vecs, gain)


def _toeplitz(w, t):
    h = w.shape[0]
    m = jnp.broadcast_to(w[:, None, :], (h, t, 2 * t)).reshape(h, 2 * t * t)
    return m[:, :t * (2 * t - 1)].reshape(h, t, 2 * t - 1)[:, :, :t]


def _bias_tiles(bias_t):
    t = ATT_TILE
    assert t + 1 >= MAX_DISTANCE
    k = jnp.arange(2 * t)
    d = jnp.where(k < t, k, k - 2 * t)
    far = _bias_of(bias_t, jnp.full((1,), 2 * t))
    diag = _toeplitz(jnp.where(d >= 0, (_bias_of(bias_t, d) - far) * LOG2E, -jnp.inf), t)
    sub = _toeplitz((_bias_of(bias_t, d + t) - far) * LOG2E, t)
    return jnp.stack([diag, sub], axis=1)


def _page_stream(pt_ref, pk_ref, pv_ref, buf_ref, sem_ref, n_seq, n_chunks):
    pps = PAGES_PER_STEP
    ahead = RING_SLOTS - 1
    b, ph, j = pl.program_id(0), pl.program_id(1), pl.program_id(2)
    step = (b * 2 + ph) * n_chunks + j
    total = n_seq * 2 * n_chunks

    def copies(pool_ref, bb, jj, slot):
        return [pltpu.make_async_copy(pool_ref.at[pt_ref[bb, jj * pps + i]],
                                      buf_ref.at[slot, i], sem_ref.at[slot])
                for i in range(pps)]

    def start(chunk):
        jj, seq_phase = chunk % n_chunks, chunk // n_chunks
        for phase, pool_ref in ((0, pk_ref), (1, pv_ref)):
            @pl.when(seq_phase % 2 == phase)
            def _(pool_ref=pool_ref):
                for i, cp in enumerate(copies(pool_ref, seq_phase // 2, jj, chunk % RING_SLOTS)):
                    cp.start(priority=i % 2)

    @pl.when(step == 0)
    def _():
        for chunk in range(min(ahead, total)):
            start(jnp.int32(chunk))

    @pl.when(step + ahead < total)
    def _():
        start(step + ahead)

    slot = step % RING_SLOTS

    def wait(pool_ref):
        for cp in copies(pool_ref, b, j, slot):
            cp.wait()

    return slot, wait


def _moba_sample_kernel(pt_ref, pk_ref, pv_ref, q_ref, kn_ref, vn_ref, bias_ref, o_ref,
                        s_ref, snew_ref, acc_ref, l_ref, m_ref, bmax_ref, gate_ref, sel_ref,
                        buf_ref, sem_ref, *, n_seq, n_chunks, tokens):
    pps = PAGES_PER_STEP
    slot, wait = _page_stream(pt_ref, pk_ref, pv_ref, buf_ref, sem_ref, n_seq, n_chunks)
    rows = q_ref.shape[1]
    groups = rows // tokens
    ppb = MOBA_BLOCK // PAGE_SIZE
    bps = pps // ppb
    n_blocks = n_chunks * bps
    ph = pl.program_id(1)
    j = pl.program_id(2)
    lane = lax.broadcasted_iota(jnp.int32, (rows, LANES), 1)
    lanef = lane.astype(F32)
    q = q_ref[0]

    @pl.when((ph == 0) & (j == 0))
    def _():
        snew_ref[...] = _dot_nt(q, kn_ref[0]) + bias_ref[2]
        bmax_ref[...] = jnp.full((rows, LANES), -jnp.inf, F32)
        gate_ref[...] = jnp.zeros((rows, LANES), F32)

    @pl.when(ph == 0)
    def _():
        wait(pk_ref)
        is_last = j == n_chunks - 1
        bmax = bmax_ref[...]
        gate = gate_ref[...]
        for b in range(bps):
            smax = ssum = None
            for pg in range(ppb):
                i = b * ppb + pg
                raw = _dot(q, buf_ref[slot, i].astype(BF16))
                bias = bias_ref[0]
                if i == pps - 1:
                    bias = jnp.where(is_last, bias_ref[1], bias)
                s = raw + bias
                s_ref[j, :, i * PAGE_SIZE:(i + 1) * PAGE_SIZE] = s
                smax = s if smax is None else jnp.maximum(smax, s)
                ssum = raw if ssum is None else ssum + raw
            blk = j * bps + b
            bmax = jnp.where(lane == blk, jnp.max(smax, axis=1, keepdims=True), bmax)
            gate = jnp.where(lane == blk, jnp.sum(ssum, axis=1, keepdims=True), gate)
        bmax_ref[...] = bmax
        gate_ref[...] = gate

    @pl.when((ph == 1) & (j == 0))
    def _():
        g = jnp.where(lane < n_blocks, gate_ref[...], -jnp.inf)
        sel = jnp.zeros((rows, LANES), F32)
        for _ in range(MOBA_TOPK):
            top = jnp.max(g, axis=1, keepdims=True)
            first = jnp.min(jnp.where(g == top, lanef, float(LANES)), axis=1, keepdims=True)
            pick = lanef == first
            finite = jnp.where(jnp.abs(top) < jnp.inf, 1.0, 0.0)
            sel = jnp.maximum(sel, jnp.where(pick, finite, 0.0))
            g = jnp.where(pick, -jnp.inf, g)
        sel_ref[...] = sel
        m_sel = jnp.max(jnp.where(sel > 0.5, bmax_ref[...], -jnp.inf), axis=1, keepdims=True)
        m_new = jnp.max(snew_ref[...], axis=1, keepdims=True)
        m_ref[...] = jnp.broadcast_to(jnp.maximum(m_sel, m_new), (rows, LANES))
        acc_ref[...] = jnp.zeros(acc_ref.shape, F32)
        l_ref[...] = jnp.zeros(l_ref.shape, F32)

    @pl.when(ph == 1)
    def _():
        wait(pv_ref)
        m = m_ref[...]
        selv = sel_ref[...]
        acc = acc_ref[...]
        l = l_ref[...]
        for i in range(pps):
            blk = j * bps + i // ppb
            on = jnp.max(jnp.where(lane == blk, selv, 0.0), axis=1, keepdims=True)
            on = jnp.broadcast_to(on, (rows, LANES)) > 0.5
            s = s_ref[j, :, i * PAGE_SIZE:(i + 1) * PAGE_SIZE]
            p = jnp.exp(jnp.where(on, s - m, -jnp.inf))
            l = l + p
            acc = acc + _dot_nt(p.astype(BF16), buf_ref[slot, i].astype(BF16))
        acc_ref[...] = acc
        l_ref[...] = l

    @pl.when((ph == 1) & (j == n_chunks - 1))
    def _():
        p = jnp.exp(snew_ref[...] - m_ref[...])
        acc = acc_ref[...] + _dot(p.astype(BF16), vn_ref[0])
        l = jnp.sum(l_ref[...] + p, axis=1, keepdims=True)
        o = acc / l
        grp = lax.broadcasted_iota(jnp.int32, (groups, WIDTH_A), 0)
        col = lax.broadcasted_iota(jnp.int32, (groups, WIDTH_A), 1)
        head_shift = HEAD_DIM.bit_length() - 1
        w = jnp.where((col >> head_shift) == grp, 1.0, 0.0)
        tok = jnp.sum(o.reshape(tokens, groups, WIDTH_A) * w[None], axis=1)
        o_ref[0] = tok.astype(o_ref.dtype)


def _moba_sample(page_table, pool_kt, pool_vt, q_bd, k_new, v_new, bias_rows, *, tokens):
    pps = PAGES_PER_STEP
    n_seq, n_pages = page_table.shape
    n_chunks = n_pages // pps
    rows = q_bd.shape[1]
    seq_map = lambda b, ph, j, pt: (b, 0, 0)
    hbm = pl.BlockSpec(memory_space=pl.ANY)
    in_specs = [hbm, hbm,
                pl.BlockSpec((1, rows, WIDTH_A), seq_map),
                pl.BlockSpec((1, LANES, WIDTH_A), seq_map),
                pl.BlockSpec((1, LANES, WIDTH_A), seq_map),
                pl.BlockSpec((3, rows, LANES), lambda b, ph, j, pt: (0, 0, 0))]
    vec = pltpu.VMEM((rows, LANES), F32)
    grid_spec = pltpu.PrefetchScalarGridSpec(
        num_scalar_prefetch=1,
        grid=(n_seq, 2, n_chunks),
        in_specs=in_specs,
        out_specs=pl.BlockSpec((1, tokens, WIDTH_A), seq_map),
        scratch_shapes=[pltpu.VMEM((n_chunks, rows, pps * PAGE_SIZE), F32),
                        vec,
                        pltpu.VMEM((rows, WIDTH_A), F32),
                        vec, vec, vec, vec, vec,
                        pltpu.VMEM((RING_SLOTS, pps) + pool_kt.shape[1:], F32),
                        pltpu.SemaphoreType.DMA((RING_SLOTS,))])
    return pl.pallas_call(
        functools.partial(_moba_sample_kernel, n_seq=n_seq, n_chunks=n_chunks, tokens=tokens),
        grid_spec=grid_spec,
        out_shape=jax.ShapeDtypeStruct((n_seq, tokens, WIDTH_A), BF16),
        compiler_params=pltpu.CompilerParams(
            dimension_semantics=("arbitrary", "arbitrary", "arbitrary"),
            vmem_limit_bytes=VMEM_LIMIT_BYTES),
        name="moba_sample",
    )(page_table, pool_kt, pool_vt, q_bd, k_new, v_new, bias_rows)


def _diff_sample_kernel(pt_ref, pk_ref, pv_ref, q_ref, kn_ref, vn_ref, bias_ref, bnew_ref,
                        lam_ref, gain_ref, o_ref, s_ref, snew_ref, acc_ref, l_ref, m_ref,
                        buf_ref, sem_ref, *, n_seq, n_chunks, tokens, lam_init):
    pps = PAGES_PER_STEP
    slot, wait = _page_stream(pt_ref, pk_ref, pv_ref, buf_ref, sem_ref, n_seq, n_chunks)
    rows = q_ref.shape[1]
    pw = buf_ref.shape[2]
    ph = pl.program_id(1)
    j = pl.program_id(2)
    q = q_ref[0]

    @pl.when((ph == 0) & (j == 0))
    def _():
        snew_ref[...] = _dot_nt(q, kn_ref[0]) + bnew_ref[...]
        m_ref[...] = jnp.full((rows, LANES), -jnp.inf, F32)

    @pl.when(ph == 0)
    def _():
        wait(pk_ref)
        is_last = j == n_chunks - 1
        mx = m_ref[...]
        for i in range(pps):
            bias = bias_ref[0]
            if i == pps - 1:
                bias = jnp.where(is_last, bias_ref[1], bias)
            s = _dot_nt(q, buf_ref[slot, i].astype(BF16)) + bias
            s_ref[j, :, i * pw:(i + 1) * pw] = s
            mx = jnp.maximum(mx, _lane_fold(jnp.maximum, s))
        m_ref[...] = mx

    @pl.when((ph == 1) & (j == 0))
    def _():
        m = jnp.maximum(jnp.max(m_ref[...], axis=1, keepdims=True),
                        jnp.max(snew_ref[...], axis=1, keepdims=True))
        m_ref[...] = jnp.broadcast_to(m, (rows, LANES))
        acc_ref[...] = jnp.zeros(acc_ref.shape, F32)
        l_ref[...] = jnp.zeros(l_ref.shape, F32)

    @pl.when(ph == 1)
    def _():
        wait(pv_ref)
        m = m_ref[...]
        mw = jnp.concatenate([m] * (pw // LANES), axis=1)
        acc = acc_ref[...]
        l = l_ref[...]
        for i in range(pps):
            p = jnp.exp(s_ref[j, :, i * pw:(i + 1) * pw] - mw)
            l = l + _lane_fold(jnp.add, p)
            acc = acc + _dot(p.astype(BF16), buf_ref[slot, i].astype(BF16))
        acc_ref[...] = acc
        l_ref[...] = l

    @pl.when((ph == 1) & (j == n_chunks - 1))
    def _():
        p = jnp.exp(snew_ref[...] - m_ref[...])
        acc = acc_ref[...] + _dot(p.astype(BF16), vn_ref[0])
        l = jnp.sum(l_ref[...] + p, axis=1, keepdims=True)
        o = (acc / l).reshape(tokens, rows // tokens, LANES)
        lam = _lambda(lam_ref, lam_init)
        sub = lax.broadcasted_iota(jnp.int32, (rows // tokens, LANES), 0)
        gain = gain_ref[...]
        parts = []
        for h in range(N_HEADS_B):
            w = jnp.where(sub == 2 * h, 1.0, jnp.where(sub == 2 * h + 1, -lam, 0.0))
            parts.append(_rmsnorm(jnp.sum(o * w[None], axis=1), gain) * (1.0 - lam_init))
        o_ref[0] = jnp.concatenate(parts, axis=1).astype(o_ref.dtype)


def _diff_sample(page_table, pool_k, pool_v, q2, k_new, v_new, bias_rows, bias_new, lam_vecs, gain,
                 *, tokens, lam_init):
    pps = PAGES_PER_STEP
    n_seq, n_pages = page_table.shape
    n_chunks = n_pages // pps
    rows = q2.shape[1]
    pw = pool_k.shape[1]
    seq_map = lambda b, ph, j, pt: (b, 0, 0)
    const2 = lambda b, ph, j, pt: (0, 0)
    hbm = pl.BlockSpec(memory_space=pl.ANY)
    in_specs = [hbm, hbm,
                pl.BlockSpec((1, rows, LANES), seq_map),
                pl.BlockSpec((1, LANES, LANES), seq_map),
                pl.BlockSpec((1, LANES, LANES), seq_map),
                pl.BlockSpec((2, rows, pw), lambda b, ph, j, pt: (0, 0, 0)),
                pl.BlockSpec((rows, LANES), const2),
                pl.BlockSpec((4, HEAD_DIM), const2),
                pl.BlockSpec((1, 2 * HEAD_DIM), const2)]
    vec = pltpu.VMEM((rows, LANES), F32)
    grid_spec = pltpu.PrefetchScalarGridSpec(
        num_scalar_prefetch=1,
        grid=(n_seq, 2, n_chunks),
        in_specs=in_specs,
        out_specs=pl.BlockSpec((1, tokens, WIDTH_B), seq_map),
        scratch_shapes=[pltpu.VMEM((n_chunks, rows, pps * pw), F32),
                        vec, vec, vec, vec,
                        pltpu.VMEM((RING_SLOTS, pps) + pool_k.shape[1:], F32),
                        pltpu.SemaphoreType.DMA((RING_SLOTS,))])
    return pl.pallas_call(
        functools.partial(_diff_sample_kernel, n_seq=n_seq, n_chunks=n_chunks, tokens=tokens,
                          lam_init=lam_init),
        grid_spec=grid_spec,
        out_shape=jax.ShapeDtypeStruct((n_seq, tokens, WIDTH_B), BF16),
        compiler_params=pltpu.CompilerParams(
            dimension_semantics=("arbitrary", "arbitrary", "arbitrary"),
            vmem_limit_bytes=VMEM_LIMIT_BYTES),
        name="diff_sample",
    )(page_table, pool_k, pool_v, q2, k_new, v_new, bias_rows, bias_new, lam_vecs, gain)


def _block_diag(q, n_seq, tokens, groups, keep):
    w = q.shape[1] // groups
    q4 = q.reshape(n_seq, tokens, groups, 1, w)
    eye = (jnp.arange(groups)[:, None] % keep == jnp.arange(keep)[None, :]).astype(q.dtype)
    return (q4 * eye.reshape(1, 1, groups, keep, 1)).reshape(n_seq, tokens * groups, keep * w)


def _pad_rows(x, n_seq):
    x = x.reshape(n_seq, -1, x.shape[-1])
    return jnp.pad(x, ((0, 0), (0, LANES - x.shape[1]), (0, 0)))


def _sample_bias(bias_t, tokens, heads_per_group):
    assert PAGE_SIZE + 1 >= MAX_DISTANCE
    tok = jnp.arange(tokens)[:, None]
    col = jnp.arange(LANES)[None, :]
    far = _bias_of(bias_t, jnp.broadcast_to(2 * PAGE_SIZE, (tokens, LANES)))
    last = _bias_of(bias_t, PAGE_SIZE + tok - col)
    new = jnp.where((col <= tok) & (col < tokens), _bias_of(bias_t, tok - col), -jnp.inf)

    def rows(x):
        x = jnp.repeat(x.transpose(1, 0, 2), heads_per_group, axis=1)
        return x.reshape(-1, LANES)

    return rows(far), rows(last), rows(new)


def _spread_heads(x, n_heads, groups):
    r, k = x.shape
    row_head = (jnp.arange(r) % groups) // (groups // n_heads)
    own = row_head[:, None, None] == jnp.arange(n_heads)[None, None, :]
    return jnp.where(own, x[:, :, None], -jnp.inf).reshape(r, k * n_heads)


def _merge_kernel(x_ref, oa_ref, ob_ref, g_ref, wg_ref, woa_ref, wob_ref, wo_ref, o_ref):
    x = x_ref[...]
    h = _rmsnorm(x, g_ref[...]).astype(BF16)
    ga = 1.0 / (1.0 + jnp.exp(-_dot(h, wg_ref[:, :D_MODEL])))
    m = ga * _dot(oa_ref[...], woa_ref[...])
    gb = 1.0 / (1.0 + jnp.exp(-_dot(h, wg_ref[:, D_MODEL:])))
    m = m + gb * _dot(ob_ref[...], wob_ref[...])
    o_ref[...] = x + _dot(m.astype(BF16), wo_ref[...])


def _merge(x, oa, ob, g, wg, woa, wob, wo, tm):
    n = x.shape[0]
    row = lambda i: (i, 0)
    const = lambda i: (0, 0)
    return pl.pallas_call(
        _merge_kernel,
        grid=(n // tm,),
        in_specs=[pl.BlockSpec((tm, D_MODEL), row),
                  pl.BlockSpec((tm, WIDTH_A), row),
                  pl.BlockSpec((tm, WIDTH_B), row),
                  pl.BlockSpec((1, D_MODEL), const),
                  pl.BlockSpec((D_MODEL, 2 * D_MODEL), const),
                  pl.BlockSpec((WIDTH_A, D_MODEL), const),
                  pl.BlockSpec((WIDTH_B, D_MODEL), const),
                  pl.BlockSpec((D_MODEL, D_MODEL), const)],
        out_specs=pl.BlockSpec((tm, D_MODEL), row),
        out_shape=jax.ShapeDtypeStruct((n, D_MODEL), F32),
        compiler_params=pltpu.CompilerParams(
            dimension_semantics=("arbitrary",), vmem_limit_bytes=VMEM_LIMIT_BYTES),
        name="merge",
    )(x, oa, ob, g, wg, woa, wob, wo)


def _ffn_kernel(*refs, tm, seq_len, has_prev):
    if has_prev:
        (x_ref, g_ref, wup_ref, cw_ref, cb_ref, wdn_ref, gfin_ref, e1_ref, e2_ref,
         y_ref, u_ref, act_ref) = refs
    else:
        (x_ref, g_ref, wup_ref, cw_ref, cb_ref, wdn_ref, gfin_ref,
         y_ref, u_ref, act_ref, carry_ref) = refs
    i = pl.program_id(0)
    x = x_ref[...]
    h = _rmsnorm(x, g_ref[...]).astype(BF16)
    row = lax.broadcasted_iota(jnp.int32, (tm, FF_CHUNK), 0)
    row8 = lax.broadcasted_iota(jnp.int32, (SUBLANES, FF_CHUNK), 0)
    if has_prev:
        pos = row & (seq_len - 1)
    else:
        @pl.when((i * tm) % seq_len == 0)
        def _():
            carry_ref[...] = jnp.zeros(carry_ref.shape, F32)

    for c in range(D_FF // FF_CHUNK):
        halves = []
        for part in range(2):
            cols = slice(part * D_FF + c * FF_CHUNK, part * D_FF + (c + 1) * FF_CHUNK)
            u = _dot(h, wup_ref[:, cols])
            um1 = pltpu.roll(u, 1, 0)
            um2 = pltpu.roll(u, 2, 0)
            if has_prev:
                um1 = jnp.where(pos == 0, e1_ref[:, cols], um1)
                um2 = jnp.where(pos < 2, e2_ref[:, cols], um2)
                u_ref[:, cols] = u
            else:
                prev = carry_ref[:, cols]
                top1 = jnp.where(row8 == 0, pltpu.roll(prev, 1, 0), um1[:SUBLANES])
                top2 = jnp.where(row8 < 2, pltpu.roll(prev, 2, 0), um2[:SUBLANES])
                um1 = jnp.concatenate([top1, um1[SUBLANES:]], axis=0)
                um2 = jnp.concatenate([top2, um2[SUBLANES:]], axis=0)
                carry_ref[:, cols] = u[tm - SUBLANES:]
                u_ref[:, cols] = u[tm - SUBLANES:]
            cw = cw_ref[:, cols]
            halves.append(((cb_ref[:, cols] + cw[0:1] * um2) + cw[1:2] * um1) + cw[2:3] * u)
        gate, val = halves
        act = (gate * (1.0 / (1.0 + jnp.exp(-gate)))) * val
        act_ref[:, c * FF_CHUNK:(c + 1) * FF_CHUNK] = act.astype(BF16)

    x3 = x + _dot(act_ref[...], wdn_ref[...])
    y_ref[...] = _rmsnorm(x3, gfin_ref[...])


def _ffn(x, g, wup, cw, cb, wdn, gfin, prev, tm, seq_len):
    n = x.shape[0]
    has_prev = prev is not None
    row = lambda i: (i, 0)
    const = lambda i: (0, 0)
    in_specs = [pl.BlockSpec((tm, D_MODEL), row),
                pl.BlockSpec((1, D_MODEL), const),
                pl.BlockSpec((D_MODEL, 2 * D_FF), const, pipeline_mode=pl.Buffered(1)),
                pl.BlockSpec((CONV_W, 2 * D_FF), const),
                pl.BlockSpec((1, 2 * D_FF), const),
                pl.BlockSpec((D_FF, D_MODEL), const, pipeline_mode=pl.Buffered(1)),
                pl.BlockSpec((1, D_MODEL), const)]
    scratch = [pltpu.VMEM((tm, D_FF), BF16)]
    if has_prev:
        assert n == tm and tm % seq_len == 0 and seq_len & (seq_len - 1) == 0
        in_specs += [pl.BlockSpec((tm, 2 * D_FF), row)] * 2
        u_shape, u_spec = (n, 2 * D_FF), pl.BlockSpec((tm, 2 * D_FF), row)
        args = (x, g, wup, cw, cb, wdn, gfin) + tuple(prev)
    else:
        assert seq_len % tm == 0
        tiles_per_seq = seq_len // tm
        u_shape = (n // seq_len * SUBLANES, 2 * D_FF)
        u_spec = pl.BlockSpec((SUBLANES, 2 * D_FF), lambda i: (i // tiles_per_seq, 0))
        scratch.append(pltpu.VMEM((SUBLANES, 2 * D_FF), F32))
        args = (x, g, wup, cw, cb, wdn, gfin)
    return pl.pallas_call(
        functools.partial(_ffn_kernel, tm=tm, seq_len=seq_len, has_prev=has_prev),
        grid=(n // tm,),
        in_specs=in_specs,
        out_specs=[pl.BlockSpec((tm, D_MODEL), row), u_spec],
        out_shape=[jax.ShapeDtypeStruct((n, D_MODEL), F32), jax.ShapeDtypeStruct(u_shape, F32)],
        scratch_shapes=scratch,
        compiler_params=pltpu.CompilerParams(
            dimension_semantics=("arbitrary",), vmem_limit_bytes=VMEM_LIMIT_BYTES),
        name="ffn_sample" if has_prev else "ffn_prompt",
    )(*args)


def kernel(x_prompt, x_sample, cache_moba_k, cache_moba_v, cache_diff_k, cache_diff_v, state_conv, page_table, rel_bias, norm_attn, w_in, w_gate, w_out_a, w_out_b, w_out, lambda_q1, lambda_k1, lambda_q2, lambda_k2, diff_norm, norm_ffn, w_up, conv_w, conv_b, w_down, norm_final):
    batch, seq, _ = x_prompt.shape
    n_seq, tokens, _ = x_sample.shape
    depth = w_in.shape[0]
    n_phys = cache_moba_k.shape[1]
    n_pages = page_table.shape[1]
    assert depth == 1 and seq % Q_TILE == 0 and ATT_TILE & (ATT_TILE - 1) == 0
    assert (n_pages * PAGE_SIZE) % MOBA_BLOCK == 0 and n_pages % PAGES_PER_STEP == 0
    assert n_pages * PAGE_SIZE // MOBA_BLOCK <= LANES and CONV_W - 1 <= tokens <= LANES // N_HEADS_B
    l = 0
    lam_init = 0.8 - 0.6 * math.exp(-0.3 * l)

    bias_a = rel_bias[:, :N_HEADS_A].T
    bias_d = rel_bias[:, N_HEADS_A:].T
    row = lambda v: v.reshape(1, -1)
    w_in_b = w_in[l].astype(BF16)
    w_gate_b = w_gate[l].astype(BF16)
    w_oa_b = w_out_a[l].astype(BF16)
    w_ob_b = w_out_b[l].astype(BF16)
    w_o_b = w_out[l].astype(BF16)
    w_up_b = w_up[l].astype(BF16)
    w_dn_b = w_down[l].astype(BF16)
    lam_vecs = jnp.stack([lambda_q1[l], lambda_k1[l], lambda_q2[l], lambda_k2[l]]).astype(F32)
    gain_d = row(diff_norm[l])
    g_attn, g_ffn, g_fin = row(norm_attn[l]), row(norm_ffn[l]), row(norm_final)
    cb = row(conv_b[l])

    xp = x_prompt.reshape(batch * seq, D_MODEL)
    qa_tb, ka, va_tb, qd_tb, kd, vd_tb, ka_t, va_t, kd_r, vd_r, km = _proj(xp, g_attn, w_in_b, 512, seq)
    oa = _moba_prompt(qa_tb, ka, va_tb, km, _bias_tiles(bias_a), batch, seq)
    ob = _diff_prompt(qd_tb, kd, vd_tb, _bias_tiles(bias_d), lam_vecs, gain_d, lam_init, batch, seq)
    x2 = _merge(xp, oa, ob, g_attn, w_gate_b, w_oa_b, w_ob_b, w_o_b, 512)
    yp, tail_p = _ffn(x2, g_ffn, w_up_b, conv_w[l], cb, w_dn_b, g_fin, None, 1024, seq)
    conv_p = tail_p.reshape(batch, SUBLANES, 2 * D_FF)[:, SUBLANES - (CONV_W - 1):]

    n_s = n_seq * tokens
    xs = x_sample.reshape(n_s, D_MODEL)
    qa_s, ka_s, va_s, qd_s, kd_s, vd_s, ka_sf, va_sf, kd_sf, vd_sf = _proj(xs, g_attn, w_in_b, n_s)

    pool_t = lambda c: jnp.transpose(c[l], (0, 2, 3, 1)).reshape(n_phys, WIDTH_A, PAGE_SIZE)
    pool_r = lambda c: c[l].reshape(n_phys, PAGE_SIZE * N_HEADS_B, 2 * HEAD_DIM)
    far_a, last_a, new_a = _sample_bias(bias_a, tokens, 1)
    oa_s = _moba_sample(page_table, pool_t(cache_moba_k), pool_t(cache_moba_v),
                        _block_diag(qa_s, n_seq, tokens, N_HEADS_A, N_HEADS_A),
                        _pad_rows(ka_s, n_seq), _pad_rows(va_s, n_seq),
                        jnp.stack([far_a, last_a, new_a]), tokens=tokens)
    far_d, last_d, new_d = _sample_bias(bias_d, tokens, 2)
    groups_d = 2 * N_HEADS_B
    spread = lambda x: _spread_heads(x, N_HEADS_B, groups_d)
    per_head = lambda x: x.reshape(n_s * N_HEADS_B, 2 * HEAD_DIM)
    oa_d = _diff_sample(page_table, pool_r(cache_diff_k), pool_r(cache_diff_v),
                        _block_diag(qd_s, n_seq, tokens, groups_d, 2),
                        _pad_rows(per_head(kd_s), n_seq), _pad_rows(per_head(vd_s), n_seq),
                        jnp.stack([spread(far_d), spread(last_d)]),
                        spread(new_d[:, :LANES // N_HEADS_B]),
                        lam_vecs, gain_d, tokens=tokens, lam_init=lam_init)
    x2s = _merge(xs, oa_s.reshape(n_s, WIDTH_A), oa_d.reshape(n_s, WIDTH_B), g_attn,
                 w_gate_b, w_oa_b, w_ob_b, w_o_b, n_s)
    st = state_conv[l]
    zero = jnp.zeros((n_seq, 1, 2 * D_FF), F32)
    e1 = jnp.concatenate([st[:, 1:2]] + [zero] * (tokens - 1), axis=1).reshape(n_s, 2 * D_FF)
    e2 = jnp.concatenate([st[:, 0:2]] + [zero] * (tokens - 2), axis=1).reshape(n_s, 2 * D_FF)
    ys, u_s = _ffn(x2s, g_ffn, w_up_b, conv_w[l], cb, w_dn_b, g_fin, (e1, e2), n_s, tokens)
    conv_s = u_s.reshape(n_seq, tokens, 2 * D_FF)[:, tokens - (CONV_W - 1):]

    shp_a = lambda a, b_, t_: a.reshape(1, b_, t_, N_HEADS_A, HEAD_DIM)
    shp_d = lambda a, b_, t_: a.reshape(1, b_, t_, N_HEADS_B, 2 * HEAD_DIM)
    untr = lambda a: a.reshape(batch, N_HEADS_A, HEAD_DIM, seq).transpose(0, 3, 1, 2)[None]
    return (yp.reshape(batch, seq, D_MODEL), ys.reshape(n_seq, tokens, D_MODEL),
            untr(ka_t), untr(va_t),
            shp_d(kd_r, batch, seq), shp_d(vd_r, batch, seq), conv_p[None],
            shp_a(ka_sf, n_seq, tokens), shp_a(va_sf, n_seq, tokens),
            shp_d(kd_sf, n_seq, tokens), shp_d(vd_sf, n_seq, tokens), conv_s[None])
```

```python
import functools
import math

import jax
import jax.numpy as jnp
from jax import lax
from jax.experimental import pallas as pl
from jax.experimental.pallas import tpu as pltpu

F32 = jnp.float32
BF16 = jnp.bfloat16

D_MODEL = 1024
HEAD_DIM = 64
N_HEADS_A = D_MODEL // 128
N_HEADS_B = D_MODEL // 256
WIDTH_A = N_HEADS_A * HEAD_DIM
WIDTH_B = N_HEADS_B * 2 * HEAD_DIM
N_IN = 3 * WIDTH_A + 3 * WIDTH_B
MOBA_BLOCK = 256
MOBA_TOPK = 3
NUM_BUCKETS = 32
MAX_DISTANCE = 128
D_FF = ((8 * D_MODEL // 3 + 127) // 128) * 128
CONV_W = 3
EPS = 1e-6
PAGE_SIZE = 128
SCALE = HEAD_DIM ** -0.5
LOG2E = math.log2(math.e)

LANES = 128
SUBLANES = 8
VMEM_LIMIT_BYTES = 56 * 1024 * 1024

ATT_TILE = MOBA_BLOCK
Q_TILE = 2 * ATT_TILE
ATT_PAIRS = 2
ATT_RING = 2
FF_CHUNK = 256
PAGES_PER_STEP = 16
RING_SLOTS = 3
NEG_BIG = -1e30


def _dot(a, b):
    return jnp.dot(a, b, preferred_element_type=F32)


def _dot_nt(a, b):
    return lax.dot_general(a, b, (((1,), (1,)), ((), ())), preferred_element_type=F32)


def _rmsnorm(x, g):
    return (x * lax.rsqrt(jnp.mean(x * x, axis=-1, keepdims=True) + EPS)) * g


def _lane_fold(op, x):
    out = x[:, :LANES]
    for c in range(1, x.shape[1] // LANES):
        out = op(out, x[:, c * LANES:(c + 1) * LANES])
    return out


def _t5_bucket(dist):
    n = jnp.maximum(dist, 0)
    max_exact = NUM_BUCKETS // 2
    nf = jnp.maximum(n, 1).astype(F32)
    large = max_exact + (jnp.log(nf / max_exact) / math.log(MAX_DISTANCE / max_exact)
                         * (NUM_BUCKETS - max_exact)).astype(jnp.int32)
    large = jnp.minimum(large, NUM_BUCKETS - 1)
    return jnp.where(n < max_exact, n, large)


def _bias_of(bias_t, dist):
    onehot = jax.nn.one_hot(_t5_bucket(dist), NUM_BUCKETS, dtype=F32)
    return jnp.einsum('hb,...b->h...', bias_t.astype(F32), onehot, precision=lax.Precision.HIGHEST)


def _lambda(lam_ref, lam_init):
    lv = lam_ref[...]
    a = jnp.sum(lv[0:1] * lv[1:2], axis=-1, keepdims=True)
    b = jnp.sum(lv[2:3] * lv[3:4], axis=-1, keepdims=True)
    return jnp.exp(a) - jnp.exp(b) + lam_init


def _proj_kernel(x_ref, g_ref, w_ref,
                 qa_ref, ka_ref, va_ref, qd_ref, kd_ref, vd_ref,
                 kaf_ref, vaf_ref, kdf_ref, vdf_ref, *km_ref, prompt):
    tm = x_ref.shape[0]
    h = _rmsnorm(x_ref[...], g_ref[...]).astype(BF16)
    outs = ((qa_ref, None, SCALE), (ka_ref, kaf_ref, None), (va_ref, vaf_ref, None),
            (qd_ref, None, SCALE), (kd_ref, kdf_ref, None), (vd_ref, vdf_ref, None))
    for c, (b_ref, f_ref, scale) in enumerate(outs):
        u = _dot(h, w_ref[:, c * WIDTH_A:(c + 1) * WIDTH_A].astype(BF16))
        if scale is not None:
            u = u * (scale * LOG2E if prompt else scale)
        if not prompt:
            if f_ref is not None:
                f_ref[...] = u
            b_ref[...] = u.astype(BF16)
            continue
        is_key = c in (1, 4)
        ut = u.T if c != 4 else None
        if c in (1, 2):
            f_ref[0] = ut
        elif f_ref is not None:
            for hd in range(N_HEADS_B):
                f_ref[pl.ds(hd, tm, stride=N_HEADS_B), :] = u[:, hd * LANES:(hd + 1) * LANES]
        if is_key:
            b_ref[...] = u.astype(BF16)
        else:
            b_ref[0] = ut.astype(BF16)
        if c == 1:
            for j in range(tm // MOBA_BLOCK):
                km_ref[0][j] = jnp.sum(u[j * MOBA_BLOCK:(j + 1) * MOBA_BLOCK], axis=0,
                                       keepdims=True) * (1.0 / MOBA_BLOCK)


def _proj(x, g, w_bf16, tm, seq=None):
    n = x.shape[0]
    row = lambda i: (i, 0)
    const = lambda i: (0, 0)
    blk = pl.BlockSpec((tm, WIDTH_A), row)
    b_specs = [blk] * 6
    b_shapes = [jax.ShapeDtypeStruct((n, WIDTH_A), BF16)] * 6
    f_specs = [blk] * 4
    f_shapes = [jax.ShapeDtypeStruct((n, WIDTH_A), F32)] * 4
    if seq is not None:
        assert seq % tm == 0 and tm % MOBA_BLOCK == 0 and 2 * HEAD_DIM == LANES
        tiles = seq // tm
        t_spec = pl.BlockSpec((1, WIDTH_A, tm), lambda i: (i // tiles, 0, i % tiles))
        r_spec = pl.BlockSpec((tm * N_HEADS_B, LANES), row)
        r_shape = jax.ShapeDtypeStruct((n * N_HEADS_B, LANES), F32)
        t_shape = lambda dt: jax.ShapeDtypeStruct((n // seq, WIDTH_A, seq), dt)
        b_specs = [t_spec, blk, t_spec, t_spec, blk, t_spec]
        b_shapes = [t_shape(BF16), b_shapes[0], t_shape(BF16), t_shape(BF16), b_shapes[0], t_shape(BF16)]
        f_specs = [t_spec, t_spec, r_spec, r_spec,
                   pl.BlockSpec((tm // MOBA_BLOCK, 1, WIDTH_A), lambda i: (i, 0, 0))]
        f_shapes = [t_shape(F32), t_shape(F32), r_shape, r_shape,
                    jax.ShapeDtypeStruct((n // MOBA_BLOCK, 1, WIDTH_A), F32)]
    return pl.pallas_call(
        functools.partial(_proj_kernel, prompt=seq is not None),
        grid=(n // tm,),
        in_specs=[pl.BlockSpec((tm, D_MODEL), row),
                  pl.BlockSpec((1, D_MODEL), const),
                  pl.BlockSpec((D_MODEL, N_IN), const, pipeline_mode=pl.Buffered(1))],
        out_specs=b_specs + f_specs,
        out_shape=b_shapes + f_shapes,
        compiler_params=pltpu.CompilerParams(
            dimension_semantics=("arbitrary",), vmem_limit_bytes=VMEM_LIMIT_BYTES),
        name="proj",
    )(x, g, w_bf16)


def _row_fold(op, x):
    y = x.reshape(x.shape[0] // SUBLANES, SUBLANES, x.shape[1])
    out = y[0]
    for g in range(1, y.shape[0]):
        out = op(out, y[g])
    return out


def _attend_static(qt_ops, k_tile, vt_ref, vt_rows, bias_ref, bidx, scratch, qi):
    t = ATT_TILE
    tq = Q_TILE
    halves = tq // t
    trips = qi + 1
    outs = []
    for i, qt_op in enumerate(qt_ops):
        s_ref, p_ref, mb_ref = scratch[i % len(scratch)]
        mx = [None] * halves
        k_tiles = [k_tile(i, jj) for jj in range(trips)]
        for hq in range(halves):
            cols = slice(hq * t, (hq + 1) * t)
            qt_cols = qt_op[:, cols]
            for jj in range(trips):
                for hk in range(halves):
                    dist = (halves * qi + hq) - (halves * jj + hk)
                    if dist < 0:
                        continue
                    blk = _dot(k_tiles[jj][hk * t:(hk + 1) * t], qt_cols)
                    if dist < 2:
                        blk = blk + bias_ref[bidx[i], dist]
                    s_ref[jj, hk * t:(hk + 1) * t, cols] = blk
                    f = _row_fold(jnp.maximum, blk)
                    mx[hq] = f if mx[hq] is None else jnp.maximum(mx[hq], f)
        m = jnp.max(jnp.concatenate(mx, axis=1), axis=0, keepdims=True)
        mb_ref[...] = jnp.broadcast_to(m, (SUBLANES, tq))
        accs, sums = [], []
        for hq in range(halves):
            cols = slice(hq * t, (hq + 1) * t)
            mb = mb_ref[:, cols]
            l = jnp.zeros((SUBLANES, t), F32)
            n_keys = (halves * qi + hq + 1) * t
            for kb in range(n_keys // t):
                jj, hk = divmod(kb, halves)
                sv = s_ref[jj, hk * t:(hk + 1) * t, cols].reshape(t // SUBLANES, SUBLANES, t)
                p = jnp.exp2(sv - mb[None]).reshape(t, t)
                l = l + _row_fold(jnp.add, p)
                p_ref[kb * t:(kb + 1) * t, cols] = p.astype(BF16)
            accs.append(_dot(vt_ref[0, vt_rows[i], :n_keys], p_ref[:n_keys, cols]))
            sums.append(l)
        outs.append((jnp.concatenate(accs, axis=1),
                     jnp.sum(jnp.concatenate(sums, axis=1), axis=0, keepdims=True)))
    return outs


def _per_query_tile(nq, body):
    qi = pl.program_id(2)
    for n in range(nq):
        pl.when(qi == n)(functools.partial(body, n))


def _half_rows(qt, half):
    zero = jnp.zeros((HEAD_DIM, qt.shape[1]), qt.dtype)
    if half == 0:
        return jnp.concatenate([qt[:HEAD_DIM], zero], axis=0)
    return jnp.concatenate([zero, qt[HEAD_DIM:]], axis=0)


def _scratch_ring(refs):
    return [tuple(refs[3 * r:3 * r + 3]) for r in range(ATT_RING)]


def _moba_prompt_kernel(qt_ref, k_ref, vt_ref, km_ref, bias_ref, o_ref, *scratch, nq):
    t = ATT_TILE
    tq = Q_TILE
    tile_shift = t.bit_length() - 1
    nb = km_ref.shape[0]
    qi = pl.program_id(2)
    blk = lax.broadcasted_iota(jnp.int32, (nb, tq), 0)
    own = (tq // t) * qi + (lax.broadcasted_iota(jnp.int32, (nb, tq), 1) >> tile_shift)

    qt_ops = []
    for pair in range(ATT_PAIRS):
        rows = slice(pair * LANES, (pair + 1) * LANES)
        qt = qt_ref[0, rows, :]
        km = km_ref[:, 0, rows]
        km_hi = km.astype(BF16)
        km_lo = (km - km_hi.astype(F32)).astype(BF16)
        for e in range(2):
            qet = _half_rows(qt, e)
            gt = _dot(km_hi, qet) + _dot(km_lo, qet)
            rank = jnp.zeros((nb, tq), jnp.int32)
            for m in range(nb):
                gm = gt[m:m + 1, :]
                ahead = (gm > gt) | ((gm == gt) & (m < blk))
                rank = rank + jnp.where(ahead & (m < own), 1, 0)
            keep = (blk < own) & (rank < MOBA_TOPK) & (jnp.abs(gt) < jnp.inf)
            pen = jnp.where(keep | (blk == own), 0.0, NEG_BIG)
            pen = jnp.concatenate([pen, jnp.zeros((LANES - nb, tq), F32)], axis=0)
            qt_ops.append(jnp.concatenate([qet, pen.astype(BF16)], axis=0))

    k_lane = lax.broadcasted_iota(jnp.int32, (tq, LANES), 1)
    k_blk = lax.broadcasted_iota(jnp.int32, (tq, LANES), 0) >> tile_shift

    def k_tile(i, jj):
        pair = i // 2
        onehot = jnp.where(k_lane == (tq // t) * jj + k_blk, 1.0, 0.0).astype(BF16)
        return jnp.concatenate(
            [k_ref[jj * tq:(jj + 1) * tq, pair * LANES:(pair + 1) * LANES], onehot], axis=1)

    n_ops = 2 * ATT_PAIRS
    vt_rows = [slice((i // 2) * LANES, (i // 2 + 1) * LANES) for i in range(n_ops)]

    def body(n):
        res = _attend_static(qt_ops, k_tile, vt_ref, vt_rows, bias_ref, tuple(range(n_ops)),
                             _scratch_ring(scratch), n)
        for pair in range(ATT_PAIRS):
            (a0, l0), (a1, l1) = res[2 * pair], res[2 * pair + 1]
            ot = jnp.concatenate([a0[:HEAD_DIM] / l0, a1[HEAD_DIM:] / l1], axis=0)
            o_ref[:, pair * LANES:(pair + 1) * LANES] = ot.T.astype(o_ref.dtype)

    _per_query_tile(nq, body)


def _attn_scratch(nq, seq):
    one = [pltpu.VMEM((nq, Q_TILE, Q_TILE), F32),
           pltpu.VMEM((seq, Q_TILE), BF16),
           pltpu.VMEM((SUBLANES, Q_TILE), F32)]
    return one * ATT_RING


def _attn_specs(w, seq, nq):
    return ([pl.BlockSpec((1, w, Q_TILE), lambda b, g, qi: (b, g, qi)),
             pl.BlockSpec((seq, w), lambda b, g, qi: (b, g)),
             pl.BlockSpec((1, w, seq), lambda b, g, qi: (b, g, 0))],
            pl.BlockSpec((Q_TILE, w), lambda b, g, qi: (b * nq + qi, g)))


def _moba_prompt(qa_t, ka, va_t, km, bias_tiles, batch, seq):
    nq = seq // Q_TILE
    nb = seq // MOBA_BLOCK
    w = ATT_PAIRS * LANES
    assert WIDTH_A % w == 0
    in_specs, out_spec = _attn_specs(w, seq, nq)
    return pl.pallas_call(
        functools.partial(_moba_prompt_kernel, nq=nq),
        grid=(batch, WIDTH_A // w, nq),
        in_specs=in_specs + [pl.BlockSpec((nb, 1, w), lambda b, g, qi: (b, 0, g)),
                             pl.BlockSpec((2 * ATT_PAIRS, 2, ATT_TILE, ATT_TILE),
                                          lambda b, g, qi: (g, 0, 0, 0))],
        out_specs=out_spec,
        out_shape=jax.ShapeDtypeStruct((batch * seq, WIDTH_A), BF16),
        scratch_shapes=_attn_scratch(nq, seq),
        compiler_params=pltpu.CompilerParams(
            dimension_semantics=("arbitrary", "arbitrary", "arbitrary"),
            vmem_limit_bytes=VMEM_LIMIT_BYTES),
        name="moba_prompt",
    )(qa_t, ka, va_t, km, bias_tiles)


def _diff_prompt_kernel(qt_ref, k_ref, vt_ref, bias_ref, lam_ref, gain_ref, o_ref, *scratch,
                        lam_init, nq):
    tq = Q_TILE
    qt_ops = []
    for head in range(ATT_PAIRS):
        qt = qt_ref[0, head * LANES:(head + 1) * LANES, :]
        qt_ops += [_half_rows(qt, c) for c in range(2)]

    def k_tile(i, jj):
        head = i // 2
        return k_ref[jj * tq:(jj + 1) * tq, head * LANES:(head + 1) * LANES]

    n_ops = 2 * ATT_PAIRS
    vt_rows = [slice((i // 2) * LANES, (i // 2 + 1) * LANES) for i in range(n_ops)]

    def body(n):
        res = _attend_static(qt_ops, k_tile, vt_ref, vt_rows, bias_ref,
                             tuple(i // 2 for i in range(n_ops)), _scratch_ring(scratch), n)
        lam = _lambda(lam_ref, lam_init)
        for head in range(ATT_PAIRS):
            (a0, l0), (a1, l1) = res[2 * head], res[2 * head + 1]
            o = (a0 / l0 - lam * (a1 / l1)).T
            o_ref[:, head * LANES:(head + 1) * LANES] = (
                _rmsnorm(o, gain_ref[...]) * (1.0 - lam_init)).astype(o_ref.dtype)

    _per_query_tile(nq, body)


def _diff_prompt(qd_t, kd, vd_t, bias_tiles, lam_vecs, gain, lam_init, batch, seq):
    nq = seq // Q_TILE
    w = ATT_PAIRS * LANES
    assert WIDTH_B % w == 0
    in_specs, out_spec = _attn_specs(w, seq, nq)
    return pl.pallas_call(
        functools.partial(_diff_prompt_kernel, lam_init=lam_init, nq=nq),
        grid=(batch, WIDTH_B // w, nq),
        in_specs=in_specs + [
                  pl.BlockSpec((ATT_PAIRS, 2, ATT_TILE, ATT_TILE), lambda b, g, qi: (g, 0, 0, 0)),
                  pl.BlockSpec((4, HEAD_DIM), lambda b, g, qi: (0, 0)),
                  pl.BlockSpec((1, 2 * HEAD_DIM), lambda b, g, qi: (0, 0))],
        out_specs=out_spec,
        out_shape=jax.ShapeDtypeStruct((batch * seq, WIDTH_B), BF16),
        scratch_shapes=_attn_scratch(nq, seq),
        compiler_params=pltpu.CompilerParams(
            dimension_semantics=("arbitrary", "arbitrary", "arbitrary"),
            vmem_limit_bytes=VMEM_LIMIT_BYTES),
        name="diff_prompt",
    )(qd_t, kd, vd_t, bias_tiles, lam_vecs, gain)


def _toeplitz(w, t):
    h = w.shape[0]
    m = jnp.broadcast_to(w[:, None, :], (h, t, 2 * t)).reshape(h, 2 * t * t)
    return m[:, :t * (2 * t - 1)].reshape(h, t, 2 * t - 1)[:, :, :t]


def _bias_tiles(bias_t):
    t = ATT_TILE
    assert t + 1 >= MAX_DISTANCE
    k = jnp.arange(2 * t)
    d = jnp.where(k < t, k, k - 2 * t)
    far = _bias_of(bias_t, jnp.full((1,), 2 * t))
    diag = _toeplitz(jnp.where(d >= 0, (_bias_of(bias_t, d) - far) * LOG2E, -jnp.inf), t)
    sub = _toeplitz((_bias_of(bias_t, d + t) - far) * LOG2E, t)
    return jnp.stack([diag, sub], axis=1)


def _page_stream(pt_ref, pk_ref, pv_ref, buf_ref, sem_ref, n_seq, n_chunks):
    pps = PAGES_PER_STEP
    ahead = RING_SLOTS - 1
    b, ph, j = pl.program_id(0), pl.program_id(1), pl.program_id(2)
    step = (b * 2 + ph) * n_chunks + j
    total = n_seq * 2 * n_chunks

    def copies(pool_ref, bb, jj, slot):
        return [pltpu.make_async_copy(pool_ref.at[pt_ref[bb, jj * pps + i]],
                                      buf_ref.at[slot, i], sem_ref.at[slot])
                for i in range(pps)]

    def start(chunk):
        jj, seq_phase = chunk % n_chunks, chunk // n_chunks
        for phase, pool_ref in ((0, pk_ref), (1, pv_ref)):
            @pl.when(seq_phase % 2 == phase)
            def _(pool_ref=pool_ref):
                for i, cp in enumerate(copies(pool_ref, seq_phase // 2, jj, chunk % RING_SLOTS)):
                    cp.start(priority=i % 2)

    @pl.when(step == 0)
    def _():
        for chunk in range(min(ahead, total)):
            start(jnp.int32(chunk))

    @pl.when(step + ahead < total)
    def _():
        start(step + ahead)

    slot = step % RING_SLOTS

    def wait(pool_ref):
        for cp in copies(pool_ref, b, j, slot):
            cp.wait()

    return slot, wait


def _moba_sample_kernel(pt_ref, pk_ref, pv_ref, q_ref, kn_ref, vn_ref, bias_ref, o_ref,
                        s_ref, snew_ref, acc_ref, l_ref, m_ref, bmax_ref, gate_ref, sel_ref,
                        buf_ref, sem_ref, *, n_seq, n_chunks, tokens):
    pps = PAGES_PER_STEP
    slot, wait = _page_stream(pt_ref, pk_ref, pv_ref, buf_ref, sem_ref, n_seq, n_chunks)
    rows = q_ref.shape[1]
    groups = rows // tokens
    ppb = MOBA_BLOCK // PAGE_SIZE
    bps = pps // ppb
    n_blocks = n_chunks * bps
    ph = pl.program_id(1)
    j = pl.program_id(2)
    lane = lax.broadcasted_iota(jnp.int32, (rows, LANES), 1)
    lanef = lane.astype(F32)
    q = q_ref[0]

    @pl.when((ph == 0) & (j == 0))
    def _():
        snew_ref[...] = _dot_nt(q, kn_ref[0]) + bias_ref[2]
        bmax_ref[...] = jnp.full((rows, LANES), -jnp.inf, F32)
        gate_ref[...] = jnp.zeros((rows, LANES), F32)

    @pl.when(ph == 0)
    def _():
        wait(pk_ref)
        is_last = j == n_chunks - 1
        bmax = bmax_ref[...]
        gate = gate_ref[...]
        for b in range(bps):
            smax = ssum = None
            for pg in range(ppb):
                i = b * ppb + pg
                raw = _dot(q, buf_ref[slot, i].astype(BF16))
                bias = bias_ref[0]
                if i == pps - 1:
                    bias = jnp.where(is_last, bias_ref[1], bias)
                s = raw + bias
                s_ref[j, :, i * PAGE_SIZE:(i + 1) * PAGE_SIZE] = s
                smax = s if smax is None else jnp.maximum(smax, s)
                ssum = raw if ssum is None else ssum + raw
            blk = j * bps + b
            bmax = jnp.where(lane == blk, jnp.max(smax, axis=1, keepdims=True), bmax)
            gate = jnp.where(lane == blk, jnp.sum(ssum, axis=1, keepdims=True), gate)
        bmax_ref[...] = bmax
        gate_ref[...] = gate

    @pl.when((ph == 1) & (j == 0))
    def _():
        g = jnp.where(lane < n_blocks, gate_ref[...], -jnp.inf)
        sel = jnp.zeros((rows, LANES), F32)
        for _ in range(MOBA_TOPK):
            top = jnp.max(g, axis=1, keepdims=True)
            first = jnp.min(jnp.where(g == top, lanef, float(LANES)), axis=1, keepdims=True)
            pick = lanef == first
            finite = jnp.where(jnp.abs(top) < jnp.inf, 1.0, 0.0)
            sel = jnp.maximum(sel, jnp.where(pick, finite, 0.0))
            g = jnp.where(pick, -jnp.inf, g)
        sel_ref[...] = sel
        m_sel = jnp.max(jnp.where(sel > 0.5, bmax_ref[...], -jnp.inf), axis=1, keepdims=True)
        m_new = jnp.max(snew_ref[...], axis=1, keepdims=True)
        m_ref[...] = jnp.broadcast_to(jnp.maximum(m_sel, m_new), (rows, LANES))
        acc_ref[...] = jnp.zeros(acc_ref.shape, F32)
        l_ref[...] = jnp.zeros(l_ref.shape, F32)

    @pl.when(ph == 1)
    def _():
        wait(pv_ref)
        m = m_ref[...]
        selv = sel_ref[...]
        acc = acc_ref[...]
        l = l_ref[...]
        for i in range(pps):
            blk = j * bps + i // ppb
            on = jnp.max(jnp.where(lane == blk, selv, 0.0), axis=1, keepdims=True)
            on = jnp.broadcast_to(on, (rows, LANES)) > 0.5
            s = s_ref[j, :, i * PAGE_SIZE:(i + 1) * PAGE_SIZE]
            p = jnp.exp(jnp.where(on, s - m, -jnp.inf))
            l = l + p
            acc = acc + _dot_nt(p.astype(BF16), buf_ref[slot, i].astype(BF16))
        acc_ref[...] = acc
        l_ref[...] = l

    @pl.when((ph == 1) & (j == n_chunks - 1))
    def _():
        p = jnp.exp(snew_ref[...] - m_ref[...])
        acc = acc_ref[...] + _dot(p.astype(BF16), vn_ref[0])
        l = jnp.sum(l_ref[...] + p, axis=1, keepdims=True)
        o = acc / l
        grp = lax.broadcasted_iota(jnp.int32, (groups, WIDTH_A), 0)
        col = lax.broadcasted_iota(jnp.int32, (groups, WIDTH_A), 1)
        head_shift = HEAD_DIM.bit_length() - 1
        w = jnp.where((col >> head_shift) == grp, 1.0, 0.0)
        tok = jnp.sum(o.reshape(tokens, groups, WIDTH_A) * w[None], axis=1)
        o_ref[0] = tok.astype(o_ref.dtype)


def _moba_sample(page_table, pool_kt, pool_vt, q_bd, k_new, v_new, bias_rows, *, tokens):
    pps = PAGES_PER_STEP
    n_seq, n_pages = page_table.shape
    n_chunks = n_pages // pps
    rows = q_bd.shape[1]
    seq_map = lambda b, ph, j, pt: (b, 0, 0)
    hbm = pl.BlockSpec(memory_space=pl.ANY)
    in_specs = [hbm, hbm,
                pl.BlockSpec((1, rows, WIDTH_A), seq_map),
                pl.BlockSpec((1, LANES, WIDTH_A), seq_map),
                pl.BlockSpec((1, LANES, WIDTH_A), seq_map),
                pl.BlockSpec((3, rows, LANES), lambda b, ph, j, pt: (0, 0, 0))]
    vec = pltpu.VMEM((rows, LANES), F32)
    grid_spec = pltpu.PrefetchScalarGridSpec(
        num_scalar_prefetch=1,
        grid=(n_seq, 2, n_chunks),
        in_specs=in_specs,
        out_specs=pl.BlockSpec((1, tokens, WIDTH_A), seq_map),
        scratch_shapes=[pltpu.VMEM((n_chunks, rows, pps * PAGE_SIZE), F32),
                        vec,
                        pltpu.VMEM((rows, WIDTH_A), F32),
                        vec, vec, vec, vec, vec,
                        pltpu.VMEM((RING_SLOTS, pps) + pool_kt.shape[1:], F32),
                        pltpu.SemaphoreType.DMA((RING_SLOTS,))])
    return pl.pallas_call(
        functools.partial(_moba_sample_kernel, n_seq=n_seq, n_chunks=n_chunks, tokens=tokens),
        grid_spec=grid_spec,
        out_shape=jax.ShapeDtypeStruct((n_seq, tokens, WIDTH_A), BF16),
        compiler_params=pltpu.CompilerParams(
            dimension_semantics=("arbitrary", "arbitrary", "arbitrary"),
            vmem_limit_bytes=VMEM_LIMIT_BYTES),
        name="moba_sample",
    )(page_table, pool_kt, pool_vt, q_bd, k_new, v_new, bias_rows)


def _diff_sample_kernel(pt_ref, pk_ref, pv_ref, q_ref, kn_ref, vn_ref, bias_ref, bnew_ref,
                        lam_ref, gain_ref, o_ref, s_ref, snew_ref, acc_ref, l_ref, m_ref,
                        buf_ref, sem_ref, *, n_seq, n_chunks, tokens, lam_init):
    pps = PAGES_PER_STEP
    slot, wait = _page_stream(pt_ref, pk_ref, pv_ref, buf_ref, sem_ref, n_seq, n_chunks)
    rows = q_ref.shape[1]
    pw = buf_ref.shape[2]
    ph = pl.program_id(1)
    j = pl.program_id(2)
    q = q_ref[0]

    @pl.when((ph == 0) & (j == 0))
    def _():
        snew_ref[...] = _dot_nt(q, kn_ref[0]) + bnew_ref[...]
        m_ref[...] = jnp.full((rows, LANES), -jnp.inf, F32)

    @pl.when(ph == 0)
    def _():
        wait(pk_ref)
        is_last = j == n_chunks - 1
        mx = m_ref[...]
        for i in range(pps):
            bias = bias_ref[0]
            if i == pps - 1:
                bias = jnp.where(is_last, bias_ref[1], bias)
            s = _dot_nt(q, buf_ref[slot, i].astype(BF16)) + bias
            s_ref[j, :, i * pw:(i + 1) * pw] = s
            mx = jnp.maximum(mx, _lane_fold(jnp.maximum, s))
        m_ref[...] = mx

    @pl.when((ph == 1) & (j == 0))
    def _():
        m = jnp.maximum(jnp.max(m_ref[...], axis=1, keepdims=True),
                        jnp.max(snew_ref[...], axis=1, keepdims=True))
        m_ref[...] = jnp.broadcast_to(m, (rows, LANES))
        acc_ref[...] = jnp.zeros(acc_ref.shape, F32)
        l_ref[...] = jnp.zeros(l_ref.shape, F32)

    @pl.when(ph == 1)
    def _():
        wait(pv_ref)
        m = m_ref[...]
        mw = jnp.concatenate([m] * (pw // LANES), axis=1)
        acc = acc_ref[...]
        l = l_ref[...]
        for i in range(pps):
            p = jnp.exp(s_ref[j, :, i * pw:(i + 1) * pw] - mw)
            l = l + _lane_fold(jnp.add, p)
            acc = acc + _dot(p.astype(BF16), buf_ref[slot, i].astype(BF16))
        acc_ref[...] = acc
        l_ref[...] = l

    @pl.when((ph == 1) & (j == n_chunks - 1))
    def _():
        p = jnp.exp(snew_ref[...] - m_ref[...])
        acc = acc_ref[...] + _dot(p.astype(BF16), vn_ref[0])
        l = jnp.sum(l_ref[...] + p, axis=1, keepdims=True)
        o = (acc / l).reshape(tokens, rows // tokens, LANES)
        lam = _lambda(lam_ref, lam_init)
        sub = lax.broadcasted_iota(jnp.int32, (rows // tokens, LANES), 0)
        gain = gain_ref[...]
        parts = []
        for h in range(N_HEADS_B):
            w = jnp.where(sub == 2 * h, 1.0, jnp.where(sub == 2 * h + 1, -lam, 0.0))
            parts.append(_rmsnorm(jnp.sum(o * w[None], axis=1), gain) * (1.0 - lam_init))
        o_ref[0] = jnp.concatenate(parts, axis=1).astype(o_ref.dtype)


def _diff_sample(page_table, pool_k, pool_v, q2, k_new, v_new, bias_rows, bias_new, lam_vecs, gain,
                 *, tokens, lam_init):
    pps = PAGES_PER_STEP
    n_seq, n_pages = page_table.shape
    n_chunks = n_pages // pps
    rows = q2.shape[1]
    pw = pool_k.shape[1]
    seq_map = lambda b, ph, j, pt: (b, 0, 0)
    const2 = lambda b, ph, j, pt: (0, 0)
    hbm = pl.BlockSpec(memory_space=pl.ANY)
    in_specs = [hbm, hbm,
                pl.BlockSpec((1, rows, LANES), seq_map),
                pl.BlockSpec((1, LANES, LANES), seq_map),
                pl.BlockSpec((1, LANES, LANES), seq_map),
                pl.BlockSpec((2, rows, pw), lambda b, ph, j, pt: (0, 0, 0)),
                pl.BlockSpec((rows, LANES), const2),
                pl.BlockSpec((4, HEAD_DIM), const2),
                pl.BlockSpec((1, 2 * HEAD_DIM), const2)]
    vec = pltpu.VMEM((rows, LANES), F32)
    grid_spec = pltpu.PrefetchScalarGridSpec(
        num_scalar_prefetch=1,
        grid=(n_seq, 2, n_chunks),
        in_specs=in_specs,
        out_specs=pl.BlockSpec((1, tokens, WIDTH_B), seq_map),
        scratch_shapes=[pltpu.VMEM((n_chunks, rows, pps * pw), F32),
                        vec, vec, vec, vec,
                        pltpu.VMEM((RING_SLOTS, pps) + pool_k.shape[1:], F32),
                        pltpu.SemaphoreType.DMA((RING_SLOTS,))])
    return pl.pallas_call(
        functools.partial(_diff_sample_kernel, n_seq=n_seq, n_chunks=n_chunks, tokens=tokens,
                          lam_init=lam_init),
        grid_spec=grid_spec,
        out_shape=jax.ShapeDtypeStruct((n_seq, tokens, WIDTH_B), BF16),
        compiler_params=pltpu.CompilerParams(
            dimension_semantics=("arbitrary", "arbitrary", "arbitrary"),
            vmem_limit_bytes=VMEM_LIMIT_BYTES),
        name="diff_sample",
    )(page_table, pool_k, pool_v, q2, k_new, v_new, bias_rows, bias_new, lam_vecs, gain)


def _block_diag(q, n_seq, tokens, groups, keep):
    w = q.shape[1] // groups
    q4 = q.reshape(n_seq, tokens, groups, 1, w)
    eye = (jnp.arange(groups)[:, None] % keep == jnp.arange(keep)[None, :]).astype(q.dtype)
    return (q4 * eye.reshape(1, 1, groups, keep, 1)).reshape(n_seq, tokens * groups, keep * w)


def _pad_rows(x, n_seq):
    x = x.reshape(n_seq, -1, x.shape[-1])
    return jnp.pad(x, ((0, 0), (0, LANES - x.shape[1]), (0, 0)))


def _sample_bias(bias_t, tokens, heads_per_group):
    assert PAGE_SIZE + 1 >= MAX_DISTANCE
    tok = jnp.arange(tokens)[:, None]
    col = jnp.arange(LANES)[None, :]
    far = _bias_of(bias_t, jnp.broadcast_to(2 * PAGE_SIZE, (tokens, LANES)))
    last = _bias_of(bias_t, PAGE_SIZE + tok - col)
    new = jnp.where((col <= tok) & (col < tokens), _bias_of(bias_t, tok - col), -jnp.inf)

    def rows(x):
        x = jnp.repeat(x.transpose(1, 0, 2), heads_per_group, axis=1)
        return x.reshape(-1, LANES)

    return rows(far), rows(last), rows(new)


def _spread_heads(x, n_heads, groups):
    r, k = x.shape
    row_head = (jnp.arange(r) % groups) // (groups // n_heads)
    own = row_head[:, None, None] == jnp.arange(n_heads)[None, None, :]
    return jnp.where(own, x[:, :, None], -jnp.inf).reshape(r, k * n_heads)


def _merge_kernel(x_ref, oa_ref, ob_ref, g_ref, wg_ref, woa_ref, wob_ref, wo_ref, o_ref):
    x = x_ref[...]
    h = _rmsnorm(x, g_ref[...]).astype(BF16)
    ga = 1.0 / (1.0 + jnp.exp(-_dot(h, wg_ref[:, :D_MODEL].astype(BF16))))
    m = ga * _dot(oa_ref[...], woa_ref[...].astype(BF16))
    gb = 1.0 / (1.0 + jnp.exp(-_dot(h, wg_ref[:, D_MODEL:].astype(BF16))))
    m = m + gb * _dot(ob_ref[...], wob_ref[...].astype(BF16))
    o_ref[...] = x + _dot(m.astype(BF16), wo_ref[...].astype(BF16))


def _merge(x, oa, ob, g, wg, woa, wob, wo, tm):
    n = x.shape[0]
    row = lambda i: (i, 0)
    const = lambda i: (0, 0)
    return pl.pallas_call(
        _merge_kernel,
        grid=(n // tm,),
        in_specs=[pl.BlockSpec((tm, D_MODEL), row),
                  pl.BlockSpec((tm, WIDTH_A), row),
                  pl.BlockSpec((tm, WIDTH_B), row),
                  pl.BlockSpec((1, D_MODEL), const),
                  pl.BlockSpec((D_MODEL, 2 * D_MODEL), const, pipeline_mode=pl.Buffered(1)),
                  pl.BlockSpec((WIDTH_A, D_MODEL), const, pipeline_mode=pl.Buffered(1)),
                  pl.BlockSpec((WIDTH_B, D_MODEL), const, pipeline_mode=pl.Buffered(1)),
                  pl.BlockSpec((D_MODEL, D_MODEL), const, pipeline_mode=pl.Buffered(1))],
        out_specs=pl.BlockSpec((tm, D_MODEL), row),
        out_shape=jax.ShapeDtypeStruct((n, D_MODEL), F32),
        compiler_params=pltpu.CompilerParams(
            dimension_semantics=("arbitrary",), vmem_limit_bytes=VMEM_LIMIT_BYTES),
        name="merge",
    )(x, oa, ob, g, wg, woa, wob, wo)


def _ffn_kernel(*refs, tm, seq_len, has_prev):
    if has_prev:
        (x_ref, g_ref, wup_ref, cw_ref, cb_ref, wdn_ref, gfin_ref, e1_ref, e2_ref,
         y_ref, u_ref, act_ref) = refs
    else:
        (x_ref, g_ref, wup_ref, cw_ref, cb_ref, wdn_ref, gfin_ref,
         y_ref, u_ref, act_ref, carry_ref) = refs
    i = pl.program_id(0)
    x = x_ref[...]
    h = _rmsnorm(x, g_ref[...]).astype(BF16)
    row = lax.broadcasted_iota(jnp.int32, (tm, FF_CHUNK), 0)
    row8 = lax.broadcasted_iota(jnp.int32, (SUBLANES, FF_CHUNK), 0)
    if has_prev:
        pos = row & (seq_len - 1)
    else:
        @pl.when((i * tm) % seq_len == 0)
        def _():
            carry_ref[...] = jnp.zeros(carry_ref.shape, F32)

    for c in range(D_FF // FF_CHUNK):
        halves = []
        for part in range(2):
            cols = slice(part * D_FF + c * FF_CHUNK, part * D_FF + (c + 1) * FF_CHUNK)
            u = _dot(h, wup_ref[:, cols])
            um1 = pltpu.roll(u, 1, 0)
            um2 = pltpu.roll(u, 2, 0)
            if has_prev:
                um1 = jnp.where(pos == 0, e1_ref[:, cols], um1)
                um2 = jnp.where(pos < 2, e2_ref[:, cols], um2)
                u_ref[:, cols] = u
            else:
                prev = carry_ref[:, cols]
                top1 = jnp.where(row8 == 0, pltpu.roll(prev, 1, 0), um1[:SUBLANES])
                top2 = jnp.where(row8 < 2, pltpu.roll(prev, 2, 0), um2[:SUBLANES])
                um1 = jnp.concatenate([top1, um1[SUBLANES:]], axis=0)
                um2 = jnp.concatenate([top2, um2[SUBLANES:]], axis=0)
                carry_ref[:, cols] = u[tm - SUBLANES:]
                u_ref[:, cols] = u[tm - SUBLANES:]
            cw = cw_ref[:, cols]
            halves.append(((cb_ref[:, cols] + cw[0:1] * um2) + cw[1:2] * um1) + cw[2:3] * u)
        gate, val = halves
        act = (gate * (1.0 / (1.0 + jnp.exp(-gate)))) * val
        act_ref[:, c * FF_CHUNK:(c + 1) * FF_CHUNK] = act.astype(BF16)

    x3 = x + _dot(act_ref[...], wdn_ref[...])
    y_ref[...] = _rmsnorm(x3, gfin_ref[...])


def _ffn(x, g, wup, cw, cb, wdn, gfin, prev, tm, seq_len):
    n = x.shape[0]
    has_prev = prev is not None
    row = lambda i: (i, 0)
    const = lambda i: (0, 0)
    in_specs = [pl.BlockSpec((tm, D_MODEL), row),
                pl.BlockSpec((1, D_MODEL), const),
                pl.BlockSpec((D_MODEL, 2 * D_FF), const, pipeline_mode=pl.Buffered(1)),
                pl.BlockSpec((CONV_W, 2 * D_FF), const),
                pl.BlockSpec((1, 2 * D_FF), const),
                pl.BlockSpec((D_FF, D_MODEL), const, pipeline_mode=pl.Buffered(1)),
                pl.BlockSpec((1, D_MODEL), const)]
    scratch = [pltpu.VMEM((tm, D_FF), BF16)]
    if has_prev:
        assert n == tm and tm % seq_len == 0 and seq_len & (seq_len - 1) == 0
        in_specs += [pl.BlockSpec((tm, 2 * D_FF), row)] * 2
        u_shape, u_spec = (n, 2 * D_FF), pl.BlockSpec((tm, 2 * D_FF), row)
        args = (x, g, wup, cw, cb, wdn, gfin) + tuple(prev)
    else:
        assert seq_len % tm == 0
        tiles_per_seq = seq_len // tm
        u_shape = (n // seq_len * SUBLANES, 2 * D_FF)
        u_spec = pl.BlockSpec((SUBLANES, 2 * D_FF), lambda i: (i // tiles_per_seq, 0))
        scratch.append(pltpu.VMEM((SUBLANES, 2 * D_FF), F32))
        args = (x, g, wup, cw, cb, wdn, gfin)
    return pl.pallas_call(
        functools.partial(_ffn_kernel, tm=tm, seq_len=seq_len, has_prev=has_prev),
        grid=(n // tm,),
        in_specs=in_specs,
        out_specs=[pl.BlockSpec((tm, D_MODEL), row), u_spec],
        out_shape=[jax.ShapeDtypeStruct((n, D_MODEL), F32), jax.ShapeDtypeStruct(u_shape, F32)],
        scratch_shapes=scratch,
        compiler_params=pltpu.CompilerParams(
            dimension_semantics=("arbitrary",), vmem_limit_bytes=VMEM_LIMIT_BYTES),
        name="ffn_sample" if has_prev else "ffn_prompt",
    )(*args)


def kernel(x_prompt, x_sample, cache_moba_k, cache_moba_v, cache_diff_k, cache_diff_v, state_conv, page_table, rel_bias, norm_attn, w_in, w_gate, w_out_a, w_out_b, w_out, lambda_q1, lambda_k1, lambda_q2, lambda_k2, diff_norm, norm_ffn, w_up, conv_w, conv_b, w_down, norm_final):
    batch, seq, _ = x_prompt.shape
    n_seq, tokens, _ = x_sample.shape
    depth = w_in.shape[0]
    n_phys = cache_moba_k.shape[1]
    n_pages = page_table.shape[1]
    assert depth == 1 and seq % Q_TILE == 0 and ATT_TILE & (ATT_TILE - 1) == 0
    assert (n_pages * PAGE_SIZE) % MOBA_BLOCK == 0 and n_pages % PAGES_PER_STEP == 0
    assert n_pages * PAGE_SIZE // MOBA_BLOCK <= LANES and CONV_W - 1 <= tokens <= LANES // N_HEADS_B
    l = 0
    lam_init = 0.8 - 0.6 * math.exp(-0.3 * l)

    bias_a = rel_bias[:, :N_HEADS_A].T
    bias_d = rel_bias[:, N_HEADS_A:].T
    row = lambda v: v.reshape(1, -1)
    w_in_b, w_gate_b, w_oa_b, w_ob_b, w_o_b = w_in[l], w_gate[l], w_out_a[l], w_out_b[l], w_out[l]
    w_up_b = w_up[l].astype(BF16)
    w_dn_b = w_down[l].astype(BF16)
    lam_vecs = jnp.stack([lambda_q1[l], lambda_k1[l], lambda_q2[l], lambda_k2[l]]).astype(F32)
    gain_d = row(diff_norm[l])
    g_attn, g_ffn, g_fin = row(norm_attn[l]), row(norm_ffn[l]), row(norm_final)
    cb = row(conv_b[l])

    xp = x_prompt.reshape(batch * seq, D_MODEL)
    qa_tb, ka, va_tb, qd_tb, kd, vd_tb, ka_t, va_t, kd_r, vd_r, km = _proj(xp, g_attn, w_in_b, 512, seq)
    oa = _moba_prompt(qa_tb, ka, va_tb, km, _bias_tiles(bias_a), batch, seq)
    ob = _diff_prompt(qd_tb, kd, vd_tb, _bias_tiles(bias_d), lam_vecs, gain_d, lam_init, batch, seq)
    x2 = _merge(xp, oa, ob, g_attn, w_gate_b, w_oa_b, w_ob_b, w_o_b, 512)
    yp, tail_p = _ffn(x2, g_ffn, w_up_b, conv_w[l], cb, w_dn_b, g_fin, None, 1024, seq)
    conv_p = tail_p.reshape(batch, SUBLANES, 2 * D_FF)[:, SUBLANES - (CONV_W - 1):]

    n_s = n_seq * tokens
    xs = x_sample.reshape(n_s, D_MODEL)
    qa_s, ka_s, va_s, qd_s, kd_s, vd_s, ka_sf, va_sf, kd_sf, vd_sf = _proj(xs, g_attn, w_in_b, n_s)

    pool_t = lambda c: jnp.transpose(c[l], (0, 2, 3, 1)).reshape(n_phys, WIDTH_A, PAGE_SIZE)
    pool_r = lambda c: c[l].reshape(n_phys, PAGE_SIZE * N_HEADS_B, 2 * HEAD_DIM)
    far_a, last_a, new_a = _sample_bias(bias_a, tokens, 1)
    oa_s = _moba_sample(page_table, pool_t(cache_moba_k), pool_t(cache_moba_v),
                        _block_diag(qa_s, n_seq, tokens, N_HEADS_A, N_HEADS_A),
                        _pad_rows(ka_s, n_seq), _pad_rows(va_s, n_seq),
                        jnp.stack([far_a, last_a, new_a]), tokens=tokens)
    far_d, last_d, new_d = _sample_bias(bias_d, tokens, 2)
    groups_d = 2 * N_HEADS_B
    spread = lambda x: _spread_heads(x, N_HEADS_B, groups_d)
    per_head = lambda x: x.reshape(n_s * N_HEADS_B, 2 * HEAD_DIM)
    oa_d = _diff_sample(page_table, pool_r(cache_diff_k), pool_r(cache_diff_v),
                        _block_diag(qd_s, n_seq, tokens, groups_d, 2),
                        _pad_rows(per_head(kd_s), n_seq), _pad_rows(per_head(vd_s), n_seq),
                        jnp.stack([spread(far_d), spread(last_d)]),
                        spread(new_d[:, :LANES // N_HEADS_B]),
                        lam_vecs, gain_d, tokens=tokens, lam_init=lam_init)
    x2s = _merge(xs, oa_s.reshape(n_s, WIDTH_A), oa_d.reshape(n_s, WIDTH_B), g_attn,
                 w_gate_b, w_oa_b, w_ob_b, w_o_b, n_s)
    st = state_conv[l]
    zero = jnp.zeros((n_seq, 1, 2 * D_FF), F32)
    e1 = jnp.concatenate([st[:, 1:2]] + [zero] * (tokens - 1), axis=1).reshape(n_s, 2 * D_FF)
    e2 = jnp.concatenate([st[:, 0:2]] + [zero] * (tokens - 2), axis=1).reshape(n_s, 2 * D_FF)
    ys, u_s = _ffn(x2s, g_ffn, w_up_b, conv_w[l], cb, w_dn_b, g_fin, (e1, e2), n_s, tokens)
    conv_s = u_s.reshape(n_seq, tokens, 2 * D_FF)[:, tokens - (CONV_W - 1):]

    shp_a = lambda a, b_, t_: a.reshape(1, b_, t_, N_HEADS_A, HEAD_DIM)
    shp_d = lambda a, b_, t_: a.reshape(1, b_, t_, N_HEADS_B, 2 * HEAD_DIM)
    untr = lambda a: a.reshape(batch, N_HEADS_A, HEAD_DIM, seq).transpose(0, 3, 1, 2)[None]
    return (yp.reshape(batch, seq, D_MODEL), ys.reshape(n_seq, tokens, D_MODEL),
            untr(ka_t), untr(va_t),
            shp_d(kd_r, batch, seq), shp_d(vd_r, batch, seq), conv_p[None],
            shp_a(ka_sf, n_seq, tokens), shp_a(va_sf, n_seq, tokens),
            shp_d(kd_sf, n_seq, tokens), shp_d(vd_sf, n_seq, tokens), conv_s[None])
```

```python
import functools
import math

import jax
import jax.numpy as jnp
from jax import lax
from jax.experimental import pallas as pl
from jax.experimental.pallas import tpu as pltpu

F32 = jnp.float32
BF16 = jnp.bfloat16

D_MODEL = 1024
HEAD_DIM = 64
N_HEADS_A = D_MODEL // 128
N_HEADS_B = D_MODEL // 256
WIDTH_A = N_HEADS_A * HEAD_DIM
WIDTH_B = N_HEADS_B * 2 * HEAD_DIM
N_IN = 3 * WIDTH_A + 3 * WIDTH_B
MOBA_BLOCK = 256
MOBA_TOPK = 3
NUM_BUCKETS = 32
MAX_DISTANCE = 128
D_FF = ((8 * D_MODEL // 3 + 127) // 128) * 128
CONV_W = 3
EPS = 1e-6
PAGE_SIZE = 128
SCALE = HEAD_DIM ** -0.5
LOG2E = math.log2(math.e)

LANES = 128
SUBLANES = 8
VMEM_LIMIT_BYTES = 56 * 1024 * 1024

ATT_TILE = MOBA_BLOCK
Q_TILE = 2 * ATT_TILE
ATT_PAIRS = 2
ATT_RING = 2
FF_CHUNK = 256
PAGES_PER_STEP = 16
RING_SLOTS = 4
NEG_BIG = -1e30


def _dot(a, b):
    return jnp.dot(a, b, preferred_element_type=F32)


def _dot_nt(a, b):
    return lax.dot_general(a, b, (((1,), (1,)), ((), ())), preferred_element_type=F32)


def _rmsnorm(x, g):
    return (x * lax.rsqrt(jnp.mean(x * x, axis=-1, keepdims=True) + EPS)) * g


def _lane_fold(op, x):
    out = x[:, :LANES]
    for c in range(1, x.shape[1] // LANES):
        out = op(out, x[:, c * LANES:(c + 1) * LANES])
    return out


def _t5_bucket(dist):
    n = jnp.maximum(dist, 0)
    max_exact = NUM_BUCKETS // 2
    nf = jnp.maximum(n, 1).astype(F32)
    large = max_exact + (jnp.log(nf / max_exact) / math.log(MAX_DISTANCE / max_exact)
                         * (NUM_BUCKETS - max_exact)).astype(jnp.int32)
    large = jnp.minimum(large, NUM_BUCKETS - 1)
    return jnp.where(n < max_exact, n, large)


def _bias_of(bias_t, dist):
    onehot = jax.nn.one_hot(_t5_bucket(dist), NUM_BUCKETS, dtype=F32)
    return jnp.einsum('hb,...b->h...', bias_t.astype(F32), onehot, precision=lax.Precision.HIGHEST)


def _lambda(lam_ref, lam_init):
    lv = lam_ref[...]
    a = jnp.sum(lv[0:1] * lv[1:2], axis=-1, keepdims=True)
    b = jnp.sum(lv[2:3] * lv[3:4], axis=-1, keepdims=True)
    return jnp.exp(a) - jnp.exp(b) + lam_init


def _proj_kernel(x_ref, g_ref, w_ref,
                 qa_ref, ka_ref, va_ref, qd_ref, kd_ref, vd_ref,
                 kaf_ref, vaf_ref, kdf_ref, vdf_ref, *km_ref, prompt):
    tm = x_ref.shape[0]
    h = _rmsnorm(x_ref[...], g_ref[...]).astype(BF16)
    outs = ((qa_ref, None, SCALE), (ka_ref, kaf_ref, None), (va_ref, vaf_ref, None),
            (qd_ref, None, SCALE), (kd_ref, kdf_ref, None), (vd_ref, vdf_ref, None))
    for c, (b_ref, f_ref, scale) in enumerate(outs):
        u = _dot(h, w_ref[:, c * WIDTH_A:(c + 1) * WIDTH_A].astype(BF16))
        if scale is not None:
            u = u * (scale * LOG2E if prompt else scale)
        if not prompt:
            if f_ref is not None:
                f_ref[...] = u
            b_ref[...] = u.astype(BF16)
            continue
        is_key = c in (1, 4)
        ut = u.T if c != 4 else None
        if c in (1, 2):
            f_ref[0] = ut
        elif f_ref is not None:
            for hd in range(N_HEADS_B):
                f_ref[pl.ds(hd, tm, stride=N_HEADS_B), :] = u[:, hd * LANES:(hd + 1) * LANES]
        if is_key:
            b_ref[...] = u.astype(BF16)
        else:
            b_ref[0] = ut.astype(BF16)
        if c == 1:
            for j in range(tm // MOBA_BLOCK):
                km_ref[0][j] = jnp.sum(u[j * MOBA_BLOCK:(j + 1) * MOBA_BLOCK], axis=0,
                                       keepdims=True) * (1.0 / MOBA_BLOCK)


def _proj(x, g, w_bf16, tm, seq=None):
    n = x.shape[0]
    row = lambda i: (i, 0)
    const = lambda i: (0, 0)
    blk = pl.BlockSpec((tm, WIDTH_A), row)
    b_specs = [blk] * 6
    b_shapes = [jax.ShapeDtypeStruct((n, WIDTH_A), BF16)] * 6
    f_specs = [blk] * 4
    f_shapes = [jax.ShapeDtypeStruct((n, WIDTH_A), F32)] * 4
    if seq is not None:
        assert seq % tm == 0 and tm % MOBA_BLOCK == 0 and 2 * HEAD_DIM == LANES
        tiles = seq // tm
        t_spec = pl.BlockSpec((1, WIDTH_A, tm), lambda i: (i // tiles, 0, i % tiles))
        r_spec = pl.BlockSpec((tm * N_HEADS_B, LANES), row)
        r_shape = jax.ShapeDtypeStruct((n * N_HEADS_B, LANES), F32)
        t_shape = lambda dt: jax.ShapeDtypeStruct((n // seq, WIDTH_A, seq), dt)
        b_specs = [t_spec, blk, t_spec, t_spec, blk, t_spec]
        b_shapes = [t_shape(BF16), b_shapes[0], t_shape(BF16), t_shape(BF16), b_shapes[0], t_shape(BF16)]
        f_specs = [t_spec, t_spec, r_spec, r_spec,
                   pl.BlockSpec((tm // MOBA_BLOCK, 1, WIDTH_A), lambda i: (i, 0, 0))]
        f_shapes = [t_shape(F32), t_shape(F32), r_shape, r_shape,
                    jax.ShapeDtypeStruct((n // MOBA_BLOCK, 1, WIDTH_A), F32)]
    return pl.pallas_call(
        functools.partial(_proj_kernel, prompt=seq is not None),
        grid=(n // tm,),
        in_specs=[pl.BlockSpec((tm, D_MODEL), row),
                  pl.BlockSpec((1, D_MODEL), const),
                  pl.BlockSpec((D_MODEL, N_IN), const, pipeline_mode=pl.Buffered(1))],
        out_specs=b_specs + f_specs,
        out_shape=b_shapes + f_shapes,
        compiler_params=pltpu.CompilerParams(
            dimension_semantics=("arbitrary",), vmem_limit_bytes=VMEM_LIMIT_BYTES),
        name="proj",
    )(x, g, w_bf16)


def _row_fold(op, x):
    y = x.reshape(x.shape[0] // SUBLANES, SUBLANES, x.shape[1])
    out = y[0]
    for g in range(1, y.shape[0]):
        out = op(out, y[g])
    return out


def _attend_static(qt_ops, k_tile, vt_ref, vt_rows, bias_ref, bidx, scratch, qi):
    t = ATT_TILE
    tq = Q_TILE
    halves = tq // t
    trips = qi + 1
    outs = [None] * len(qt_ops)

    def pass1(i):
        s_ref, mb_ref = scratch[i % len(scratch)]
        mx = [None] * halves
        k_tiles = [k_tile(i, jj) for jj in range(trips)]
        for hq in range(halves):
            cols = slice(hq * t, (hq + 1) * t)
            qt_cols = qt_ops[i][:, cols]
            for jj in range(trips):
                for hk in range(halves):
                    dist = (halves * qi + hq) - (halves * jj + hk)
                    if dist < 0:
                        continue
                    blk = _dot(k_tiles[jj][hk * t:(hk + 1) * t], qt_cols)
                    if dist < 2:
                        blk = blk + bias_ref[bidx[i], dist]
                    s_ref[jj, hk * t:(hk + 1) * t, cols] = blk
                    f = _row_fold(jnp.maximum, blk)
                    mx[hq] = f if mx[hq] is None else jnp.maximum(mx[hq], f)
                    yield
        m = jnp.max(jnp.concatenate(mx, axis=1), axis=0, keepdims=True)
        mb_ref[...] = jnp.broadcast_to(m, (SUBLANES, tq))

    def pass2(i):
        s_ref, mb_ref = scratch[i % len(scratch)]
        accs, sums = [], []
        for hq in range(halves):
            cols = slice(hq * t, (hq + 1) * t)
            mb = mb_ref[:, cols]
            l = jnp.zeros((SUBLANES, t), F32)
            acc = None
            n_keys = (halves * qi + hq + 1) * t
            for kb in range(n_keys // t):
                jj, hk = divmod(kb, halves)
                sv = s_ref[jj, hk * t:(hk + 1) * t, cols].reshape(t // SUBLANES, SUBLANES, t)
                p = jnp.exp2(sv - mb[None]).reshape(t, t)
                l = l + _row_fold(jnp.add, p)
                part = _dot(vt_ref[0, vt_rows[i], kb * t:(kb + 1) * t], p.astype(BF16))
                acc = part if acc is None else acc + part
                yield
            accs.append(acc)
            sums.append(l)
        outs[i] = (jnp.concatenate(accs, axis=1),
                   jnp.sum(jnp.concatenate(sums, axis=1), axis=0, keepdims=True))

    for _ in pass1(0):
        pass
    for i in range(1, len(qt_ops)):
        streams = [pass1(i), pass2(i - 1)]
        while streams:
            streams = [g for g in streams if next(g, StopIteration) is not StopIteration]
    for _ in pass2(len(qt_ops) - 1):
        pass
    return outs


def _per_query_tile(nq, body):
    qi = pl.program_id(2)
    for n in range(nq):
        pl.when(qi == n)(functools.partial(body, n))


def _half_rows(qt, half):
    zero = jnp.zeros((HEAD_DIM, qt.shape[1]), qt.dtype)
    if half == 0:
        return jnp.concatenate([qt[:HEAD_DIM], zero], axis=0)
    return jnp.concatenate([zero, qt[HEAD_DIM:]], axis=0)


def _scratch_ring(refs):
    return [tuple(refs[2 * r:2 * r + 2]) for r in range(ATT_RING)]


def _moba_prompt_kernel(qt_ref, k_ref, vt_ref, km_ref, bias_ref, o_ref, *scratch, nq):
    t = ATT_TILE
    tq = Q_TILE
    tile_shift = t.bit_length() - 1
    nb = km_ref.shape[0]
    qi = pl.program_id(2)
    blk = lax.broadcasted_iota(jnp.int32, (nb, tq), 0)
    own = (tq // t) * qi + (lax.broadcasted_iota(jnp.int32, (nb, tq), 1) >> tile_shift)

    qt_ops = []
    for pair in range(ATT_PAIRS):
        rows = slice(pair * LANES, (pair + 1) * LANES)
        qt = qt_ref[0, rows, :]
        km = km_ref[:, 0, rows]
        km_hi = km.astype(BF16)
        km_lo = (km - km_hi.astype(F32)).astype(BF16)
        for e in range(2):
            qet = _half_rows(qt, e)
            gt = _dot(km_hi, qet) + _dot(km_lo, qet)
            rank = jnp.zeros((nb, tq), jnp.int32)
            for m in range(nb):
                gm = gt[m:m + 1, :]
                ahead = (gm > gt) | ((gm == gt) & (m < blk))
                rank = rank + jnp.where(ahead & (m < own), 1, 0)
            keep = (blk < own) & (rank < MOBA_TOPK) & (jnp.abs(gt) < jnp.inf)
            pen = jnp.where(keep | (blk == own), 0.0, NEG_BIG)
            pen = jnp.concatenate([pen, jnp.zeros((LANES - nb, tq), F32)], axis=0)
            qt_ops.append(jnp.concatenate([qet, pen.astype(BF16)], axis=0))

    k_lane = lax.broadcasted_iota(jnp.int32, (tq, LANES), 1)
    k_blk = lax.broadcasted_iota(jnp.int32, (tq, LANES), 0) >> tile_shift

    def k_tile(i, jj):
        pair = i // 2
        onehot = jnp.where(k_lane == (tq // t) * jj + k_blk, 1.0, 0.0).astype(BF16)
        return jnp.concatenate(
            [k_ref[jj * tq:(jj + 1) * tq, pair * LANES:(pair + 1) * LANES], onehot], axis=1)

    n_ops = 2 * ATT_PAIRS
    vt_rows = [slice((i // 2) * LANES, (i // 2 + 1) * LANES) for i in range(n_ops)]

    def body(n):
        res = _attend_static(qt_ops, k_tile, vt_ref, vt_rows, bias_ref, tuple(range(n_ops)),
                             _scratch_ring(scratch), n)
        for pair in range(ATT_PAIRS):
            (a0, l0), (a1, l1) = res[2 * pair], res[2 * pair + 1]
            ot = jnp.concatenate([a0[:HEAD_DIM] / l0, a1[HEAD_DIM:] / l1], axis=0)
            o_ref[:, pair * LANES:(pair + 1) * LANES] = ot.T.astype(o_ref.dtype)

    _per_query_tile(nq, body)


def _attn_scratch(nq):
    one = [pltpu.VMEM((nq, Q_TILE, Q_TILE), F32),
           pltpu.VMEM((SUBLANES, Q_TILE), F32)]
    return one * ATT_RING


def _attn_specs(w, seq, nq):
    return ([pl.BlockSpec((1, w, Q_TILE), lambda b, g, qi: (b, g, qi)),
             pl.BlockSpec((seq, w), lambda b, g, qi: (b, g)),
             pl.BlockSpec((1, w, seq), lambda b, g, qi: (b, g, 0))],
            pl.BlockSpec((Q_TILE, w), lambda b, g, qi: (b * nq + qi, g)))


def _moba_prompt(qa_t, ka, va_t, km, bias_tiles, batch, seq):
    nq = seq // Q_TILE
    nb = seq // MOBA_BLOCK
    w = ATT_PAIRS * LANES
    assert WIDTH_A % w == 0
    in_specs, out_spec = _attn_specs(w, seq, nq)
    return pl.pallas_call(
        functools.partial(_moba_prompt_kernel, nq=nq),
        grid=(batch, WIDTH_A // w, nq),
        in_specs=in_specs + [pl.BlockSpec((nb, 1, w), lambda b, g, qi: (b, 0, g)),
                             pl.BlockSpec((2 * ATT_PAIRS, 2, ATT_TILE, ATT_TILE),
                                          lambda b, g, qi: (g, 0, 0, 0))],
        out_specs=out_spec,
        out_shape=jax.ShapeDtypeStruct((batch * seq, WIDTH_A), BF16),
        scratch_shapes=_attn_scratch(nq),
        compiler_params=pltpu.CompilerParams(
            dimension_semantics=("arbitrary", "arbitrary", "arbitrary"),
            vmem_limit_bytes=VMEM_LIMIT_BYTES),
        name="moba_prompt",
    )(qa_t, ka, va_t, km, bias_tiles)


def _diff_prompt_kernel(qt_ref, k_ref, vt_ref, bias_ref, lam_ref, gain_ref, o_ref, *scratch,
                        lam_init, nq):
    tq = Q_TILE
    qt_ops = []
    for head in range(ATT_PAIRS):
        qt = qt_ref[0, head * LANES:(head + 1) * LANES, :]
        qt_ops += [_half_rows(qt, c) for c in range(2)]

    def k_tile(i, jj):
        head = i // 2
        return k_ref[jj * tq:(jj + 1) * tq, head * LANES:(head + 1) * LANES]

    n_ops = 2 * ATT_PAIRS
    vt_rows = [slice((i // 2) * LANES, (i // 2 + 1) * LANES) for i in range(n_ops)]

    def body(n):
        res = _attend_static(qt_ops, k_tile, vt_ref, vt_rows, bias_ref,
                             tuple(i // 2 for i in range(n_ops)), _scratch_ring(scratch), n)
        lam = _lambda(lam_ref, lam_init)
        for head in range(ATT_PAIRS):
            (a0, l0), (a1, l1) = res[2 * head], res[2 * head + 1]
            o = (a0 / l0 - lam * (a1 / l1)).T
            o_ref[:, head * LANES:(head + 1) * LANES] = (
                _rmsnorm(o, gain_ref[...]) * (1.0 - lam_init)).astype(o_ref.dtype)

    _per_query_tile(nq, body)


def _diff_prompt(qd_t, kd, vd_t, bias_tiles, lam_vecs, gain, lam_init, batch, seq):
    nq = seq // Q_TILE
    w = ATT_PAIRS * LANES
    assert WIDTH_B % w == 0
    in_specs, out_spec = _attn_specs(w, seq, nq)
    return pl.pallas_call(
        functools.partial(_diff_prompt_kernel, lam_init=lam_init, nq=nq),
        grid=(batch, WIDTH_B // w, nq),
        in_specs=in_specs + [
                  pl.BlockSpec((ATT_PAIRS, 2, ATT_TILE, ATT_TILE), lambda b, g, qi: (g, 0, 0, 0)),
                  pl.BlockSpec((4, HEAD_DIM), lambda b, g, qi: (0, 0)),
                  pl.BlockSpec((1, 2 * HEAD_DIM), lambda b, g, qi: (0, 0))],
        out_specs=out_spec,
        out_shape=jax.ShapeDtypeStruct((batch * seq, WIDTH_B), BF16),
        scratch_shapes=_attn_scratch(nq),
        compiler_params=pltpu.CompilerParams(
            dimension_semantics=("arbitrary", "arbitrary", "arbitrary"),
            vmem_limit_bytes=VMEM_LIMIT_BYTES),
        name="diff_prompt",
    )(qd_t, kd, vd_t, bias_tiles, lam_vecs, gain)


def _toeplitz(w, t):
    h = w.shape[0]
    m = jnp.broadcast_to(w[:, None, :], (h, t, 2 * t)).reshape(h, 2 * t * t)
    return m[:, :t * (2 * t - 1)].reshape(h, t, 2 * t - 1)[:, :, :t]


def _bias_tiles(bias_t):
    t = ATT_TILE
    assert t + 1 >= MAX_DISTANCE
    k = jnp.arange(2 * t)
    d = jnp.where(k < t, k, k - 2 * t)
    far = _bias_of(bias_t, jnp.full((1,), 2 * t))
    diag = _toeplitz(jnp.where(d >= 0, (_bias_of(bias_t, d) - far) * LOG2E, -jnp.inf), t)
    sub = _toeplitz((_bias_of(bias_t, d + t) - far) * LOG2E, t)
    return jnp.stack([diag, sub], axis=1)


def _page_stream(pt_ref, pk_ref, pv_ref, buf_ref, sem_ref, n_seq, n_chunks):
    pps = PAGES_PER_STEP
    ahead = RING_SLOTS - 1
    b, ph, j = pl.program_id(0), pl.program_id(1), pl.program_id(2)
    step = (b * 2 + ph) * n_chunks + j
    total = n_seq * 2 * n_chunks

    def copies(pool_ref, bb, jj, slot):
        return [pltpu.make_async_copy(pool_ref.at[pt_ref[bb, jj * pps + i]],
                                      buf_ref.at[slot, i], sem_ref.at[slot])
                for i in range(pps)]

    def start(chunk):
        jj, seq_phase = chunk % n_chunks, chunk // n_chunks
        for phase, pool_ref in ((0, pk_ref), (1, pv_ref)):
            @pl.when(seq_phase % 2 == phase)
            def _(pool_ref=pool_ref):
                for i, cp in enumerate(copies(pool_ref, seq_phase // 2, jj, chunk % RING_SLOTS)):
                    cp.start(priority=i % 2)

    @pl.when(step == 0)
    def _():
        for chunk in range(min(ahead, total)):
            start(jnp.int32(chunk))

    @pl.when(step + ahead < total)
    def _():
        start(step + ahead)

    slot = step % RING_SLOTS

    def wait(pool_ref):
        for cp in copies(pool_ref, b, j, slot):
            cp.wait()

    return slot, wait


def _moba_sample_kernel(pt_ref, pk_ref, pv_ref, q_ref, kn_ref, vn_ref, bias_ref, o_ref,
                        s_ref, snew_ref, acc_ref, l_ref, m_ref, bmax_ref, gate_ref, sel_ref,
                        buf_ref, sem_ref, *, n_seq, n_chunks, tokens):
    pps = PAGES_PER_STEP
    slot, wait = _page_stream(pt_ref, pk_ref, pv_ref, buf_ref, sem_ref, n_seq, n_chunks)
    rows = q_ref.shape[1]
    groups = rows // tokens
    ppb = MOBA_BLOCK // PAGE_SIZE
    bps = pps // ppb
    n_blocks = n_chunks * bps
    ph = pl.program_id(1)
    j = pl.program_id(2)
    lane = lax.broadcasted_iota(jnp.int32, (rows, LANES), 1)
    lanef = lane.astype(F32)
    q = q_ref[0]

    @pl.when((ph == 0) & (j == 0))
    def _():
        snew_ref[...] = _dot_nt(q, kn_ref[0]) + bias_ref[2]
        bmax_ref[...] = jnp.full((rows, LANES), -jnp.inf, F32)
        gate_ref[...] = jnp.zeros((rows, LANES), F32)

    @pl.when(ph == 0)
    def _():
        wait(pk_ref)
        is_last = j == n_chunks - 1
        bmax = bmax_ref[...]
        gate = gate_ref[...]
        for b in range(bps):
            smax = ssum = None
            for pg in range(ppb):
                i = b * ppb + pg
                raw = _dot(q, buf_ref[slot, i].astype(BF16))
                bias = bias_ref[0]
                if i == pps - 1:
                    bias = jnp.where(is_last, bias_ref[1], bias)
                s = raw + bias
                s_ref[j, :, i * PAGE_SIZE:(i + 1) * PAGE_SIZE] = s
                smax = s if smax is None else jnp.maximum(smax, s)
                ssum = raw if ssum is None else ssum + raw
            blk = j * bps + b
            bmax = jnp.where(lane == blk, jnp.max(smax, axis=1, keepdims=True), bmax)
            gate = jnp.where(lane == blk, jnp.sum(ssum, axis=1, keepdims=True), gate)
        bmax_ref[...] = bmax
        gate_ref[...] = gate

    @pl.when((ph == 1) & (j == 0))
    def _():
        g = jnp.where(lane < n_blocks, gate_ref[...], -jnp.inf)
        sel = jnp.zeros((rows, LANES), F32)
        for _ in range(MOBA_TOPK):
            top = jnp.max(g, axis=1, keepdims=True)
            first = jnp.min(jnp.where(g == top, lanef, float(LANES)), axis=1, keepdims=True)
            pick = lanef == first
            finite = jnp.where(jnp.abs(top) < jnp.inf, 1.0, 0.0)
            sel = jnp.maximum(sel, jnp.where(pick, finite, 0.0))
            g = jnp.where(pick, -jnp.inf, g)
        sel_ref[...] = sel
        m_sel = jnp.max(jnp.where(sel > 0.5, bmax_ref[...], -jnp.inf), axis=1, keepdims=True)
        m_new = jnp.max(snew_ref[...], axis=1, keepdims=True)
        m_ref[...] = jnp.broadcast_to(jnp.maximum(m_sel, m_new), (rows, LANES))
        acc_ref[...] = jnp.zeros(acc_ref.shape, F32)
        l_ref[...] = jnp.zeros(l_ref.shape, F32)

    @pl.when(ph == 1)
    def _():
        wait(pv_ref)
        m = m_ref[...]
        selv = sel_ref[...]
        acc = acc_ref[...]
        l = l_ref[...]
        for i in range(pps):
            blk = j * bps + i // ppb
            on = jnp.max(jnp.where(lane == blk, selv, 0.0), axis=1, keepdims=True)
            on = jnp.broadcast_to(on, (rows, LANES)) > 0.5
            s = s_ref[j, :, i * PAGE_SIZE:(i + 1) * PAGE_SIZE]
            p = jnp.exp(jnp.where(on, s - m, -jnp.inf))
            l = l + p
            acc = acc + _dot_nt(p.astype(BF16), buf_ref[slot, i].astype(BF16))
        acc_ref[...] = acc
        l_ref[...] = l

    @pl.when((ph == 1) & (j == n_chunks - 1))
    def _():
        p = jnp.exp(snew_ref[...] - m_ref[...])
        acc = acc_ref[...] + _dot(p.astype(BF16), vn_ref[0])
        l = jnp.sum(l_ref[...] + p, axis=1, keepdims=True)
        o = acc / l
        grp = lax.broadcasted_iota(jnp.int32, (groups, WIDTH_A), 0)
        col = lax.broadcasted_iota(jnp.int32, (groups, WIDTH_A), 1)
        head_shift = HEAD_DIM.bit_length() - 1
        w = jnp.where((col >> head_shift) == grp, 1.0, 0.0)
        tok = jnp.sum(o.reshape(tokens, groups, WIDTH_A) * w[None], axis=1)
        o_ref[0] = tok.astype(o_ref.dtype)


def _moba_sample(page_table, pool_kt, pool_vt, q_bd, k_new, v_new, bias_rows, *, tokens):
    pps = PAGES_PER_STEP
    n_seq, n_pages = page_table.shape
    n_chunks = n_pages // pps
    rows = q_bd.shape[1]
    seq_map = lambda b, ph, j, pt: (b, 0, 0)
    hbm = pl.BlockSpec(memory_space=pl.ANY)
    in_specs = [hbm, hbm,
                pl.BlockSpec((1, rows, WIDTH_A), seq_map),
                pl.BlockSpec((1, LANES, WIDTH_A), seq_map),
                pl.BlockSpec((1, LANES, WIDTH_A), seq_map),
                pl.BlockSpec((3, rows, LANES), lambda b, ph, j, pt: (0, 0, 0))]
    vec = pltpu.VMEM((rows, LANES), F32)
    grid_spec = pltpu.PrefetchScalarGridSpec(
        num_scalar_prefetch=1,
        grid=(n_seq, 2, n_chunks),
        in_specs=in_specs,
        out_specs=pl.BlockSpec((1, tokens, WIDTH_A), seq_map),
        scratch_shapes=[pltpu.VMEM((n_chunks, rows, pps * PAGE_SIZE), F32),
                        vec,
                        pltpu.VMEM((rows, WIDTH_A), F32),
                        vec, vec, vec, vec, vec,
                        pltpu.VMEM((RING_SLOTS, pps) + pool_kt.shape[1:], F32),
                        pltpu.SemaphoreType.DMA((RING_SLOTS,))])
    return pl.pallas_call(
        functools.partial(_moba_sample_kernel, n_seq=n_seq, n_chunks=n_chunks, tokens=tokens),
        grid_spec=grid_spec,
        out_shape=jax.ShapeDtypeStruct((n_seq, tokens, WIDTH_A), BF16),
        compiler_params=pltpu.CompilerParams(
            dimension_semantics=("arbitrary", "arbitrary", "arbitrary"),
            vmem_limit_bytes=VMEM_LIMIT_BYTES),
        name="moba_sample",
    )(page_table, pool_kt, pool_vt, q_bd, k_new, v_new, bias_rows)


def _diff_sample_kernel(pt_ref, pk_ref, pv_ref, q_ref, kn_ref, vn_ref, bias_ref, bnew_ref,
                        lam_ref, gain_ref, o_ref, s_ref, snew_ref, acc_ref, l_ref, m_ref,
                        buf_ref, sem_ref, *, n_seq, n_chunks, tokens, lam_init):
    pps = PAGES_PER_STEP
    slot, wait = _page_stream(pt_ref, pk_ref, pv_ref, buf_ref, sem_ref, n_seq, n_chunks)
    rows = q_ref.shape[1]
    pw = buf_ref.shape[2]
    ph = pl.program_id(1)
    j = pl.program_id(2)
    q = q_ref[0]

    @pl.when((ph == 0) & (j == 0))
    def _():
        snew_ref[...] = _dot_nt(q, kn_ref[0]) + bnew_ref[...]
        m_ref[...] = jnp.full((rows, LANES), -jnp.inf, F32)

    @pl.when(ph == 0)
    def _():
        wait(pk_ref)
        is_last = j == n_chunks - 1
        mx = m_ref[...]
        for i in range(pps):
            bias = bias_ref[0]
            if i == pps - 1:
                bias = jnp.where(is_last, bias_ref[1], bias)
            s = _dot_nt(q, buf_ref[slot, i].astype(BF16)) + bias
            s_ref[j, :, i * pw:(i + 1) * pw] = s
            mx = jnp.maximum(mx, _lane_fold(jnp.maximum, s))
        m_ref[...] = mx

    @pl.when((ph == 1) & (j == 0))
    def _():
        m = jnp.maximum(jnp.max(m_ref[...], axis=1, keepdims=True),
                        jnp.max(snew_ref[...], axis=1, keepdims=True))
        m_ref[...] = jnp.broadcast_to(m, (rows, LANES))
        acc_ref[...] = jnp.zeros(acc_ref.shape, F32)
        l_ref[...] = jnp.zeros(l_ref.shape, F32)

    @pl.when(ph == 1)
    def _():
        wait(pv_ref)
        m = m_ref[...]
        mw = jnp.concatenate([m] * (pw // LANES), axis=1)
        acc = acc_ref[...]
        l = l_ref[...]
        for i in range(pps):
            p = jnp.exp(s_ref[j, :, i * pw:(i + 1) * pw] - mw)
            l = l + _lane_fold(jnp.add, p)
            acc = acc + _dot(p.astype(BF16), buf_ref[slot, i].astype(BF16))
        acc_ref[...] = acc
        l_ref[...] = l

    @pl.when((ph == 1) & (j == n_chunks - 1))
    def _():
        p = jnp.exp(snew_ref[...] - m_ref[...])
        acc = acc_ref[...] + _dot(p.astype(BF16), vn_ref[0])
        l = jnp.sum(l_ref[...] + p, axis=1, keepdims=True)
        o = (acc / l).reshape(tokens, rows // tokens, LANES)
        lam = _lambda(lam_ref, lam_init)
        sub = lax.broadcasted_iota(jnp.int32, (rows // tokens, LANES), 0)
        gain = gain_ref[...]
        parts = []
        for h in range(N_HEADS_B):
            w = jnp.where(sub == 2 * h, 1.0, jnp.where(sub == 2 * h + 1, -lam, 0.0))
            parts.append(_rmsnorm(jnp.sum(o * w[None], axis=1), gain) * (1.0 - lam_init))
        o_ref[0] = jnp.concatenate(parts, axis=1).astype(o_ref.dtype)


def _diff_sample(page_table, pool_k, pool_v, q2, k_new, v_new, bias_rows, bias_new, lam_vecs, gain,
                 *, tokens, lam_init):
    pps = PAGES_PER_STEP
    n_seq, n_pages = page_table.shape
    n_chunks = n_pages // pps
    rows = q2.shape[1]
    pw = pool_k.shape[1]
    seq_map = lambda b, ph, j, pt: (b, 0, 0)
    const2 = lambda b, ph, j, pt: (0, 0)
    hbm = pl.BlockSpec(memory_space=pl.ANY)
    in_specs = [hbm, hbm,
                pl.BlockSpec((1, rows, LANES), seq_map),
                pl.BlockSpec((1, LANES, LANES), seq_map),
                pl.BlockSpec((1, LANES, LANES), seq_map),
                pl.BlockSpec((2, rows, pw), lambda b, ph, j, pt: (0, 0, 0)),
                pl.BlockSpec((rows, LANES), const2),
                pl.BlockSpec((4, HEAD_DIM), const2),
                pl.BlockSpec((1, 2 * HEAD_DIM), const2)]
    vec = pltpu.VMEM((rows, LANES), F32)
    grid_spec = pltpu.PrefetchScalarGridSpec(
        num_scalar_prefetch=1,
        grid=(n_seq, 2, n_chunks),
        in_specs=in_specs,
        out_specs=pl.BlockSpec((1, tokens, WIDTH_B), seq_map),
        scratch_shapes=[pltpu.VMEM((n_chunks, rows, pps * pw), F32),
                        vec, vec, vec, vec,
                        pltpu.VMEM((RING_SLOTS, pps) + pool_k.shape[1:], F32),
                        pltpu.SemaphoreType.DMA((RING_SLOTS,))])
    return pl.pallas_call(
        functools.partial(_diff_sample_kernel, n_seq=n_seq, n_chunks=n_chunks, tokens=tokens,
                          lam_init=lam_init),
        grid_spec=grid_spec,
        out_shape=jax.ShapeDtypeStruct((n_seq, tokens, WIDTH_B), BF16),
        compiler_params=pltpu.CompilerParams(
            dimension_semantics=("arbitrary", "arbitrary", "arbitrary"),
            vmem_limit_bytes=VMEM_LIMIT_BYTES),
        name="diff_sample",
    )(page_table, pool_k, pool_v, q2, k_new, v_new, bias_rows, bias_new, lam_vecs, gain)


def _block_diag(q, n_seq, tokens, groups, keep):
    w = q.shape[1] // groups
    q4 = q.reshape(n_seq, tokens, groups, 1, w)
    eye = (jnp.arange(groups)[:, None] % keep == jnp.arange(keep)[None, :]).astype(q.dtype)
    return (q4 * eye.reshape(1, 1, groups, keep, 1)).reshape(n_seq, tokens * groups, keep * w)


def _pad_rows(x, n_seq):
    x = x.reshape(n_seq, -1, x.shape[-1])
    return jnp.pad(x, ((0, 0), (0, LANES - x.shape[1]), (0, 0)))


def _sample_bias(bias_t, tokens, heads_per_group):
    assert PAGE_SIZE + 1 >= MAX_DISTANCE
    tok = jnp.arange(tokens)[:, None]
    col = jnp.arange(LANES)[None, :]
    far = _bias_of(bias_t, jnp.broadcast_to(2 * PAGE_SIZE, (tokens, LANES)))
    last = _bias_of(bias_t, PAGE_SIZE + tok - col)
    new = jnp.where((col <= tok) & (col < tokens), _bias_of(bias_t, tok - col), -jnp.inf)

    def rows(x):
        x = jnp.repeat(x.transpose(1, 0, 2), heads_per_group, axis=1)
        return x.reshape(-1, LANES)

    return rows(far), rows(last), rows(new)


def _spread_heads(x, n_heads, groups):
    r, k = x.shape
    row_head = (jnp.arange(r) % groups) // (groups // n_heads)
    own = row_head[:, None, None] == jnp.arange(n_heads)[None, None, :]
    return jnp.where(own, x[:, :, None], -jnp.inf).reshape(r, k * n_heads)


def _merge_kernel(x_ref, oa_ref, ob_ref, g_ref, wg_ref, woa_ref, wob_ref, wo_ref, o_ref):
    x = x_ref[...]
    h = _rmsnorm(x, g_ref[...]).astype(BF16)
    ga = 1.0 / (1.0 + jnp.exp(-_dot(h, wg_ref[:, :D_MODEL].astype(BF16))))
    m = ga * _dot(oa_ref[...], woa_ref[...].astype(BF16))
    gb = 1.0 / (1.0 + jnp.exp(-_dot(h, wg_ref[:, D_MODEL:].astype(BF16))))
    m = m + gb * _dot(ob_ref[...], wob_ref[...].astype(BF16))
    o_ref[...] = x + _dot(m.astype(BF16), wo_ref[...].astype(BF16))


def _merge(x, oa, ob, g, wg, woa, wob, wo, tm):
    n = x.shape[0]
    row = lambda i: (i, 0)
    const = lambda i: (0, 0)
    return pl.pallas_call(
        _merge_kernel,
        grid=(n // tm,),
        in_specs=[pl.BlockSpec((tm, D_MODEL), row),
                  pl.BlockSpec((tm, WIDTH_A), row),
                  pl.BlockSpec((tm, WIDTH_B), row),
                  pl.BlockSpec((1, D_MODEL), const),
                  pl.BlockSpec((D_MODEL, 2 * D_MODEL), const, pipeline_mode=pl.Buffered(1)),
                  pl.BlockSpec((WIDTH_A, D_MODEL), const, pipeline_mode=pl.Buffered(1)),
                  pl.BlockSpec((WIDTH_B, D_MODEL), const, pipeline_mode=pl.Buffered(1)),
                  pl.BlockSpec((D_MODEL, D_MODEL), const, pipeline_mode=pl.Buffered(1))],
        out_specs=pl.BlockSpec((tm, D_MODEL), row),
        out_shape=jax.ShapeDtypeStruct((n, D_MODEL), F32),
        compiler_params=pltpu.CompilerParams(
            dimension_semantics=("arbitrary",), vmem_limit_bytes=VMEM_LIMIT_BYTES),
        name="merge",
    )(x, oa, ob, g, wg, woa, wob, wo)


def _ffn_kernel(*refs, tm, seq_len, has_prev):
    if has_prev:
        (x_ref, g_ref, wup_ref, cw_ref, cb_ref, wdn_ref, gfin_ref, e1_ref, e2_ref,
         y_ref, u_ref, act_ref) = refs
    else:
        (x_ref, g_ref, wup_ref, cw_ref, cb_ref, wdn_ref, gfin_ref,
         y_ref, u_ref, act_ref, carry_ref) = refs
    i = pl.program_id(0)
    x = x_ref[...]
    h = _rmsnorm(x, g_ref[...]).astype(BF16)
    row = lax.broadcasted_iota(jnp.int32, (tm, FF_CHUNK), 0)
    row8 = lax.broadcasted_iota(jnp.int32, (SUBLANES, FF_CHUNK), 0)
    if has_prev:
        pos = row & (seq_len - 1)
    else:
        @pl.when((i * tm) % seq_len == 0)
        def _():
            carry_ref[...] = jnp.zeros(carry_ref.shape, F32)

    for c in range(D_FF // FF_CHUNK):
        halves = []
        for part in range(2):
            cols = slice(part * D_FF + c * FF_CHUNK, part * D_FF + (c + 1) * FF_CHUNK)
            u = _dot(h, wup_ref[:, cols])
            um1 = pltpu.roll(u, 1, 0)
            um2 = pltpu.roll(u, 2, 0)
            if has_prev:
                um1 = jnp.where(pos == 0, e1_ref[:, cols], um1)
                um2 = jnp.where(pos < 2, e2_ref[:, cols], um2)
                u_ref[:, cols] = u
            else:
                prev = carry_ref[:, cols]
                top1 = jnp.where(row8 == 0, pltpu.roll(prev, 1, 0), um1[:SUBLANES])
                top2 = jnp.where(row8 < 2, pltpu.roll(prev, 2, 0), um2[:SUBLANES])
                um1 = jnp.concatenate([top1, um1[SUBLANES:]], axis=0)
                um2 = jnp.concatenate([top2, um2[SUBLANES:]], axis=0)
                carry_ref[:, cols] = u[tm - SUBLANES:]
                u_ref[:, cols] = u[tm - SUBLANES:]
            cw = cw_ref[:, cols]
            halves.append(((cb_ref[:, cols] + cw[0:1] * um2) + cw[1:2] * um1) + cw[2:3] * u)
        gate, val = halves
        act = (gate * (1.0 / (1.0 + jnp.exp(-gate)))) * val
        act_ref[:, c * FF_CHUNK:(c + 1) * FF_CHUNK] = act.astype(BF16)

    x3 = x + _dot(act_ref[...], wdn_ref[...])
    y_ref[...] = _rmsnorm(x3, gfin_ref[...])


def _ffn(x, g, wup, cw, cb, wdn, gfin, prev, tm, seq_len):
    n = x.shape[0]
    has_prev = prev is not None
    row = lambda i: (i, 0)
    const = lambda i: (0, 0)
    in_specs = [pl.BlockSpec((tm, D_MODEL), row),
                pl.BlockSpec((1, D_MODEL), const),
                pl.BlockSpec((D_MODEL, 2 * D_FF), const, pipeline_mode=pl.Buffered(1)),
                pl.BlockSpec((CONV_W, 2 * D_FF), const),
                pl.BlockSpec((1, 2 * D_FF), const),
                pl.BlockSpec((D_FF, D_MODEL), const, pipeline_mode=pl.Buffered(1)),
                pl.BlockSpec((1, D_MODEL), const)]
    scratch = [pltpu.VMEM((tm, D_FF), BF16)]
    if has_prev:
        assert n == tm and tm % seq_len == 0 and seq_len & (seq_len - 1) == 0
        in_specs += [pl.BlockSpec((tm, 2 * D_FF), row)] * 2
        u_shape, u_spec = (n, 2 * D_FF), pl.BlockSpec((tm, 2 * D_FF), row)
        args = (x, g, wup, cw, cb, wdn, gfin) + tuple(prev)
    else:
        assert seq_len % tm == 0
        tiles_per_seq = seq_len // tm
        u_shape = (n // seq_len * SUBLANES, 2 * D_FF)
        u_spec = pl.BlockSpec((SUBLANES, 2 * D_FF), lambda i: (i // tiles_per_seq, 0))
        scratch.append(pltpu.VMEM((SUBLANES, 2 * D_FF), F32))
        args = (x, g, wup, cw, cb, wdn, gfin)
    return pl.pallas_call(
        functools.partial(_ffn_kernel, tm=tm, seq_len=seq_len, has_prev=has_prev),
        grid=(n // tm,),
        in_specs=in_specs,
        out_specs=[pl.BlockSpec((tm, D_MODEL), row), u_spec],
        out_shape=[jax.ShapeDtypeStruct((n, D_MODEL), F32), jax.ShapeDtypeStruct(u_shape, F32)],
        scratch_shapes=scratch,
        compiler_params=pltpu.CompilerParams(
            dimension_semantics=("arbitrary",), vmem_limit_bytes=VMEM_LIMIT_BYTES),
        name="ffn_sample" if has_prev else "ffn_prompt",
    )(*args)


def kernel(x_prompt, x_sample, cache_moba_k, cache_moba_v, cache_diff_k, cache_diff_v, state_conv, page_table, rel_bias, norm_attn, w_in, w_gate, w_out_a, w_out_b, w_out, lambda_q1, lambda_k1, lambda_q2, lambda_k2, diff_norm, norm_ffn, w_up, conv_w, conv_b, w_down, norm_final):
    batch, seq, _ = x_prompt.shape
    n_seq, tokens, _ = x_sample.shape
    depth = w_in.shape[0]
    n_phys = cache_moba_k.shape[1]
    n_pages = page_table.shape[1]
    assert depth == 1 and seq % Q_TILE == 0 and ATT_TILE & (ATT_TILE - 1) == 0
    assert (n_pages * PAGE_SIZE) % MOBA_BLOCK == 0 and n_pages % PAGES_PER_STEP == 0
    assert n_pages * PAGE_SIZE // MOBA_BLOCK <= LANES and CONV_W - 1 <= tokens <= LANES // N_HEADS_B
    l = 0
    lam_init = 0.8 - 0.6 * math.exp(-0.3 * l)

    bias_a = rel_bias[:, :N_HEADS_A].T
    bias_d = rel_bias[:, N_HEADS_A:].T
    row = lambda v: v.reshape(1, -1)
    w_in_b, w_gate_b, w_oa_b, w_ob_b, w_o_b = w_in[l], w_gate[l], w_out_a[l], w_out_b[l], w_out[l]
    w_up_b = w_up[l].astype(BF16)
    w_dn_b = w_down[l].astype(BF16)
    lam_vecs = jnp.stack([lambda_q1[l], lambda_k1[l], lambda_q2[l], lambda_k2[l]]).astype(F32)
    gain_d = row(diff_norm[l])
    g_attn, g_ffn, g_fin = row(norm_attn[l]), row(norm_ffn[l]), row(norm_final)
    cb = row(conv_b[l])

    xp = x_prompt.reshape(batch * seq, D_MODEL)
    qa_tb, ka, va_tb, qd_tb, kd, vd_tb, ka_t, va_t, kd_r, vd_r, km = _proj(xp, g_attn, w_in_b, 512, seq)
    oa = _moba_prompt(qa_tb, ka, va_tb, km, _bias_tiles(bias_a), batch, seq)
    ob = _diff_prompt(qd_tb, kd, vd_tb, _bias_tiles(bias_d), lam_vecs, gain_d, lam_init, batch, seq)
    x2 = _merge(xp, oa, ob, g_attn, w_gate_b, w_oa_b, w_ob_b, w_o_b, 512)
    yp, tail_p = _ffn(x2, g_ffn, w_up_b, conv_w[l], cb, w_dn_b, g_fin, None, 1024, seq)
    conv_p = tail_p.reshape(batch, SUBLANES, 2 * D_FF)[:, SUBLANES - (CONV_W - 1):]

    n_s = n_seq * tokens
    xs = x_sample.reshape(n_s, D_MODEL)
    qa_s, ka_s, va_s, qd_s, kd_s, vd_s, ka_sf, va_sf, kd_sf, vd_sf = _proj(xs, g_attn, w_in_b, n_s)

    pool_t = lambda c: jnp.transpose(c[l], (0, 2, 3, 1)).reshape(n_phys, WIDTH_A, PAGE_SIZE)
    pool_r = lambda c: c[l].reshape(n_phys, PAGE_SIZE * N_HEADS_B, 2 * HEAD_DIM)
    far_a, last_a, new_a = _sample_bias(bias_a, tokens, 1)
    oa_s = _moba_sample(page_table, pool_t(cache_moba_k), pool_t(cache_moba_v),
                        _block_diag(qa_s, n_seq, tokens, N_HEADS_A, N_HEADS_A),
                        _pad_rows(ka_s, n_seq), _pad_rows(va_s, n_seq),
                        jnp.stack([far_a, last_a, new_a]), tokens=tokens)
    far_d, last_d, new_d = _sample_bias(bias_d, tokens, 2)
    groups_d = 2 * N_HEADS_B
    spread = lambda x: _spread_heads(x, N_HEADS_B, groups_d)
    per_head = lambda x: x.reshape(n_s * N_HEADS_B, 2 * HEAD_DIM)
    oa_d = _diff_sample(page_table, pool_r(cache_diff_k), pool_r(cache_diff_v),
                        _block_diag(qd_s, n_seq, tokens, groups_d, 2),
                        _pad_rows(per_head(kd_s), n_seq), _pad_rows(per_head(vd_s), n_seq),
                        jnp.stack([spread(far_d), spread(last_d)]),
                        spread(new_d[:, :LANES // N_HEADS_B]),
                        lam_vecs, gain_d, tokens=tokens, lam_init=lam_init)
    x2s = _merge(xs, oa_s.reshape(n_s, WIDTH_A), oa_d.reshape(n_s, WIDTH_B), g_attn,
                 w_gate_b, w_oa_b, w_ob_b, w_o_b, n_s)
    st = state_conv[l]
    zero = jnp.zeros((n_seq, 1, 2 * D_FF), F32)
    e1 = jnp.concatenate([st[:, 1:2]] + [zero] * (tokens - 1), axis=1).reshape(n_s, 2 * D_FF)
    e2 = jnp.concatenate([st[:, 0:2]] + [zero] * (tokens - 2), axis=1).reshape(n_s, 2 * D_FF)
    ys, u_s = _ffn(x2s, g_ffn, w_up_b, conv_w[l], cb, w_dn_b, g_fin, (e1, e2), n_s, tokens)
    conv_s = u_s.reshape(n_seq, tokens, 2 * D_FF)[:, tokens - (CONV_W - 1):]

    shp_a = lambda a, b_, t_: a.reshape(1, b_, t_, N_HEADS_A, HEAD_DIM)
    shp_d = lambda a, b_, t_: a.reshape(1, b_, t_, N_HEADS_B, 2 * HEAD_DIM)
    untr = lambda a: a.reshape(batch, N_HEADS_A, HEAD_DIM, seq).transpose(0, 3, 1, 2)[None]
    return (yp.reshape(batch, seq, D_MODEL), ys.reshape(n_seq, tokens, D_MODEL),
            untr(ka_t), untr(va_t),
            shp_d(kd_r, batch, seq), shp_d(vd_r, batch, seq), conv_p[None],
            shp_a(ka_sf, n_seq, tokens), shp_a(va_sf, n_seq, tokens),
            shp_d(kd_sf, n_seq, tokens), shp_d(vd_sf, n_seq, tokens), conv_s[None])
```

```python
import functools
import math

import jax
import jax.numpy as jnp
from jax import lax
from jax.experimental import pallas as pl
from jax.experimental.pallas import tpu as pltpu

F32 = jnp.float32
BF16 = jnp.bfloat16

D_MODEL = 1024
HEAD_DIM = 64
N_HEADS_A = D_MODEL // 128
N_HEADS_B = D_MODEL // 256
WIDTH_A = N_HEADS_A * HEAD_DIM
WIDTH_B = N_HEADS_B * 2 * HEAD_DIM
N_IN = 3 * WIDTH_A + 3 * WIDTH_B
MOBA_BLOCK = 256
MOBA_TOPK = 3
NUM_BUCKETS = 32
MAX_DISTANCE = 128
D_FF = ((8 * D_MODEL // 3 + 127) // 128) * 128
CONV_W = 3
EPS = 1e-6
PAGE_SIZE = 128
SCALE = HEAD_DIM ** -0.5
LOG2E = math.log2(math.e)

LANES = 128
SUBLANES = 8
VMEM_LIMIT_BYTES = 56 * 1024 * 1024

ATT_TILE = MOBA_BLOCK
Q_TILE = 2 * ATT_TILE
ATT_PAIRS = 2
ATT_RING = 2
FF_CHUNK = 256
PAGES_PER_STEP = 16
RING_SLOTS = 4
NEG_BIG = -1e30


def _dot(a, b):
    return jnp.dot(a, b, preferred_element_type=F32)


def _dot_nt(a, b):
    return lax.dot_general(a, b, (((1,), (1,)), ((), ())), preferred_element_type=F32)


def _rmsnorm(x, g):
    return (x * lax.rsqrt(jnp.mean(x * x, axis=-1, keepdims=True) + EPS)) * g


def _lane_fold(op, x):
    out = x[:, :LANES]
    for c in range(1, x.shape[1] // LANES):
        out = op(out, x[:, c * LANES:(c + 1) * LANES])
    return out


def _t5_bucket(dist):
    n = jnp.maximum(dist, 0)
    max_exact = NUM_BUCKETS // 2
    nf = jnp.maximum(n, 1).astype(F32)
    large = max_exact + (jnp.log(nf / max_exact) / math.log(MAX_DISTANCE / max_exact)
                         * (NUM_BUCKETS - max_exact)).astype(jnp.int32)
    large = jnp.minimum(large, NUM_BUCKETS - 1)
    return jnp.where(n < max_exact, n, large)


def _bias_of(bias_t, dist):
    onehot = jax.nn.one_hot(_t5_bucket(dist), NUM_BUCKETS, dtype=F32)
    return jnp.einsum('hb,...b->h...', bias_t.astype(F32), onehot, precision=lax.Precision.HIGHEST)


def _lambda(lam_ref, lam_init):
    lv = lam_ref[...]
    a = jnp.sum(lv[0:1] * lv[1:2], axis=-1, keepdims=True)
    b = jnp.sum(lv[2:3] * lv[3:4], axis=-1, keepdims=True)
    return jnp.exp(a) - jnp.exp(b) + lam_init


def _proj_kernel(x_ref, g_ref, w_ref,
                 qa_ref, ka_ref, va_ref, qd_ref, kd_ref, vd_ref,
                 kaf_ref, vaf_ref, kdf_ref, vdf_ref, *km_ref, prompt):
    tm = x_ref.shape[0]
    h = _rmsnorm(x_ref[...], g_ref[...]).astype(BF16)
    outs = ((qa_ref, None, SCALE), (ka_ref, kaf_ref, None), (va_ref, vaf_ref, None),
            (qd_ref, None, SCALE), (kd_ref, kdf_ref, None), (vd_ref, vdf_ref, None))
    for c, (b_ref, f_ref, scale) in enumerate(outs):
        u = _dot(h, w_ref[:, c * WIDTH_A:(c + 1) * WIDTH_A].astype(BF16))
        if scale is not None:
            u = u * (scale * LOG2E if prompt else scale)
        if not prompt:
            if f_ref is not None:
                f_ref[...] = u
            b_ref[...] = u.astype(BF16)
            continue
        is_key = c in (1, 4)
        ut = u.T if c != 4 else None
        if c in (1, 2):
            f_ref[0] = ut
        elif f_ref is not None:
            for hd in range(N_HEADS_B):
                f_ref[pl.ds(hd, tm, stride=N_HEADS_B), :] = u[:, hd * LANES:(hd + 1) * LANES]
        if is_key:
            b_ref[...] = u.astype(BF16)
        else:
            b_ref[0] = ut.astype(BF16)
        if c == 1:
            for j in range(tm // MOBA_BLOCK):
                km_ref[0][j] = jnp.sum(u[j * MOBA_BLOCK:(j + 1) * MOBA_BLOCK], axis=0,
                                       keepdims=True) * (1.0 / MOBA_BLOCK)


def _proj(x, g, w_bf16, tm, seq=None):
    n = x.shape[0]
    row = lambda i: (i, 0)
    const = lambda i: (0, 0)
    blk = pl.BlockSpec((tm, WIDTH_A), row)
    b_specs = [blk] * 6
    b_shapes = [jax.ShapeDtypeStruct((n, WIDTH_A), BF16)] * 6
    f_specs = [blk] * 4
    f_shapes = [jax.ShapeDtypeStruct((n, WIDTH_A), F32)] * 4
    if seq is not None:
        assert seq % tm == 0 and tm % MOBA_BLOCK == 0 and 2 * HEAD_DIM == LANES
        tiles = seq // tm
        t_spec = pl.BlockSpec((1, WIDTH_A, tm), lambda i: (i // tiles, 0, i % tiles))
        r_spec = pl.BlockSpec((tm * N_HEADS_B, LANES), row)
        r_shape = jax.ShapeDtypeStruct((n * N_HEADS_B, LANES), F32)
        t_shape = lambda dt: jax.ShapeDtypeStruct((n // seq, WIDTH_A, seq), dt)
        b_specs = [t_spec, blk, t_spec, t_spec, blk, t_spec]
        b_shapes = [t_shape(BF16), b_shapes[0], t_shape(BF16), t_shape(BF16), b_shapes[0], t_shape(BF16)]
        f_specs = [t_spec, t_spec, r_spec, r_spec,
                   pl.BlockSpec((tm // MOBA_BLOCK, 1, WIDTH_A), lambda i: (i, 0, 0))]
        f_shapes = [t_shape(F32), t_shape(F32), r_shape, r_shape,
                    jax.ShapeDtypeStruct((n // MOBA_BLOCK, 1, WIDTH_A), F32)]
    return pl.pallas_call(
        functools.partial(_proj_kernel, prompt=seq is not None),
        grid=(n // tm,),
        in_specs=[pl.BlockSpec((tm, D_MODEL), row),
                  pl.BlockSpec((1, D_MODEL), const),
                  pl.BlockSpec((D_MODEL, N_IN), const, pipeline_mode=pl.Buffered(1))],
        out_specs=b_specs + f_specs,
        out_shape=b_shapes + f_shapes,
        compiler_params=pltpu.CompilerParams(
            dimension_semantics=("arbitrary",), vmem_limit_bytes=VMEM_LIMIT_BYTES),
        name="proj",
    )(x, g, w_bf16)


def _row_fold(op, x):
    y = x.reshape(x.shape[0] // SUBLANES, SUBLANES, x.shape[1])
    out = y[0]
    for g in range(1, y.shape[0]):
        out = op(out, y[g])
    return out


def _attend_static(qt_ops, k_tile, vt_ref, vt_rows, bias_ref, bidx, scratch, qi):
    t = ATT_TILE
    tq = Q_TILE
    halves = tq // t
    trips = qi + 1
    outs = [None] * len(qt_ops)

    def pass1(i):
        s_ref, mb_ref = scratch[i % len(scratch)]
        mx = [None] * halves
        k_tiles = [k_tile(i, jj) for jj in range(trips)]
        for hq in range(halves):
            cols = slice(hq * t, (hq + 1) * t)
            qt_cols = qt_ops[i][:, cols]
            for jj in range(trips):
                for hk in range(halves):
                    dist = (halves * qi + hq) - (halves * jj + hk)
                    if dist < 0:
                        continue
                    blk = _dot(k_tiles[jj][hk * t:(hk + 1) * t], qt_cols)
                    if dist < 2:
                        blk = blk + bias_ref[bidx[i], dist]
                    s_ref[jj, hk * t:(hk + 1) * t, cols] = blk
                    f = _row_fold(jnp.maximum, blk)
                    mx[hq] = f if mx[hq] is None else jnp.maximum(mx[hq], f)
                    yield
        m = jnp.max(jnp.concatenate(mx, axis=1), axis=0, keepdims=True)
        mb_ref[...] = jnp.broadcast_to(m, (SUBLANES, tq))

    def pass2(i):
        s_ref, mb_ref = scratch[i % len(scratch)]
        accs, sums = [], []
        for hq in range(halves):
            cols = slice(hq * t, (hq + 1) * t)
            mb = mb_ref[:, cols]
            l = jnp.zeros((SUBLANES, t), F32)
            acc = None
            n_keys = (halves * qi + hq + 1) * t
            for kb in range(n_keys // t):
                jj, hk = divmod(kb, halves)
                sv = s_ref[jj, hk * t:(hk + 1) * t, cols].reshape(t // SUBLANES, SUBLANES, t)
                p = jnp.exp2(sv - mb[None]).reshape(t, t)
                l = l + _row_fold(jnp.add, p)
                part = _dot(vt_ref[0, vt_rows[i], kb * t:(kb + 1) * t], p.astype(BF16))
                acc = part if acc is None else acc + part
                yield
            accs.append(acc)
            sums.append(l)
        outs[i] = (jnp.concatenate(accs, axis=1),
                   jnp.sum(jnp.concatenate(sums, axis=1), axis=0, keepdims=True))

    for _ in pass1(0):
        pass
    for i in range(1, len(qt_ops)):
        streams = [pass1(i), pass2(i - 1)]
        while streams:
            streams = [g for g in streams if next(g, StopIteration) is not StopIteration]
    for _ in pass2(len(qt_ops) - 1):
        pass
    return outs


def _per_query_tile(nq, body):
    qi = pl.program_id(2)
    for n in range(nq):
        pl.when(qi == n)(functools.partial(body, n))


def _half_rows(qt, half):
    zero = jnp.zeros((HEAD_DIM, qt.shape[1]), qt.dtype)
    if half == 0:
        return jnp.concatenate([qt[:HEAD_DIM], zero], axis=0)
    return jnp.concatenate([zero, qt[HEAD_DIM:]], axis=0)


def _scratch_ring(refs):
    return [tuple(refs[2 * r:2 * r + 2]) for r in range(ATT_RING)]


def _moba_prompt_kernel(qt_ref, k_ref, vt_ref, km_ref, bias_ref, o_ref, *scratch, nq):
    t = ATT_TILE
    tq = Q_TILE
    tile_shift = t.bit_length() - 1
    nb = km_ref.shape[0]
    qi = pl.program_id(2)
    blk = lax.broadcasted_iota(jnp.int32, (nb, tq), 0)
    own = (tq // t) * qi + (lax.broadcasted_iota(jnp.int32, (nb, tq), 1) >> tile_shift)

    qt_ops = []
    for pair in range(ATT_PAIRS):
        rows = slice(pair * LANES, (pair + 1) * LANES)
        qt = qt_ref[0, rows, :]
        km = km_ref[:, 0, rows]
        km_hi = km.astype(BF16)
        km_lo = (km - km_hi.astype(F32)).astype(BF16)
        for e in range(2):
            qet = _half_rows(qt, e)
            gt = _dot(km_hi, qet) + _dot(km_lo, qet)
            rank = jnp.zeros((nb, tq), jnp.int32)
            for m in range(nb):
                gm = gt[m:m + 1, :]
                ahead = (gm > gt) | ((gm == gt) & (m < blk))
                rank = rank + jnp.where(ahead & (m < own), 1, 0)
            keep = (blk < own) & (rank < MOBA_TOPK) & (jnp.abs(gt) < jnp.inf)
            pen = jnp.where(keep | (blk == own), 0.0, NEG_BIG)
            pen = jnp.concatenate([pen, jnp.zeros((LANES - nb, tq), F32)], axis=0)
            qt_ops.append(jnp.concatenate([qet, pen.astype(BF16)], axis=0))

    k_lane = lax.broadcasted_iota(jnp.int32, (tq, LANES), 1)
    k_blk = lax.broadcasted_iota(jnp.int32, (tq, LANES), 0) >> tile_shift

    def k_tile(i, jj):
        pair = i // 2
        onehot = jnp.where(k_lane == (tq // t) * jj + k_blk, 1.0, 0.0).astype(BF16)
        return jnp.concatenate(
            [k_ref[jj * tq:(jj + 1) * tq, pair * LANES:(pair + 1) * LANES], onehot], axis=1)

    n_ops = 2 * ATT_PAIRS
    vt_rows = [slice(i * HEAD_DIM, (i + 1) * HEAD_DIM) for i in range(n_ops)]

    def body(n):
        res = _attend_static(qt_ops, k_tile, vt_ref, vt_rows, bias_ref, tuple(range(n_ops)),
                             _scratch_ring(scratch), n)
        for pair in range(ATT_PAIRS):
            (a0, l0), (a1, l1) = res[2 * pair], res[2 * pair + 1]
            ot = jnp.concatenate([a0 / l0, a1 / l1], axis=0)
            o_ref[:, pair * LANES:(pair + 1) * LANES] = ot.T.astype(o_ref.dtype)

    _per_query_tile(nq, body)


def _attn_scratch(nq):
    one = [pltpu.VMEM((nq, Q_TILE, Q_TILE), F32),
           pltpu.VMEM((SUBLANES, Q_TILE), F32)]
    return one * ATT_RING


def _attn_specs(w, seq, nq):
    return ([pl.BlockSpec((1, w, Q_TILE), lambda b, g, qi: (b, g, qi)),
             pl.BlockSpec((seq, w), lambda b, g, qi: (b, g)),
             pl.BlockSpec((1, w, seq), lambda b, g, qi: (b, g, 0))],
            pl.BlockSpec((Q_TILE, w), lambda b, g, qi: (b * nq + qi, g)))


def _moba_prompt(qa_t, ka, va_t, km, bias_tiles, batch, seq):
    nq = seq // Q_TILE
    nb = seq // MOBA_BLOCK
    w = ATT_PAIRS * LANES
    assert WIDTH_A % w == 0
    in_specs, out_spec = _attn_specs(w, seq, nq)
    return pl.pallas_call(
        functools.partial(_moba_prompt_kernel, nq=nq),
        grid=(batch, WIDTH_A // w, nq),
        in_specs=in_specs + [pl.BlockSpec((nb, 1, w), lambda b, g, qi: (b, 0, g)),
                             pl.BlockSpec((2 * ATT_PAIRS, 2, ATT_TILE, ATT_TILE),
                                          lambda b, g, qi: (g, 0, 0, 0))],
        out_specs=out_spec,
        out_shape=jax.ShapeDtypeStruct((batch * seq, WIDTH_A), BF16),
        scratch_shapes=_attn_scratch(nq),
        compiler_params=pltpu.CompilerParams(
            dimension_semantics=("arbitrary", "arbitrary", "arbitrary"),
            vmem_limit_bytes=VMEM_LIMIT_BYTES),
        name="moba_prompt",
    )(qa_t, ka, va_t, km, bias_tiles)


def _diff_prompt_kernel(qt_ref, k_ref, vt_ref, bias_ref, lam_ref, gain_ref, o_ref, *scratch,
                        lam_init, nq):
    tq = Q_TILE
    qt_ops = []
    for head in range(ATT_PAIRS):
        qt = qt_ref[0, head * LANES:(head + 1) * LANES, :]
        qt_ops += [_half_rows(qt, c) for c in range(2)]

    def k_tile(i, jj):
        head = i // 2
        return k_ref[jj * tq:(jj + 1) * tq, head * LANES:(head + 1) * LANES]

    n_ops = 2 * ATT_PAIRS
    vt_rows = [slice((i // 2) * LANES, (i // 2 + 1) * LANES) for i in range(n_ops)]

    def body(n):
        res = _attend_static(qt_ops, k_tile, vt_ref, vt_rows, bias_ref,
                             tuple(i // 2 for i in range(n_ops)), _scratch_ring(scratch), n)
        lam = _lambda(lam_ref, lam_init)
        for head in range(ATT_PAIRS):
            (a0, l0), (a1, l1) = res[2 * head], res[2 * head + 1]
            o = (a0 / l0 - lam * (a1 / l1)).T
            o_ref[:, head * LANES:(head + 1) * LANES] = (
                _rmsnorm(o, gain_ref[...]) * (1.0 - lam_init)).astype(o_ref.dtype)

    _per_query_tile(nq, body)


def _diff_prompt(qd_t, kd, vd_t, bias_tiles, lam_vecs, gain, lam_init, batch, seq):
    nq = seq // Q_TILE
    w = ATT_PAIRS * LANES
    assert WIDTH_B % w == 0
    in_specs, out_spec = _attn_specs(w, seq, nq)
    return pl.pallas_call(
        functools.partial(_diff_prompt_kernel, lam_init=lam_init, nq=nq),
        grid=(batch, WIDTH_B // w, nq),
        in_specs=in_specs + [
                  pl.BlockSpec((ATT_PAIRS, 2, ATT_TILE, ATT_TILE), lambda b, g, qi: (g, 0, 0, 0)),
                  pl.BlockSpec((4, HEAD_DIM), lambda b, g, qi: (0, 0)),
                  pl.BlockSpec((1, 2 * HEAD_DIM), lambda b, g, qi: (0, 0))],
        out_specs=out_spec,
        out_shape=jax.ShapeDtypeStruct((batch * seq, WIDTH_B), BF16),
        scratch_shapes=_attn_scratch(nq),
        compiler_params=pltpu.CompilerParams(
            dimension_semantics=("arbitrary", "arbitrary", "arbitrary"),
            vmem_limit_bytes=VMEM_LIMIT_BYTES),
        name="diff_prompt",
    )(qd_t, kd, vd_t, bias_tiles, lam_vecs, gain)


def _toeplitz(w, t):
    h = w.shape[0]
    m = jnp.broadcast_to(w[:, None, :], (h, t, 2 * t)).reshape(h, 2 * t * t)
    return m[:, :t * (2 * t - 1)].reshape(h, t, 2 * t - 1)[:, :, :t]


def _bias_tiles(bias_t):
    t = ATT_TILE
    assert t + 1 >= MAX_DISTANCE
    k = jnp.arange(2 * t)
    d = jnp.where(k < t, k, k - 2 * t)
    far = _bias_of(bias_t, jnp.full((1,), 2 * t))
    diag = _toeplitz(jnp.where(d >= 0, (_bias_of(bias_t, d) - far) * LOG2E, -jnp.inf), t)
    sub = _toeplitz((_bias_of(bias_t, d + t) - far) * LOG2E, t)
    return jnp.stack([diag, sub], axis=1)


def _page_stream(pt_ref, pk_ref, pv_ref, buf_ref, sem_ref, n_seq, n_chunks):
    pps = PAGES_PER_STEP
    ahead = RING_SLOTS - 1
    b, ph, j = pl.program_id(0), pl.program_id(1), pl.program_id(2)
    step = (b * 2 + ph) * n_chunks + j
    total = n_seq * 2 * n_chunks

    def copies(pool_ref, bb, jj, slot):
        return [pltpu.make_async_copy(pool_ref.at[pt_ref[bb, jj * pps + i]],
                                      buf_ref.at[slot, i], sem_ref.at[slot])
                for i in range(pps)]

    def start(chunk):
        jj, seq_phase = chunk % n_chunks, chunk // n_chunks
        for phase, pool_ref in ((0, pk_ref), (1, pv_ref)):
            @pl.when(seq_phase % 2 == phase)
            def _(pool_ref=pool_ref):
                for i, cp in enumerate(copies(pool_ref, seq_phase // 2, jj, chunk % RING_SLOTS)):
                    cp.start(priority=i % 2)

    @pl.when(step == 0)
    def _():
        for chunk in range(min(ahead, total)):
            start(jnp.int32(chunk))

    @pl.when(step + ahead < total)
    def _():
        start(step + ahead)

    slot = step % RING_SLOTS

    def wait(pool_ref):
        for cp in copies(pool_ref, b, j, slot):
            cp.wait()

    return slot, wait


def _moba_sample_kernel(pt_ref, pk_ref, pv_ref, q_ref, kn_ref, vn_ref, bias_ref, o_ref,
                        s_ref, snew_ref, acc_ref, l_ref, m_ref, bmax_ref, gate_ref, sel_ref,
                        buf_ref, sem_ref, *, n_seq, n_chunks, tokens):
    pps = PAGES_PER_STEP
    slot, wait = _page_stream(pt_ref, pk_ref, pv_ref, buf_ref, sem_ref, n_seq, n_chunks)
    rows = q_ref.shape[1]
    groups = rows // tokens
    ppb = MOBA_BLOCK // PAGE_SIZE
    bps = pps // ppb
    n_blocks = n_chunks * bps
    ph = pl.program_id(1)
    j = pl.program_id(2)
    lane = lax.broadcasted_iota(jnp.int32, (rows, LANES), 1)
    lanef = lane.astype(F32)
    q = q_ref[0]

    @pl.when((ph == 0) & (j == 0))
    def _():
        snew_ref[...] = _dot_nt(q, kn_ref[0]) + bias_ref[2]
        bmax_ref[...] = jnp.full((rows, LANES), -jnp.inf, F32)
        gate_ref[...] = jnp.zeros((rows, LANES), F32)

    @pl.when(ph == 0)
    def _():
        wait(pk_ref)
        is_last = j == n_chunks - 1
        bmax = bmax_ref[...]
        gate = gate_ref[...]
        for b in range(bps):
            smax = ssum = None
            for pg in range(ppb):
                i = b * ppb + pg
                raw = _dot(q, buf_ref[slot, i].astype(BF16))
                bias = bias_ref[0]
                if i == pps - 1:
                    bias = jnp.where(is_last, bias_ref[1], bias)
                s = raw + bias
                s_ref[j, :, i * PAGE_SIZE:(i + 1) * PAGE_SIZE] = s
                smax = s if smax is None else jnp.maximum(smax, s)
                ssum = raw if ssum is None else ssum + raw
            blk = j * bps + b
            bmax = jnp.where(lane == blk, jnp.max(smax, axis=1, keepdims=True), bmax)
            gate = jnp.where(lane == blk, jnp.sum(ssum, axis=1, keepdims=True), gate)
        bmax_ref[...] = bmax
        gate_ref[...] = gate

    @pl.when((ph == 1) & (j == 0))
    def _():
        g = jnp.where(lane < n_blocks, gate_ref[...], -jnp.inf)
        sel = jnp.zeros((rows, LANES), F32)
        for _ in range(MOBA_TOPK):
            top = jnp.max(g, axis=1, keepdims=True)
            first = jnp.min(jnp.where(g == top, lanef, float(LANES)), axis=1, keepdims=True)
            pick = lanef == first
            finite = jnp.where(jnp.abs(top) < jnp.inf, 1.0, 0.0)
            sel = jnp.maximum(sel, jnp.where(pick, finite, 0.0))
            g = jnp.where(pick, -jnp.inf, g)
        sel_ref[...] = sel
        m_sel = jnp.max(jnp.where(sel > 0.5, bmax_ref[...], -jnp.inf), axis=1, keepdims=True)
        m_new = jnp.max(snew_ref[...], axis=1, keepdims=True)
        m_ref[...] = jnp.broadcast_to(jnp.maximum(m_sel, m_new), (rows, LANES))
        acc_ref[...] = jnp.zeros(acc_ref.shape, F32)
        l_ref[...] = jnp.zeros(l_ref.shape, F32)

    @pl.when(ph == 1)
    def _():
        wait(pv_ref)
        m = m_ref[...]
        selv = sel_ref[...]
        acc = acc_ref[...]
        l = l_ref[...]
        for i in range(pps):
            blk = j * bps + i // ppb
            on = jnp.max(jnp.where(lane == blk, selv, 0.0), axis=1, keepdims=True)
            on = jnp.broadcast_to(on, (rows, LANES)) > 0.5
            s = s_ref[j, :, i * PAGE_SIZE:(i + 1) * PAGE_SIZE]
            p = jnp.exp(jnp.where(on, s - m, -jnp.inf))
            l = l + p
            acc = acc + _dot_nt(p.astype(BF16), buf_ref[slot, i].astype(BF16))
        acc_ref[...] = acc
        l_ref[...] = l

    @pl.when((ph == 1) & (j == n_chunks - 1))
    def _():
        p = jnp.exp(snew_ref[...] - m_ref[...])
        acc = acc_ref[...] + _dot(p.astype(BF16), vn_ref[0])
        l = jnp.sum(l_ref[...] + p, axis=1, keepdims=True)
        o = acc / l
        grp = lax.broadcasted_iota(jnp.int32, (groups, WIDTH_A), 0)
        col = lax.broadcasted_iota(jnp.int32, (groups, WIDTH_A), 1)
        head_shift = HEAD_DIM.bit_length() - 1
        w = jnp.where((col >> head_shift) == grp, 1.0, 0.0)
        tok = jnp.sum(o.reshape(tokens, groups, WIDTH_A) * w[None], axis=1)
        o_ref[0] = tok.astype(o_ref.dtype)


def _moba_sample(page_table, pool_kt, pool_vt, q_bd, k_new, v_new, bias_rows, *, tokens):
    pps = PAGES_PER_STEP
    n_seq, n_pages = page_table.shape
    n_chunks = n_pages // pps
    rows = q_bd.shape[1]
    seq_map = lambda b, ph, j, pt: (b, 0, 0)
    hbm = pl.BlockSpec(memory_space=pl.ANY)
    in_specs = [hbm, hbm,
                pl.BlockSpec((1, rows, WIDTH_A), seq_map),
                pl.BlockSpec((1, LANES, WIDTH_A), seq_map),
                pl.BlockSpec((1, LANES, WIDTH_A), seq_map),
                pl.BlockSpec((3, rows, LANES), lambda b, ph, j, pt: (0, 0, 0))]
    vec = pltpu.VMEM((rows, LANES), F32)
    grid_spec = pltpu.PrefetchScalarGridSpec(
        num_scalar_prefetch=1,
        grid=(n_seq, 2, n_chunks),
        in_specs=in_specs,
        out_specs=pl.BlockSpec((1, tokens, WIDTH_A), seq_map),
        scratch_shapes=[pltpu.VMEM((n_chunks, rows, pps * PAGE_SIZE), F32),
                        vec,
                        pltpu.VMEM((rows, WIDTH_A), F32),
                        vec, vec, vec, vec, vec,
                        pltpu.VMEM((RING_SLOTS, pps) + pool_kt.shape[1:], F32),
                        pltpu.SemaphoreType.DMA((RING_SLOTS,))])
    return pl.pallas_call(
        functools.partial(_moba_sample_kernel, n_seq=n_seq, n_chunks=n_chunks, tokens=tokens),
        grid_spec=grid_spec,
        out_shape=jax.ShapeDtypeStruct((n_seq, tokens, WIDTH_A), BF16),
        compiler_params=pltpu.CompilerParams(
            dimension_semantics=("arbitrary", "arbitrary", "arbitrary"),
            vmem_limit_bytes=VMEM_LIMIT_BYTES),
        name="moba_sample",
    )(page_table, pool_kt, pool_vt, q_bd, k_new, v_new, bias_rows)


def _diff_sample_kernel(pt_ref, pk_ref, pv_ref, q_ref, kn_ref, vn_ref, bias_ref, bnew_ref,
                        lam_ref, gain_ref, o_ref, s_ref, snew_ref, acc_ref, l_ref, m_ref,
                        buf_ref, sem_ref, *, n_seq, n_chunks, tokens, lam_init):
    pps = PAGES_PER_STEP
    slot, wait = _page_stream(pt_ref, pk_ref, pv_ref, buf_ref, sem_ref, n_seq, n_chunks)
    rows = q_ref.shape[1]
    pw = buf_ref.shape[2]
    ph = pl.program_id(1)
    j = pl.program_id(2)
    q = q_ref[0]

    @pl.when((ph == 0) & (j == 0))
    def _():
        snew_ref[...] = _dot_nt(q, kn_ref[0]) + bnew_ref[...]
        m_ref[...] = jnp.full((rows, LANES), -jnp.inf, F32)

    @pl.when(ph == 0)
    def _():
        wait(pk_ref)
        is_last = j == n_chunks - 1
        mx = m_ref[...]
        for i in range(pps):
            bias = bias_ref[0]
            if i == pps - 1:
                bias = jnp.where(is_last, bias_ref[1], bias)
            s = _dot_nt(q, buf_ref[slot, i].astype(BF16)) + bias
            s_ref[j, :, i * pw:(i + 1) * pw] = s
            mx = jnp.maximum(mx, _lane_fold(jnp.maximum, s))
        m_ref[...] = mx

    @pl.when((ph == 1) & (j == 0))
    def _():
        m = jnp.maximum(jnp.max(m_ref[...], axis=1, keepdims=True),
                        jnp.max(snew_ref[...], axis=1, keepdims=True))
        m_ref[...] = jnp.broadcast_to(m, (rows, LANES))
        acc_ref[...] = jnp.zeros(acc_ref.shape, F32)
        l_ref[...] = jnp.zeros(l_ref.shape, F32)

    @pl.when(ph == 1)
    def _():
        wait(pv_ref)
        m = m_ref[...]
        mw = jnp.concatenate([m] * (pw // LANES), axis=1)
        acc = acc_ref[...]
        l = l_ref[...]
        for i in range(pps):
            p = jnp.exp(s_ref[j, :, i * pw:(i + 1) * pw] - mw)
            l = l + _lane_fold(jnp.add, p)
            acc = acc + _dot(p.astype(BF16), buf_ref[slot, i].astype(BF16))
        acc_ref[...] = acc
        l_ref[...] = l

    @pl.when((ph == 1) & (j == n_chunks - 1))
    def _():
        p = jnp.exp(snew_ref[...] - m_ref[...])
        acc = acc_ref[...] + _dot(p.astype(BF16), vn_ref[0])
        l = jnp.sum(l_ref[...] + p, axis=1, keepdims=True)
        o = (acc / l).reshape(tokens, rows // tokens, LANES)
        lam = _lambda(lam_ref, lam_init)
        sub = lax.broadcasted_iota(jnp.int32, (rows // tokens, LANES), 0)
        gain = gain_ref[...]
        parts = []
        for h in range(N_HEADS_B):
            w = jnp.where(sub == 2 * h, 1.0, jnp.where(sub == 2 * h + 1, -lam, 0.0))
            parts.append(_rmsnorm(jnp.sum(o * w[None], axis=1), gain) * (1.0 - lam_init))
        o_ref[0] = jnp.concatenate(parts, axis=1).astype(o_ref.dtype)


def _diff_sample(page_table, pool_k, pool_v, q2, k_new, v_new, bias_rows, bias_new, lam_vecs, gain,
                 *, tokens, lam_init):
    pps = PAGES_PER_STEP
    n_seq, n_pages = page_table.shape
    n_chunks = n_pages // pps
    rows = q2.shape[1]
    pw = pool_k.shape[1]
    seq_map = lambda b, ph, j, pt: (b, 0, 0)
    const2 = lambda b, ph, j, pt: (0, 0)
    hbm = pl.BlockSpec(memory_space=pl.ANY)
    in_specs = [hbm, hbm,
                pl.BlockSpec((1, rows, LANES), seq_map),
                pl.BlockSpec((1, LANES, LANES), seq_map),
                pl.BlockSpec((1, LANES, LANES), seq_map),
                pl.BlockSpec((2, rows, pw), lambda b, ph, j, pt: (0, 0, 0)),
                pl.BlockSpec((rows, LANES), const2),
                pl.BlockSpec((4, HEAD_DIM), const2),
                pl.BlockSpec((1, 2 * HEAD_DIM), const2)]
    vec = pltpu.VMEM((rows, LANES), F32)
    grid_spec = pltpu.PrefetchScalarGridSpec(
        num_scalar_prefetch=1,
        grid=(n_seq, 2, n_chunks),
        in_specs=in_specs,
        out_specs=pl.BlockSpec((1, tokens, WIDTH_B), seq_map),
        scratch_shapes=[pltpu.VMEM((n_chunks, rows, pps * pw), F32),
                        vec, vec, vec, vec,
                        pltpu.VMEM((RING_SLOTS, pps) + pool_k.shape[1:], F32),
                        pltpu.SemaphoreType.DMA((RING_SLOTS,))])
    return pl.pallas_call(
        functools.partial(_diff_sample_kernel, n_seq=n_seq, n_chunks=n_chunks, tokens=tokens,
                          lam_init=lam_init),
        grid_spec=grid_spec,
        out_shape=jax.ShapeDtypeStruct((n_seq, tokens, WIDTH_B), BF16),
        compiler_params=pltpu.CompilerParams(
            dimension_semantics=("arbitrary", "arbitrary", "arbitrary"),
            vmem_limit_bytes=VMEM_LIMIT_BYTES),
        name="diff_sample",
    )(page_table, pool_k, pool_v, q2, k_new, v_new, bias_rows, bias_new, lam_vecs, gain)


def _block_diag(q, n_seq, tokens, groups, keep):
    w = q.shape[1] // groups
    q4 = q.reshape(n_seq, tokens, groups, 1, w)
    eye = (jnp.arange(groups)[:, None] % keep == jnp.arange(keep)[None, :]).astype(q.dtype)
    return (q4 * eye.reshape(1, 1, groups, keep, 1)).reshape(n_seq, tokens * groups, keep * w)


def _pad_rows(x, n_seq):
    x = x.reshape(n_seq, -1, x.shape[-1])
    return jnp.pad(x, ((0, 0), (0, LANES - x.shape[1]), (0, 0)))


def _sample_bias(bias_t, tokens, heads_per_group):
    assert PAGE_SIZE + 1 >= MAX_DISTANCE
    tok = jnp.arange(tokens)[:, None]
    col = jnp.arange(LANES)[None, :]
    far = _bias_of(bias_t, jnp.broadcast_to(2 * PAGE_SIZE, (tokens, LANES)))
    last = _bias_of(bias_t, PAGE_SIZE + tok - col)
    new = jnp.where((col <= tok) & (col < tokens), _bias_of(bias_t, tok - col), -jnp.inf)

    def rows(x):
        x = jnp.repeat(x.transpose(1, 0, 2), heads_per_group, axis=1)
        return x.reshape(-1, LANES)

    return rows(far), rows(last), rows(new)


def _spread_heads(x, n_heads, groups):
    r, k = x.shape
    row_head = (jnp.arange(r) % groups) // (groups // n_heads)
    own = row_head[:, None, None] == jnp.arange(n_heads)[None, None, :]
    return jnp.where(own, x[:, :, None], -jnp.inf).reshape(r, k * n_heads)


def _merge_kernel(x_ref, oa_ref, ob_ref, g_ref, wg_ref, woa_ref, wob_ref, wo_ref, o_ref):
    x = x_ref[...]
    h = _rmsnorm(x, g_ref[...]).astype(BF16)
    ga = 1.0 / (1.0 + jnp.exp(-_dot(h, wg_ref[:, :D_MODEL].astype(BF16))))
    m = ga * _dot(oa_ref[...], woa_ref[...].astype(BF16))
    gb = 1.0 / (1.0 + jnp.exp(-_dot(h, wg_ref[:, D_MODEL:].astype(BF16))))
    m = m + gb * _dot(ob_ref[...], wob_ref[...].astype(BF16))
    o_ref[...] = x + _dot(m.astype(BF16), wo_ref[...].astype(BF16))


def _merge(x, oa, ob, g, wg, woa, wob, wo, tm):
    n = x.shape[0]
    row = lambda i: (i, 0)
    const = lambda i: (0, 0)
    return pl.pallas_call(
        _merge_kernel,
        grid=(n // tm,),
        in_specs=[pl.BlockSpec((tm, D_MODEL), row),
                  pl.BlockSpec((tm, WIDTH_A), row),
                  pl.BlockSpec((tm, WIDTH_B), row),
                  pl.BlockSpec((1, D_MODEL), const),
                  pl.BlockSpec((D_MODEL, 2 * D_MODEL), const, pipeline_mode=pl.Buffered(1)),
                  pl.BlockSpec((WIDTH_A, D_MODEL), const, pipeline_mode=pl.Buffered(1)),
                  pl.BlockSpec((WIDTH_B, D_MODEL), const, pipeline_mode=pl.Buffered(1)),
                  pl.BlockSpec((D_MODEL, D_MODEL), const, pipeline_mode=pl.Buffered(1))],
        out_specs=pl.BlockSpec((tm, D_MODEL), row),
        out_shape=jax.ShapeDtypeStruct((n, D_MODEL), F32),
        compiler_params=pltpu.CompilerParams(
            dimension_semantics=("arbitrary",), vmem_limit_bytes=VMEM_LIMIT_BYTES),
        name="merge",
    )(x, oa, ob, g, wg, woa, wob, wo)


def _ffn_kernel(*refs, tm, seq_len, has_prev):
    if has_prev:
        (x_ref, g_ref, wup_ref, cw_ref, cb_ref, wdn_ref, gfin_ref, e1_ref, e2_ref,
         y_ref, u_ref, act_ref) = refs
    else:
        (x_ref, g_ref, wup_ref, cw_ref, cb_ref, wdn_ref, gfin_ref,
         y_ref, u_ref, act_ref, carry_ref) = refs
    i = pl.program_id(0)
    x = x_ref[...]
    h = _rmsnorm(x, g_ref[...]).astype(BF16)
    row = lax.broadcasted_iota(jnp.int32, (tm, FF_CHUNK), 0)
    row8 = lax.broadcasted_iota(jnp.int32, (SUBLANES, FF_CHUNK), 0)
    if has_prev:
        pos = row & (seq_len - 1)
    else:
        @pl.when((i * tm) % seq_len == 0)
        def _():
            carry_ref[...] = jnp.zeros(carry_ref.shape, F32)

    for c in range(D_FF // FF_CHUNK):
        halves = []
        for part in range(2):
            cols = slice(part * D_FF + c * FF_CHUNK, part * D_FF + (c + 1) * FF_CHUNK)
            u = _dot(h, wup_ref[:, cols])
            um1 = pltpu.roll(u, 1, 0)
            um2 = pltpu.roll(u, 2, 0)
            if has_prev:
                um1 = jnp.where(pos == 0, e1_ref[:, cols], um1)
                um2 = jnp.where(pos < 2, e2_ref[:, cols], um2)
                u_ref[:, cols] = u
            else:
                prev = carry_ref[:, cols]
                top1 = jnp.where(row8 == 0, pltpu.roll(prev, 1, 0), um1[:SUBLANES])
                top2 = jnp.where(row8 < 2, pltpu.roll(prev, 2, 0), um2[:SUBLANES])
                um1 = jnp.concatenate([top1, um1[SUBLANES:]], axis=0)
                um2 = jnp.concatenate([top2, um2[SUBLANES:]], axis=0)
                carry_ref[:, cols] = u[tm - SUBLANES:]
                u_ref[:, cols] = u[tm - SUBLANES:]
            cw = cw_ref[:, cols]
            halves.append(((cb_ref[:, cols] + cw[0:1] * um2) + cw[1:2] * um1) + cw[2:3] * u)
        gate, val = halves
        act = (gate * (1.0 / (1.0 + jnp.exp(-gate)))) * val
        act_ref[:, c * FF_CHUNK:(c + 1) * FF_CHUNK] = act.astype(BF16)

    x3 = x + _dot(act_ref[...], wdn_ref[...].astype(BF16))
    y_ref[...] = _rmsnorm(x3, gfin_ref[...])


def _ffn(x, g, wup, cw, cb, wdn, gfin, prev, tm, seq_len):
    n = x.shape[0]
    has_prev = prev is not None
    row = lambda i: (i, 0)
    const = lambda i: (0, 0)
    in_specs = [pl.BlockSpec((tm, D_MODEL), row),
                pl.BlockSpec((1, D_MODEL), const),
                pl.BlockSpec((D_MODEL, 2 * D_FF), const, pipeline_mode=pl.Buffered(1)),
                pl.BlockSpec((CONV_W, 2 * D_FF), const),
                pl.BlockSpec((1, 2 * D_FF), const),
                pl.BlockSpec((D_FF, D_MODEL), const, pipeline_mode=pl.Buffered(1)),
                pl.BlockSpec((1, D_MODEL), const)]
    scratch = [pltpu.VMEM((tm, D_FF), BF16)]
    if has_prev:
        assert n == tm and tm % seq_len == 0 and seq_len & (seq_len - 1) == 0
        in_specs += [pl.BlockSpec((tm, 2 * D_FF), row)] * 2
        u_shape, u_spec = (n, 2 * D_FF), pl.BlockSpec((tm, 2 * D_FF), row)
        args = (x, g, wup, cw, cb, wdn, gfin) + tuple(prev)
    else:
        assert seq_len % tm == 0
        tiles_per_seq = seq_len // tm
        u_shape = (n // seq_len * SUBLANES, 2 * D_FF)
        u_spec = pl.BlockSpec((SUBLANES, 2 * D_FF), lambda i: (i // tiles_per_seq, 0))
        scratch.append(pltpu.VMEM((SUBLANES, 2 * D_FF), F32))
        args = (x, g, wup, cw, cb, wdn, gfin)
    return pl.pallas_call(
        functools.partial(_ffn_kernel, tm=tm, seq_len=seq_len, has_prev=has_prev),
        grid=(n // tm,),
        in_specs=in_specs,
        out_specs=[pl.BlockSpec((tm, D_MODEL), row), u_spec],
        out_shape=[jax.ShapeDtypeStruct((n, D_MODEL), F32), jax.ShapeDtypeStruct(u_shape, F32)],
        scratch_shapes=scratch,
        compiler_params=pltpu.CompilerParams(
            dimension_semantics=("arbitrary",), vmem_limit_bytes=VMEM_LIMIT_BYTES),
        name="ffn_sample" if has_prev else "ffn_prompt",
    )(*args)


def kernel(x_prompt, x_sample, cache_moba_k, cache_moba_v, cache_diff_k, cache_diff_v, state_conv, page_table, rel_bias, norm_attn, w_in, w_gate, w_out_a, w_out_b, w_out, lambda_q1, lambda_k1, lambda_q2, lambda_k2, diff_norm, norm_ffn, w_up, conv_w, conv_b, w_down, norm_final):
    batch, seq, _ = x_prompt.shape
    n_seq, tokens, _ = x_sample.shape
    depth = w_in.shape[0]
    n_phys = cache_moba_k.shape[1]
    n_pages = page_table.shape[1]
    assert depth == 1 and seq % Q_TILE == 0 and ATT_TILE & (ATT_TILE - 1) == 0
    assert (n_pages * PAGE_SIZE) % MOBA_BLOCK == 0 and n_pages % PAGES_PER_STEP == 0
    assert n_pages * PAGE_SIZE // MOBA_BLOCK <= LANES and CONV_W - 1 <= tokens <= LANES // N_HEADS_B
    l = 0
    lam_init = 0.8 - 0.6 * math.exp(-0.3 * l)

    bias_a = rel_bias[:, :N_HEADS_A].T
    bias_d = rel_bias[:, N_HEADS_A:].T
    row = lambda v: v.reshape(1, -1)
    w_in_b, w_gate_b, w_oa_b, w_ob_b, w_o_b = w_in[l], w_gate[l], w_out_a[l], w_out_b[l], w_out[l]
    w_up_b = w_up[l].astype(BF16)
    w_dn_b = w_down[l]
    lam_vecs = jnp.stack([lambda_q1[l], lambda_k1[l], lambda_q2[l], lambda_k2[l]]).astype(F32)
    gain_d = row(diff_norm[l])
    g_attn, g_ffn, g_fin = row(norm_attn[l]), row(norm_ffn[l]), row(norm_final)
    cb = row(conv_b[l])

    xp = x_prompt.reshape(batch * seq, D_MODEL)
    qa_tb, ka, va_tb, qd_tb, kd, vd_tb, ka_t, va_t, kd_r, vd_r, km = _proj(xp, g_attn, w_in_b, 512, seq)
    oa = _moba_prompt(qa_tb, ka, va_tb, km, _bias_tiles(bias_a), batch, seq)
    ob = _diff_prompt(qd_tb, kd, vd_tb, _bias_tiles(bias_d), lam_vecs, gain_d, lam_init, batch, seq)
    x2 = _merge(xp, oa, ob, g_attn, w_gate_b, w_oa_b, w_ob_b, w_o_b, 512)
    yp, tail_p = _ffn(x2, g_ffn, w_up_b, conv_w[l], cb, w_dn_b, g_fin, None, 1024, seq)
    conv_p = tail_p.reshape(batch, SUBLANES, 2 * D_FF)[:, SUBLANES - (CONV_W - 1):]

    n_s = n_seq * tokens
    xs = x_sample.reshape(n_s, D_MODEL)
    qa_s, ka_s, va_s, qd_s, kd_s, vd_s, ka_sf, va_sf, kd_sf, vd_sf = _proj(xs, g_attn, w_in_b, n_s)

    pool_t = lambda c: jnp.transpose(c[l], (0, 2, 3, 1)).reshape(n_phys, WIDTH_A, PAGE_SIZE)
    pool_r = lambda c: c[l].reshape(n_phys, PAGE_SIZE * N_HEADS_B, 2 * HEAD_DIM)
    far_a, last_a, new_a = _sample_bias(bias_a, tokens, 1)
    oa_s = _moba_sample(page_table, pool_t(cache_moba_k), pool_t(cache_moba_v),
                        _block_diag(qa_s, n_seq, tokens, N_HEADS_A, N_HEADS_A),
                        _pad_rows(ka_s, n_seq), _pad_rows(va_s, n_seq),
                        jnp.stack([far_a, last_a, new_a]), tokens=tokens)
    far_d, last_d, new_d = _sample_bias(bias_d, tokens, 2)
    groups_d = 2 * N_HEADS_B
    spread = lambda x: _spread_heads(x, N_HEADS_B, groups_d)
    per_head = lambda x: x.reshape(n_s * N_HEADS_B, 2 * HEAD_DIM)
    oa_d = _diff_sample(page_table, pool_r(cache_diff_k), pool_r(cache_diff_v),
                        _block_diag(qd_s, n_seq, tokens, groups_d, 2),
                        _pad_rows(per_head(kd_s), n_seq), _pad_rows(per_head(vd_s), n_seq),
                        jnp.stack([spread(far_d), spread(last_d)]),
                        spread(new_d[:, :LANES // N_HEADS_B]),
                        lam_vecs, gain_d, tokens=tokens, lam_init=lam_init)
    x2s = _merge(xs, oa_s.reshape(n_s, WIDTH_A), oa_d.reshape(n_s, WIDTH_B), g_attn,
                 w_gate_b, w_oa_b, w_ob_b, w_o_b, n_s)
    st = state_conv[l]
    zero = jnp.zeros((n_seq, 1, 2 * D_FF), F32)
    e1 = jnp.concatenate([st[:, 1:2]] + [zero] * (tokens - 1), axis=1).reshape(n_s, 2 * D_FF)
    e2 = jnp.concatenate([st[:, 0:2]] + [zero] * (tokens - 2), axis=1).reshape(n_s, 2 * D_FF)
    ys, u_s = _ffn(x2s, g_ffn, w_up_b, conv_w[l], cb, w_dn_b, g_fin, (e1, e2), n_s, tokens)
    conv_s = u_s.reshape(n_seq, tokens, 2 * D_FF)[:, tokens - (CONV_W - 1):]

    shp_a = lambda a, b_, t_: a.reshape(1, b_, t_, N_HEADS_A, HEAD_DIM)
    shp_d = lambda a, b_, t_: a.reshape(1, b_, t_, N_HEADS_B, 2 * HEAD_DIM)
    untr = lambda a: a.reshape(batch, N_HEADS_A, HEAD_DIM, seq).transpose(0, 3, 1, 2)[None]
    return (yp.reshape(batch, seq, D_MODEL), ys.reshape(n_seq, tokens, D_MODEL),
            untr(ka_t), untr(va_t),
            shp_d(kd_r, batch, seq), shp_d(vd_r, batch, seq), conv_p[None],
            shp_a(ka_sf, n_seq, tokens), shp_a(va_sf, n_seq, tokens),
            shp_d(kd_sf, n_seq, tokens), shp_d(vd_sf, n_seq, tokens), conv_s[None])
```

```python
import functools
import math

import jax
import jax.numpy as jnp
from jax import lax
from jax.experimental import pallas as pl
from jax.experimental.pallas import tpu as pltpu

F32 = jnp.float32
BF16 = jnp.bfloat16

D_MODEL = 1024
HEAD_DIM = 64
N_HEADS_A = D_MODEL // 128
N_HEADS_B = D_MODEL // 256
WIDTH_A = N_HEADS_A * HEAD_DIM
WIDTH_B = N_HEADS_B * 2 * HEAD_DIM
N_IN = 3 * WIDTH_A + 3 * WIDTH_B
MOBA_BLOCK = 256
MOBA_TOPK = 3
NUM_BUCKETS = 32
MAX_DISTANCE = 128
D_FF = ((8 * D_MODEL // 3 + 127) // 128) * 128
CONV_W = 3
EPS = 1e-6
PAGE_SIZE = 128
SCALE = HEAD_DIM ** -0.5
LOG2E = math.log2(math.e)

LANES = 128
SUBLANES = 8
VMEM_LIMIT_BYTES = 56 * 1024 * 1024

ATT_TILE = MOBA_BLOCK
Q_TILE = 2 * ATT_TILE
ATT_PAIRS = 2
ATT_RING = 2
FF_CHUNK = 256
PAGES_PER_STEP = 32
RING_SLOTS = 3
NEG_BIG = -1e30


def _dot(a, b):
    return jnp.dot(a, b, preferred_element_type=F32)


def _dot_nt(a, b):
    return lax.dot_general(a, b, (((1,), (1,)), ((), ())), preferred_element_type=F32)


def _rmsnorm(x, g):
    return (x * lax.rsqrt(jnp.mean(x * x, axis=-1, keepdims=True) + EPS)) * g


def _lane_fold(op, x):
    out = x[:, :LANES]
    for c in range(1, x.shape[1] // LANES):
        out = op(out, x[:, c * LANES:(c + 1) * LANES])
    return out


def _t5_bucket(dist):
    n = jnp.maximum(dist, 0)
    max_exact = NUM_BUCKETS // 2
    nf = jnp.maximum(n, 1).astype(F32)
    large = max_exact + (jnp.log(nf / max_exact) / math.log(MAX_DISTANCE / max_exact)
                         * (NUM_BUCKETS - max_exact)).astype(jnp.int32)
    large = jnp.minimum(large, NUM_BUCKETS - 1)
    return jnp.where(n < max_exact, n, large)


def _bias_of(bias_t, dist):
    onehot = jax.nn.one_hot(_t5_bucket(dist), NUM_BUCKETS, dtype=F32)
    return jnp.einsum('hb,...b->h...', bias_t.astype(F32), onehot, precision=lax.Precision.HIGHEST)


def _lambda(lam_ref, lam_init):
    lv = lam_ref[...]
    a = jnp.sum(lv[0:1] * lv[1:2], axis=-1, keepdims=True)
    b = jnp.sum(lv[2:3] * lv[3:4], axis=-1, keepdims=True)
    return jnp.exp(a) - jnp.exp(b) + lam_init


def _proj_kernel(x_ref, g_ref, w_ref,
                 qa_ref, ka_ref, va_ref, qd_ref, kd_ref, vd_ref,
                 kaf_ref, vaf_ref, kdf_ref, vdf_ref, *km_ref, prompt):
    tm = x_ref.shape[0]
    h = _rmsnorm(x_ref[...], g_ref[...]).astype(BF16)
    outs = ((qa_ref, None, SCALE), (ka_ref, kaf_ref, None), (va_ref, vaf_ref, None),
            (qd_ref, None, SCALE), (kd_ref, kdf_ref, None), (vd_ref, vdf_ref, None))
    for c, (b_ref, f_ref, scale) in enumerate(outs):
        u = _dot(h, w_ref[:, c * WIDTH_A:(c + 1) * WIDTH_A].astype(BF16))
        if scale is not None:
            u = u * (scale * LOG2E if prompt else scale)
        if not prompt:
            if f_ref is not None:
                f_ref[...] = u
            b_ref[...] = u.astype(BF16)
            continue
        is_key = c in (1, 4)
        ut = u.T if c != 4 else None
        if c in (1, 2):
            f_ref[0] = ut
        elif f_ref is not None:
            for hd in range(N_HEADS_B):
                f_ref[pl.ds(hd, tm, stride=N_HEADS_B), :] = u[:, hd * LANES:(hd + 1) * LANES]
        if is_key:
            b_ref[...] = u.astype(BF16)
        else:
            b_ref[0] = ut.astype(BF16)
        if c == 1:
            for j in range(tm // MOBA_BLOCK):
                km_ref[0][j] = jnp.sum(u[j * MOBA_BLOCK:(j + 1) * MOBA_BLOCK], axis=0,
                                       keepdims=True) * (1.0 / MOBA_BLOCK)


def _proj(x, g, w_bf16, tm, seq=None):
    n = x.shape[0]
    row = lambda i: (i, 0)
    const = lambda i: (0, 0)
    blk = pl.BlockSpec((tm, WIDTH_A), row)
    b_specs = [blk] * 6
    b_shapes = [jax.ShapeDtypeStruct((n, WIDTH_A), BF16)] * 6
    f_specs = [blk] * 4
    f_shapes = [jax.ShapeDtypeStruct((n, WIDTH_A), F32)] * 4
    if seq is not None:
        assert seq % tm == 0 and tm % MOBA_BLOCK == 0 and 2 * HEAD_DIM == LANES
        tiles = seq // tm
        t_spec = pl.BlockSpec((1, WIDTH_A, tm), lambda i: (i // tiles, 0, i % tiles))
        r_spec = pl.BlockSpec((tm * N_HEADS_B, LANES), row)
        r_shape = jax.ShapeDtypeStruct((n * N_HEADS_B, LANES), F32)
        t_shape = lambda dt: jax.ShapeDtypeStruct((n // seq, WIDTH_A, seq), dt)
        b_specs = [t_spec, blk, t_spec, t_spec, blk, t_spec]
        b_shapes = [t_shape(BF16), b_shapes[0], t_shape(BF16), t_shape(BF16), b_shapes[0], t_shape(BF16)]
        f_specs = [t_spec, t_spec, r_spec, r_spec,
                   pl.BlockSpec((tm // MOBA_BLOCK, 1, WIDTH_A), lambda i: (i, 0, 0))]
        f_shapes = [t_shape(F32), t_shape(F32), r_shape, r_shape,
                    jax.ShapeDtypeStruct((n // MOBA_BLOCK, 1, WIDTH_A), F32)]
    return pl.pallas_call(
        functools.partial(_proj_kernel, prompt=seq is not None),
        grid=(n // tm,),
        in_specs=[pl.BlockSpec((tm, D_MODEL), row),
                  pl.BlockSpec((1, D_MODEL), const),
                  pl.BlockSpec((D_MODEL, N_IN), const, pipeline_mode=pl.Buffered(1))],
        out_specs=b_specs + f_specs,
        out_shape=b_shapes + f_shapes,
        compiler_params=pltpu.CompilerParams(
            dimension_semantics=("arbitrary",), vmem_limit_bytes=VMEM_LIMIT_BYTES),
        name="proj",
    )(x, g, w_bf16)


def _row_fold(op, x):
    y = x.reshape(x.shape[0] // SUBLANES, SUBLANES, x.shape[1])
    out = y[0]
    for g in range(1, y.shape[0]):
        out = op(out, y[g])
    return out


def _attend_static(qt_ops, k_tile, vt_ref, vt_rows, bias_ref, bidx, scratch, qi):
    t = ATT_TILE
    tq = Q_TILE
    halves = tq // t
    trips = qi + 1
    outs = [None] * len(qt_ops)

    def pass1(i):
        s_ref, mb_ref = scratch[i % len(scratch)]
        mx = [None] * halves
        k_tiles = [k_tile(i, jj) for jj in range(trips)]
        for hq in range(halves):
            cols = slice(hq * t, (hq + 1) * t)
            qt_cols = qt_ops[i][:, cols]
            for jj in range(trips):
                for hk in range(halves):
                    dist = (halves * qi + hq) - (halves * jj + hk)
                    if dist < 0:
                        continue
                    blk = _dot(k_tiles[jj][hk * t:(hk + 1) * t], qt_cols)
                    if dist < 2:
                        blk = blk + bias_ref[bidx[i], dist]
                    s_ref[jj, hk * t:(hk + 1) * t, cols] = blk
                    f = _row_fold(jnp.maximum, blk)
                    mx[hq] = f if mx[hq] is None else jnp.maximum(mx[hq], f)
                    yield
        m = jnp.max(jnp.concatenate(mx, axis=1), axis=0, keepdims=True)
        mb_ref[...] = jnp.broadcast_to(m, (SUBLANES, tq))

    def pass2(i):
        s_ref, mb_ref = scratch[i % len(scratch)]
        accs, sums = [], []
        for hq in range(halves):
            cols = slice(hq * t, (hq + 1) * t)
            mb = mb_ref[:, cols]
            l = jnp.zeros((SUBLANES, t), F32)
            acc = None
            n_keys = (halves * qi + hq + 1) * t
            for kb in range(n_keys // t):
                jj, hk = divmod(kb, halves)
                sv = s_ref[jj, hk * t:(hk + 1) * t, cols].reshape(t // SUBLANES, SUBLANES, t)
                p = jnp.exp2(sv - mb[None]).reshape(t, t)
                l = l + _row_fold(jnp.add, p)
                part = _dot(vt_ref[0, vt_rows[i], kb * t:(kb + 1) * t], p.astype(BF16))
                acc = part if acc is None else acc + part
                yield
            accs.append(acc)
            sums.append(l)
        outs[i] = (jnp.concatenate(accs, axis=1),
                   jnp.sum(jnp.concatenate(sums, axis=1), axis=0, keepdims=True))

    for _ in pass1(0):
        pass
    for i in range(1, len(qt_ops)):
        streams = [pass1(i), pass2(i - 1)]
        while streams:
            streams = [g for g in streams if next(g, StopIteration) is not StopIteration]
    for _ in pass2(len(qt_ops) - 1):
        pass
    return outs


def _per_query_tile(nq, body):
    qi = pl.program_id(2)
    for n in range(nq):
        pl.when(qi == n)(functools.partial(body, n))


def _half_rows(qt, half):
    zero = jnp.zeros((HEAD_DIM, qt.shape[1]), qt.dtype)
    if half == 0:
        return jnp.concatenate([qt[:HEAD_DIM], zero], axis=0)
    return jnp.concatenate([zero, qt[HEAD_DIM:]], axis=0)


def _scratch_ring(refs):
    return [tuple(refs[2 * r:2 * r + 2]) for r in range(ATT_RING)]


def _moba_prompt_kernel(qt_ref, k_ref, vt_ref, km_ref, bias_ref, o_ref, *scratch, nq):
    t = ATT_TILE
    tq = Q_TILE
    tile_shift = t.bit_length() - 1
    nb = km_ref.shape[0]
    qi = pl.program_id(2)
    blk = lax.broadcasted_iota(jnp.int32, (nb, tq), 0)
    own = (tq // t) * qi + (lax.broadcasted_iota(jnp.int32, (nb, tq), 1) >> tile_shift)

    qt_ops = []
    for pair in range(ATT_PAIRS):
        rows = slice(pair * LANES, (pair + 1) * LANES)
        qt = qt_ref[0, rows, :]
        km = km_ref[:, 0, rows]
        km_hi = km.astype(BF16)
        km_lo = (km - km_hi.astype(F32)).astype(BF16)
        for e in range(2):
            qet = _half_rows(qt, e)
            gt = _dot(km_hi, qet) + _dot(km_lo, qet)
            rank = jnp.zeros((nb, tq), jnp.int32)
            for m in range(nb):
                gm = gt[m:m + 1, :]
                ahead = (gm > gt) | ((gm == gt) & (m < blk))
                rank = rank + jnp.where(ahead & (m < own), 1, 0)
            keep = (blk < own) & (rank < MOBA_TOPK) & (jnp.abs(gt) < jnp.inf)
            pen = jnp.where(keep | (blk == own), 0.0, NEG_BIG)
            pen = jnp.concatenate([pen, jnp.zeros((LANES - nb, tq), F32)], axis=0)
            qt_ops.append(jnp.concatenate([qet, pen.astype(BF16)], axis=0))

    k_lane = lax.broadcasted_iota(jnp.int32, (tq, LANES), 1)
    k_blk = lax.broadcasted_iota(jnp.int32, (tq, LANES), 0) >> tile_shift

    def k_tile(i, jj):
        pair = i // 2
        onehot = jnp.where(k_lane == (tq // t) * jj + k_blk, 1.0, 0.0).astype(BF16)
        return jnp.concatenate(
            [k_ref[jj * tq:(jj + 1) * tq, pair * LANES:(pair + 1) * LANES], onehot], axis=1)

    n_ops = 2 * ATT_PAIRS
    vt_rows = [slice(i * HEAD_DIM, (i + 1) * HEAD_DIM) for i in range(n_ops)]

    def body(n):
        res = _attend_static(qt_ops, k_tile, vt_ref, vt_rows, bias_ref, tuple(range(n_ops)),
                             _scratch_ring(scratch), n)
        for pair in range(ATT_PAIRS):
            (a0, l0), (a1, l1) = res[2 * pair], res[2 * pair + 1]
            ot = jnp.concatenate([a0 / l0, a1 / l1], axis=0)
            o_ref[:, pair * LANES:(pair + 1) * LANES] = ot.T.astype(o_ref.dtype)

    _per_query_tile(nq, body)


def _attn_scratch(nq):
    one = [pltpu.VMEM((nq, Q_TILE, Q_TILE), F32),
           pltpu.VMEM((SUBLANES, Q_TILE), F32)]
    return one * ATT_RING


def _attn_specs(w, seq, nq):
    return ([pl.BlockSpec((1, w, Q_TILE), lambda b, g, qi: (b, g, qi)),
             pl.BlockSpec((seq, w), lambda b, g, qi: (b, g)),
             pl.BlockSpec((1, w, seq), lambda b, g, qi: (b, g, 0))],
            pl.BlockSpec((Q_TILE, w), lambda b, g, qi: (b * nq + qi, g)))


def _moba_prompt(qa_t, ka, va_t, km, bias_tiles, batch, seq):
    nq = seq // Q_TILE
    nb = seq // MOBA_BLOCK
    w = ATT_PAIRS * LANES
    assert WIDTH_A % w == 0
    in_specs, out_spec = _attn_specs(w, seq, nq)
    return pl.pallas_call(
        functools.partial(_moba_prompt_kernel, nq=nq),
        grid=(batch, WIDTH_A // w, nq),
        in_specs=in_specs + [pl.BlockSpec((nb, 1, w), lambda b, g, qi: (b, 0, g)),
                             pl.BlockSpec((2 * ATT_PAIRS, 2, ATT_TILE, ATT_TILE),
                                          lambda b, g, qi: (g, 0, 0, 0))],
        out_specs=out_spec,
        out_shape=jax.ShapeDtypeStruct((batch * seq, WIDTH_A), BF16),
        scratch_shapes=_attn_scratch(nq),
        compiler_params=pltpu.CompilerParams(
            dimension_semantics=("arbitrary", "arbitrary", "arbitrary"),
            vmem_limit_bytes=VMEM_LIMIT_BYTES),
        name="moba_prompt",
    )(qa_t, ka, va_t, km, bias_tiles)


def _diff_prompt_kernel(qt_ref, k_ref, vt_ref, bias_ref, lam_ref, gain_ref, o_ref, *scratch,
                        lam_init, nq):
    tq = Q_TILE
    qt_ops = []
    for head in range(ATT_PAIRS):
        qt = qt_ref[0, head * LANES:(head + 1) * LANES, :]
        qt_ops += [_half_rows(qt, c) for c in range(2)]

    def k_tile(i, jj):
        head = i // 2
        return k_ref[jj * tq:(jj + 1) * tq, head * LANES:(head + 1) * LANES]

    n_ops = 2 * ATT_PAIRS
    vt_rows = [slice((i // 2) * LANES, (i // 2 + 1) * LANES) for i in range(n_ops)]

    def body(n):
        res = _attend_static(qt_ops, k_tile, vt_ref, vt_rows, bias_ref,
                             tuple(i // 2 for i in range(n_ops)), _scratch_ring(scratch), n)
        lam = _lambda(lam_ref, lam_init)
        for head in range(ATT_PAIRS):
            (a0, l0), (a1, l1) = res[2 * head], res[2 * head + 1]
            o = (a0 / l0 - lam * (a1 / l1)).T
            o_ref[:, head * LANES:(head + 1) * LANES] = (
                _rmsnorm(o, gain_ref[...]) * (1.0 - lam_init)).astype(o_ref.dtype)

    _per_query_tile(nq, body)


def _diff_prompt(qd_t, kd, vd_t, bias_tiles, lam_vecs, gain, lam_init, batch, seq):
    nq = seq // Q_TILE
    w = ATT_PAIRS * LANES
    assert WIDTH_B % w == 0
    in_specs, out_spec = _attn_specs(w, seq, nq)
    return pl.pallas_call(
        functools.partial(_diff_prompt_kernel, lam_init=lam_init, nq=nq),
        grid=(batch, WIDTH_B // w, nq),
        in_specs=in_specs + [
                  pl.BlockSpec((ATT_PAIRS, 2, ATT_TILE, ATT_TILE), lambda b, g, qi: (g, 0, 0, 0)),
                  pl.BlockSpec((4, HEAD_DIM), lambda b, g, qi: (0, 0)),
                  pl.BlockSpec((1, 2 * HEAD_DIM), lambda b, g, qi: (0, 0))],
        out_specs=out_spec,
        out_shape=jax.ShapeDtypeStruct((batch * seq, WIDTH_B), BF16),
        scratch_shapes=_attn_scratch(nq),
        compiler_params=pltpu.CompilerParams(
            dimension_semantics=("arbitrary", "arbitrary", "arbitrary"),
            vmem_limit_bytes=VMEM_LIMIT_BYTES),
        name="diff_prompt",
    )(qd_t, kd, vd_t, bias_tiles, lam_vecs, gain)


def _toeplitz(w, t):
    h = w.shape[0]
    m = jnp.broadcast_to(w[:, None, :], (h, t, 2 * t)).reshape(h, 2 * t * t)
    return m[:, :t * (2 * t - 1)].reshape(h, t, 2 * t - 1)[:, :, :t]


def _bias_tiles(bias_t):
    t = ATT_TILE
    assert t + 1 >= MAX_DISTANCE
    k = jnp.arange(2 * t)
    d = jnp.where(k < t, k, k - 2 * t)
    far = _bias_of(bias_t, jnp.full((1,), 2 * t))
    diag = _toeplitz(jnp.where(d >= 0, (_bias_of(bias_t, d) - far) * LOG2E, -jnp.inf), t)
    sub = _toeplitz((_bias_of(bias_t, d + t) - far) * LOG2E, t)
    return jnp.stack([diag, sub], axis=1)


def _page_stream(pt_ref, pk_ref, pv_ref, buf_ref, sem_ref, n_seq, n_chunks):
    pps = PAGES_PER_STEP
    ahead = RING_SLOTS - 1
    b, ph, j = pl.program_id(0), pl.program_id(1), pl.program_id(2)
    step = (b * 2 + ph) * n_chunks + j
    total = n_seq * 2 * n_chunks

    def copies(pool_ref, bb, jj, slot):
        return [pltpu.make_async_copy(pool_ref.at[pt_ref[bb, jj * pps + i]],
                                      buf_ref.at[slot, i], sem_ref.at[slot])
                for i in range(pps)]

    def start(chunk):
        jj, seq_phase = chunk % n_chunks, chunk // n_chunks
        for phase, pool_ref in ((0, pk_ref), (1, pv_ref)):
            @pl.when(seq_phase % 2 == phase)
            def _(pool_ref=pool_ref):
                for i, cp in enumerate(copies(pool_ref, seq_phase // 2, jj, chunk % RING_SLOTS)):
                    cp.start(priority=i % 2)

    @pl.when(step == 0)
    def _():
        for chunk in range(min(ahead, total)):
            start(jnp.int32(chunk))

    @pl.when(step + ahead < total)
    def _():
        start(step + ahead)

    slot = step % RING_SLOTS

    def wait(pool_ref):
        for cp in copies(pool_ref, b, j, slot):
            cp.wait()

    return slot, wait


def _moba_sample_kernel(pt_ref, pk_ref, pv_ref, q_ref, kn_ref, vn_ref, bias_ref, o_ref,
                        s_ref, snew_ref, acc_ref, l_ref, m_ref, bmax_ref, gate_ref, sel_ref,
                        buf_ref, sem_ref, *, n_seq, n_chunks, tokens):
    pps = PAGES_PER_STEP
    slot, wait = _page_stream(pt_ref, pk_ref, pv_ref, buf_ref, sem_ref, n_seq, n_chunks)
    rows = q_ref.shape[1]
    groups = rows // tokens
    ppb = MOBA_BLOCK // PAGE_SIZE
    bps = pps // ppb
    n_blocks = n_chunks * bps
    ph = pl.program_id(1)
    j = pl.program_id(2)
    lane = lax.broadcasted_iota(jnp.int32, (rows, LANES), 1)
    lanef = lane.astype(F32)
    q = q_ref[0]

    @pl.when((ph == 0) & (j == 0))
    def _():
        snew_ref[...] = _dot_nt(q, kn_ref[0]) + bias_ref[2]
        bmax_ref[...] = jnp.full((rows, LANES), -jnp.inf, F32)
        gate_ref[...] = jnp.zeros((rows, LANES), F32)

    @pl.when(ph == 0)
    def _():
        wait(pk_ref)
        is_last = j == n_chunks - 1
        bmax = bmax_ref[...]
        gate = gate_ref[...]
        for b in range(bps):
            smax = ssum = None
            for pg in range(ppb):
                i = b * ppb + pg
                raw = _dot(q, buf_ref[slot, i].astype(BF16))
                bias = bias_ref[0]
                if i == pps - 1:
                    bias = jnp.where(is_last, bias_ref[1], bias)
                s = raw + bias
                s_ref[j, :, i * PAGE_SIZE:(i + 1) * PAGE_SIZE] = s
                smax = s if smax is None else jnp.maximum(smax, s)
                ssum = raw if ssum is None else ssum + raw
            blk = j * bps + b
            bmax = jnp.where(lane == blk, jnp.max(smax, axis=1, keepdims=True), bmax)
            gate = jnp.where(lane == blk, jnp.sum(ssum, axis=1, keepdims=True), gate)
        bmax_ref[...] = bmax
        gate_ref[...] = gate

    @pl.when((ph == 1) & (j == 0))
    def _():
        g = jnp.where(lane < n_blocks, gate_ref[...], -jnp.inf)
        sel = jnp.zeros((rows, LANES), F32)
        for _ in range(MOBA_TOPK):
            top = jnp.max(g, axis=1, keepdims=True)
            first = jnp.min(jnp.where(g == top, lanef, float(LANES)), axis=1, keepdims=True)
            pick = lanef == first
            finite = jnp.where(jnp.abs(top) < jnp.inf, 1.0, 0.0)
            sel = jnp.maximum(sel, jnp.where(pick, finite, 0.0))
            g = jnp.where(pick, -jnp.inf, g)
        sel_ref[...] = sel
        m_sel = jnp.max(jnp.where(sel > 0.5, bmax_ref[...], -jnp.inf), axis=1, keepdims=True)
        m_new = jnp.max(snew_ref[...], axis=1, keepdims=True)
        m_ref[...] = jnp.broadcast_to(jnp.maximum(m_sel, m_new), (rows, LANES))
        acc_ref[...] = jnp.zeros(acc_ref.shape, F32)
        l_ref[...] = jnp.zeros(l_ref.shape, F32)

    @pl.when(ph == 1)
    def _():
        wait(pv_ref)
        m = m_ref[...]
        selv = sel_ref[...]
        acc = acc_ref[...]
        l = l_ref[...]
        for i in range(pps):
            blk = j * bps + i // ppb
            on = jnp.max(jnp.where(lane == blk, selv, 0.0), axis=1, keepdims=True)
            on = jnp.broadcast_to(on, (rows, LANES)) > 0.5
            s = s_ref[j, :, i * PAGE_SIZE:(i + 1) * PAGE_SIZE]
            p = jnp.exp(jnp.where(on, s - m, -jnp.inf))
            l = l + p
            acc = acc + _dot_nt(p.astype(BF16), buf_ref[slot, i].astype(BF16))
        acc_ref[...] = acc
        l_ref[...] = l

    @pl.when((ph == 1) & (j == n_chunks - 1))
    def _():
        p = jnp.exp(snew_ref[...] - m_ref[...])
        acc = acc_ref[...] + _dot(p.astype(BF16), vn_ref[0])
        l = jnp.sum(l_ref[...] + p, axis=1, keepdims=True)
        o = acc / l
        grp = lax.broadcasted_iota(jnp.int32, (groups, WIDTH_A), 0)
        col = lax.broadcasted_iota(jnp.int32, (groups, WIDTH_A), 1)
        head_shift = HEAD_DIM.bit_length() - 1
        w = jnp.where((col >> head_shift) == grp, 1.0, 0.0)
        tok = jnp.sum(o.reshape(tokens, groups, WIDTH_A) * w[None], axis=1)
        o_ref[0] = tok.astype(o_ref.dtype)


def _moba_sample(page_table, pool_kt, pool_vt, q_bd, k_new, v_new, bias_rows, *, tokens):
    pps = PAGES_PER_STEP
    n_seq, n_pages = page_table.shape
    n_chunks = n_pages // pps
    rows = q_bd.shape[1]
    seq_map = lambda b, ph, j, pt: (b, 0, 0)
    hbm = pl.BlockSpec(memory_space=pl.ANY)
    in_specs = [hbm, hbm,
                pl.BlockSpec((1, rows, WIDTH_A), seq_map),
                pl.BlockSpec((1, LANES, WIDTH_A), seq_map),
                pl.BlockSpec((1, LANES, WIDTH_A), seq_map),
                pl.BlockSpec((3, rows, LANES), lambda b, ph, j, pt: (0, 0, 0))]
    vec = pltpu.VMEM((rows, LANES), F32)
    grid_spec = pltpu.PrefetchScalarGridSpec(
        num_scalar_prefetch=1,
        grid=(n_seq, 2, n_chunks),
        in_specs=in_specs,
        out_specs=pl.BlockSpec((1, tokens, WIDTH_A), seq_map),
        scratch_shapes=[pltpu.VMEM((n_chunks, rows, pps * PAGE_SIZE), F32),
                        vec,
                        pltpu.VMEM((rows, WIDTH_A), F32),
                        vec, vec, vec, vec, vec,
                        pltpu.VMEM((RING_SLOTS, pps) + pool_kt.shape[1:], F32),
                        pltpu.SemaphoreType.DMA((RING_SLOTS,))])
    return pl.pallas_call(
        functools.partial(_moba_sample_kernel, n_seq=n_seq, n_chunks=n_chunks, tokens=tokens),
        grid_spec=grid_spec,
        out_shape=jax.ShapeDtypeStruct((n_seq, tokens, WIDTH_A), BF16),
        compiler_params=pltpu.CompilerParams(
            dimension_semantics=("arbitrary", "arbitrary", "arbitrary"),
            vmem_limit_bytes=VMEM_LIMIT_BYTES),
        name="moba_sample",
    )(page_table, pool_kt, pool_vt, q_bd, k_new, v_new, bias_rows)


def _diff_sample_kernel(pt_ref, pk_ref, pv_ref, q_ref, kn_ref, vn_ref, bias_ref, bnew_ref,
                        lam_ref, gain_ref, o_ref, s_ref, snew_ref, acc_ref, l_ref, m_ref,
                        buf_ref, sem_ref, *, n_seq, n_chunks, tokens, lam_init):
    pps = PAGES_PER_STEP
    slot, wait = _page_stream(pt_ref, pk_ref, pv_ref, buf_ref, sem_ref, n_seq, n_chunks)
    rows = q_ref.shape[1]
    pw = buf_ref.shape[2]
    ph = pl.program_id(1)
    j = pl.program_id(2)
    q = q_ref[0]

    @pl.when((ph == 0) & (j == 0))
    def _():
        snew_ref[...] = _dot_nt(q, kn_ref[0]) + bnew_ref[...]
        m_ref[...] = jnp.full((rows, LANES), -jnp.inf, F32)

    @pl.when(ph == 0)
    def _():
        wait(pk_ref)
        is_last = j == n_chunks - 1
        mx = m_ref[...]
        for i in range(pps):
            bias = bias_ref[0]
            if i == pps - 1:
                bias = jnp.where(is_last, bias_ref[1], bias)
            s = _dot_nt(q, buf_ref[slot, i].astype(BF16)) + bias
            s_ref[j, :, i * pw:(i + 1) * pw] = s
            mx = jnp.maximum(mx, _lane_fold(jnp.maximum, s))
        m_ref[...] = mx

    @pl.when((ph == 1) & (j == 0))
    def _():
        m = jnp.maximum(jnp.max(m_ref[...], axis=1, keepdims=True),
                        jnp.max(snew_ref[...], axis=1, keepdims=True))
        m_ref[...] = jnp.broadcast_to(m, (rows, LANES))
        acc_ref[...] = jnp.zeros(acc_ref.shape, F32)
        l_ref[...] = jnp.zeros(l_ref.shape, F32)

    @pl.when(ph == 1)
    def _():
        wait(pv_ref)
        m = m_ref[...]
        mw = jnp.concatenate([m] * (pw // LANES), axis=1)
        acc = acc_ref[...]
        l = l_ref[...]
        for i in range(pps):
            p = jnp.exp(s_ref[j, :, i * pw:(i + 1) * pw] - mw)
            l = l + _lane_fold(jnp.add, p)
            acc = acc + _dot(p.astype(BF16), buf_ref[slot, i].astype(BF16))
        acc_ref[...] = acc
        l_ref[...] = l

    @pl.when((ph == 1) & (j == n_chunks - 1))
    def _():
        p = jnp.exp(snew_ref[...] - m_ref[...])
        acc = acc_ref[...] + _dot(p.astype(BF16), vn_ref[0])
        l = jnp.sum(l_ref[...] + p, axis=1, keepdims=True)
        o = (acc / l).reshape(tokens, rows // tokens, LANES)
        lam = _lambda(lam_ref, lam_init)
        sub = lax.broadcasted_iota(jnp.int32, (rows // tokens, LANES), 0)
        gain = gain_ref[...]
        parts = []
        for h in range(N_HEADS_B):
            w = jnp.where(sub == 2 * h, 1.0, jnp.where(sub == 2 * h + 1, -lam, 0.0))
            parts.append(_rmsnorm(jnp.sum(o * w[None], axis=1), gain) * (1.0 - lam_init))
        o_ref[0] = jnp.concatenate(parts, axis=1).astype(o_ref.dtype)


def _diff_sample(page_table, pool_k, pool_v, q2, k_new, v_new, bias_rows, bias_new, lam_vecs, gain,
                 *, tokens, lam_init):
    pps = PAGES_PER_STEP
    n_seq, n_pages = page_table.shape
    n_chunks = n_pages // pps
    rows = q2.shape[1]
    pw = pool_k.shape[1]
    seq_map = lambda b, ph, j, pt: (b, 0, 0)
    const2 = lambda b, ph, j, pt: (0, 0)
    hbm = pl.BlockSpec(memory_space=pl.ANY)
    in_specs = [hbm, hbm,
                pl.BlockSpec((1, rows, LANES), seq_map),
                pl.BlockSpec((1, LANES, LANES), seq_map),
                pl.BlockSpec((1, LANES, LANES), seq_map),
                pl.BlockSpec((2, rows, pw), lambda b, ph, j, pt: (0, 0, 0)),
                pl.BlockSpec((rows, LANES), const2),
                pl.BlockSpec((4, HEAD_DIM), const2),
                pl.BlockSpec((1, 2 * HEAD_DIM), const2)]
    vec = pltpu.VMEM((rows, LANES), F32)
    grid_spec = pltpu.PrefetchScalarGridSpec(
        num_scalar_prefetch=1,
        grid=(n_seq, 2, n_chunks),
        in_specs=in_specs,
        out_specs=pl.BlockSpec((1, tokens, WIDTH_B), seq_map),
        scratch_shapes=[pltpu.VMEM((n_chunks, rows, pps * pw), F32),
                        vec, vec, vec, vec,
                        pltpu.VMEM((RING_SLOTS, pps) + pool_k.shape[1:], F32),
                        pltpu.SemaphoreType.DMA((RING_SLOTS,))])
    return pl.pallas_call(
        functools.partial(_diff_sample_kernel, n_seq=n_seq, n_chunks=n_chunks, tokens=tokens,
                          lam_init=lam_init),
        grid_spec=grid_spec,
        out_shape=jax.ShapeDtypeStruct((n_seq, tokens, WIDTH_B), BF16),
        compiler_params=pltpu.CompilerParams(
            dimension_semantics=("arbitrary", "arbitrary", "arbitrary"),
            vmem_limit_bytes=VMEM_LIMIT_BYTES),
        name="diff_sample",
    )(page_table, pool_k, pool_v, q2, k_new, v_new, bias_rows, bias_new, lam_vecs, gain)


def _block_diag(q, n_seq, tokens, groups, keep):
    w = q.shape[1] // groups
    q4 = q.reshape(n_seq, tokens, groups, 1, w)
    eye = (jnp.arange(groups)[:, None] % keep == jnp.arange(keep)[None, :]).astype(q.dtype)
    return (q4 * eye.reshape(1, 1, groups, keep, 1)).reshape(n_seq, tokens * groups, keep * w)


def _pad_rows(x, n_seq):
    x = x.reshape(n_seq, -1, x.shape[-1])
    return jnp.pad(x, ((0, 0), (0, LANES - x.shape[1]), (0, 0)))


def _sample_bias(bias_t, tokens, heads_per_group):
    assert PAGE_SIZE + 1 >= MAX_DISTANCE
    tok = jnp.arange(tokens)[:, None]
    col = jnp.arange(LANES)[None, :]
    far = _bias_of(bias_t, jnp.broadcast_to(2 * PAGE_SIZE, (tokens, LANES)))
    last = _bias_of(bias_t, PAGE_SIZE + tok - col)
    new = jnp.where((col <= tok) & (col < tokens), _bias_of(bias_t, tok - col), -jnp.inf)

    def rows(x):
        x = jnp.repeat(x.transpose(1, 0, 2), heads_per_group, axis=1)
        return x.reshape(-1, LANES)

    return rows(far), rows(last), rows(new)


def _spread_heads(x, n_heads, groups):
    r, k = x.shape
    row_head = (jnp.arange(r) % groups) // (groups // n_heads)
    own = row_head[:, None, None] == jnp.arange(n_heads)[None, None, :]
    return jnp.where(own, x[:, :, None], -jnp.inf).reshape(r, k * n_heads)


def _merge_kernel(x_ref, oa_ref, ob_ref, g_ref, wg_ref, woa_ref, wob_ref, wo_ref, o_ref):
    x = x_ref[...]
    h = _rmsnorm(x, g_ref[...]).astype(BF16)
    ga = 1.0 / (1.0 + jnp.exp(-_dot(h, wg_ref[:, :D_MODEL].astype(BF16))))
    m = ga * _dot(oa_ref[...], woa_ref[...].astype(BF16))
    gb = 1.0 / (1.0 + jnp.exp(-_dot(h, wg_ref[:, D_MODEL:].astype(BF16))))
    m = m + gb * _dot(ob_ref[...], wob_ref[...].astype(BF16))
    o_ref[...] = x + _dot(m.astype(BF16), wo_ref[...].astype(BF16))


def _merge(x, oa, ob, g, wg, woa, wob, wo, tm):
    n = x.shape[0]
    row = lambda i: (i, 0)
    const = lambda i: (0, 0)
    return pl.pallas_call(
        _merge_kernel,
        grid=(n // tm,),
        in_specs=[pl.BlockSpec((tm, D_MODEL), row),
                  pl.BlockSpec((tm, WIDTH_A), row),
                  pl.BlockSpec((tm, WIDTH_B), row),
                  pl.BlockSpec((1, D_MODEL), const),
                  pl.BlockSpec((D_MODEL, 2 * D_MODEL), const, pipeline_mode=pl.Buffered(1)),
                  pl.BlockSpec((WIDTH_A, D_MODEL), const, pipeline_mode=pl.Buffered(1)),
                  pl.BlockSpec((WIDTH_B, D_MODEL), const, pipeline_mode=pl.Buffered(1)),
                  pl.BlockSpec((D_MODEL, D_MODEL), const, pipeline_mode=pl.Buffered(1))],
        out_specs=pl.BlockSpec((tm, D_MODEL), row),
        out_shape=jax.ShapeDtypeStruct((n, D_MODEL), F32),
        compiler_params=pltpu.CompilerParams(
            dimension_semantics=("arbitrary",), vmem_limit_bytes=VMEM_LIMIT_BYTES),
        name="merge",
    )(x, oa, ob, g, wg, woa, wob, wo)


def _ffn_kernel(*refs, tm, seq_len, has_prev):
    if has_prev:
        (x_ref, g_ref, wup_ref, cw_ref, cb_ref, wdn_ref, gfin_ref, e1_ref, e2_ref,
         y_ref, u_ref, act_ref) = refs
    else:
        (x_ref, g_ref, wup_ref, cw_ref, cb_ref, wdn_ref, gfin_ref,
         y_ref, u_ref, act_ref, carry_ref) = refs
    i = pl.program_id(0)
    x = x_ref[...]
    h = _rmsnorm(x, g_ref[...]).astype(BF16)
    row = lax.broadcasted_iota(jnp.int32, (tm, FF_CHUNK), 0)
    row8 = lax.broadcasted_iota(jnp.int32, (SUBLANES, FF_CHUNK), 0)
    if has_prev:
        pos = row & (seq_len - 1)
    else:
        @pl.when((i * tm) % seq_len == 0)
        def _():
            carry_ref[...] = jnp.zeros(carry_ref.shape, F32)

    for c in range(D_FF // FF_CHUNK):
        halves = []
        for part in range(2):
            cols = slice(part * D_FF + c * FF_CHUNK, part * D_FF + (c + 1) * FF_CHUNK)
            u = _dot(h, wup_ref[:, cols])
            um1 = pltpu.roll(u, 1, 0)
            um2 = pltpu.roll(u, 2, 0)
            if has_prev:
                um1 = jnp.where(pos == 0, e1_ref[:, cols], um1)
                um2 = jnp.where(pos < 2, e2_ref[:, cols], um2)
                u_ref[:, cols] = u
            else:
                prev = carry_ref[:, cols]
                top1 = jnp.where(row8 == 0, pltpu.roll(prev, 1, 0), um1[:SUBLANES])
                top2 = jnp.where(row8 < 2, pltpu.roll(prev, 2, 0), um2[:SUBLANES])
                um1 = jnp.concatenate([top1, um1[SUBLANES:]], axis=0)
                um2 = jnp.concatenate([top2, um2[SUBLANES:]], axis=0)
                carry_ref[:, cols] = u[tm - SUBLANES:]
                u_ref[:, cols] = u[tm - SUBLANES:]
            cw = cw_ref[:, cols]
            halves.append(((cb_ref[:, cols] + cw[0:1] * um2) + cw[1:2] * um1) + cw[2:3] * u)
        gate, val = halves
        act = (gate * (1.0 / (1.0 + jnp.exp(-gate)))) * val
        act_ref[:, c * FF_CHUNK:(c + 1) * FF_CHUNK] = act.astype(BF16)

    x3 = x + _dot(act_ref[...], wdn_ref[...].astype(BF16))
    y_ref[...] = _rmsnorm(x3, gfin_ref[...])


def _ffn(x, g, wup, cw, cb, wdn, gfin, prev, tm, seq_len):
    n = x.shape[0]
    has_prev = prev is not None
    row = lambda i: (i, 0)
    const = lambda i: (0, 0)
    in_specs = [pl.BlockSpec((tm, D_MODEL), row),
                pl.BlockSpec((1, D_MODEL), const),
                pl.BlockSpec((D_MODEL, 2 * D_FF), const, pipeline_mode=pl.Buffered(1)),
                pl.BlockSpec((CONV_W, 2 * D_FF), const),
                pl.BlockSpec((1, 2 * D_FF), const),
                pl.BlockSpec((D_FF, D_MODEL), const, pipeline_mode=pl.Buffered(1)),
                pl.BlockSpec((1, D_MODEL), const)]
    scratch = [pltpu.VMEM((tm, D_FF), BF16)]
    if has_prev:
        assert n == tm and tm % seq_len == 0 and seq_len & (seq_len - 1) == 0
        in_specs += [pl.BlockSpec((tm, 2 * D_FF), row)] * 2
        u_shape, u_spec = (n, 2 * D_FF), pl.BlockSpec((tm, 2 * D_FF), row)
        args = (x, g, wup, cw, cb, wdn, gfin) + tuple(prev)
    else:
        assert seq_len % tm == 0
        tiles_per_seq = seq_len // tm
        u_shape = (n // seq_len * SUBLANES, 2 * D_FF)
        u_spec = pl.BlockSpec((SUBLANES, 2 * D_FF), lambda i: (i // tiles_per_seq, 0))
        scratch.append(pltpu.VMEM((SUBLANES, 2 * D_FF), F32))
        args = (x, g, wup, cw, cb, wdn, gfin)
    return pl.pallas_call(
        functools.partial(_ffn_kernel, tm=tm, seq_len=seq_len, has_prev=has_prev),
        grid=(n // tm,),
        in_specs=in_specs,
        out_specs=[pl.BlockSpec((tm, D_MODEL), row), u_spec],
        out_shape=[jax.ShapeDtypeStruct((n, D_MODEL), F32), jax.ShapeDtypeStruct(u_shape, F32)],
        scratch_shapes=scratch,
        compiler_params=pltpu.CompilerParams(
            dimension_semantics=("arbitrary",), vmem_limit_bytes=VMEM_LIMIT_BYTES),
        name="ffn_sample" if has_prev else "ffn_prompt",
    )(*args)


def kernel(x_prompt, x_sample, cache_moba_k, cache_moba_v, cache_diff_k, cache_diff_v, state_conv, page_table, rel_bias, norm_attn, w_in, w_gate, w_out_a, w_out_b, w_out, lambda_q1, lambda_k1, lambda_q2, lambda_k2, diff_norm, norm_ffn, w_up, conv_w, conv_b, w_down, norm_final):
    batch, seq, _ = x_prompt.shape
    n_seq, tokens, _ = x_sample.shape
    depth = w_in.shape[0]
    n_phys = cache_moba_k.shape[1]
    n_pages = page_table.shape[1]
    assert depth == 1 and seq % Q_TILE == 0 and ATT_TILE & (ATT_TILE - 1) == 0
    assert (n_pages * PAGE_SIZE) % MOBA_BLOCK == 0 and n_pages % PAGES_PER_STEP == 0
    assert n_pages * PAGE_SIZE // MOBA_BLOCK <= LANES and CONV_W - 1 <= tokens <= LANES // N_HEADS_B
    l = 0
    lam_init = 0.8 - 0.6 * math.exp(-0.3 * l)

    bias_a = rel_bias[:, :N_HEADS_A].T
    bias_d = rel_bias[:, N_HEADS_A:].T
    row = lambda v: v.reshape(1, -1)
    w_in_b, w_gate_b, w_oa_b, w_ob_b, w_o_b = w_in[l], w_gate[l], w_out_a[l], w_out_b[l], w_out[l]
    w_up_b = w_up[l].astype(BF16)
    w_dn_b = w_down[l]
    lam_vecs = jnp.stack([lambda_q1[l], lambda_k1[l], lambda_q2[l], lambda_k2[l]]).astype(F32)
    gain_d = row(diff_norm[l])
    g_attn, g_ffn, g_fin = row(norm_attn[l]), row(norm_ffn[l]), row(norm_final)
    cb = row(conv_b[l])

    xp = x_prompt.reshape(batch * seq, D_MODEL)
    qa_tb, ka, va_tb, qd_tb, kd, vd_tb, ka_t, va_t, kd_r, vd_r, km = _proj(xp, g_attn, w_in_b, 512, seq)
    oa = _moba_prompt(qa_tb, ka, va_tb, km, _bias_tiles(bias_a), batch, seq)
    ob = _diff_prompt(qd_tb, kd, vd_tb, _bias_tiles(bias_d), lam_vecs, gain_d, lam_init, batch, seq)
    x2 = _merge(xp, oa, ob, g_attn, w_gate_b, w_oa_b, w_ob_b, w_o_b, 512)
    yp, tail_p = _ffn(x2, g_ffn, w_up_b, conv_w[l], cb, w_dn_b, g_fin, None, 1024, seq)
    conv_p = tail_p.reshape(batch, SUBLANES, 2 * D_FF)[:, SUBLANES - (CONV_W - 1):]

    n_s = n_seq * tokens
    xs = x_sample.reshape(n_s, D_MODEL)
    qa_s, ka_s, va_s, qd_s, kd_s, vd_s, ka_sf, va_sf, kd_sf, vd_sf = _proj(xs, g_attn, w_in_b, n_s)

    pool_t = lambda c: jnp.transpose(c[l], (0, 2, 3, 1)).reshape(n_phys, WIDTH_A, PAGE_SIZE)
    pool_r = lambda c: c[l].reshape(n_phys, PAGE_SIZE * N_HEADS_B, 2 * HEAD_DIM)
    far_a, last_a, new_a = _sample_bias(bias_a, tokens, 1)
    oa_s = _moba_sample(page_table, pool_t(cache_moba_k), pool_t(cache_moba_v),
                        _block_diag(qa_s, n_seq, tokens, N_HEADS_A, N_HEADS_A),
                        _pad_rows(ka_s, n_seq), _pad_rows(va_s, n_seq),
                        jnp.stack([far_a, last_a, new_a]), tokens=tokens)
    far_d, last_d, new_d = _sample_bias(bias_d, tokens, 2)
    groups_d = 2 * N_HEADS_B
    spread = lambda x: _spread_heads(x, N_HEADS_B, groups_d)
    per_head = lambda x: x.reshape(n_s * N_HEADS_B, 2 * HEAD_DIM)
    oa_d = _diff_sample(page_table, pool_r(cache_diff_k), pool_r(cache_diff_v),
                        _block_diag(qd_s, n_seq, tokens, groups_d, 2),
                        _pad_rows(per_head(kd_s), n_seq), _pad_rows(per_head(vd_s), n_seq),
                        jnp.stack([spread(far_d), spread(last_d)]),
                        spread(new_d[:, :LANES // N_HEADS_B]),
                        lam_vecs, gain_d, tokens=tokens, lam_init=lam_init)
    x2s = _merge(xs, oa_s.reshape(n_s, WIDTH_A), oa_d.reshape(n_s, WIDTH_B), g_attn,
                 w_gate_b, w_oa_b, w_ob_b, w_o_b, n_s)
    st = state_conv[l]
    zero = jnp.zeros((n_seq, 1, 2 * D_FF), F32)
    e1 = jnp.concatenate([st[:, 1:2]] + [zero] * (tokens - 1), axis=1).reshape(n_s, 2 * D_FF)
    e2 = jnp.concatenate([st[:, 0:2]] + [zero] * (tokens - 2), axis=1).reshape(n_s, 2 * D_FF)
    ys, u_s = _ffn(x2s, g_ffn, w_up_b, conv_w[l], cb, w_dn_b, g_fin, (e1, e2), n_s, tokens)
    conv_s = u_s.reshape(n_seq, tokens, 2 * D_FF)[:, tokens - (CONV_W - 1):]

    shp_a = lambda a, b_, t_: a.reshape(1, b_, t_, N_HEADS_A, HEAD_DIM)
    shp_d = lambda a, b_, t_: a.reshape(1, b_, t_, N_HEADS_B, 2 * HEAD_DIM)
    untr = lambda a: a.reshape(batch, N_HEADS_A, HEAD_DIM, seq).transpose(0, 3, 1, 2)[None]
    return (yp.reshape(batch, seq, D_MODEL), ys.reshape(n_seq, tokens, D_MODEL),
            untr(ka_t), untr(va_t),
            shp_d(kd_r, batch, seq), shp_d(vd_r, batch, seq), conv_p[None],
            shp_a(ka_sf, n_seq, tokens), shp_a(va_sf, n_seq, tokens),
            shp_d(kd_sf, n_seq, tokens), shp_d(vd_sf, n_seq, tokens), conv_s[None])
```

```python
import functools
import math

import jax
import jax.numpy as jnp
from jax import lax
from jax.experimental import pallas as pl
from jax.experimental.pallas import tpu as pltpu

F32 = jnp.float32
BF16 = jnp.bfloat16

D_MODEL = 1024
HEAD_DIM = 64
N_HEADS_A = D_MODEL // 128
N_HEADS_B = D_MODEL // 256
WIDTH_A = N_HEADS_A * HEAD_DIM
WIDTH_B = N_HEADS_B * 2 * HEAD_DIM
N_IN = 3 * WIDTH_A + 3 * WIDTH_B
MOBA_BLOCK = 256
MOBA_TOPK = 3
NUM_BUCKETS = 32
MAX_DISTANCE = 128
D_FF = ((8 * D_MODEL // 3 + 127) // 128) * 128
CONV_W = 3
EPS = 1e-6
PAGE_SIZE = 128
SCALE = HEAD_DIM ** -0.5
LOG2E = math.log2(math.e)

LANES = 128
SUBLANES = 8
VMEM_LIMIT_BYTES = 56 * 1024 * 1024

ATT_TILE = MOBA_BLOCK
Q_TILE = 2 * ATT_TILE
ATT_PAIRS = 2
ATT_RING = 2
FF_CHUNK = 256
PAGES_PER_STEP = 16
RING_SLOTS = 4
NEG_BIG = -1e30


def _dot(a, b):
    return jnp.dot(a, b, preferred_element_type=F32)


def _dot_nt(a, b):
    return lax.dot_general(a, b, (((1,), (1,)), ((), ())), preferred_element_type=F32)


def _rmsnorm(x, g):
    return (x * lax.rsqrt(jnp.mean(x * x, axis=-1, keepdims=True) + EPS)) * g


def _lane_fold(op, x):
    out = x[:, :LANES]
    for c in range(1, x.shape[1] // LANES):
        out = op(out, x[:, c * LANES:(c + 1) * LANES])
    return out


def _t5_bucket(dist):
    n = jnp.maximum(dist, 0)
    max_exact = NUM_BUCKETS // 2
    nf = jnp.maximum(n, 1).astype(F32)
    large = max_exact + (jnp.log(nf / max_exact) / math.log(MAX_DISTANCE / max_exact)
                         * (NUM_BUCKETS - max_exact)).astype(jnp.int32)
    large = jnp.minimum(large, NUM_BUCKETS - 1)
    return jnp.where(n < max_exact, n, large)


def _bias_of(bias_t, dist):
    onehot = jax.nn.one_hot(_t5_bucket(dist), NUM_BUCKETS, dtype=F32)
    return jnp.einsum('hb,...b->h...', bias_t.astype(F32), onehot, precision=lax.Precision.HIGHEST)


def _lambda(lam_ref, lam_init):
    lv = lam_ref[...]
    a = jnp.sum(lv[0:1] * lv[1:2], axis=-1, keepdims=True)
    b = jnp.sum(lv[2:3] * lv[3:4], axis=-1, keepdims=True)
    return jnp.exp(a) - jnp.exp(b) + lam_init


def _proj_kernel(x_ref, g_ref, w_ref,
                 qa_ref, ka_ref, va_ref, qd_ref, kd_ref, vd_ref,
                 kaf_ref, vaf_ref, kdf_ref, vdf_ref, *km_ref, prompt):
    tm = x_ref.shape[0]
    h = _rmsnorm(x_ref[...], g_ref[...]).astype(BF16)
    outs = ((qa_ref, None, SCALE), (ka_ref, kaf_ref, None), (va_ref, vaf_ref, None),
            (qd_ref, None, SCALE), (kd_ref, kdf_ref, None), (vd_ref, vdf_ref, None))
    for c, (b_ref, f_ref, scale) in enumerate(outs):
        u = _dot(h, w_ref[:, c * WIDTH_A:(c + 1) * WIDTH_A].astype(BF16))
        if scale is not None:
            u = u * (scale * LOG2E if prompt else scale)
        if not prompt:
            if f_ref is not None:
                f_ref[...] = u
            b_ref[...] = u.astype(BF16)
            continue
        is_key = c in (1, 4)
        ut = u.T if c != 4 else None
        if c in (1, 2):
            f_ref[0] = ut
        elif f_ref is not None:
            for hd in range(N_HEADS_B):
                f_ref[pl.ds(hd, tm, stride=N_HEADS_B), :] = u[:, hd * LANES:(hd + 1) * LANES]
        if is_key:
            b_ref[...] = u.astype(BF16)
        else:
            b_ref[0] = ut.astype(BF16)
        if c == 1:
            for j in range(tm // MOBA_BLOCK):
                km_ref[0][j] = jnp.sum(u[j * MOBA_BLOCK:(j + 1) * MOBA_BLOCK], axis=0,
                                       keepdims=True) * (1.0 / MOBA_BLOCK)


def _proj(x, g, w_in, tm, seq=None):
    n = x.shape[0]
    row = lambda i: (i, 0)
    const = lambda i: (0, 0)
    blk = pl.BlockSpec((tm, WIDTH_A), row)
    b_specs = [blk] * 6
    b_shapes = [jax.ShapeDtypeStruct((n, WIDTH_A), BF16)] * 6
    f_specs = [blk] * 4
    f_shapes = [jax.ShapeDtypeStruct((n, WIDTH_A), F32)] * 4
    if seq is not None:
        assert seq % tm == 0 and tm % MOBA_BLOCK == 0 and 2 * HEAD_DIM == LANES
        tiles = seq // tm
        t_spec = pl.BlockSpec((1, WIDTH_A, tm), lambda i: (i // tiles, 0, i % tiles))
        r_spec = pl.BlockSpec((tm * N_HEADS_B, LANES), row)
        r_shape = jax.ShapeDtypeStruct((n * N_HEADS_B, LANES), F32)
        t_shape = lambda dt: jax.ShapeDtypeStruct((n // seq, WIDTH_A, seq), dt)
        b_specs = [t_spec, blk, t_spec, t_spec, blk, t_spec]
        b_shapes = [t_shape(BF16), b_shapes[0], t_shape(BF16), t_shape(BF16), b_shapes[0], t_shape(BF16)]
        f_specs = [t_spec, t_spec, r_spec, r_spec,
                   pl.BlockSpec((tm // MOBA_BLOCK, 1, WIDTH_A), lambda i: (i, 0, 0))]
        f_shapes = [t_shape(F32), t_shape(F32), r_shape, r_shape,
                    jax.ShapeDtypeStruct((n // MOBA_BLOCK, 1, WIDTH_A), F32)]
    return pl.pallas_call(
        functools.partial(_proj_kernel, prompt=seq is not None),
        grid=(n // tm,),
        in_specs=[pl.BlockSpec((tm, D_MODEL), row),
                  pl.BlockSpec((1, D_MODEL), const),
                  pl.BlockSpec((D_MODEL, N_IN), const, pipeline_mode=pl.Buffered(1))],
        out_specs=b_specs + f_specs,
        out_shape=b_shapes + f_shapes,
        compiler_params=pltpu.CompilerParams(
            dimension_semantics=("arbitrary",), vmem_limit_bytes=VMEM_LIMIT_BYTES),
        name="proj",
    )(x, g, w_in)


def _row_fold(op, x):
    y = x.reshape(x.shape[0] // SUBLANES, SUBLANES, x.shape[1])
    out = y[0]
    for g in range(1, y.shape[0]):
        out = op(out, y[g])
    return out


def _attend_static(qt_ops, k_tile, vt_ref, vt_rows, bias_ref, bidx, scratch, qi):
    t = ATT_TILE
    tq = Q_TILE
    halves = tq // t
    trips = qi + 1
    outs = [None] * len(qt_ops)

    def pass1(i):
        s_ref, mb_ref = scratch[i % len(scratch)]
        mx = [None] * halves
        k_tiles = [k_tile(i, jj) for jj in range(trips)]
        for hq in range(halves):
            cols = slice(hq * t, (hq + 1) * t)
            qt_cols = qt_ops[i][:, cols]
            for jj in range(trips):
                for hk in range(halves):
                    dist = (halves * qi + hq) - (halves * jj + hk)
                    if dist < 0:
                        continue
                    blk = _dot(k_tiles[jj][hk * t:(hk + 1) * t], qt_cols)
                    if dist < 2:
                        blk = blk + bias_ref[bidx[i], dist]
                    s_ref[jj, hk * t:(hk + 1) * t, cols] = blk
                    f = _row_fold(jnp.maximum, blk)
                    mx[hq] = f if mx[hq] is None else jnp.maximum(mx[hq], f)
                    yield
        m = jnp.max(jnp.concatenate(mx, axis=1), axis=0, keepdims=True)
        mb_ref[...] = jnp.broadcast_to(m, (SUBLANES, tq))

    def pass2(i):
        s_ref, mb_ref = scratch[i % len(scratch)]
        accs, sums = [], []
        for hq in range(halves):
            cols = slice(hq * t, (hq + 1) * t)
            mb = mb_ref[:, cols]
            l = jnp.zeros((SUBLANES, t), F32)
            acc = None
            n_keys = (halves * qi + hq + 1) * t
            for kb in range(n_keys // t):
                jj, hk = divmod(kb, halves)
                sv = s_ref[jj, hk * t:(hk + 1) * t, cols].reshape(t // SUBLANES, SUBLANES, t)
                p = jnp.exp2(sv - mb[None]).reshape(t, t)
                l = l + _row_fold(jnp.add, p)
                part = _dot(vt_ref[0, vt_rows[i], kb * t:(kb + 1) * t], p.astype(BF16))
                acc = part if acc is None else acc + part
                yield
            accs.append(acc)
            sums.append(l)
        outs[i] = (jnp.concatenate(accs, axis=1),
                   jnp.sum(jnp.concatenate(sums, axis=1), axis=0, keepdims=True))

    for _ in pass1(0):
        pass
    for i in range(1, len(qt_ops)):
        streams = [pass1(i), pass2(i - 1)]
        while streams:
            streams = [g for g in streams if next(g, StopIteration) is not StopIteration]
    for _ in pass2(len(qt_ops) - 1):
        pass
    return outs


def _per_query_tile(nq, body):
    qi = pl.program_id(2)
    for n in range(nq):
        pl.when(qi == n)(functools.partial(body, n))


def _half_rows(qt, half):
    zero = jnp.zeros((HEAD_DIM, qt.shape[1]), qt.dtype)
    if half == 0:
        return jnp.concatenate([qt[:HEAD_DIM], zero], axis=0)
    return jnp.concatenate([zero, qt[HEAD_DIM:]], axis=0)


def _scratch_ring(refs):
    return [tuple(refs[2 * r:2 * r + 2]) for r in range(ATT_RING)]


def _moba_prompt_kernel(qt_ref, k_ref, vt_ref, km_ref, bias_ref, o_ref, *scratch, nq):
    t = ATT_TILE
    tq = Q_TILE
    tile_shift = t.bit_length() - 1
    nb = km_ref.shape[0]
    qi = pl.program_id(2)
    blk = lax.broadcasted_iota(jnp.int32, (nb, tq), 0)
    own = (tq // t) * qi + (lax.broadcasted_iota(jnp.int32, (nb, tq), 1) >> tile_shift)

    qt_ops = []
    for pair in range(ATT_PAIRS):
        rows = slice(pair * LANES, (pair + 1) * LANES)
        qt = qt_ref[0, rows, :]
        km = km_ref[:, 0, rows]
        km_hi = km.astype(BF16)
        km_lo = (km - km_hi.astype(F32)).astype(BF16)
        for e in range(2):
            qet = _half_rows(qt, e)
            gt = _dot(km_hi, qet) + _dot(km_lo, qet)
            rank = jnp.zeros((nb, tq), jnp.int32)
            for m in range(nb):
                gm = gt[m:m + 1, :]
                ahead = (gm > gt) | ((gm == gt) & (m < blk))
                rank = rank + jnp.where(ahead & (m < own), 1, 0)
            keep = (blk < own) & (rank < MOBA_TOPK) & (jnp.abs(gt) < jnp.inf)
            pen = jnp.where(keep | (blk == own), 0.0, NEG_BIG)
            pen = jnp.concatenate([pen, jnp.zeros((LANES - nb, tq), F32)], axis=0)
            qt_ops.append(jnp.concatenate([qet, pen.astype(BF16)], axis=0))

    k_lane = lax.broadcasted_iota(jnp.int32, (tq, LANES), 1)
    k_blk = lax.broadcasted_iota(jnp.int32, (tq, LANES), 0) >> tile_shift

    def k_tile(i, jj):
        pair = i // 2
        onehot = jnp.where(k_lane == (tq // t) * jj + k_blk, 1.0, 0.0).astype(BF16)
        return jnp.concatenate(
            [k_ref[jj * tq:(jj + 1) * tq, pair * LANES:(pair + 1) * LANES], onehot], axis=1)

    n_ops = 2 * ATT_PAIRS
    vt_rows = [slice(i * HEAD_DIM, (i + 1) * HEAD_DIM) for i in range(n_ops)]

    def body(n):
        res = _attend_static(qt_ops, k_tile, vt_ref, vt_rows, bias_ref, tuple(range(n_ops)),
                             _scratch_ring(scratch), n)
        for pair in range(ATT_PAIRS):
            (a0, l0), (a1, l1) = res[2 * pair], res[2 * pair + 1]
            ot = jnp.concatenate([a0 / l0, a1 / l1], axis=0)
            o_ref[:, pair * LANES:(pair + 1) * LANES] = ot.T.astype(o_ref.dtype)

    _per_query_tile(nq, body)


def _attn_scratch(nq):
    one = [pltpu.VMEM((nq, Q_TILE, Q_TILE), F32),
           pltpu.VMEM((SUBLANES, Q_TILE), F32)]
    return one * ATT_RING


def _attn_specs(w, seq, nq):
    return ([pl.BlockSpec((1, w, Q_TILE), lambda b, g, qi: (b, g, qi)),
             pl.BlockSpec((seq, w), lambda b, g, qi: (b, g)),
             pl.BlockSpec((1, w, seq), lambda b, g, qi: (b, g, 0))],
            pl.BlockSpec((Q_TILE, w), lambda b, g, qi: (b * nq + qi, g)))


def _moba_prompt(qa_t, ka, va_t, km, bias_tiles, batch, seq):
    nq = seq // Q_TILE
    nb = seq // MOBA_BLOCK
    w = ATT_PAIRS * LANES
    assert WIDTH_A % w == 0
    in_specs, out_spec = _attn_specs(w, seq, nq)
    return pl.pallas_call(
        functools.partial(_moba_prompt_kernel, nq=nq),
        grid=(batch, WIDTH_A // w, nq),
        in_specs=in_specs + [pl.BlockSpec((nb, 1, w), lambda b, g, qi: (b, 0, g)),
                             pl.BlockSpec((2 * ATT_PAIRS, 2, ATT_TILE, ATT_TILE),
                                          lambda b, g, qi: (g, 0, 0, 0))],
        out_specs=out_spec,
        out_shape=jax.ShapeDtypeStruct((batch * seq, WIDTH_A), BF16),
        scratch_shapes=_attn_scratch(nq),
        compiler_params=pltpu.CompilerParams(
            dimension_semantics=("arbitrary", "arbitrary", "arbitrary"),
            vmem_limit_bytes=VMEM_LIMIT_BYTES),
        name="moba_prompt",
    )(qa_t, ka, va_t, km, bias_tiles)


def _diff_prompt_kernel(qt_ref, k_ref, vt_ref, bias_ref, lam_ref, gain_ref, o_ref, *scratch,
                        lam_init, nq):
    tq = Q_TILE
    qt_ops = []
    for head in range(ATT_PAIRS):
        qt = qt_ref[0, head * LANES:(head + 1) * LANES, :]
        qt_ops += [_half_rows(qt, c) for c in range(2)]

    def k_tile(i, jj):
        head = i // 2
        return k_ref[jj * tq:(jj + 1) * tq, head * LANES:(head + 1) * LANES]

    n_ops = 2 * ATT_PAIRS
    vt_rows = [slice((i // 2) * LANES, (i // 2 + 1) * LANES) for i in range(n_ops)]

    def body(n):
        res = _attend_static(qt_ops, k_tile, vt_ref, vt_rows, bias_ref,
                             tuple(i // 2 for i in range(n_ops)), _scratch_ring(scratch), n)
        lam = _lambda(lam_ref, lam_init)
        for head in range(ATT_PAIRS):
            (a0, l0), (a1, l1) = res[2 * head], res[2 * head + 1]
            o = (a0 / l0 - lam * (a1 / l1)).T
            o_ref[:, head * LANES:(head + 1) * LANES] = (
                _rmsnorm(o, gain_ref[...]) * (1.0 - lam_init)).astype(o_ref.dtype)

    _per_query_tile(nq, body)


def _diff_prompt(qd_t, kd, vd_t, bias_tiles, lam_vecs, gain, lam_init, batch, seq):
    nq = seq // Q_TILE
    w = ATT_PAIRS * LANES
    assert WIDTH_B % w == 0
    in_specs, out_spec = _attn_specs(w, seq, nq)
    return pl.pallas_call(
        functools.partial(_diff_prompt_kernel, lam_init=lam_init, nq=nq),
        grid=(batch, WIDTH_B // w, nq),
        in_specs=in_specs + [
                  pl.BlockSpec((ATT_PAIRS, 2, ATT_TILE, ATT_TILE), lambda b, g, qi: (g, 0, 0, 0)),
                  pl.BlockSpec((4, HEAD_DIM), lambda b, g, qi: (0, 0)),
                  pl.BlockSpec((1, 2 * HEAD_DIM), lambda b, g, qi: (0, 0))],
        out_specs=out_spec,
        out_shape=jax.ShapeDtypeStruct((batch * seq, WIDTH_B), BF16),
        scratch_shapes=_attn_scratch(nq),
        compiler_params=pltpu.CompilerParams(
            dimension_semantics=("arbitrary", "arbitrary", "arbitrary"),
            vmem_limit_bytes=VMEM_LIMIT_BYTES),
        name="diff_prompt",
    )(qd_t, kd, vd_t, bias_tiles, lam_vecs, gain)


def _toeplitz(w, t):
    h = w.shape[0]
    m = jnp.broadcast_to(w[:, None, :], (h, t, 2 * t)).reshape(h, 2 * t * t)
    return m[:, :t * (2 * t - 1)].reshape(h, t, 2 * t - 1)[:, :, :t]


def _bias_tiles(bias_t):
    t = ATT_TILE
    assert t + 1 >= MAX_DISTANCE
    k = jnp.arange(2 * t)
    d = jnp.where(k < t, k, k - 2 * t)
    far = _bias_of(bias_t, jnp.full((1,), 2 * t))
    diag = _toeplitz(jnp.where(d >= 0, (_bias_of(bias_t, d) - far) * LOG2E, -jnp.inf), t)
    sub = _toeplitz((_bias_of(bias_t, d + t) - far) * LOG2E, t)
    return jnp.stack([diag, sub], axis=1)


def _page_stream(pt_ref, pk_ref, pv_ref, buf_ref, sem_ref, n_seq, n_chunks):
    pps = PAGES_PER_STEP
    ahead = RING_SLOTS - 1
    b, ph, j = pl.program_id(0), pl.program_id(1), pl.program_id(2)
    step = (b * 2 + ph) * n_chunks + j
    total = n_seq * 2 * n_chunks

    def copies(pool_ref, bb, jj, slot):
        return [pltpu.make_async_copy(pool_ref.at[pt_ref[bb, jj * pps + i]],
                                      buf_ref.at[slot, i], sem_ref.at[slot])
                for i in range(pps)]

    def start(chunk):
        jj, seq_phase = chunk % n_chunks, chunk // n_chunks
        for phase, pool_ref in ((0, pk_ref), (1, pv_ref)):
            @pl.when(seq_phase % 2 == phase)
            def _(pool_ref=pool_ref):
                for i, cp in enumerate(copies(pool_ref, seq_phase // 2, jj, chunk % RING_SLOTS)):
                    cp.start(priority=i % 2)

    @pl.when(step == 0)
    def _():
        for chunk in range(min(ahead, total)):
            start(jnp.int32(chunk))

    @pl.when(step + ahead < total)
    def _():
        start(step + ahead)

    slot = step % RING_SLOTS

    def wait(pool_ref):
        for cp in copies(pool_ref, b, j, slot):
            cp.wait()

    return slot, wait


def _moba_sample_kernel(pt_ref, pk_ref, pv_ref, q_ref, kn_ref, vn_ref, bias_ref, o_ref,
                        s_ref, snew_ref, acc_ref, l_ref, m_ref, bmax_ref, gate_ref, sel_ref,
                        buf_ref, sem_ref, *, n_seq, n_chunks, tokens):
    pps = PAGES_PER_STEP
    slot, wait = _page_stream(pt_ref, pk_ref, pv_ref, buf_ref, sem_ref, n_seq, n_chunks)
    rows = q_ref.shape[1]
    groups = rows // tokens
    ppb = MOBA_BLOCK // PAGE_SIZE
    bps = pps // ppb
    n_blocks = n_chunks * bps
    ph = pl.program_id(1)
    j = pl.program_id(2)
    lane = lax.broadcasted_iota(jnp.int32, (rows, LANES), 1)
    lanef = lane.astype(F32)
    q = q_ref[0]

    @pl.when((ph == 0) & (j == 0))
    def _():
        snew_ref[...] = _dot_nt(q, kn_ref[0]) + bias_ref[2]
        bmax_ref[...] = jnp.full((rows, LANES), -jnp.inf, F32)
        gate_ref[...] = jnp.zeros((rows, LANES), F32)

    @pl.when(ph == 0)
    def _():
        wait(pk_ref)
        is_last = j == n_chunks - 1
        bmax = bmax_ref[...]
        gate = gate_ref[...]
        for b in range(bps):
            smax = ssum = None
            for pg in range(ppb):
                i = b * ppb + pg
                raw = _dot(q, buf_ref[slot, i].astype(BF16))
                bias = bias_ref[0]
                if i == pps - 1:
                    bias = jnp.where(is_last, bias_ref[1], bias)
                s = raw + bias
                s_ref[j, :, i * PAGE_SIZE:(i + 1) * PAGE_SIZE] = s
                smax = s if smax is None else jnp.maximum(smax, s)
                ssum = raw if ssum is None else ssum + raw
            blk = j * bps + b
            bmax = jnp.where(lane == blk, jnp.max(smax, axis=1, keepdims=True), bmax)
            gate = jnp.where(lane == blk, jnp.sum(ssum, axis=1, keepdims=True), gate)
        bmax_ref[...] = bmax
        gate_ref[...] = gate

    @pl.when((ph == 1) & (j == 0))
    def _():
        g = jnp.where(lane < n_blocks, gate_ref[...], -jnp.inf)
        sel = jnp.zeros((rows, LANES), F32)
        for _ in range(MOBA_TOPK):
            top = jnp.max(g, axis=1, keepdims=True)
            first = jnp.min(jnp.where(g == top, lanef, float(LANES)), axis=1, keepdims=True)
            pick = lanef == first
            finite = jnp.where(jnp.abs(top) < jnp.inf, 1.0, 0.0)
            sel = jnp.maximum(sel, jnp.where(pick, finite, 0.0))
            g = jnp.where(pick, -jnp.inf, g)
        sel_ref[...] = sel
        m_sel = jnp.max(jnp.where(sel > 0.5, bmax_ref[...], -jnp.inf), axis=1, keepdims=True)
        m_new = jnp.max(snew_ref[...], axis=1, keepdims=True)
        m_ref[...] = jnp.broadcast_to(jnp.maximum(m_sel, m_new), (rows, LANES))
        acc_ref[...] = jnp.zeros(acc_ref.shape, F32)
        l_ref[...] = jnp.zeros(l_ref.shape, F32)

    @pl.when(ph == 1)
    def _():
        wait(pv_ref)
        m = m_ref[...]
        selv = sel_ref[...]
        acc = acc_ref[...]
        l = l_ref[...]
        for i in range(pps):
            blk = j * bps + i // ppb
            on = jnp.max(jnp.where(lane == blk, selv, 0.0), axis=1, keepdims=True)
            on = jnp.broadcast_to(on, (rows, LANES)) > 0.5
            s = s_ref[j, :, i * PAGE_SIZE:(i + 1) * PAGE_SIZE]
            p = jnp.exp(jnp.where(on, s - m, -jnp.inf))
            l = l + p
            acc = acc + _dot_nt(p.astype(BF16), buf_ref[slot, i].astype(BF16))
        acc_ref[...] = acc
        l_ref[...] = l

    @pl.when((ph == 1) & (j == n_chunks - 1))
    def _():
        p = jnp.exp(snew_ref[...] - m_ref[...])
        acc = acc_ref[...] + _dot(p.astype(BF16), vn_ref[0])
        l = jnp.sum(l_ref[...] + p, axis=1, keepdims=True)
        o = acc / l
        grp = lax.broadcasted_iota(jnp.int32, (groups, WIDTH_A), 0)
        col = lax.broadcasted_iota(jnp.int32, (groups, WIDTH_A), 1)
        head_shift = HEAD_DIM.bit_length() - 1
        w = jnp.where((col >> head_shift) == grp, 1.0, 0.0)
        tok = jnp.sum(o.reshape(tokens, groups, WIDTH_A) * w[None], axis=1)
        o_ref[0] = tok.astype(o_ref.dtype)


def _moba_sample(page_table, pool_kt, pool_vt, q_bd, k_new, v_new, bias_rows, *, tokens):
    pps = PAGES_PER_STEP
    n_seq, n_pages = page_table.shape
    n_chunks = n_pages // pps
    rows = q_bd.shape[1]
    seq_map = lambda b, ph, j, pt: (b, 0, 0)
    hbm = pl.BlockSpec(memory_space=pl.ANY)
    in_specs = [hbm, hbm,
                pl.BlockSpec((1, rows, WIDTH_A), seq_map),
                pl.BlockSpec((1, LANES, WIDTH_A), seq_map),
                pl.BlockSpec((1, LANES, WIDTH_A), seq_map),
                pl.BlockSpec((3, rows, LANES), lambda b, ph, j, pt: (0, 0, 0))]
    vec = pltpu.VMEM((rows, LANES), F32)
    grid_spec = pltpu.PrefetchScalarGridSpec(
        num_scalar_prefetch=1,
        grid=(n_seq, 2, n_chunks),
        in_specs=in_specs,
        out_specs=pl.BlockSpec((1, tokens, WIDTH_A), seq_map),
        scratch_shapes=[pltpu.VMEM((n_chunks, rows, pps * PAGE_SIZE), F32),
                        vec,
                        pltpu.VMEM((rows, WIDTH_A), F32),
                        vec, vec, vec, vec, vec,
                        pltpu.VMEM((RING_SLOTS, pps) + pool_kt.shape[1:], F32),
                        pltpu.SemaphoreType.DMA((RING_SLOTS,))])
    return pl.pallas_call(
        functools.partial(_moba_sample_kernel, n_seq=n_seq, n_chunks=n_chunks, tokens=tokens),
        grid_spec=grid_spec,
        out_shape=jax.ShapeDtypeStruct((n_seq, tokens, WIDTH_A), BF16),
        compiler_params=pltpu.CompilerParams(
            dimension_semantics=("arbitrary", "arbitrary", "arbitrary"),
            vmem_limit_bytes=VMEM_LIMIT_BYTES),
        name="moba_sample",
    )(page_table, pool_kt, pool_vt, q_bd, k_new, v_new, bias_rows)


def _diff_sample_kernel(pt_ref, pk_ref, pv_ref, q_ref, kn_ref, vn_ref, bias_ref, bnew_ref,
                        lam_ref, gain_ref, o_ref, s_ref, snew_ref, acc_ref, l_ref, m_ref,
                        buf_ref, sem_ref, *, n_seq, n_chunks, tokens, lam_init):
    pps = PAGES_PER_STEP
    slot, wait = _page_stream(pt_ref, pk_ref, pv_ref, buf_ref, sem_ref, n_seq, n_chunks)
    rows = q_ref.shape[1]
    pw = buf_ref.shape[2]
    ph = pl.program_id(1)
    j = pl.program_id(2)
    q = q_ref[0]

    @pl.when((ph == 0) & (j == 0))
    def _():
        snew_ref[...] = _dot_nt(q, kn_ref[0]) + bnew_ref[...]
        m_ref[...] = jnp.full((rows, LANES), -jnp.inf, F32)

    @pl.when(ph == 0)
    def _():
        wait(pk_ref)
        is_last = j == n_chunks - 1
        mx = m_ref[...]
        for i in range(pps):
            bias = bias_ref[0]
            if i == pps - 1:
                bias = jnp.where(is_last, bias_ref[1], bias)
            s = _dot_nt(q, buf_ref[slot, i].astype(BF16)) + bias
            s_ref[j, :, i * pw:(i + 1) * pw] = s
            mx = jnp.maximum(mx, _lane_fold(jnp.maximum, s))
        m_ref[...] = mx

    @pl.when((ph == 1) & (j == 0))
    def _():
        m = jnp.maximum(jnp.max(m_ref[...], axis=1, keepdims=True),
                        jnp.max(snew_ref[...], axis=1, keepdims=True))
        m_ref[...] = jnp.broadcast_to(m, (rows, LANES))
        acc_ref[...] = jnp.zeros(acc_ref.shape, F32)
        l_ref[...] = jnp.zeros(l_ref.shape, F32)

    @pl.when(ph == 1)
    def _():
        wait(pv_ref)
        m = m_ref[...]
        mw = jnp.concatenate([m] * (pw // LANES), axis=1)
        acc = acc_ref[...]
        l = l_ref[...]
        for i in range(pps):
            p = jnp.exp(s_ref[j, :, i * pw:(i + 1) * pw] - mw)
            l = l + _lane_fold(jnp.add, p)
            acc = acc + _dot(p.astype(BF16), buf_ref[slot, i].astype(BF16))
        acc_ref[...] = acc
        l_ref[...] = l

    @pl.when((ph == 1) & (j == n_chunks - 1))
    def _():
        p = jnp.exp(snew_ref[...] - m_ref[...])
        acc = acc_ref[...] + _dot(p.astype(BF16), vn_ref[0])
        l = jnp.sum(l_ref[...] + p, axis=1, keepdims=True)
        o = (acc / l).reshape(tokens, rows // tokens, LANES)
        lam = _lambda(lam_ref, lam_init)
        sub = lax.broadcasted_iota(jnp.int32, (rows // tokens, LANES), 0)
        gain = gain_ref[...]
        parts = []
        for h in range(N_HEADS_B):
            w = jnp.where(sub == 2 * h, 1.0, jnp.where(sub == 2 * h + 1, -lam, 0.0))
            parts.append(_rmsnorm(jnp.sum(o * w[None], axis=1), gain) * (1.0 - lam_init))
        o_ref[0] = jnp.concatenate(parts, axis=1).astype(o_ref.dtype)


def _diff_sample(page_table, pool_k, pool_v, q2, k_new, v_new, bias_rows, bias_new, lam_vecs, gain,
                 *, tokens, lam_init):
    pps = PAGES_PER_STEP
    n_seq, n_pages = page_table.shape
    n_chunks = n_pages // pps
    rows = q2.shape[1]
    pw = pool_k.shape[1]
    seq_map = lambda b, ph, j, pt: (b, 0, 0)
    const2 = lambda b, ph, j, pt: (0, 0)
    hbm = pl.BlockSpec(memory_space=pl.ANY)
    in_specs = [hbm, hbm,
                pl.BlockSpec((1, rows, LANES), seq_map),
                pl.BlockSpec((1, LANES, LANES), seq_map),
                pl.BlockSpec((1, LANES, LANES), seq_map),
                pl.BlockSpec((2, rows, pw), lambda b, ph, j, pt: (0, 0, 0)),
                pl.BlockSpec((rows, LANES), const2),
                pl.BlockSpec((4, HEAD_DIM), const2),
                pl.BlockSpec((1, 2 * HEAD_DIM), const2)]
    vec = pltpu.VMEM((rows, LANES), F32)
    grid_spec = pltpu.PrefetchScalarGridSpec(
        num_scalar_prefetch=1,
        grid=(n_seq, 2, n_chunks),
        in_specs=in_specs,
        out_specs=pl.BlockSpec((1, tokens, WIDTH_B), seq_map),
        scratch_shapes=[pltpu.VMEM((n_chunks, rows, pps * pw), F32),
                        vec, vec, vec, vec,
                        pltpu.VMEM((RING_SLOTS, pps) + pool_k.shape[1:], F32),
                        pltpu.SemaphoreType.DMA((RING_SLOTS,))])
    return pl.pallas_call(
        functools.partial(_diff_sample_kernel, n_seq=n_seq, n_chunks=n_chunks, tokens=tokens,
                          lam_init=lam_init),
        grid_spec=grid_spec,
        out_shape=jax.ShapeDtypeStruct((n_seq, tokens, WIDTH_B), BF16),
        compiler_params=pltpu.CompilerParams(
            dimension_semantics=("arbitrary", "arbitrary", "arbitrary"),
            vmem_limit_bytes=VMEM_LIMIT_BYTES),
        name="diff_sample",
    )(page_table, pool_k, pool_v, q2, k_new, v_new, bias_rows, bias_new, lam_vecs, gain)


def _block_diag(q, n_seq, tokens, groups, keep):
    w = q.shape[1] // groups
    q4 = q.reshape(n_seq, tokens, groups, 1, w)
    eye = (jnp.arange(groups)[:, None] % keep == jnp.arange(keep)[None, :]).astype(q.dtype)
    return (q4 * eye.reshape(1, 1, groups, keep, 1)).reshape(n_seq, tokens * groups, keep * w)


def _pad_rows(x, n_seq):
    x = x.reshape(n_seq, -1, x.shape[-1])
    return jnp.pad(x, ((0, 0), (0, LANES - x.shape[1]), (0, 0)))


def _sample_bias(bias_t, tokens, heads_per_group):
    assert PAGE_SIZE + 1 >= MAX_DISTANCE
    tok = jnp.arange(tokens)[:, None]
    col = jnp.arange(LANES)[None, :]
    far = _bias_of(bias_t, jnp.broadcast_to(2 * PAGE_SIZE, (tokens, LANES)))
    last = _bias_of(bias_t, PAGE_SIZE + tok - col)
    new = jnp.where((col <= tok) & (col < tokens), _bias_of(bias_t, tok - col), -jnp.inf)

    def rows(x):
        x = jnp.repeat(x.transpose(1, 0, 2), heads_per_group, axis=1)
        return x.reshape(-1, LANES)

    return rows(far), rows(last), rows(new)


def _spread_heads(x, n_heads, groups):
    r, k = x.shape
    row_head = (jnp.arange(r) % groups) // (groups // n_heads)
    own = row_head[:, None, None] == jnp.arange(n_heads)[None, None, :]
    return jnp.where(own, x[:, :, None], -jnp.inf).reshape(r, k * n_heads)


def _merge_kernel(x_ref, oa_ref, ob_ref, g_ref, wg_ref, woa_ref, wob_ref, wo_ref, o_ref):
    x = x_ref[...]
    h = _rmsnorm(x, g_ref[...]).astype(BF16)
    ga = 1.0 / (1.0 + jnp.exp(-_dot(h, wg_ref[:, :D_MODEL].astype(BF16))))
    m = ga * _dot(oa_ref[...], woa_ref[...].astype(BF16))
    gb = 1.0 / (1.0 + jnp.exp(-_dot(h, wg_ref[:, D_MODEL:].astype(BF16))))
    m = m + gb * _dot(ob_ref[...], wob_ref[...].astype(BF16))
    o_ref[...] = x + _dot(m.astype(BF16), wo_ref[...].astype(BF16))


def _merge(x, oa, ob, g, wg, woa, wob, wo, tm):
    n = x.shape[0]
    row = lambda i: (i, 0)
    const = lambda i: (0, 0)
    return pl.pallas_call(
        _merge_kernel,
        grid=(n // tm,),
        in_specs=[pl.BlockSpec((tm, D_MODEL), row),
                  pl.BlockSpec((tm, WIDTH_A), row),
                  pl.BlockSpec((tm, WIDTH_B), row),
                  pl.BlockSpec((1, D_MODEL), const),
                  pl.BlockSpec((D_MODEL, 2 * D_MODEL), const, pipeline_mode=pl.Buffered(1)),
                  pl.BlockSpec((WIDTH_A, D_MODEL), const, pipeline_mode=pl.Buffered(1)),
                  pl.BlockSpec((WIDTH_B, D_MODEL), const, pipeline_mode=pl.Buffered(1)),
                  pl.BlockSpec((D_MODEL, D_MODEL), const, pipeline_mode=pl.Buffered(1))],
        out_specs=pl.BlockSpec((tm, D_MODEL), row),
        out_shape=jax.ShapeDtypeStruct((n, D_MODEL), F32),
        compiler_params=pltpu.CompilerParams(
            dimension_semantics=("arbitrary",), vmem_limit_bytes=VMEM_LIMIT_BYTES),
        name="merge",
    )(x, oa, ob, g, wg, woa, wob, wo)


def _ffn_kernel(*refs, tm, seq_len, has_prev):
    if has_prev:
        (x_ref, g_ref, wup_ref, cw_ref, cb_ref, wdn_ref, gfin_ref, e1_ref, e2_ref,
         y_ref, u_ref, act_ref) = refs
    else:
        (x_ref, g_ref, wup_ref, cw_ref, cb_ref, wdn_ref, gfin_ref,
         y_ref, u_ref, act_ref, carry_ref) = refs
    i = pl.program_id(0)
    x = x_ref[...]
    h = _rmsnorm(x, g_ref[...]).astype(BF16)
    row = lax.broadcasted_iota(jnp.int32, (tm, FF_CHUNK), 0)
    row8 = lax.broadcasted_iota(jnp.int32, (SUBLANES, FF_CHUNK), 0)
    if has_prev:
        pos = row & (seq_len - 1)
    else:
        @pl.when((i * tm) % seq_len == 0)
        def _():
            carry_ref[...] = jnp.zeros(carry_ref.shape, F32)

    for c in range(D_FF // FF_CHUNK):
        halves = []
        for part in range(2):
            cols = slice(part * D_FF + c * FF_CHUNK, part * D_FF + (c + 1) * FF_CHUNK)
            u = _dot(h, wup_ref[:, cols])
            um1 = pltpu.roll(u, 1, 0)
            um2 = pltpu.roll(u, 2, 0)
            if has_prev:
                um1 = jnp.where(pos == 0, e1_ref[:, cols], um1)
                um2 = jnp.where(pos < 2, e2_ref[:, cols], um2)
                u_ref[:, cols] = u
            else:
                prev = carry_ref[:, cols]
                top1 = jnp.where(row8 == 0, pltpu.roll(prev, 1, 0), um1[:SUBLANES])
                top2 = jnp.where(row8 < 2, pltpu.roll(prev, 2, 0), um2[:SUBLANES])
                um1 = jnp.concatenate([top1, um1[SUBLANES:]], axis=0)
                um2 = jnp.concatenate([top2, um2[SUBLANES:]], axis=0)
                carry_ref[:, cols] = u[tm - SUBLANES:]
                u_ref[:, cols] = u[tm - SUBLANES:]
            cw = cw_ref[:, cols]
            halves.append(((cb_ref[:, cols] + cw[0:1] * um2) + cw[1:2] * um1) + cw[2:3] * u)
        gate, val = halves
        act = (gate * (1.0 / (1.0 + jnp.exp(-gate)))) * val
        act_ref[:, c * FF_CHUNK:(c + 1) * FF_CHUNK] = act.astype(BF16)

    x3 = x + _dot(act_ref[...], wdn_ref[...].astype(BF16))
    y_ref[...] = _rmsnorm(x3, gfin_ref[...])


def _ffn(x, g, wup, cw, cb, wdn, gfin, prev, tm, seq_len):
    n = x.shape[0]
    has_prev = prev is not None
    row = lambda i: (i, 0)
    const = lambda i: (0, 0)
    in_specs = [pl.BlockSpec((tm, D_MODEL), row),
                pl.BlockSpec((1, D_MODEL), const),
                pl.BlockSpec((D_MODEL, 2 * D_FF), const, pipeline_mode=pl.Buffered(1)),
                pl.BlockSpec((CONV_W, 2 * D_FF), const),
                pl.BlockSpec((1, 2 * D_FF), const),
                pl.BlockSpec((D_FF, D_MODEL), const, pipeline_mode=pl.Buffered(1)),
                pl.BlockSpec((1, D_MODEL), const)]
    scratch = [pltpu.VMEM((tm, D_FF), BF16)]
    if has_prev:
        assert n == tm and tm % seq_len == 0 and seq_len & (seq_len - 1) == 0
        in_specs += [pl.BlockSpec((tm, 2 * D_FF), row)] * 2
        u_shape, u_spec = (n, 2 * D_FF), pl.BlockSpec((tm, 2 * D_FF), row)
        args = (x, g, wup, cw, cb, wdn, gfin) + tuple(prev)
    else:
        assert seq_len % tm == 0
        tiles_per_seq = seq_len // tm
        u_shape = (n // seq_len * SUBLANES, 2 * D_FF)
        u_spec = pl.BlockSpec((SUBLANES, 2 * D_FF), lambda i: (i // tiles_per_seq, 0))
        scratch.append(pltpu.VMEM((SUBLANES, 2 * D_FF), F32))
        args = (x, g, wup, cw, cb, wdn, gfin)
    return pl.pallas_call(
        functools.partial(_ffn_kernel, tm=tm, seq_len=seq_len, has_prev=has_prev),
        grid=(n // tm,),
        in_specs=in_specs,
        out_specs=[pl.BlockSpec((tm, D_MODEL), row), u_spec],
        out_shape=[jax.ShapeDtypeStruct((n, D_MODEL), F32), jax.ShapeDtypeStruct(u_shape, F32)],
        scratch_shapes=scratch,
        compiler_params=pltpu.CompilerParams(
            dimension_semantics=("arbitrary",), vmem_limit_bytes=VMEM_LIMIT_BYTES),
        name="ffn_sample" if has_prev else "ffn_prompt",
    )(*args)


def kernel(x_prompt, x_sample, cache_moba_k, cache_moba_v, cache_diff_k, cache_diff_v, state_conv, page_table, rel_bias, norm_attn, w_in, w_gate, w_out_a, w_out_b, w_out, lambda_q1, lambda_k1, lambda_q2, lambda_k2, diff_norm, norm_ffn, w_up, conv_w, conv_b, w_down, norm_final):
    batch, seq, _ = x_prompt.shape
    n_seq, tokens, _ = x_sample.shape
    depth = w_in.shape[0]
    n_phys = cache_moba_k.shape[1]
    n_pages = page_table.shape[1]
    assert depth == 1 and seq % Q_TILE == 0 and ATT_TILE & (ATT_TILE - 1) == 0
    assert (n_pages * PAGE_SIZE) % MOBA_BLOCK == 0 and n_pages % PAGES_PER_STEP == 0
    assert n_pages * PAGE_SIZE // MOBA_BLOCK <= LANES and CONV_W - 1 <= tokens <= LANES // N_HEADS_B
    l = 0
    lam_init = 0.8 - 0.6 * math.exp(-0.3 * l)

    bias_a = rel_bias[:, :N_HEADS_A].T
    bias_d = rel_bias[:, N_HEADS_A:].T
    row = lambda v: v.reshape(1, -1)
    w_in_l, w_gate_l, w_oa_l, w_ob_l, w_o_l = w_in[l], w_gate[l], w_out_a[l], w_out_b[l], w_out[l]
    w_up_b = w_up[l].astype(BF16)
    w_dn_l = w_down[l]
    lam_vecs = jnp.stack([lambda_q1[l], lambda_k1[l], lambda_q2[l], lambda_k2[l]]).astype(F32)
    gain_d = row(diff_norm[l])
    g_attn, g_ffn, g_fin = row(norm_attn[l]), row(norm_ffn[l]), row(norm_final)
    cb = row(conv_b[l])

    xp = x_prompt.reshape(batch * seq, D_MODEL)
    qa_tb, ka, va_tb, qd_tb, kd, vd_tb, ka_t, va_t, kd_r, vd_r, km = _proj(xp, g_attn, w_in_l, 512, seq)
    oa = _moba_prompt(qa_tb, ka, va_tb, km, _bias_tiles(bias_a), batch, seq)
    ob = _diff_prompt(qd_tb, kd, vd_tb, _bias_tiles(bias_d), lam_vecs, gain_d, lam_init, batch, seq)
    x2 = _merge(xp, oa, ob, g_attn, w_gate_l, w_oa_l, w_ob_l, w_o_l, 512)
    yp, tail_p = _ffn(x2, g_ffn, w_up_b, conv_w[l], cb, w_dn_l, g_fin, None, 1024, seq)
    conv_p = tail_p.reshape(batch, SUBLANES, 2 * D_FF)[:, SUBLANES - (CONV_W - 1):]

    n_s = n_seq * tokens
    xs = x_sample.reshape(n_s, D_MODEL)
    qa_s, ka_s, va_s, qd_s, kd_s, vd_s, ka_sf, va_sf, kd_sf, vd_sf = _proj(xs, g_attn, w_in_l, n_s)

    pool_t = lambda c: jnp.transpose(c[l], (0, 2, 3, 1)).reshape(n_phys, WIDTH_A, PAGE_SIZE)
    pool_r = lambda c: c[l].reshape(n_phys, PAGE_SIZE * N_HEADS_B, 2 * HEAD_DIM)
    far_a, last_a, new_a = _sample_bias(bias_a, tokens, 1)
    oa_s = _moba_sample(page_table, pool_t(cache_moba_k), pool_t(cache_moba_v),
                        _block_diag(qa_s, n_seq, tokens, N_HEADS_A, N_HEADS_A),
                        _pad_rows(ka_s, n_seq), _pad_rows(va_s, n_seq),
                        jnp.stack([far_a, last_a, new_a]), tokens=tokens)
    far_d, last_d, new_d = _sample_bias(bias_d, tokens, 2)
    groups_d = 2 * N_HEADS_B
    spread = lambda x: _spread_heads(x, N_HEADS_B, groups_d)
    per_head = lambda x: x.reshape(n_s * N_HEADS_B, 2 * HEAD_DIM)
    oa_d = _diff_sample(page_table, pool_r(cache_diff_k), pool_r(cache_diff_v),
                        _block_diag(qd_s, n_seq, tokens, groups_d, 2),
                        _pad_rows(per_head(kd_s), n_seq), _pad_rows(per_head(vd_s), n_seq),
                        jnp.stack([spread(far_d), spread(last_d)]),
                        spread(new_d[:, :LANES // N_HEADS_B]),
                        lam_vecs, gain_d, tokens=tokens, lam_init=lam_init)
    x2s = _merge(xs, oa_s.reshape(n_s, WIDTH_A), oa_d.reshape(n_s, WIDTH_B), g_attn,
                 w_gate_l, w_oa_l, w_ob_l, w_o_l, n_s)
    st = state_conv[l]
    zero = jnp.zeros((n_seq, 1, 2 * D_FF), F32)
    e1 = jnp.concatenate([st[:, 1:2]] + [zero] * (tokens - 1), axis=1).reshape(n_s, 2 * D_FF)
    e2 = jnp.concatenate([st[:, 0:2]] + [zero] * (tokens - 2), axis=1).reshape(n_s, 2 * D_FF)
    ys, u_s = _ffn(x2s, g_ffn, w_up_b, conv_w[l], cb, w_dn_l, g_fin, (e1, e2), n_s, tokens)
    conv_s = u_s.reshape(n_seq, tokens, 2 * D_FF)[:, tokens - (CONV_W - 1):]

    shp_a = lambda a, b_, t_: a.reshape(1, b_, t_, N_HEADS_A, HEAD_DIM)
    shp_d = lambda a, b_, t_: a.reshape(1, b_, t_, N_HEADS_B, 2 * HEAD_DIM)
    untr = lambda a: a.reshape(batch, N_HEADS_A, HEAD_DIM, seq).transpose(0, 3, 1, 2)[None]
    return (yp.reshape(batch, seq, D_MODEL), ys.reshape(n_seq, tokens, D_MODEL),
            untr(ka_t), untr(va_t),
            shp_d(kd_r, batch, seq), shp_d(vd_r, batch, seq), conv_p[None],
            shp_a(ka_sf, n_seq, tokens), shp_a(va_sf, n_seq, tokens),
            shp_d(kd_sf, n_seq, tokens), shp_d(vd_sf, n_seq, tokens), conv_s[None])
```

```python
import functools
import math

import jax
import jax.numpy as jnp
from jax import lax
from jax.experimental import pallas as pl
from jax.experimental.pallas import tpu as pltpu

F32 = jnp.float32
BF16 = jnp.bfloat16

D_MODEL = 1024
HEAD_DIM = 64
N_HEADS_A = D_MODEL // 128
N_HEADS_B = D_MODEL // 256
WIDTH_A = N_HEADS_A * HEAD_DIM
WIDTH_B = N_HEADS_B * 2 * HEAD_DIM
N_IN = 3 * WIDTH_A + 3 * WIDTH_B
MOBA_BLOCK = 256
MOBA_TOPK = 3
NUM_BUCKETS = 32
MAX_DISTANCE = 128
D_FF = ((8 * D_MODEL // 3 + 127) // 128) * 128
CONV_W = 3
EPS = 1e-6
PAGE_SIZE = 128
SCALE = HEAD_DIM ** -0.5
LOG2E = math.log2(math.e)

LANES = 128
SUBLANES = 8
VMEM_LIMIT_BYTES = 56 * 1024 * 1024

ATT_TILE = MOBA_BLOCK
Q_TILE = 2 * ATT_TILE
ATT_PAIRS = 2
ATT_RING = 2
FF_CHUNK = 256
PAGES_PER_STEP = 16
RING_SLOTS = 6
NEG_BIG = -1e30


def _dot(a, b):
    return jnp.dot(a, b, preferred_element_type=F32)


def _dot_nt(a, b):
    return lax.dot_general(a, b, (((1,), (1,)), ((), ())), preferred_element_type=F32)


def _rmsnorm(x, g):
    return (x * lax.rsqrt(jnp.mean(x * x, axis=-1, keepdims=True) + EPS)) * g


def _lane_fold(op, x):
    out = x[:, :LANES]
    for c in range(1, x.shape[1] // LANES):
        out = op(out, x[:, c * LANES:(c + 1) * LANES])
    return out


def _t5_bucket(dist):
    n = jnp.maximum(dist, 0)
    max_exact = NUM_BUCKETS // 2
    nf = jnp.maximum(n, 1).astype(F32)
    large = max_exact + (jnp.log(nf / max_exact) / math.log(MAX_DISTANCE / max_exact)
                         * (NUM_BUCKETS - max_exact)).astype(jnp.int32)
    large = jnp.minimum(large, NUM_BUCKETS - 1)
    return jnp.where(n < max_exact, n, large)


def _bias_of(bias_t, dist):
    onehot = jax.nn.one_hot(_t5_bucket(dist), NUM_BUCKETS, dtype=F32)
    return jnp.einsum('hb,...b->h...', bias_t.astype(F32), onehot, precision=lax.Precision.HIGHEST)


def _lambda(lam_ref, lam_init):
    lv = lam_ref[...]
    a = jnp.sum(lv[0:1] * lv[1:2], axis=-1, keepdims=True)
    b = jnp.sum(lv[2:3] * lv[3:4], axis=-1, keepdims=True)
    return jnp.exp(a) - jnp.exp(b) + lam_init


def _proj_kernel(x_ref, g_ref, w_ref,
                 qa_ref, ka_ref, va_ref, qd_ref, kd_ref, vd_ref,
                 kaf_ref, vaf_ref, kdf_ref, vdf_ref, *km_ref, prompt):
    tm = x_ref.shape[0]
    h = _rmsnorm(x_ref[...], g_ref[...]).astype(BF16)
    outs = ((qa_ref, None, SCALE), (ka_ref, kaf_ref, None), (va_ref, vaf_ref, None),
            (qd_ref, None, SCALE), (kd_ref, kdf_ref, None), (vd_ref, vdf_ref, None))
    for c, (b_ref, f_ref, scale) in enumerate(outs):
        u = _dot(h, w_ref[:, c * WIDTH_A:(c + 1) * WIDTH_A].astype(BF16))
        if scale is not None:
            u = u * (scale * LOG2E if prompt else scale)
        if not prompt:
            if f_ref is not None:
                f_ref[...] = u
            b_ref[...] = u.astype(BF16)
            continue
        is_key = c in (1, 4)
        ut = u.T if c != 4 else None
        if c in (1, 2):
            f_ref[0] = ut
        elif f_ref is not None:
            for hd in range(N_HEADS_B):
                f_ref[pl.ds(hd, tm, stride=N_HEADS_B), :] = u[:, hd * LANES:(hd + 1) * LANES]
        if is_key:
            b_ref[...] = u.astype(BF16)
        else:
            b_ref[0] = ut.astype(BF16)
        if c == 1:
            for j in range(tm // MOBA_BLOCK):
                km_ref[0][j] = jnp.sum(u[j * MOBA_BLOCK:(j + 1) * MOBA_BLOCK], axis=0,
                                       keepdims=True) * (1.0 / MOBA_BLOCK)


def _proj(x, g, w_in, tm, seq=None):
    n = x.shape[0]
    row = lambda i: (i, 0)
    const = lambda i: (0, 0)
    blk = pl.BlockSpec((tm, WIDTH_A), row)
    b_specs = [blk] * 6
    b_shapes = [jax.ShapeDtypeStruct((n, WIDTH_A), BF16)] * 6
    f_specs = [blk] * 4
    f_shapes = [jax.ShapeDtypeStruct((n, WIDTH_A), F32)] * 4
    if seq is not None:
        assert seq % tm == 0 and tm % MOBA_BLOCK == 0 and 2 * HEAD_DIM == LANES
        tiles = seq // tm
        t_spec = pl.BlockSpec((1, WIDTH_A, tm), lambda i: (i // tiles, 0, i % tiles))
        r_spec = pl.BlockSpec((tm * N_HEADS_B, LANES), row)
        r_shape = jax.ShapeDtypeStruct((n * N_HEADS_B, LANES), F32)
        t_shape = lambda dt: jax.ShapeDtypeStruct((n // seq, WIDTH_A, seq), dt)
        b_specs = [t_spec, blk, t_spec, t_spec, blk, t_spec]
        b_shapes = [t_shape(BF16), b_shapes[0], t_shape(BF16), t_shape(BF16), b_shapes[0], t_shape(BF16)]
        f_specs = [t_spec, t_spec, r_spec, r_spec,
                   pl.BlockSpec((tm // MOBA_BLOCK, 1, WIDTH_A), lambda i: (i, 0, 0))]
        f_shapes = [t_shape(F32), t_shape(F32), r_shape, r_shape,
                    jax.ShapeDtypeStruct((n // MOBA_BLOCK, 1, WIDTH_A), F32)]
    return pl.pallas_call(
        functools.partial(_proj_kernel, prompt=seq is not None),
        grid=(n // tm,),
        in_specs=[pl.BlockSpec((tm, D_MODEL), row),
                  pl.BlockSpec((1, D_MODEL), const),
                  pl.BlockSpec((D_MODEL, N_IN), const, pipeline_mode=pl.Buffered(1))],
        out_specs=b_specs + f_specs,
        out_shape=b_shapes + f_shapes,
        compiler_params=pltpu.CompilerParams(
            dimension_semantics=("arbitrary",), vmem_limit_bytes=VMEM_LIMIT_BYTES),
        name="proj",
    )(x, g, w_in)


def _row_fold(op, x):
    y = x.reshape(x.shape[0] // SUBLANES, SUBLANES, x.shape[1])
    out = y[0]
    for g in range(1, y.shape[0]):
        out = op(out, y[g])
    return out


def _attend_static(qt_ops, k_tile, vt_ref, vt_rows, bias_ref, bidx, scratch, qi):
    t = ATT_TILE
    tq = Q_TILE
    halves = tq // t
    trips = qi + 1
    outs = [None] * len(qt_ops)

    def pass1(i):
        s_ref, mb_ref = scratch[i % len(scratch)]
        mx = [None] * halves
        k_tiles = [k_tile(i, jj) for jj in range(trips)]
        for hq in range(halves):
            cols = slice(hq * t, (hq + 1) * t)
            qt_cols = qt_ops[i][:, cols]
            for jj in range(trips):
                for hk in range(halves):
                    dist = (halves * qi + hq) - (halves * jj + hk)
                    if dist < 0:
                        continue
                    blk = _dot(k_tiles[jj][hk * t:(hk + 1) * t], qt_cols)
                    if dist < 2:
                        blk = blk + bias_ref[bidx[i], dist]
                    s_ref[jj, hk * t:(hk + 1) * t, cols] = blk
                    f = _row_fold(jnp.maximum, blk)
                    mx[hq] = f if mx[hq] is None else jnp.maximum(mx[hq], f)
                    yield
        m = jnp.max(jnp.concatenate(mx, axis=1), axis=0, keepdims=True)
        mb_ref[...] = jnp.broadcast_to(m, (SUBLANES, tq))

    def pass2(i):
        s_ref, mb_ref = scratch[i % len(scratch)]
        accs, sums = [], []
        for hq in range(halves):
            cols = slice(hq * t, (hq + 1) * t)
            mb = mb_ref[:, cols]
            l = jnp.zeros((SUBLANES, t), F32)
            acc = None
            n_keys = (halves * qi + hq + 1) * t
            for kb in range(n_keys // t):
                jj, hk = divmod(kb, halves)
                sv = s_ref[jj, hk * t:(hk + 1) * t, cols].reshape(t // SUBLANES, SUBLANES, t)
                p = jnp.exp2(sv - mb[None]).reshape(t, t)
                l = l + _row_fold(jnp.add, p)
                part = _dot(vt_ref[0, vt_rows[i], kb * t:(kb + 1) * t], p.astype(BF16))
                acc = part if acc is None else acc + part
                yield
            accs.append(acc)
            sums.append(l)
        outs[i] = (jnp.concatenate(accs, axis=1),
                   jnp.sum(jnp.concatenate(sums, axis=1), axis=0, keepdims=True))

    for _ in pass1(0):
        pass
    for i in range(1, len(qt_ops)):
        streams = [pass1(i), pass2(i - 1)]
        while streams:
            streams = [g for g in streams if next(g, StopIteration) is not StopIteration]
    for _ in pass2(len(qt_ops) - 1):
        pass
    return outs


def _per_query_tile(nq, body):
    qi = pl.program_id(2)
    for n in range(nq):
        pl.when(qi == n)(functools.partial(body, n))


def _half_rows(qt, half):
    zero = jnp.zeros((HEAD_DIM, qt.shape[1]), qt.dtype)
    if half == 0:
        return jnp.concatenate([qt[:HEAD_DIM], zero], axis=0)
    return jnp.concatenate([zero, qt[HEAD_DIM:]], axis=0)


def _scratch_ring(refs):
    return [tuple(refs[2 * r:2 * r + 2]) for r in range(ATT_RING)]


def _moba_prompt_kernel(qt_ref, k_ref, vt_ref, km_ref, bias_ref, o_ref, *scratch, nq):
    t = ATT_TILE
    tq = Q_TILE
    tile_shift = t.bit_length() - 1
    nb = km_ref.shape[0]
    qi = pl.program_id(2)
    blk = lax.broadcasted_iota(jnp.int32, (nb, tq), 0)
    own = (tq // t) * qi + (lax.broadcasted_iota(jnp.int32, (nb, tq), 1) >> tile_shift)

    qt_ops = []
    for pair in range(ATT_PAIRS):
        rows = slice(pair * LANES, (pair + 1) * LANES)
        qt = qt_ref[0, rows, :]
        km = km_ref[:, 0, rows]
        km_hi = km.astype(BF16)
        km_lo = (km - km_hi.astype(F32)).astype(BF16)
        for e in range(2):
            qet = _half_rows(qt, e)
            gt = _dot(km_hi, qet) + _dot(km_lo, qet)
            rank = jnp.zeros((nb, tq), jnp.int32)
            for m in range(nb):
                gm = gt[m:m + 1, :]
                ahead = (gm > gt) | ((gm == gt) & (m < blk))
                rank = rank + jnp.where(ahead & (m < own), 1, 0)
            keep = (blk < own) & (rank < MOBA_TOPK) & (jnp.abs(gt) < jnp.inf)
            pen = jnp.where(keep | (blk == own), 0.0, NEG_BIG)
            pen = jnp.concatenate([pen, jnp.zeros((LANES - nb, tq), F32)], axis=0)
            qt_ops.append(jnp.concatenate([qet, pen.astype(BF16)], axis=0))

    k_lane = lax.broadcasted_iota(jnp.int32, (tq, LANES), 1)
    k_blk = lax.broadcasted_iota(jnp.int32, (tq, LANES), 0) >> tile_shift

    def k_tile(i, jj):
        pair = i // 2
        onehot = jnp.where(k_lane == (tq // t) * jj + k_blk, 1.0, 0.0).astype(BF16)
        return jnp.concatenate(
            [k_ref[jj * tq:(jj + 1) * tq, pair * LANES:(pair + 1) * LANES], onehot], axis=1)

    n_ops = 2 * ATT_PAIRS
    vt_rows = [slice(i * HEAD_DIM, (i + 1) * HEAD_DIM) for i in range(n_ops)]

    def body(n):
        res = _attend_static(qt_ops, k_tile, vt_ref, vt_rows, bias_ref, tuple(range(n_ops)),
                             _scratch_ring(scratch), n)
        for pair in range(ATT_PAIRS):
            (a0, l0), (a1, l1) = res[2 * pair], res[2 * pair + 1]
            ot = jnp.concatenate([a0 / l0, a1 / l1], axis=0)
            o_ref[:, pair * LANES:(pair + 1) * LANES] = ot.T.astype(o_ref.dtype)

    _per_query_tile(nq, body)


def _attn_scratch(nq):
    one = [pltpu.VMEM((nq, Q_TILE, Q_TILE), F32),
           pltpu.VMEM((SUBLANES, Q_TILE), F32)]
    return one * ATT_RING


def _attn_specs(w, seq, nq):
    return ([pl.BlockSpec((1, w, Q_TILE), lambda b, g, qi: (b, g, qi)),
             pl.BlockSpec((seq, w), lambda b, g, qi: (b, g)),
             pl.BlockSpec((1, w, seq), lambda b, g, qi: (b, g, 0))],
            pl.BlockSpec((Q_TILE, w), lambda b, g, qi: (b * nq + qi, g)))


def _moba_prompt(qa_t, ka, va_t, km, bias_tiles, batch, seq):
    nq = seq // Q_TILE
    nb = seq // MOBA_BLOCK
    w = ATT_PAIRS * LANES
    assert WIDTH_A % w == 0
    in_specs, out_spec = _attn_specs(w, seq, nq)
    return pl.pallas_call(
        functools.partial(_moba_prompt_kernel, nq=nq),
        grid=(batch, WIDTH_A // w, nq),
        in_specs=in_specs + [pl.BlockSpec((nb, 1, w), lambda b, g, qi: (b, 0, g)),
                             pl.BlockSpec((2 * ATT_PAIRS, 2, ATT_TILE, ATT_TILE),
                                          lambda b, g, qi: (g, 0, 0, 0))],
        out_specs=out_spec,
        out_shape=jax.ShapeDtypeStruct((batch * seq, WIDTH_A), BF16),
        scratch_shapes=_attn_scratch(nq),
        compiler_params=pltpu.CompilerParams(
            dimension_semantics=("arbitrary", "arbitrary", "arbitrary"),
            vmem_limit_bytes=VMEM_LIMIT_BYTES),
        name="moba_prompt",
    )(qa_t, ka, va_t, km, bias_tiles)


def _diff_prompt_kernel(qt_ref, k_ref, vt_ref, bias_ref, lam_ref, gain_ref, o_ref, *scratch,
                        lam_init, nq):
    tq = Q_TILE
    qt_ops = []
    for head in range(ATT_PAIRS):
        qt = qt_ref[0, head * LANES:(head + 1) * LANES, :]
        qt_ops += [_half_rows(qt, c) for c in range(2)]

    def k_tile(i, jj):
        head = i // 2
        return k_ref[jj * tq:(jj + 1) * tq, head * LANES:(head + 1) * LANES]

    n_ops = 2 * ATT_PAIRS
    vt_rows = [slice((i // 2) * LANES, (i // 2 + 1) * LANES) for i in range(n_ops)]

    def body(n):
        res = _attend_static(qt_ops, k_tile, vt_ref, vt_rows, bias_ref,
                             tuple(i // 2 for i in range(n_ops)), _scratch_ring(scratch), n)
        lam = _lambda(lam_ref, lam_init)
        for head in range(ATT_PAIRS):
            (a0, l0), (a1, l1) = res[2 * head], res[2 * head + 1]
            o = (a0 / l0 - lam * (a1 / l1)).T
            o_ref[:, head * LANES:(head + 1) * LANES] = (
                _rmsnorm(o, gain_ref[...]) * (1.0 - lam_init)).astype(o_ref.dtype)

    _per_query_tile(nq, body)


def _diff_prompt(qd_t, kd, vd_t, bias_tiles, lam_vecs, gain, lam_init, batch, seq):
    nq = seq // Q_TILE
    w = ATT_PAIRS * LANES
    assert WIDTH_B % w == 0
    in_specs, out_spec = _attn_specs(w, seq, nq)
    return pl.pallas_call(
        functools.partial(_diff_prompt_kernel, lam_init=lam_init, nq=nq),
        grid=(batch, WIDTH_B // w, nq),
        in_specs=in_specs + [
                  pl.BlockSpec((ATT_PAIRS, 2, ATT_TILE, ATT_TILE), lambda b, g, qi: (g, 0, 0, 0)),
                  pl.BlockSpec((4, HEAD_DIM), lambda b, g, qi: (0, 0)),
                  pl.BlockSpec((1, 2 * HEAD_DIM), lambda b, g, qi: (0, 0))],
        out_specs=out_spec,
        out_shape=jax.ShapeDtypeStruct((batch * seq, WIDTH_B), BF16),
        scratch_shapes=_attn_scratch(nq),
        compiler_params=pltpu.CompilerParams(
            dimension_semantics=("arbitrary", "arbitrary", "arbitrary"),
            vmem_limit_bytes=VMEM_LIMIT_BYTES),
        name="diff_prompt",
    )(qd_t, kd, vd_t, bias_tiles, lam_vecs, gain)


def _toeplitz(w, t):
    h = w.shape[0]
    m = jnp.broadcast_to(w[:, None, :], (h, t, 2 * t)).reshape(h, 2 * t * t)
    return m[:, :t * (2 * t - 1)].reshape(h, t, 2 * t - 1)[:, :, :t]


def _bias_tiles(bias_t):
    t = ATT_TILE
    assert t + 1 >= MAX_DISTANCE
    k = jnp.arange(2 * t)
    d = jnp.where(k < t, k, k - 2 * t)
    far = _bias_of(bias_t, jnp.full((1,), 2 * t))
    diag = _toeplitz(jnp.where(d >= 0, (_bias_of(bias_t, d) - far) * LOG2E, -jnp.inf), t)
    sub = _toeplitz((_bias_of(bias_t, d + t) - far) * LOG2E, t)
    return jnp.stack([diag, sub], axis=1)


def _page_stream(pt_ref, pk_ref, pv_ref, buf_ref, sem_ref, n_seq, n_chunks):
    pps = PAGES_PER_STEP
    ahead = RING_SLOTS - 1
    b, ph, j = pl.program_id(0), pl.program_id(1), pl.program_id(2)
    step = (b * 2 + ph) * n_chunks + j
    total = n_seq * 2 * n_chunks

    def copies(pool_ref, bb, jj, slot):
        return [pltpu.make_async_copy(pool_ref.at[pt_ref[bb, jj * pps + i]],
                                      buf_ref.at[slot, i], sem_ref.at[slot])
                for i in range(pps)]

    def start(chunk):
        jj, seq_phase = chunk % n_chunks, chunk // n_chunks
        for phase, pool_ref in ((0, pk_ref), (1, pv_ref)):
            @pl.when(seq_phase % 2 == phase)
            def _(pool_ref=pool_ref):
                for i, cp in enumerate(copies(pool_ref, seq_phase // 2, jj, chunk % RING_SLOTS)):
                    cp.start(priority=i % 2)

    @pl.when(step == 0)
    def _():
        for chunk in range(min(ahead, total)):
            start(jnp.int32(chunk))

    @pl.when(step + ahead < total)
    def _():
        start(step + ahead)

    slot = step % RING_SLOTS

    def wait(pool_ref):
        for cp in copies(pool_ref, b, j, slot):
            cp.wait()

    return slot, wait


def _moba_sample_kernel(pt_ref, pk_ref, pv_ref, q_ref, kn_ref, vn_ref, bias_ref, o_ref,
                        s_ref, snew_ref, acc_ref, l_ref, m_ref, bmax_ref, gate_ref, sel_ref,
                        buf_ref, sem_ref, *, n_seq, n_chunks, tokens):
    pps = PAGES_PER_STEP
    slot, wait = _page_stream(pt_ref, pk_ref, pv_ref, buf_ref, sem_ref, n_seq, n_chunks)
    rows = q_ref.shape[1]
    groups = rows // tokens
    ppb = MOBA_BLOCK // PAGE_SIZE
    bps = pps // ppb
    n_blocks = n_chunks * bps
    ph = pl.program_id(1)
    j = pl.program_id(2)
    lane = lax.broadcasted_iota(jnp.int32, (rows, LANES), 1)
    lanef = lane.astype(F32)
    q = q_ref[0]

    @pl.when((ph == 0) & (j == 0))
    def _():
        snew_ref[...] = _dot_nt(q, kn_ref[0]) + bias_ref[2]
        bmax_ref[...] = jnp.full((rows, LANES), -jnp.inf, F32)
        gate_ref[...] = jnp.zeros((rows, LANES), F32)

    @pl.when(ph == 0)
    def _():
        wait(pk_ref)
        is_last = j == n_chunks - 1
        bmax = bmax_ref[...]
        gate = gate_ref[...]
        for b in range(bps):
            smax = ssum = None
            for pg in range(ppb):
                i = b * ppb + pg
                raw = _dot(q, buf_ref[slot, i].astype(BF16))
                bias = bias_ref[0]
                if i == pps - 1:
                    bias = jnp.where(is_last, bias_ref[1], bias)
                s = raw + bias
                s_ref[j, :, i * PAGE_SIZE:(i + 1) * PAGE_SIZE] = s
                smax = s if smax is None else jnp.maximum(smax, s)
                ssum = raw if ssum is None else ssum + raw
            blk = j * bps + b
            bmax = jnp.where(lane == blk, jnp.max(smax, axis=1, keepdims=True), bmax)
            gate = jnp.where(lane == blk, jnp.sum(ssum, axis=1, keepdims=True), gate)
        bmax_ref[...] = bmax
        gate_ref[...] = gate

    @pl.when((ph == 1) & (j == 0))
    def _():
        g = jnp.where(lane < n_blocks, gate_ref[...], -jnp.inf)
        sel = jnp.zeros((rows, LANES), F32)
        for _ in range(MOBA_TOPK):
            top = jnp.max(g, axis=1, keepdims=True)
            first = jnp.min(jnp.where(g == top, lanef, float(LANES)), axis=1, keepdims=True)
            pick = lanef == first
            finite = jnp.where(jnp.abs(top) < jnp.inf, 1.0, 0.0)
            sel = jnp.maximum(sel, jnp.where(pick, finite, 0.0))
            g = jnp.where(pick, -jnp.inf, g)
        sel_ref[...] = sel
        m_sel = jnp.max(jnp.where(sel > 0.5, bmax_ref[...], -jnp.inf), axis=1, keepdims=True)
        m_new = jnp.max(snew_ref[...], axis=1, keepdims=True)
        m_ref[...] = jnp.broadcast_to(jnp.maximum(m_sel, m_new), (rows, LANES))
        acc_ref[...] = jnp.zeros(acc_ref.shape, F32)
        l_ref[...] = jnp.zeros(l_ref.shape, F32)

    @pl.when(ph == 1)
    def _():
        wait(pv_ref)
        m = m_ref[...]
        selv = sel_ref[...]
        acc = acc_ref[...]
        l = l_ref[...]
        for i in range(pps):
            blk = j * bps + i // ppb
            on = jnp.max(jnp.where(lane == blk, selv, 0.0), axis=1, keepdims=True)
            on = jnp.broadcast_to(on, (rows, LANES)) > 0.5
            s = s_ref[j, :, i * PAGE_SIZE:(i + 1) * PAGE_SIZE]
            p = jnp.exp(jnp.where(on, s - m, -jnp.inf))
            l = l + p
            acc = acc + _dot_nt(p.astype(BF16), buf_ref[slot, i].astype(BF16))
        acc_ref[...] = acc
        l_ref[...] = l

    @pl.when((ph == 1) & (j == n_chunks - 1))
    def _():
        p = jnp.exp(snew_ref[...] - m_ref[...])
        acc = acc_ref[...] + _dot(p.astype(BF16), vn_ref[0])
        l = jnp.sum(l_ref[...] + p, axis=1, keepdims=True)
        o = acc / l
        grp = lax.broadcasted_iota(jnp.int32, (groups, WIDTH_A), 0)
        col = lax.broadcasted_iota(jnp.int32, (groups, WIDTH_A), 1)
        head_shift = HEAD_DIM.bit_length() - 1
        w = jnp.where((col >> head_shift) == grp, 1.0, 0.0)
        tok = jnp.sum(o.reshape(tokens, groups, WIDTH_A) * w[None], axis=1)
        o_ref[0] = tok.astype(o_ref.dtype)


def _moba_sample(page_table, pool_kt, pool_vt, q_bd, k_new, v_new, bias_rows, *, tokens):
    pps = PAGES_PER_STEP
    n_seq, n_pages = page_table.shape
    n_chunks = n_pages // pps
    rows = q_bd.shape[1]
    seq_map = lambda b, ph, j, pt: (b, 0, 0)
    hbm = pl.BlockSpec(memory_space=pl.ANY)
    in_specs = [hbm, hbm,
                pl.BlockSpec((1, rows, WIDTH_A), seq_map),
                pl.BlockSpec((1, LANES, WIDTH_A), seq_map),
                pl.BlockSpec((1, LANES, WIDTH_A), seq_map),
                pl.BlockSpec((3, rows, LANES), lambda b, ph, j, pt: (0, 0, 0))]
    vec = pltpu.VMEM((rows, LANES), F32)
    grid_spec = pltpu.PrefetchScalarGridSpec(
        num_scalar_prefetch=1,
        grid=(n_seq, 2, n_chunks),
        in_specs=in_specs,
        out_specs=pl.BlockSpec((1, tokens, WIDTH_A), seq_map),
        scratch_shapes=[pltpu.VMEM((n_chunks, rows, pps * PAGE_SIZE), F32),
                        vec,
                        pltpu.VMEM((rows, WIDTH_A), F32),
                        vec, vec, vec, vec, vec,
                        pltpu.VMEM((RING_SLOTS, pps) + pool_kt.shape[1:], F32),
                        pltpu.SemaphoreType.DMA((RING_SLOTS,))])
    return pl.pallas_call(
        functools.partial(_moba_sample_kernel, n_seq=n_seq, n_chunks=n_chunks, tokens=tokens),
        grid_spec=grid_spec,
        out_shape=jax.ShapeDtypeStruct((n_seq, tokens, WIDTH_A), BF16),
        compiler_params=pltpu.CompilerParams(
            dimension_semantics=("arbitrary", "arbitrary", "arbitrary"),
            vmem_limit_bytes=VMEM_LIMIT_BYTES),
        name="moba_sample",
    )(page_table, pool_kt, pool_vt, q_bd, k_new, v_new, bias_rows)


def _diff_sample_kernel(pt_ref, pk_ref, pv_ref, q_ref, kn_ref, vn_ref, bias_ref, bnew_ref,
                        lam_ref, gain_ref, o_ref, s_ref, snew_ref, acc_ref, l_ref, m_ref,
                        buf_ref, sem_ref, *, n_seq, n_chunks, tokens, lam_init):
    pps = PAGES_PER_STEP
    slot, wait = _page_stream(pt_ref, pk_ref, pv_ref, buf_ref, sem_ref, n_seq, n_chunks)
    rows = q_ref.shape[1]
    pw = buf_ref.shape[2]
    ph = pl.program_id(1)
    j = pl.program_id(2)
    q = q_ref[0]

    @pl.when((ph == 0) & (j == 0))
    def _():
        snew_ref[...] = _dot_nt(q, kn_ref[0]) + bnew_ref[...]
        m_ref[...] = jnp.full((rows, LANES), -jnp.inf, F32)

    @pl.when(ph == 0)
    def _():
        wait(pk_ref)
        is_last = j == n_chunks - 1
        mx = m_ref[...]
        for i in range(pps):
            bias = bias_ref[0]
            if i == pps - 1:
                bias = jnp.where(is_last, bias_ref[1], bias)
            s = _dot_nt(q, buf_ref[slot, i].astype(BF16)) + bias
            s_ref[j, :, i * pw:(i + 1) * pw] = s
            mx = jnp.maximum(mx, _lane_fold(jnp.maximum, s))
        m_ref[...] = mx

    @pl.when((ph == 1) & (j == 0))
    def _():
        m = jnp.maximum(jnp.max(m_ref[...], axis=1, keepdims=True),
                        jnp.max(snew_ref[...], axis=1, keepdims=True))
        m_ref[...] = jnp.broadcast_to(m, (rows, LANES))
        acc_ref[...] = jnp.zeros(acc_ref.shape, F32)
        l_ref[...] = jnp.zeros(l_ref.shape, F32)

    @pl.when(ph == 1)
    def _():
        wait(pv_ref)
        m = m_ref[...]
        mw = jnp.concatenate([m] * (pw // LANES), axis=1)
        acc = acc_ref[...]
        l = l_ref[...]
        for i in range(pps):
            p = jnp.exp(s_ref[j, :, i * pw:(i + 1) * pw] - mw)
            l = l + _lane_fold(jnp.add, p)
            acc = acc + _dot(p.astype(BF16), buf_ref[slot, i].astype(BF16))
        acc_ref[...] = acc
        l_ref[...] = l

    @pl.when((ph == 1) & (j == n_chunks - 1))
    def _():
        p = jnp.exp(snew_ref[...] - m_ref[...])
        acc = acc_ref[...] + _dot(p.astype(BF16), vn_ref[0])
        l = jnp.sum(l_ref[...] + p, axis=1, keepdims=True)
        o = (acc / l).reshape(tokens, rows // tokens, LANES)
        lam = _lambda(lam_ref, lam_init)
        sub = lax.broadcasted_iota(jnp.int32, (rows // tokens, LANES), 0)
        gain = gain_ref[...]
        parts = []
        for h in range(N_HEADS_B):
            w = jnp.where(sub == 2 * h, 1.0, jnp.where(sub == 2 * h + 1, -lam, 0.0))
            parts.append(_rmsnorm(jnp.sum(o * w[None], axis=1), gain) * (1.0 - lam_init))
        o_ref[0] = jnp.concatenate(parts, axis=1).astype(o_ref.dtype)


def _diff_sample(page_table, pool_k, pool_v, q2, k_new, v_new, bias_rows, bias_new, lam_vecs, gain,
                 *, tokens, lam_init):
    pps = PAGES_PER_STEP
    n_seq, n_pages = page_table.shape
    n_chunks = n_pages // pps
    rows = q2.shape[1]
    pw = pool_k.shape[1]
    seq_map = lambda b, ph, j, pt: (b, 0, 0)
    const2 = lambda b, ph, j, pt: (0, 0)
    hbm = pl.BlockSpec(memory_space=pl.ANY)
    in_specs = [hbm, hbm,
                pl.BlockSpec((1, rows, LANES), seq_map),
                pl.BlockSpec((1, LANES, LANES), seq_map),
                pl.BlockSpec((1, LANES, LANES), seq_map),
                pl.BlockSpec((2, rows, pw), lambda b, ph, j, pt: (0, 0, 0)),
                pl.BlockSpec((rows, LANES), const2),
                pl.BlockSpec((4, HEAD_DIM), const2),
                pl.BlockSpec((1, 2 * HEAD_DIM), const2)]
    vec = pltpu.VMEM((rows, LANES), F32)
    grid_spec = pltpu.PrefetchScalarGridSpec(
        num_scalar_prefetch=1,
        grid=(n_seq, 2, n_chunks),
        in_specs=in_specs,
        out_specs=pl.BlockSpec((1, tokens, WIDTH_B), seq_map),
        scratch_shapes=[pltpu.VMEM((n_chunks, rows, pps * pw), F32),
                        vec, vec, vec, vec,
                        pltpu.VMEM((RING_SLOTS, pps) + pool_k.shape[1:], F32),
                        pltpu.SemaphoreType.DMA((RING_SLOTS,))])
    return pl.pallas_call(
        functools.partial(_diff_sample_kernel, n_seq=n_seq, n_chunks=n_chunks, tokens=tokens,
                          lam_init=lam_init),
        grid_spec=grid_spec,
        out_shape=jax.ShapeDtypeStruct((n_seq, tokens, WIDTH_B), BF16),
        compiler_params=pltpu.CompilerParams(
            dimension_semantics=("arbitrary", "arbitrary", "arbitrary"),
            vmem_limit_bytes=VMEM_LIMIT_BYTES),
        name="diff_sample",
    )(page_table, pool_k, pool_v, q2, k_new, v_new, bias_rows, bias_new, lam_vecs, gain)


def _block_diag(q, n_seq, tokens, groups, keep):
    w = q.shape[1] // groups
    q4 = q.reshape(n_seq, tokens, groups, 1, w)
    eye = (jnp.arange(groups)[:, None] % keep == jnp.arange(keep)[None, :]).astype(q.dtype)
    return (q4 * eye.reshape(1, 1, groups, keep, 1)).reshape(n_seq, tokens * groups, keep * w)


def _pad_rows(x, n_seq):
    x = x.reshape(n_seq, -1, x.shape[-1])
    return jnp.pad(x, ((0, 0), (0, LANES - x.shape[1]), (0, 0)))


def _sample_bias(bias_t, tokens, heads_per_group):
    assert PAGE_SIZE + 1 >= MAX_DISTANCE
    tok = jnp.arange(tokens)[:, None]
    col = jnp.arange(LANES)[None, :]
    far = _bias_of(bias_t, jnp.broadcast_to(2 * PAGE_SIZE, (tokens, LANES)))
    last = _bias_of(bias_t, PAGE_SIZE + tok - col)
    new = jnp.where((col <= tok) & (col < tokens), _bias_of(bias_t, tok - col), -jnp.inf)

    def rows(x):
        x = jnp.repeat(x.transpose(1, 0, 2), heads_per_group, axis=1)
        return x.reshape(-1, LANES)

    return rows(far), rows(last), rows(new)


def _spread_heads(x, n_heads, groups):
    r, k = x.shape
    row_head = (jnp.arange(r) % groups) // (groups // n_heads)
    own = row_head[:, None, None] == jnp.arange(n_heads)[None, None, :]
    return jnp.where(own, x[:, :, None], -jnp.inf).reshape(r, k * n_heads)


def _merge_kernel(x_ref, oa_ref, ob_ref, g_ref, wg_ref, woa_ref, wob_ref, wo_ref, o_ref):
    x = x_ref[...]
    h = _rmsnorm(x, g_ref[...]).astype(BF16)
    ga = 1.0 / (1.0 + jnp.exp(-_dot(h, wg_ref[:, :D_MODEL].astype(BF16))))
    m = ga * _dot(oa_ref[...], woa_ref[...].astype(BF16))
    gb = 1.0 / (1.0 + jnp.exp(-_dot(h, wg_ref[:, D_MODEL:].astype(BF16))))
    m = m + gb * _dot(ob_ref[...], wob_ref[...].astype(BF16))
    o_ref[...] = x + _dot(m.astype(BF16), wo_ref[...].astype(BF16))


def _merge(x, oa, ob, g, wg, woa, wob, wo, tm):
    n = x.shape[0]
    row = lambda i: (i, 0)
    const = lambda i: (0, 0)
    return pl.pallas_call(
        _merge_kernel,
        grid=(n // tm,),
        in_specs=[pl.BlockSpec((tm, D_MODEL), row),
                  pl.BlockSpec((tm, WIDTH_A), row),
                  pl.BlockSpec((tm, WIDTH_B), row),
                  pl.BlockSpec((1, D_MODEL), const),
                  pl.BlockSpec((D_MODEL, 2 * D_MODEL), const, pipeline_mode=pl.Buffered(1)),
                  pl.BlockSpec((WIDTH_A, D_MODEL), const, pipeline_mode=pl.Buffered(1)),
                  pl.BlockSpec((WIDTH_B, D_MODEL), const, pipeline_mode=pl.Buffered(1)),
                  pl.BlockSpec((D_MODEL, D_MODEL), const, pipeline_mode=pl.Buffered(1))],
        out_specs=pl.BlockSpec((tm, D_MODEL), row),
        out_shape=jax.ShapeDtypeStruct((n, D_MODEL), F32),
        compiler_params=pltpu.CompilerParams(
            dimension_semantics=("arbitrary",), vmem_limit_bytes=VMEM_LIMIT_BYTES),
        name="merge",
    )(x, oa, ob, g, wg, woa, wob, wo)


def _ffn_kernel(*refs, tm, seq_len, has_prev):
    if has_prev:
        (x_ref, g_ref, wup_ref, cw_ref, cb_ref, wdn_ref, gfin_ref, e1_ref, e2_ref,
         y_ref, u_ref, act_ref) = refs
    else:
        (x_ref, g_ref, wup_ref, cw_ref, cb_ref, wdn_ref, gfin_ref,
         y_ref, u_ref, act_ref, carry_ref) = refs
    i = pl.program_id(0)
    x = x_ref[...]
    h = _rmsnorm(x, g_ref[...]).astype(BF16)
    row = lax.broadcasted_iota(jnp.int32, (tm, FF_CHUNK), 0)
    row8 = lax.broadcasted_iota(jnp.int32, (SUBLANES, FF_CHUNK), 0)
    if has_prev:
        pos = row & (seq_len - 1)
    else:
        @pl.when((i * tm) % seq_len == 0)
        def _():
            carry_ref[...] = jnp.zeros(carry_ref.shape, F32)

    for c in range(D_FF // FF_CHUNK):
        halves = []
        for part in range(2):
            cols = slice(part * D_FF + c * FF_CHUNK, part * D_FF + (c + 1) * FF_CHUNK)
            u = _dot(h, wup_ref[:, cols])
            um1 = pltpu.roll(u, 1, 0)
            um2 = pltpu.roll(u, 2, 0)
            if has_prev:
                um1 = jnp.where(pos == 0, e1_ref[:, cols], um1)
                um2 = jnp.where(pos < 2, e2_ref[:, cols], um2)
                u_ref[:, cols] = u
            else:
                prev = carry_ref[:, cols]
                top1 = jnp.where(row8 == 0, pltpu.roll(prev, 1, 0), um1[:SUBLANES])
                top2 = jnp.where(row8 < 2, pltpu.roll(prev, 2, 0), um2[:SUBLANES])
                um1 = jnp.concatenate([top1, um1[SUBLANES:]], axis=0)
                um2 = jnp.concatenate([top2, um2[SUBLANES:]], axis=0)
                carry_ref[:, cols] = u[tm - SUBLANES:]
                u_ref[:, cols] = u[tm - SUBLANES:]
            cw = cw_ref[:, cols]
            halves.append(((cb_ref[:, cols] + cw[0:1] * um2) + cw[1:2] * um1) + cw[2:3] * u)
        gate, val = halves
        act = (gate * (1.0 / (1.0 + jnp.exp(-gate)))) * val
        act_ref[:, c * FF_CHUNK:(c + 1) * FF_CHUNK] = act.astype(BF16)

    x3 = x + _dot(act_ref[...], wdn_ref[...].astype(BF16))
    y_ref[...] = _rmsnorm(x3, gfin_ref[...])


def _ffn(x, g, wup, cw, cb, wdn, gfin, prev, tm, seq_len):
    n = x.shape[0]
    has_prev = prev is not None
    row = lambda i: (i, 0)
    const = lambda i: (0, 0)
    in_specs = [pl.BlockSpec((tm, D_MODEL), row),
                pl.BlockSpec((1, D_MODEL), const),
                pl.BlockSpec((D_MODEL, 2 * D_FF), const, pipeline_mode=pl.Buffered(1)),
                pl.BlockSpec((CONV_W, 2 * D_FF), const),
                pl.BlockSpec((1, 2 * D_FF), const),
                pl.BlockSpec((D_FF, D_MODEL), const, pipeline_mode=pl.Buffered(1)),
                pl.BlockSpec((1, D_MODEL), const)]
    scratch = [pltpu.VMEM((tm, D_FF), BF16)]
    if has_prev:
        assert n == tm and tm % seq_len == 0 and seq_len & (seq_len - 1) == 0
        in_specs += [pl.BlockSpec((tm, 2 * D_FF), row)] * 2
        u_shape, u_spec = (n, 2 * D_FF), pl.BlockSpec((tm, 2 * D_FF), row)
        args = (x, g, wup, cw, cb, wdn, gfin) + tuple(prev)
    else:
        assert seq_len % tm == 0
        tiles_per_seq = seq_len // tm
        u_shape = (n // seq_len * SUBLANES, 2 * D_FF)
        u_spec = pl.BlockSpec((SUBLANES, 2 * D_FF), lambda i: (i // tiles_per_seq, 0))
        scratch.append(pltpu.VMEM((SUBLANES, 2 * D_FF), F32))
        args = (x, g, wup, cw, cb, wdn, gfin)
    return pl.pallas_call(
        functools.partial(_ffn_kernel, tm=tm, seq_len=seq_len, has_prev=has_prev),
        grid=(n // tm,),
        in_specs=in_specs,
        out_specs=[pl.BlockSpec((tm, D_MODEL), row), u_spec],
        out_shape=[jax.ShapeDtypeStruct((n, D_MODEL), F32), jax.ShapeDtypeStruct(u_shape, F32)],
        scratch_shapes=scratch,
        compiler_params=pltpu.CompilerParams(
            dimension_semantics=("arbitrary",), vmem_limit_bytes=VMEM_LIMIT_BYTES),
        name="ffn_sample" if has_prev else "ffn_prompt",
    )(*args)


def kernel(x_prompt, x_sample, cache_moba_k, cache_moba_v, cache_diff_k, cache_diff_v, state_conv, page_table, rel_bias, norm_attn, w_in, w_gate, w_out_a, w_out_b, w_out, lambda_q1, lambda_k1, lambda_q2, lambda_k2, diff_norm, norm_ffn, w_up, conv_w, conv_b, w_down, norm_final):
    batch, seq, _ = x_prompt.shape
    n_seq, tokens, _ = x_sample.shape
    depth = w_in.shape[0]
    n_phys = cache_moba_k.shape[1]
    n_pages = page_table.shape[1]
    assert depth == 1 and seq % Q_TILE == 0 and ATT_TILE & (ATT_TILE - 1) == 0
    assert (n_pages * PAGE_SIZE) % MOBA_BLOCK == 0 and n_pages % PAGES_PER_STEP == 0
    assert n_pages * PAGE_SIZE // MOBA_BLOCK <= LANES and CONV_W - 1 <= tokens <= LANES // N_HEADS_B
    l = 0
    lam_init = 0.8 - 0.6 * math.exp(-0.3 * l)

    bias_a = rel_bias[:, :N_HEADS_A].T
    bias_d = rel_bias[:, N_HEADS_A:].T
    row = lambda v: v.reshape(1, -1)
    w_in_l, w_gate_l, w_oa_l, w_ob_l, w_o_l = w_in[l], w_gate[l], w_out_a[l], w_out_b[l], w_out[l]
    w_up_b = w_up[l].astype(BF16)
    w_dn_l = w_down[l]
    lam_vecs = jnp.stack([lambda_q1[l], lambda_k1[l], lambda_q2[l], lambda_k2[l]]).astype(F32)
    gain_d = row(diff_norm[l])
    g_attn, g_ffn, g_fin = row(norm_attn[l]), row(norm_ffn[l]), row(norm_final)
    cb = row(conv_b[l])

    xp = x_prompt.reshape(batch * seq, D_MODEL)
    qa_tb, ka, va_tb, qd_tb, kd, vd_tb, ka_t, va_t, kd_r, vd_r, km = _proj(xp, g_attn, w_in_l, 1024, seq)
    oa = _moba_prompt(qa_tb, ka, va_tb, km, _bias_tiles(bias_a), batch, seq)
    ob = _diff_prompt(qd_tb, kd, vd_tb, _bias_tiles(bias_d), lam_vecs, gain_d, lam_init, batch, seq)
    x2 = _merge(xp, oa, ob, g_attn, w_gate_l, w_oa_l, w_ob_l, w_o_l, 512)
    yp, tail_p = _ffn(x2, g_ffn, w_up_b, conv_w[l], cb, w_dn_l, g_fin, None, 1024, seq)
    conv_p = tail_p.reshape(batch, SUBLANES, 2 * D_FF)[:, SUBLANES - (CONV_W - 1):]

    n_s = n_seq * tokens
    xs = x_sample.reshape(n_s, D_MODEL)
    qa_s, ka_s, va_s, qd_s, kd_s, vd_s, ka_sf, va_sf, kd_sf, vd_sf = _proj(xs, g_attn, w_in_l, n_s)

    pool_t = lambda c: jnp.transpose(c[l], (0, 2, 3, 1)).reshape(n_phys, WIDTH_A, PAGE_SIZE)
    pool_r = lambda c: c[l].reshape(n_phys, PAGE_SIZE * N_HEADS_B, 2 * HEAD_DIM)
    far_a, last_a, new_a = _sample_bias(bias_a, tokens, 1)
    oa_s = _moba_sample(page_table, pool_t(cache_moba_k), pool_t(cache_moba_v),
                        _block_diag(qa_s, n_seq, tokens, N_HEADS_A, N_HEADS_A),
                        _pad_rows(ka_s, n_seq), _pad_rows(va_s, n_seq),
                        jnp.stack([far_a, last_a, new_a]), tokens=tokens)
    far_d, last_d, new_d = _sample_bias(bias_d, tokens, 2)
    groups_d = 2 * N_HEADS_B
    spread = lambda x: _spread_heads(x, N_HEADS_B, groups_d)
    per_head = lambda x: x.reshape(n_s * N_HEADS_B, 2 * HEAD_DIM)
    oa_d = _diff_sample(page_table, pool_r(cache_diff_k), pool_r(cache_diff_v),
                        _block_diag(qd_s, n_seq, tokens, groups_d, 2),
                        _pad_rows(per_head(kd_s), n_seq), _pad_rows(per_head(vd_s), n_seq),
                        jnp.stack([spread(far_d), spread(last_d)]),
                        spread(new_d[:, :LANES // N_HEADS_B]),
                        lam_vecs, gain_d, tokens=tokens, lam_init=lam_init)
    x2s = _merge(xs, oa_s.reshape(n_s, WIDTH_A), oa_d.reshape(n_s, WIDTH_B), g_attn,
                 w_gate_l, w_oa_l, w_ob_l, w_o_l, n_s)
    st = state_conv[l]
    zero = jnp.zeros((n_seq, 1, 2 * D_FF), F32)
    e1 = jnp.concatenate([st[:, 1:2]] + [zero] * (tokens - 1), axis=1).reshape(n_s, 2 * D_FF)
    e2 = jnp.concatenate([st[:, 0:2]] + [zero] * (tokens - 2), axis=1).reshape(n_s, 2 * D_FF)
    ys, u_s = _ffn(x2s, g_ffn, w_up_b, conv_w[l], cb, w_dn_l, g_fin, (e1, e2), n_s, tokens)
    conv_s = u_s.reshape(n_seq, tokens, 2 * D_FF)[:, tokens - (CONV_W - 1):]

    shp_a = lambda a, b_, t_: a.reshape(1, b_, t_, N_HEADS_A, HEAD_DIM)
    shp_d = lambda a, b_, t_: a.reshape(1, b_, t_, N_HEADS_B, 2 * HEAD_DIM)
    untr = lambda a: a.reshape(batch, N_HEADS_A, HEAD_DIM, seq).transpose(0, 3, 1, 2)[None]
    return (yp.reshape(batch, seq, D_MODEL), ys.reshape(n_seq, tokens, D_MODEL),
            untr(ka_t), untr(va_t),
            shp_d(kd_r, batch, seq), shp_d(vd_r, batch, seq), conv_p[None],
            shp_a(ka_sf, n_seq, tokens), shp_a(va_sf, n_seq, tokens),
            shp_d(kd_sf, n_seq, tokens), shp_d(vd_sf, n_seq, tokens), conv_s[None])
```
